```python
import math
import jax, jax.numpy as jnp
from jax import lax
import numpy as np

D_MODEL = 1024
BATCH = 8
SEQ = 2048
DEPTH = 1
DEC_BATCH = 128
DEC_SEQ = 4
PAST_LEN = 16384
PAGE_SIZE = 128

D_MIX = D_MODEL
D_S5 = D_MIX // 2
D_LRU = D_MIX - D_S5
S5_GROUP_CH = 16
S5_GROUPS = D_S5 // S5_GROUP_CH
S5_STATE = 64
LRU_HEADS = 8
LRU_HEAD_DIM = D_LRU // LRU_HEADS
LRU_C = 8.0
CONV_WIDTH = 4
N_EXPERTS = 32
TOP_K = 4
D_FF = D_MODEL
SWIGLU_LIMIT = 7.0
SWIGLU_ALPHA = 1.702
EXPERT_BLOCK = 256
N_MOD = 6
EPS = 1e-6

kernel_name = 'hymba_s5_rglru_moe_adaln_step'


def rms_norm(x, g):
    xf = x.astype(jnp.float32)
    return xf * lax.rsqrt(jnp.mean(xf * xf, axis=-1, keepdims=True) + EPS) * g.astype(jnp.float32)


def diag_scan(a, b, h0):
    b = b.at[:, 0].add(a[:, 0] * h0)

    def combine(l, r):
        return r[0] * l[0], r[0] * l[1] + r[1]

    _, h = lax.associative_scan(combine, (a, b), axis=1)
    return h


def complex_diag_scan(a_re, a_im, b_re, b_im, h0_re, h0_im):
    b_re = b_re.at[:, 0].add(a_re[:, 0] * h0_re - a_im[:, 0] * h0_im)
    b_im = b_im.at[:, 0].add(a_re[:, 0] * h0_im + a_im[:, 0] * h0_re)

    def combine(l, r):
        lar, lai, lbr, lbi = l
        rar, rai, rbr, rbi = r
        return (rar * lar - rai * lai, rar * lai + rai * lar,
                rar * lbr - rai * lbi + rbr, rar * lbi + rai * lbr + rbi)

    _, _, h_re, h_im = lax.associative_scan(combine, (a_re, a_im, b_re, b_im), axis=1)
    return h_re, h_im


def s5_mixer(u, h_re0, h_im0, a_re, a_im, log_dt, b_re, b_im, c_re, c_im, d, w_glu, b_glu):
    bsz, L, _ = u.shape
    a_re = a_re.astype(jnp.float32)
    a_im = a_im.astype(jnp.float32)
    dt = jnp.exp(log_dt.astype(jnp.float32))[:, None]
    mag = jnp.exp(dt * a_re)
    ang = dt * a_im
    ab_re = mag * jnp.cos(ang)
    ab_im = mag * jnp.sin(ang)
    den = a_re * a_re + a_im * a_im
    q_re = ((ab_re - 1.0) * a_re + ab_im * a_im) / den
    q_im = (ab_im * a_re - (ab_re - 1.0) * a_im) / den
    bb_re = q_re[..., None] * b_re - q_im[..., None] * b_im
    bb_im = q_re[..., None] * b_im + q_im[..., None] * b_re
    ug = u.reshape(bsz, L, S5_GROUPS, S5_GROUP_CH)
    bu_re = jnp.einsum('blgj,gnj->blgn', ug, bb_re)
    bu_im = jnp.einsum('blgj,gnj->blgn', ug, bb_im)
    shape = bu_re.shape
    h_re, h_im = complex_diag_scan(jnp.broadcast_to(ab_re, shape), jnp.broadcast_to(ab_im, shape),
                                   bu_re, bu_im, h_re0, h_im0)
    y = jnp.einsum('blgn,gjn->blgj', h_re, c_re) - jnp.einsum('blgn,gjn->blgj', h_im, c_im)
    y = y.reshape(bsz, L, D_S5) + d * u
    g = jax.nn.gelu(y, approximate=False)
    out = g * jax.nn.sigmoid(g @ w_glu + b_glu)
    return out, h_re[:, -1], h_im[:, -1]


def rglru_mixer(xr, gr, h0, conv0, conv_w, conv_b, w_a, b_a, w_x, b_x, lam):
    bsz, L, _ = xr.shape
    xp = jnp.concatenate([conv0.astype(xr.dtype), xr], axis=1)
    xc = conv_b + sum(xp[:, k:k + L] * conv_w[k] for k in range(CONV_WIDTH))
    xh = xc.reshape(bsz, L, LRU_HEADS, LRU_HEAD_DIM)
    r = jax.nn.sigmoid(jnp.einsum('blhi,hij->blhj', xh, w_a).reshape(bsz, L, D_LRU) + b_a)
    i = jax.nn.sigmoid(jnp.einsum('blhi,hij->blhj', xh, w_x).reshape(bsz, L, D_LRU) + b_x)
    log_a = -LRU_C * r * jax.nn.softplus(-lam.astype(jnp.float32))
    a = jnp.exp(log_a)
    b = jnp.sqrt(-jnp.expm1(2.0 * log_a)) * (i * xc)
    h = diag_scan(a, b, h0)
    y = h * jax.nn.gelu(gr, approximate=False)
    return y, h[:, -1], xp[:, L:]


def moe_ffn(x, router_w, router_b, w_gu, b_gu, w_down, b_down):
    T = x.shape[0]
    TK = T * TOP_K
    logits = (x @ router_w + router_b).astype(jnp.float32)
    top_logits, top_idx = lax.top_k(logits, TOP_K)
    gates = jax.nn.softmax(top_logits, axis=-1)
    flat_e = top_idx.reshape(-1)
    flat_g = gates.reshape(-1)
    order = jnp.argsort(flat_e, stable=True)
    sorted_e = flat_e[order]
    counts = jnp.bincount(flat_e, length=N_EXPERTS)
    padded = (counts + EXPERT_BLOCK - 1) // EXPERT_BLOCK * EXPERT_BLOCK
    padded_end = jnp.cumsum(padded)
    padded_start = padded_end - padded
    group_start = jnp.cumsum(counts) - counts
    dest = padded_start[sorted_e] + jnp.arange(TK) - group_start[sorted_e]
    n_blocks = (TK + EXPERT_BLOCK - 1) // EXPERT_BLOCK + N_EXPERTS
    n_rows = n_blocks * EXPERT_BLOCK
    row_token = jnp.zeros((n_rows,), jnp.int32).at[dest].set((order // TOP_K).astype(jnp.int32))
    row_gate = jnp.zeros((n_rows,), jnp.float32).at[dest].set(flat_g[order])
    block_expert = jnp.minimum(
        jnp.searchsorted(padded_end, jnp.arange(n_blocks) * EXPERT_BLOCK, side='right'), N_EXPERTS - 1)
    xb = x[row_token].reshape(n_blocks, EXPERT_BLOCK, x.shape[-1])

    def expert_block(args):
        xe, e = args
        hg = xe @ w_gu[e] + b_gu[e]
        gate, up = jnp.split(hg, 2, axis=-1)
        gate = jnp.minimum(gate, SWIGLU_LIMIT)
        up = jnp.clip(up, -SWIGLU_LIMIT, SWIGLU_LIMIT)
        return ((up + 1.0) * (gate * jax.nn.sigmoid(SWIGLU_ALPHA * gate))) @ w_down[e] + b_down[e]

    yb = lax.map(expert_block, (xb, block_expert)).reshape(n_rows, -1)
    return jnp.zeros((T, x.shape[-1]), jnp.float32).at[row_token].add(yb * row_gate[:, None])


def decoder_layer(h, c, s5_re0, s5_im0, lru_h0, conv0, p):
    bsz, L, _ = h.shape
    mod = jax.nn.silu(c.astype(jnp.float32)) @ p['w_ada'] + p['b_ada']
    shift1, scale1, gate1, shift2, scale2, gate2 = jnp.split(mod[:, None, :], N_MOD, axis=-1)
    xn = rms_norm(h, p['g_mix']) * (1.0 + scale1) + shift1
    proj = xn @ p['w_in']
    u, xr, gr = jnp.split(proj, [D_S5, D_S5 + D_LRU], axis=-1)
    y_s5, s5_re, s5_im = s5_mixer(u, s5_re0, s5_im0, p['s5_a_re'], p['s5_a_im'], p['s5_log_dt'],
                                  p['s5_b_re'], p['s5_b_im'], p['s5_c_re'], p['s5_c_im'], p['s5_d'],
                                  p['s5_w_glu'], p['s5_b_glu'])
    y_lru, lru_h, conv = rglru_mixer(xr, gr, lru_h0, conv0, p['lru_conv_w'], p['lru_conv_b'],
                                     p['lru_w_a'], p['lru_b_a'], p['lru_w_x'], p['lru_b_x'], p['lru_lambda'])
    mixed = jnp.concatenate([rms_norm(y_s5, p['g_out_s5']), rms_norm(y_lru, p['g_out_lru'])], axis=-1) @ p['w_out']
    h = h + gate1 * mixed
    xn2 = rms_norm(h, p['g_ffn']) * (1.0 + scale2) + shift2
    ff = moe_ffn(xn2.reshape(bsz * L, D_MODEL), p['router_w'], p['router_b'], p['moe_w_gu'],
                 p['moe_b_gu'], p['moe_w_down'], p['moe_b_down']).reshape(bsz, L, D_MODEL)
    h = h + gate2 * ff
    return h, s5_re, s5_im, lru_h, conv


def trunk(x, c, s5_re, s5_im, lru_h, conv, layers, g_final):
    h = x.astype(jnp.float32)
    n_re, n_im, n_h, n_cv = [], [], [], []
    for l in range(DEPTH):
        h, sr, si, sh, sc = decoder_layer(h, c, s5_re[l].astype(jnp.float32), s5_im[l].astype(jnp.float32),
                                          lru_h[l].astype(jnp.float32), conv[l], layers[l])
        n_re.append(sr.astype(s5_re.dtype))
        n_im.append(si.astype(s5_im.dtype))
        n_h.append(sh.astype(lru_h.dtype))
        n_cv.append(sc.astype(conv.dtype))
    y = rms_norm(h, g_final).astype(x.dtype)
    return y, jnp.stack(n_re), jnp.stack(n_im), jnp.stack(n_h), jnp.stack(n_cv)


def setup_inputs(seed: int = 0) -> dict:
    key = jax.random.key(seed)
    ks = iter(jax.random.split(key, 64))
    f32 = jnp.float32

    def nrm(shape, scale):
        return scale * jax.random.normal(next(ks), shape, f32)

    def gain(n):
        return 1.0 + nrm((DEPTH, n), 0.02)

    inp = {}
    inp['x_prompt'] = nrm((BATCH, SEQ, D_MODEL), 1.0)
    inp['x_sample'] = nrm((DEC_BATCH, DEC_SEQ, D_MODEL), 1.0)
    inp['state_s5_re'] = nrm((DEPTH, DEC_BATCH, S5_GROUPS, S5_STATE), 0.1)
    inp['state_s5_im'] = nrm((DEPTH, DEC_BATCH, S5_GROUPS, S5_STATE), 0.1)
    inp['state_lru_h'] = nrm((DEPTH, DEC_BATCH, D_LRU), 0.5)
    inp['state_conv'] = nrm((DEPTH, DEC_BATCH, CONV_WIDTH - 1, D_LRU), 1.0)
    inp['c_prompt'] = nrm((BATCH, D_MODEL), 1.0)
    inp['c_sample'] = nrm((DEC_BATCH, D_MODEL), 1.0)
    inp['w_ada'] = nrm((DEPTH, D_MODEL, N_MOD * D_MODEL), 0.5 * D_MODEL ** -0.5)
    inp['b_ada'] = nrm((DEPTH, N_MOD * D_MODEL), 0.02)
    inp['g_mix'] = gain(D_MODEL)
    inp['w_in'] = nrm((DEPTH, D_MODEL, D_S5 + 2 * D_LRU), D_MODEL ** -0.5)
    inp['s5_a_re'] = -0.5 + nrm((DEPTH, S5_GROUPS, S5_STATE), 0.01)
    inp['s5_a_im'] = math.pi * jnp.arange(S5_STATE, dtype=f32) + nrm((DEPTH, S5_GROUPS, S5_STATE), 0.01)
    inp['s5_log_dt'] = jax.random.uniform(next(ks), (DEPTH, S5_GROUPS), f32, math.log(1e-3), math.log(1e-1))
    inp['s5_b_re'] = nrm((DEPTH, S5_GROUPS, S5_STATE, S5_GROUP_CH), (2.0 * S5_GROUP_CH) ** -0.5)
    inp['s5_b_im'] = nrm((DEPTH, S5_GROUPS, S5_STATE, S5_GROUP_CH), (2.0 * S5_GROUP_CH) ** -0.5)
    inp['s5_c_re'] = nrm((DEPTH, S5_GROUPS, S5_GROUP_CH, S5_STATE), 0.5 ** 0.5)
    inp['s5_c_im'] = nrm((DEPTH, S5_GROUPS, S5_GROUP_CH, S5_STATE), 0.5 ** 0.5)
    inp['s5_d'] = nrm((DEPTH, D_S5), 0.5)
    inp['s5_w_glu'] = nrm((DEPTH, D_S5, D_S5), D_S5 ** -0.5)
    inp['s5_b_glu'] = nrm((DEPTH, D_S5), 0.02)
    inp['lru_conv_w'] = nrm((DEPTH, CONV_WIDTH, D_LRU), CONV_WIDTH ** -0.5)
    inp['lru_conv_b'] = nrm((DEPTH, D_LRU), 0.02)
    inp['lru_w_a'] = nrm((DEPTH, LRU_HEADS, LRU_HEAD_DIM, LRU_HEAD_DIM), LRU_HEAD_DIM ** -0.5)
    inp['lru_b_a'] = nrm((DEPTH, D_LRU), 0.02)
    inp['lru_w_x'] = nrm((DEPTH, LRU_HEADS, LRU_HEAD_DIM, LRU_HEAD_DIM), LRU_HEAD_DIM ** -0.5)
    inp['lru_b_x'] = nrm((DEPTH, D_LRU), 0.02)
    a_c = jax.random.uniform(next(ks), (DEPTH, D_LRU), f32, 0.9, 0.999) ** (1.0 / LRU_C)
    inp['lru_lambda'] = jnp.log(a_c) - jnp.log1p(-a_c)
    inp['g_out_s5'] = gain(D_S5)
    inp['g_out_lru'] = gain(D_LRU)
    inp['w_out'] = nrm((DEPTH, D_MIX, D_MODEL), D_MIX ** -0.5)
    inp['g_ffn'] = gain(D_MODEL)
    inp['router_w'] = nrm((DEPTH, D_MODEL, N_EXPERTS), D_MODEL ** -0.5)
    inp['router_b'] = nrm((DEPTH, N_EXPERTS), 0.01)
    inp['moe_w_gu'] = nrm((DEPTH, N_EXPERTS, D_MODEL, 2 * D_FF), D_MODEL ** -0.5)
    inp['moe_b_gu'] = nrm((DEPTH, N_EXPERTS, 2 * D_FF), 0.02)
    inp['moe_w_down'] = nrm((DEPTH, N_EXPERTS, D_FF, D_MODEL), D_FF ** -0.5)
    inp['moe_b_down'] = nrm((DEPTH, N_EXPERTS, D_MODEL), 0.02)
    inp['g_final'] = 1.0 + nrm((D_MODEL,), 0.02)
    return inp


def reference(x_prompt, x_sample, state_s5_re, state_s5_im, state_lru_h, state_conv, c_prompt, c_sample,
              w_ada, b_ada, g_mix, w_in, s5_a_re, s5_a_im, s5_log_dt, s5_b_re, s5_b_im, s5_c_re, s5_c_im,
              s5_d, s5_w_glu, s5_b_glu, lru_conv_w, lru_conv_b, lru_w_a, lru_b_a, lru_w_x, lru_b_x,
              lru_lambda, g_out_s5, g_out_lru, w_out, g_ffn, router_w, router_b, moe_w_gu, moe_b_gu,
              moe_w_down, moe_b_down, g_final):
    layers = [dict(w_ada=w_ada[l], b_ada=b_ada[l], g_mix=g_mix[l], w_in=w_in[l],
                   s5_a_re=s5_a_re[l], s5_a_im=s5_a_im[l], s5_log_dt=s5_log_dt[l],
                   s5_b_re=s5_b_re[l], s5_b_im=s5_b_im[l], s5_c_re=s5_c_re[l], s5_c_im=s5_c_im[l],
                   s5_d=s5_d[l], s5_w_glu=s5_w_glu[l], s5_b_glu=s5_b_glu[l],
                   lru_conv_w=lru_conv_w[l], lru_conv_b=lru_conv_b[l], lru_w_a=lru_w_a[l], lru_b_a=lru_b_a[l],
                   lru_w_x=lru_w_x[l], lru_b_x=lru_b_x[l], lru_lambda=lru_lambda[l],
                   g_out_s5=g_out_s5[l], g_out_lru=g_out_lru[l], w_out=w_out[l], g_ffn=g_ffn[l],
                   router_w=router_w[l], router_b=router_b[l], moe_w_gu=moe_w_gu[l], moe_b_gu=moe_b_gu[l],
                   moe_w_down=moe_w_down[l], moe_b_down=moe_b_down[l])
              for l in range(DEPTH)]
    bp = x_prompt.shape[0]
    z_s5 = jnp.zeros((DEPTH, bp, S5_GROUPS, S5_STATE), state_s5_re.dtype)
    z_h = jnp.zeros((DEPTH, bp, D_LRU), state_lru_h.dtype)
    z_cv = jnp.zeros((DEPTH, bp, CONV_WIDTH - 1, D_LRU), state_conv.dtype)
    y_prompt, s5_re_p, s5_im_p, lru_h_p, conv_p = trunk(x_prompt, c_prompt, z_s5, z_s5, z_h, z_cv, layers, g_final)
    y_sample, s5_re_s, s5_im_s, lru_h_s, conv_s = trunk(x_sample, c_sample, state_s5_re, state_s5_im,
                                                        state_lru_h, state_conv, layers, g_final)
    return (y_prompt, y_sample, s5_re_p, s5_im_p, lru_h_p, conv_p, s5_re_s, s5_im_s, lru_h_s, conv_s)
```

```python
import functools

import jax
import jax.numpy as jnp
from jax import lax
from jax.experimental import pallas as pl
from jax.experimental.pallas import tpu as pltpu

D_MODEL = 1024
D_S5 = 512
D_LRU = 512
S5_GROUPS = 32
S5_GROUP_CH = 16
S5_STATE = 64
S5_COLS = 2 * S5_GROUPS * S5_STATE
S5_BLOCKS = 4
LRU_HEADS = 8
LRU_HEAD_DIM = 64
LRU_C = 8.0
CONV_WIDTH = 4
N_EXPERTS = 32
TOP_K = 4
D_FF = 1024
SWIGLU_LIMIT = 7.0
SWIGLU_ALPHA = 1.702
N_MOD = 6
EPS = 1e-6

ROWS = 512
MOE_TM = 256
SUBLANES = 8
VMEM_LIMIT = 48 * 1024 * 1024

BF16 = jnp.bfloat16
F32 = jnp.float32


def _params(*sem):
    return pltpu.CompilerParams(dimension_semantics=sem, vmem_limit_bytes=VMEM_LIMIT)


def _rms(x, g):
    return x * lax.rsqrt(jnp.mean(x * x, axis=-1, keepdims=True) + EPS) * g


def _gelu(x):
    return 0.5 * x * (1.0 + lax.erf(x * (2.0 ** -0.5)))


def _expm1(x):
    u = jnp.exp(x)
    d = u - 1.0
    return jnp.where(d == 0.0, x, jnp.where(d == -1.0, -1.0, d * x / jnp.log(u)))


def _mod_body(c_ref, w_ref, b_ref, o_ref):
    c = c_ref[...]
    s = (c * jax.nn.sigmoid(c)).astype(BF16)
    o_ref[...] = jnp.dot(s, w_ref[...].astype(BF16), preferred_element_type=F32) + b_ref[...]


def _adaln(c, w_ada, b_ada):
    m = c.shape[0]
    return pl.pallas_call(
        _mod_body,
        grid=(N_MOD,),
        in_specs=[pl.BlockSpec((m, D_MODEL), lambda j: (0, 0)),
                  pl.BlockSpec((D_MODEL, D_MODEL), lambda j: (0, j)),
                  pl.BlockSpec((1, D_MODEL), lambda j: (0, j))],
        out_specs=pl.BlockSpec((m, D_MODEL), lambda j: (0, j)),
        out_shape=jax.ShapeDtypeStruct((m, N_MOD * D_MODEL), F32),
        compiler_params=_params("arbitrary"),
        name="adaln",
    )(c, w_ada, b_ada)


def _in_body(x_ref, sc_ref, sh_ref, g_ref, w_ref, u_ref, xr_ref, gr_ref):
    x = x_ref[...]
    tc, bt, _ = x.shape
    xn = _rms(x, g_ref[...]) * (1.0 + sc_ref[...]) + sh_ref[...]
    p = jnp.dot(xn.reshape(tc * bt, D_MODEL).astype(BF16), w_ref[...], preferred_element_type=F32)
    u_ref[...] = p[:, :D_S5]
    xr_ref[...] = p[:, D_S5:D_S5 + D_LRU]
    gr_ref[...] = p[:, D_S5 + D_LRU:]


def _in_proj(x3, scale1, shift1, g_mix, w_in_bf):
    length, bt, _ = x3.shape
    tc = ROWS // bt
    t_rows = length * bt
    row_spec = pl.BlockSpec((ROWS, D_S5), lambda i: (i, 0))
    full = lambda shape: pl.BlockSpec(shape, lambda i: (0,) * len(shape))
    return pl.pallas_call(
        _in_body,
        grid=(length // tc,),
        in_specs=[pl.BlockSpec((tc, bt, D_MODEL), lambda i: (i, 0, 0)),
                  full((bt, D_MODEL)), full((bt, D_MODEL)), full((1, D_MODEL)),
                  full((D_MODEL, D_S5 + 2 * D_LRU))],
        out_specs=[row_spec, row_spec, row_spec],
        out_shape=[jax.ShapeDtypeStruct((t_rows, D_S5), F32)] * 3,
        compiler_params=_params("arbitrary"),
        name="in_proj",
    )(x3, scale1, shift1, g_mix, w_in_bf)


def _s5_prep_body(are_ref, aim_ref, ldt_ref, bre_ref, bim_ref, abre_ref, abim_ref, bbre_ref, bbim_ref):
    a_re = are_ref[...]
    a_im = aim_ref[...]
    dt = jnp.exp(ldt_ref[...])
    mag = jnp.exp(dt * a_re)
    ang = dt * a_im
    ab_re = mag * jnp.cos(ang)
    ab_im = mag * jnp.sin(ang)
    den = a_re * a_re + a_im * a_im
    q_re = ((ab_re - 1.0) * a_re + ab_im * a_im) / den
    q_im = (ab_im * a_re - (ab_re - 1.0) * a_im) / den
    abre_ref[...] = ab_re
    abim_ref[...] = ab_im
    b_re = bre_ref[...]
    b_im = bim_ref[...]
    bbre_ref[...] = q_re[:, None, :] * b_re - q_im[:, None, :] * b_im
    bbim_ref[...] = q_re[:, None, :] * b_im + q_im[:, None, :] * b_re


def _s5_prep(a_re, a_im, log_dt, b_re_t, b_im_t):
    gn = jax.ShapeDtypeStruct((S5_GROUPS, S5_STATE), F32)
    gjn = jax.ShapeDtypeStruct((S5_GROUPS, S5_GROUP_CH, S5_STATE), F32)
    return pl.pallas_call(_s5_prep_body, out_shape=[gn, gn, gjn, gjn], name="s5_prep")(
        a_re, a_im, log_dt, b_re_t, b_im_t)


def _s5_body(tc, bt, u_ref, s0_ref, ar_ref, ai_ref, bb_ref, cc_ref, d_ref, wg_ref, bg_ref, go_ref,
             y_ref, sout_ref, bu_ref, st_ref):
    half = S5_COLS // S5_BLOCKS // 2

    @pl.when(pl.program_id(0) == 0)
    def _():
        st_ref[...] = s0_ref[...]

    u = u_ref[...]
    ub = u.astype(BF16)
    for j in range(S5_BLOCKS):
        bu_ref[:, 2 * half * j:2 * half * (j + 1)] = jnp.dot(
            ub[:, 128 * j:128 * (j + 1)], bb_ref[j], preferred_element_type=F32)

    def sb_loop(sb, carry):
        base = pl.multiple_of(sb * SUBLANES, SUBLANES)
        h0 = st_ref[pl.ds(base, SUBLANES), :]
        hr0 = tuple(h0[:, 2 * half * j:2 * half * j + half] for j in range(S5_BLOCKS))
        hi0 = tuple(h0[:, 2 * half * j + half:2 * half * (j + 1)] for j in range(S5_BLOCKS))

        def t_loop(t, h):
            hr, hi = h
            r0 = pl.multiple_of(t * bt + base, SUBLANES)
            nr, ni = [], []
            for j in range(S5_BLOCKS):
                ar = ar_ref[:, half * j:half * (j + 1)]
                ai = ai_ref[:, half * j:half * (j + 1)]
                re_cols = pl.ds(2 * half * j, half)
                im_cols = pl.ds(2 * half * j + half, half)
                b_re = bu_ref[pl.ds(r0, SUBLANES), re_cols]
                b_im = bu_ref[pl.ds(r0, SUBLANES), im_cols]
                h_re = ar * hr[j] - ai * hi[j] + b_re
                h_im = ar * hi[j] + ai * hr[j] + b_im
                bu_ref[pl.ds(r0, SUBLANES), re_cols] = h_re
                bu_ref[pl.ds(r0, SUBLANES), im_cols] = h_im
                nr.append(h_re)
                ni.append(h_im)
            return tuple(nr), tuple(ni)

        hr, hi = lax.fori_loop(0, tc, t_loop, (hr0, hi0))
        for j in range(S5_BLOCKS):
            st_ref[pl.ds(base, SUBLANES), pl.ds(2 * half * j, half)] = hr[j]
            st_ref[pl.ds(base, SUBLANES), pl.ds(2 * half * j + half, half)] = hi[j]
        return carry

    lax.fori_loop(0, bt // SUBLANES, sb_loop, 0)
    sout_ref[...] = st_ref[...]

    y = jnp.concatenate(
        [jnp.dot(bu_ref[:, 2 * half * j:2 * half * (j + 1)].astype(BF16), cc_ref[j], preferred_element_type=F32)
         for j in range(S5_BLOCKS)], axis=-1)
    y = y + d_ref[...] * u
    g = _gelu(y)
    z = jnp.dot(g.astype(BF16), wg_ref[...], preferred_element_type=F32) + bg_ref[...]
    out = g * jax.nn.sigmoid(z)
    y_ref[...] = _rms(out, go_ref[...]).astype(BF16)


def _s5_mixer(u, s0, ar8, ai8, bb, cc, d, w_glu_bf, b_glu, g_out, bt):
    t_rows = u.shape[0]
    tc = ROWS // bt
    full = lambda shape: pl.BlockSpec(shape, lambda i: (0,) * len(shape))
    return pl.pallas_call(
        functools.partial(_s5_body, tc, bt),
        grid=(t_rows // ROWS,),
        in_specs=[pl.BlockSpec((ROWS, D_S5), lambda i: (i, 0)),
                  full((bt, S5_COLS)), full(ar8.shape), full(ai8.shape), full(bb.shape), full(cc.shape),
                  full((1, D_S5)), full((D_S5, D_S5)), full((1, D_S5)), full((1, D_S5))],
        out_specs=[pl.BlockSpec((ROWS, D_S5), lambda i: (i, 0)), full((bt, S5_COLS))],
        out_shape=[jax.ShapeDtypeStruct((t_rows, D_S5), BF16), jax.ShapeDtypeStruct((bt, S5_COLS), F32)],
        scratch_shapes=[pltpu.VMEM((ROWS, S5_COLS), F32), pltpu.VMEM((bt, S5_COLS), F32)],
        compiler_params=_params("arbitrary"),
        name="s5_mixer",
    )(u, s0, ar8, ai8, bb, cc, d, w_glu_bf, b_glu, g_out)


def _lru_body(tc, bt, xr_ref, gr_ref, h0_ref, cv0_ref, cw_ref, cb_ref, wa_ref, ba_ref, wx_ref, bx_ref,
              lam_ref, go_ref, y_ref, hout_ref, cvout_ref, xp_ref, a_ref, b_ref, h_ref):
    halo = (CONV_WIDTH - 1) * bt

    @pl.when(pl.program_id(0) == 0)
    def _():
        xp_ref[0:halo, :] = cv0_ref[...]
        h_ref[...] = h0_ref[...]

    xp_ref[halo:, :] = xr_ref[...]
    xc = cb_ref[...] + sum(xp_ref[k * bt:k * bt + ROWS, :] * cw_ref[k:k + 1, :] for k in range(CONV_WIDTH))
    xcb = xc.astype(BF16)
    r = jax.nn.sigmoid(jnp.dot(xcb, wa_ref[...], preferred_element_type=F32) + ba_ref[...])
    i = jax.nn.sigmoid(jnp.dot(xcb, wx_ref[...], preferred_element_type=F32) + bx_ref[...])
    lam = lam_ref[...]
    softplus_neg_lam = jnp.maximum(-lam, 0.0) + jnp.log1p(jnp.exp(-jnp.abs(lam)))
    log_a = -LRU_C * r * softplus_neg_lam
    a_ref[...] = jnp.exp(log_a)
    b_ref[...] = jnp.sqrt(-_expm1(2.0 * log_a)) * (i * xc)

    def sb_loop(sb, carry):
        base = pl.multiple_of(sb * SUBLANES, SUBLANES)

        def t_loop(t, h):
            r0 = pl.multiple_of(t * bt + base, SUBLANES)
            h = a_ref[pl.ds(r0, SUBLANES), :] * h + b_ref[pl.ds(r0, SUBLANES), :]
            b_ref[pl.ds(r0, SUBLANES), :] = h
            return h

        h_ref[pl.ds(base, SUBLANES), :] = lax.fori_loop(0, tc, t_loop, h_ref[pl.ds(base, SUBLANES), :])
        return carry

    lax.fori_loop(0, bt // SUBLANES, sb_loop, 0)

    y = b_ref[...] * _gelu(gr_ref[...])
    y_ref[...] = _rms(y, go_ref[...]).astype(BF16)
    tail = xp_ref[ROWS:ROWS + halo, :]
    xp_ref[0:halo, :] = tail
    cvout_ref[...] = tail
    hout_ref[...] = h_ref[...]


def _lru_mixer(xr, gr, h0, conv0_tm, conv_w, conv_b, wa_bd, b_a, wx_bd, b_x, lam, g_out, bt):
    t_rows = xr.shape[0]
    tc = ROWS // bt
    halo = (CONV_WIDTH - 1) * bt
    full = lambda shape: pl.BlockSpec(shape, lambda i: (0,) * len(shape))
    row_spec = pl.BlockSpec((ROWS, D_LRU), lambda i: (i, 0))
    vec = full((1, D_LRU))
    return pl.pallas_call(
        functools.partial(_lru_body, tc, bt),
        grid=(t_rows // ROWS,),
        in_specs=[row_spec, row_spec, full((bt, D_LRU)), full((halo, D_LRU)), full((CONV_WIDTH, D_LRU)), vec,
                  full((D_LRU, D_LRU)), vec, full((D_LRU, D_LRU)), vec, vec, vec],
        out_specs=[row_spec, full((bt, D_LRU)), full((halo, D_LRU))],
        out_shape=[jax.ShapeDtypeStruct((t_rows, D_LRU), BF16), jax.ShapeDtypeStruct((bt, D_LRU), F32),
                   jax.ShapeDtypeStruct((halo, D_LRU), F32)],
        scratch_shapes=[pltpu.VMEM((ROWS + halo, D_LRU), F32), pltpu.VMEM((ROWS, D_LRU), F32),
                        pltpu.VMEM((ROWS, D_LRU), F32), pltpu.VMEM((bt, D_LRU), F32)],
        compiler_params=_params("arbitrary"),
        name="lru_mixer",
    )(xr, gr, h0, conv0_tm, conv_w, conv_b, wa_bd, b_a, wx_bd, b_x, lam, g_out)


def _out_body(ys_ref, yl_ref, x_ref, g1_ref, sc_ref, sh_ref, wo_ref, gf_ref, rwh_ref, rwl_ref, rb_ref,
              h_ref, xn_ref, meta_ref, gates_ref, cnt_ref, run_ref):
    tc, bt, _ = x_ref.shape

    @pl.when(pl.program_id(0) == 0)
    def _():
        run_ref[...] = jnp.zeros_like(run_ref)

    mixed = (jnp.dot(ys_ref[...], wo_ref[0], preferred_element_type=F32)
             + jnp.dot(yl_ref[...], wo_ref[1], preferred_element_type=F32))
    h = x_ref[...] + g1_ref[...] * mixed.reshape(tc, bt, D_MODEL)
    h_ref[...] = h.reshape(ROWS, D_MODEL)
    xn = (_rms(h, gf_ref[...]) * (1.0 + sc_ref[...]) + sh_ref[...]).reshape(ROWS, D_MODEL)
    xn_ref[...] = xn

    x_hi = xn.astype(BF16)
    x_lo = (xn - x_hi.astype(F32)).astype(BF16)
    nt = (((1,), (1,)), ((), ()))
    logits = (lax.dot_general(rwh_ref[...], x_hi, nt, preferred_element_type=F32)
              + lax.dot_general(rwh_ref[...], x_lo, nt, preferred_element_type=F32)
              + lax.dot_general(rwl_ref[...], x_hi, nt, preferred_element_type=F32)) + rb_ref[...]

    e_iota = lax.broadcasted_iota(jnp.int32, (N_EXPERTS, ROWS), 0).astype(F32)
    work = logits
    sels, vals, idxs = [], [], []
    for _ in range(TOP_K):
        m = jnp.max(work, axis=0, keepdims=True)
        idx = jnp.min(jnp.where(work == m, e_iota, float(N_EXPERTS)), axis=0, keepdims=True)
        sel = e_iota == idx
        work = jnp.where(sel, -jnp.inf, work)
        sels.append(sel)
        vals.append(m)
        idxs.append(idx)
    exps = [jnp.exp(v - vals[0]) for v in vals]
    denom = exps[0] + exps[1] + exps[2] + exps[3]
    gates = [e / denom for e in exps]

    onehot = sels[0] | sels[1] | sels[2] | sels[3]
    rr = lax.broadcasted_iota(jnp.int32, (ROWS, ROWS), 0)
    cc = lax.broadcasted_iota(jnp.int32, (ROWS, ROWS), 1)
    before = (rr < cc).astype(BF16)
    prefix = jnp.dot(onehot.astype(BF16), before, preferred_element_type=F32) + run_ref[...]
    ranks = [jnp.sum(jnp.where(s, prefix, 0.0), axis=0, keepdims=True) for s in sels]
    run_ref[...] = run_ref[...] + jnp.sum(onehot.astype(F32), axis=1, keepdims=True)

    s_iota = lax.broadcasted_iota(jnp.int32, (2 * TOP_K, ROWS), 0)
    meta = jnp.zeros((2 * TOP_K, ROWS), F32)
    gmat = jnp.zeros((2 * TOP_K, ROWS), F32)
    for k in range(TOP_K):
        meta = jnp.where(s_iota == k, idxs[k], meta)
        meta = jnp.where(s_iota == TOP_K + k, ranks[k], meta)
        gmat = jnp.where(s_iota == k, gates[k], gmat)
    meta_ref[...] = meta.astype(jnp.int32)
    gates_ref[...] = gmat
    cnt_ref[...] = jnp.broadcast_to(run_ref[...], cnt_ref.shape)


def _out_proj(ys5, ylru, x3, gate1, scale2, shift2, w_out_bf, g_ffn, rw_hi, rw_lo, router_b):
    length, bt, _ = x3.shape
    tc = ROWS // bt
    t_rows = length * bt
    full = lambda shape: pl.BlockSpec(shape, lambda i: (0,) * len(shape))
    half_spec = pl.BlockSpec((ROWS, D_S5), lambda i: (i, 0))
    row_spec = pl.BlockSpec((ROWS, D_MODEL), lambda i: (i, 0))
    col_spec = pl.BlockSpec((2 * TOP_K, ROWS), lambda i: (0, i))
    return pl.pallas_call(
        _out_body,
        grid=(t_rows // ROWS,),
        in_specs=[half_spec, half_spec, pl.BlockSpec((tc, bt, D_MODEL), lambda i: (i, 0, 0)),
                  full((bt, D_MODEL)), full((bt, D_MODEL)), full((bt, D_MODEL)),
                  full((2, D_S5, D_MODEL)), full((1, D_MODEL)),
                  full((N_EXPERTS, D_MODEL)), full((N_EXPERTS, D_MODEL)), full((N_EXPERTS, 1))],
        out_specs=[row_spec, row_spec, col_spec, col_spec, full((N_EXPERTS, 128))],
        out_shape=[jax.ShapeDtypeStruct((t_rows, D_MODEL), F32), jax.ShapeDtypeStruct((t_rows, D_MODEL), F32),
                   jax.ShapeDtypeStruct((2 * TOP_K, t_rows), jnp.int32),
                   jax.ShapeDtypeStruct((2 * TOP_K, t_rows), F32),
                   jax.ShapeDtypeStruct((N_EXPERTS, 128), F32)],
        scratch_shapes=[pltpu.VMEM((N_EXPERTS, 1), F32)],
        compiler_params=_params("arbitrary"),
        name="out_proj_router",
    )(ys5, ylru, x3, gate1, scale2, shift2, w_out_bf, g_ffn, rw_hi, rw_lo, router_b)


def _moe_body(be_ref, nv_ref, nb_ref, tokc_ref, tokn_ref, dst_ref, x_hbm, wgu_ref, bgu_ref, wd_ref, bd_ref, ys_hbm,
              lhs, obuf, wgu_bf, wd_bf, gsem, ssem):
    i = pl.program_id(0)
    nb = nb_ref[0]
    slot = i % 2

    def start_gather(tok_ref, s):
        def issue(r, c):
            pltpu.make_async_copy(x_hbm.at[pl.ds(tok_ref[0, 0, r], 1)], lhs.at[s, pl.ds(r, 1)], gsem.at[s]).start()
            return c
        lax.fori_loop(0, MOE_TM, issue, 0)

    def wait_gather(s):
        pltpu.make_async_copy(x_hbm.at[pl.ds(0, MOE_TM)], lhs.at[s], gsem.at[s]).wait()

    def start_scatter(s, nv):
        def issue(r, c):
            pltpu.make_async_copy(obuf.at[s, pl.ds(r, 1)], ys_hbm.at[pl.ds(dst_ref[0, 0, r], 1)], ssem.at[s]).start()
            return c
        lax.fori_loop(0, nv, issue, 0)

    def wait_scatter(s, nv):
        bit = MOE_TM
        while bit:
            @pl.when((nv & bit) != 0)
            def _(bit=bit):
                pltpu.make_async_copy(obuf.at[s, pl.ds(0, bit)], ys_hbm.at[pl.ds(0, bit)], ssem.at[s]).wait()
            bit //= 2

    @pl.when(jnp.logical_and(i == 0, nb > 0))
    def _():
        start_gather(tokc_ref, 0)

    @pl.when(i < nb)
    def _():
        wait_gather(slot)

        @pl.when(i + 1 < nb)
        def _():
            start_gather(tokn_ref, 1 - slot)

        prev = be_ref[jnp.maximum(i - 1, 0)]

        @pl.when(jnp.logical_or(i == 0, be_ref[i] != prev))
        def _():
            wgu_bf[...] = wgu_ref[0].astype(BF16)
            wd_bf[...] = wd_ref[0].astype(BF16)

        @pl.when(i >= 2)
        def _():
            wait_scatter(slot, nv_ref[jnp.maximum(i - 2, 0)])

        x = lhs[slot].astype(BF16)
        hg = jnp.dot(x, wgu_bf[...], preferred_element_type=F32) + bgu_ref[0]
        gate = jnp.minimum(hg[:, :D_FF], SWIGLU_LIMIT)
        up = jnp.clip(hg[:, D_FF:], -SWIGLU_LIMIT, SWIGLU_LIMIT)
        act = (up + 1.0) * (gate * jax.nn.sigmoid(SWIGLU_ALPHA * gate))
        obuf[slot] = jnp.dot(act.astype(BF16), wd_bf[...], preferred_element_type=F32) + bd_ref[0]
        start_scatter(slot, nv_ref[i])

        @pl.when(i == nb - 1)
        def _():
            wait_scatter(slot, nv_ref[i])

            @pl.when(i >= 1)
            def _():
                wait_scatter(1 - slot, nv_ref[jnp.maximum(i - 1, 0)])


def _moe(xn, row_token, row_dst, block_expert, n_valid, nb_used, w_gu, b_gu, w_down, b_down, out_rows):
    n_blocks = row_token.shape[0]
    smem_spec = lambda fn: pl.BlockSpec((1, 1, MOE_TM), fn, memory_space=pltpu.SMEM)
    grid_spec = pltpu.PrefetchScalarGridSpec(
        num_scalar_prefetch=3,
        grid=(n_blocks,),
        in_specs=[smem_spec(lambda i, be, nv, nb: (i, 0, 0)),
                  smem_spec(lambda i, be, nv, nb: (jnp.minimum(i + 1, n_blocks - 1), 0, 0)),
                  smem_spec(lambda i, be, nv, nb: (i, 0, 0)),
                  pl.BlockSpec(memory_space=pl.ANY),
                  pl.BlockSpec((1, D_MODEL, 2 * D_FF), lambda i, be, nv, nb: (be[i], 0, 0)),
                  pl.BlockSpec((1, 1, 2 * D_FF), lambda i, be, nv, nb: (be[i], 0, 0)),
                  pl.BlockSpec((1, D_FF, D_MODEL), lambda i, be, nv, nb: (be[i], 0, 0)),
                  pl.BlockSpec((1, 1, D_MODEL), lambda i, be, nv, nb: (be[i], 0, 0))],
        out_specs=pl.BlockSpec(memory_space=pl.ANY),
        scratch_shapes=[pltpu.VMEM((2, MOE_TM, D_MODEL), F32), pltpu.VMEM((2, MOE_TM, D_MODEL), F32),
                        pltpu.VMEM((D_MODEL, 2 * D_FF), BF16), pltpu.VMEM((D_FF, D_MODEL), BF16),
                        pltpu.SemaphoreType.DMA((2,)), pltpu.SemaphoreType.DMA((2,))],
    )
    return pl.pallas_call(
        _moe_body,
        grid_spec=grid_spec,
        out_shape=jax.ShapeDtypeStruct((out_rows, D_MODEL), F32),
        compiler_params=_params("arbitrary"),
        name="moe_experts",
    )(block_expert, n_valid, nb_used, row_token, row_token, row_dst, xn, w_gu, b_gu, w_down, b_down)


def _fin_body(h_ref, y0_ref, y1_ref, y2_ref, y3_ref, gk_ref, g2_ref, gf_ref, o_ref):
    tc, bt, _ = o_ref.shape
    gk = gk_ref[...]
    ff = (y0_ref[...] * gk[:, 0:1] + y1_ref[...] * gk[:, 1:2] + y2_ref[...] * gk[:, 2:3] + y3_ref[...] * gk[:, 3:4])
    h = h_ref[...].reshape(tc, bt, D_MODEL) + g2_ref[...] * ff.reshape(tc, bt, D_MODEL)
    o_ref[...] = _rms(h, gf_ref[...])


def _final(h, ys4, gates_tk, gate2, g_final, length, bt):
    tc = ROWS // bt
    t_rows = length * bt
    steps = t_rows // ROWS
    full = lambda shape: pl.BlockSpec(shape, lambda i: (0,) * len(shape))
    row_spec = pl.BlockSpec((ROWS, D_MODEL), lambda i: (i, 0))
    slot_spec = lambda k: pl.BlockSpec((ROWS, D_MODEL), lambda i: (k * steps + i, 0))
    return pl.pallas_call(
        _fin_body,
        grid=(steps,),
        in_specs=[row_spec, slot_spec(0), slot_spec(1), slot_spec(2), slot_spec(3),
                  pl.BlockSpec((ROWS, TOP_K), lambda i: (i, 0)), full((bt, D_MODEL)), full((1, D_MODEL))],
        out_specs=pl.BlockSpec((tc, bt, D_MODEL), lambda i: (i, 0, 0)),
        out_shape=jax.ShapeDtypeStruct((length, bt, D_MODEL), F32),
        compiler_params=_params("arbitrary"),
        name="combine_final",
    )(h, ys4, ys4, ys4, ys4, gates_tk, gate2, g_final)


def _block_diag(w):
    h, i, j = w.shape
    return jnp.einsum('hij,hk->hikj', w, jnp.eye(h, dtype=w.dtype)).reshape(h * i, h * j)


def _s5_cols(re, im):
    b = re.shape[0]
    stack = jnp.stack([re.reshape(b, S5_BLOCKS, -1), im.reshape(b, S5_BLOCKS, -1)], axis=2)
    return stack.reshape(b, S5_COLS)


def _s5_uncols(cols):
    b = cols.shape[0]
    c = cols.reshape(b, S5_BLOCKS, 2, S5_GROUPS // S5_BLOCKS, S5_STATE)
    return (c[:, :, 0].reshape(b, S5_GROUPS, S5_STATE), c[:, :, 1].reshape(b, S5_GROUPS, S5_STATE))


def _route(meta, counts, t_rows):
    idx = meta[:TOP_K]
    rank = meta[TOP_K:]
    counts = counts[:, 0].astype(jnp.int32)
    padded = (counts + MOE_TM - 1) // MOE_TM * MOE_TM
    pend = jnp.cumsum(padded)
    pstart = pend - padded
    dest = (pstart[idx] + rank).reshape(-1)
    n_blocks = (TOP_K * t_rows + MOE_TM - 1) // MOE_TM + N_EXPERTS
    n_rows = n_blocks * MOE_TM
    tok = jnp.tile(jnp.arange(t_rows, dtype=jnp.int32), TOP_K)
    slot_row = jnp.arange(TOP_K * t_rows, dtype=jnp.int32)
    row_token = jnp.zeros((n_rows,), jnp.int32).at[dest].set(tok, unique_indices=True)
    row_dst = jnp.zeros((n_rows,), jnp.int32).at[dest].set(slot_row, unique_indices=True)
    block_row0 = jnp.arange(n_blocks, dtype=jnp.int32) * MOE_TM
    block_expert = jnp.minimum(jnp.searchsorted(pend, block_row0, side='right'), N_EXPERTS - 1).astype(jnp.int32)
    n_valid = jnp.clip(pstart[block_expert] + counts[block_expert] - block_row0, 0, MOE_TM).astype(jnp.int32)
    nb_used = (pend[-1] // MOE_TM).astype(jnp.int32).reshape(1)
    return (row_token.reshape(n_blocks, 1, MOE_TM), row_dst.reshape(n_blocks, 1, MOE_TM), block_expert, n_valid,
            nb_used)


def _trunk(x3, mod, s5_cols0, lru_h0, conv0_tm, p):
    length, bt, _ = x3.shape
    t_rows = length * bt
    shift1, scale1, gate1, shift2, scale2, gate2 = (mod[:, k * D_MODEL:(k + 1) * D_MODEL] for k in range(N_MOD))
    u, xr, gr = _in_proj(x3, scale1, shift1, p['g_mix'], p['w_in'])
    ys5, s5_cols = _s5_mixer(u, s5_cols0, p['ar8'], p['ai8'], p['bb'], p['cc'], p['s5_d'], p['w_glu'],
                             p['b_glu'], p['g_out_s5'], bt)
    ylru, lru_h, conv = _lru_mixer(xr, gr, lru_h0, conv0_tm, p['conv_w'], p['conv_b'], p['wa_bd'], p['b_a'],
                                   p['wx_bd'], p['b_x'], p['lam'], p['g_out_lru'], bt)
    h, xn, meta, gates, counts = _out_proj(ys5, ylru, x3, gate1, scale2, shift2, p['w_out'], p['g_ffn'],
                                           p['rw_hi'], p['rw_lo'], p['router_b'])
    row_token, row_dst, block_expert, n_valid, nb_used = _route(meta, counts, t_rows)
    ys4 = _moe(xn, row_token, row_dst, block_expert, n_valid, nb_used, p['w_gu'], p['b_gu'], p['w_down'],
               p['b_down'], TOP_K * t_rows)
    y = _final(h, ys4, gates[:TOP_K].T, gate2, p['g_final'], length, bt)
    return y, s5_cols, lru_h, conv


def kernel(x_prompt, x_sample, state_s5_re, state_s5_im, state_lru_h, state_conv, c_prompt, c_sample, w_ada, b_ada, g_mix, w_in, s5_a_re, s5_a_im, s5_log_dt, s5_b_re, s5_b_im, s5_c_re, s5_c_im, s5_d, s5_w_glu, s5_b_glu, lru_conv_w, lru_conv_b, lru_w_a, lru_b_a, lru_w_x, lru_b_x, lru_lambda, g_out_s5, g_out_lru, w_out, g_ffn, router_w, router_b, moe_w_gu, moe_b_gu, moe_w_down, moe_b_down, g_final):
    assert w_ada.shape[0] == 1, "one layer"
    bp, lp, _ = x_prompt.shape
    bs, ls, _ = x_sample.shape
    row = lambda v: v.reshape(1, -1)

    ab_re, ab_im, bb_re, bb_im = _s5_prep(s5_a_re[0], s5_a_im[0], s5_log_dt[0].reshape(S5_GROUPS, 1),
                                          jnp.swapaxes(s5_b_re[0], 1, 2), jnp.swapaxes(s5_b_im[0], 1, 2))
    gpb = S5_GROUPS // S5_BLOCKS
    eye = jnp.eye(gpb, dtype=F32)

    def in_blocks(b):
        b = b.reshape(S5_BLOCKS, gpb, S5_GROUP_CH, S5_STATE)
        return jnp.einsum('bgjn,gh->bgjhn', b, eye).reshape(S5_BLOCKS, gpb * S5_GROUP_CH, gpb * S5_STATE)

    def out_blocks(c):
        c = c.reshape(S5_BLOCKS, gpb, S5_GROUP_CH, S5_STATE)
        return jnp.einsum('bgjn,gh->bgnhj', c, eye).reshape(S5_BLOCKS, gpb * S5_STATE, gpb * S5_GROUP_CH)

    rw_t = router_w[0].T
    rw_hi = rw_t.astype(BF16)
    p = dict(
        g_mix=row(g_mix[0]), w_in=w_in[0].astype(BF16),
        ar8=jnp.broadcast_to(ab_re.reshape(1, -1), (SUBLANES, S5_GROUPS * S5_STATE)),
        ai8=jnp.broadcast_to(ab_im.reshape(1, -1), (SUBLANES, S5_GROUPS * S5_STATE)),
        bb=jnp.concatenate([in_blocks(bb_re), in_blocks(bb_im)], axis=-1).astype(BF16),
        cc=jnp.concatenate([out_blocks(s5_c_re[0]), -out_blocks(s5_c_im[0])], axis=1).astype(BF16),
        s5_d=row(s5_d[0]), w_glu=s5_w_glu[0].astype(BF16), b_glu=row(s5_b_glu[0]), g_out_s5=row(g_out_s5[0]),
        conv_w=lru_conv_w[0], conv_b=row(lru_conv_b[0]),
        wa_bd=_block_diag(lru_w_a[0]).astype(BF16), b_a=row(lru_b_a[0]),
        wx_bd=_block_diag(lru_w_x[0]).astype(BF16), b_x=row(lru_b_x[0]),
        lam=row(lru_lambda[0]), g_out_lru=row(g_out_lru[0]),
        w_out=w_out[0].astype(BF16).reshape(2, D_S5, D_MODEL), g_ffn=row(g_ffn[0]),
        rw_hi=rw_hi, rw_lo=(rw_t - rw_hi.astype(F32)).astype(BF16), router_b=router_b[0].reshape(N_EXPERTS, 1),
        w_gu=moe_w_gu[0], b_gu=moe_b_gu[0].reshape(N_EXPERTS, 1, 2 * D_FF),
        w_down=moe_w_down[0], b_down=moe_b_down[0].reshape(N_EXPERTS, 1, D_MODEL),
        g_final=row(g_final),
    )

    mod = _adaln(jnp.concatenate([c_prompt, c_sample], axis=0), w_ada[0], row(b_ada[0]))

    def conv_tm(cv):
        return jnp.swapaxes(cv, 0, 1).reshape(-1, D_LRU)

    def conv_bm(cv, b):
        return jnp.swapaxes(cv.reshape(CONV_WIDTH - 1, b, D_LRU), 0, 1)

    yp, s5p, hp, cvp = _trunk(jnp.swapaxes(x_prompt, 0, 1), mod[:bp],
                              jnp.zeros((bp, S5_COLS), F32), jnp.zeros((bp, D_LRU), F32),
                              jnp.zeros(((CONV_WIDTH - 1) * bp, D_LRU), F32), p)
    ys, s5s, hs, cvs = _trunk(jnp.swapaxes(x_sample, 0, 1), mod[bp:],
                              _s5_cols(state_s5_re[0], state_s5_im[0]), state_lru_h[0],
                              conv_tm(state_conv[0]), p)
    s5p_re, s5p_im = _s5_uncols(s5p)
    s5s_re, s5s_im = _s5_uncols(s5s)
    return (jnp.swapaxes(yp, 0, 1), jnp.swapaxes(ys, 0, 1),
            s5p_re[None], s5p_im[None], hp[None], conv_bm(cvp, bp)[None],
            s5s_re[None], s5s_im[None], hs[None], conv_bm(cvs, bs)[None])
```

```python
import functools

import jax
import jax.numpy as jnp
from jax import lax
from jax.experimental import pallas as pl
from jax.experimental.pallas import tpu as pltpu

D_MODEL = 1024
D_S5 = 512
D_LRU = 512
S5_GROUPS = 32
S5_GROUP_CH = 16
S5_STATE = 64
S5_COLS = 2 * S5_GROUPS * S5_STATE
S5_BLOCKS = 4
LRU_HEADS = 8
LRU_HEAD_DIM = 64
LRU_C = 8.0
CONV_WIDTH = 4
N_EXPERTS = 32
TOP_K = 4
D_FF = 1024
SWIGLU_LIMIT = 7.0
SWIGLU_ALPHA = 1.702
N_MOD = 6
EPS = 1e-6

ROWS = 512
MOE_TM = 256
SEG_ALIGN = 16
SORT_ROWS = 2560
SORT_CHUNK = 512
SUBLANES = 8
LANES = 128
VMEM_LIMIT = 48 * 1024 * 1024

BF16 = jnp.bfloat16
F32 = jnp.float32


def _params(*sem):
    return pltpu.CompilerParams(dimension_semantics=sem, vmem_limit_bytes=VMEM_LIMIT)


def _full(shape):
    return pl.BlockSpec(shape, lambda *_: (0,) * len(shape))


def _rms(x, g):
    return x * lax.rsqrt(jnp.mean(x * x, axis=-1, keepdims=True) + EPS) * g


def _gelu(x):
    return 0.5 * x * (1.0 + lax.erf(x * (2.0 ** -0.5)))


def _expm1(x):
    u = jnp.exp(x)
    d = u - 1.0
    return jnp.where(d == 0.0, x, jnp.where(d == -1.0, -1.0, d * x / jnp.log(u)))


def _pow2_chunks(n, largest, smallest, fn):
    off = 0
    bit = largest
    while bit >= smallest:
        @pl.when((n & bit) != 0)
        def _(off=off, bit=bit):
            fn(off, bit)
        off = off + (n & bit)
        bit //= 2


def _mod_body(c_ref, w_ref, b_ref, o_ref):
    c = c_ref[...]
    s = (c * jax.nn.sigmoid(c)).astype(BF16)
    o_ref[...] = jnp.dot(s, w_ref[...].astype(BF16), preferred_element_type=F32) + b_ref[...]


def _adaln(c, w_ada, b_ada):
    m = c.shape[0]
    return pl.pallas_call(
        _mod_body,
        grid=(N_MOD,),
        in_specs=[pl.BlockSpec((m, D_MODEL), lambda j: (0, 0)),
                  pl.BlockSpec((D_MODEL, D_MODEL), lambda j: (0, j)),
                  pl.BlockSpec((1, D_MODEL), lambda j: (0, j))],
        out_specs=pl.BlockSpec((m, D_MODEL), lambda j: (0, j)),
        out_shape=jax.ShapeDtypeStruct((m, N_MOD * D_MODEL), F32),
        compiler_params=_params("arbitrary"),
        name="adaln",
    )(c, w_ada, b_ada)


def _mod_spec(k, tiles_p):
    return pl.BlockSpec((1, ROWS, D_MODEL), lambda i, *_: (jnp.where(i < tiles_p, 0, 1), 0, k))


def _in_body(x_ref, sc_ref, sh_ref, g_ref, w_ref, u_ref, xr_ref, gr_ref):
    xn = _rms(x_ref[...], g_ref[...]) * (1.0 + sc_ref[0]) + sh_ref[0]
    p = jnp.dot(xn.astype(BF16), w_ref[...], preferred_element_type=F32)
    u_ref[...] = p[:, :D_S5]
    xr_ref[...] = p[:, D_S5:D_S5 + D_LRU]
    gr_ref[...] = p[:, D_S5 + D_LRU:]


def _in_proj(x_rows, modpat, g_mix, w_in_bf, tiles_p):
    t_rows = x_rows.shape[0]
    row_spec = pl.BlockSpec((ROWS, D_S5), lambda i: (i, 0))
    return pl.pallas_call(
        _in_body,
        grid=(t_rows // ROWS,),
        in_specs=[pl.BlockSpec((ROWS, D_MODEL), lambda i: (i, 0)),
                  _mod_spec(1, tiles_p), _mod_spec(0, tiles_p), _full((1, D_MODEL)),
                  _full((D_MODEL, D_S5 + 2 * D_LRU))],
        out_specs=[row_spec, row_spec, row_spec],
        out_shape=[jax.ShapeDtypeStruct((t_rows, D_S5), F32)] * 3,
        compiler_params=_params("arbitrary"),
        name="in_proj",
    )(x_rows, modpat, modpat, g_mix, w_in_bf)


def _s5_prep_body(are_ref, aim_ref, ldt_ref, bre_ref, bim_ref, abre_ref, abim_ref, bbre_ref, bbim_ref):
    a_re = are_ref[...]
    a_im = aim_ref[...]
    dt = jnp.exp(ldt_ref[...])
    mag = jnp.exp(dt * a_re)
    ang = dt * a_im
    ab_re = mag * jnp.cos(ang)
    ab_im = mag * jnp.sin(ang)
    den = a_re * a_re + a_im * a_im
    q_re = ((ab_re - 1.0) * a_re + ab_im * a_im) / den
    q_im = (ab_im * a_re - (ab_re - 1.0) * a_im) / den
    abre_ref[...] = ab_re
    abim_ref[...] = ab_im
    b_re = bre_ref[...]
    b_im = bim_ref[...]
    bbre_ref[...] = q_re[:, None, :] * b_re - q_im[:, None, :] * b_im
    bbim_ref[...] = q_re[:, None, :] * b_im + q_im[:, None, :] * b_re


def _s5_prep(a_re, a_im, log_dt, b_re_t, b_im_t):
    gn = jax.ShapeDtypeStruct((S5_GROUPS, S5_STATE), F32)
    gjn = jax.ShapeDtypeStruct((S5_GROUPS, S5_GROUP_CH, S5_STATE), F32)
    return pl.pallas_call(_s5_prep_body, out_shape=[gn, gn, gjn, gjn], name="s5_prep")(
        a_re, a_im, log_dt, b_re_t, b_im_t)


def _s5_body(tc, bt, u_ref, s0_ref, ar_ref, ai_ref, bb_ref, cc_ref, d_ref, wg_ref, bg_ref, go_ref,
             y_ref, sout_ref, bu_ref, st_ref):
    half = S5_COLS // S5_BLOCKS // 2

    @pl.when(pl.program_id(0) == 0)
    def _():
        st_ref[...] = s0_ref[...]

    u = u_ref[...]
    ub = u.astype(BF16)
    for j in range(S5_BLOCKS):
        bu_ref[:, 2 * half * j:2 * half * (j + 1)] = jnp.dot(
            ub[:, LANES * j:LANES * (j + 1)], bb_ref[j], preferred_element_type=F32)

    def sb_loop(sb, carry):
        base = pl.multiple_of(sb * SUBLANES, SUBLANES)
        h0 = st_ref[pl.ds(base, SUBLANES), :]
        hr0 = tuple(h0[:, 2 * half * j:2 * half * j + half] for j in range(S5_BLOCKS))
        hi0 = tuple(h0[:, 2 * half * j + half:2 * half * (j + 1)] for j in range(S5_BLOCKS))

        def t_loop(t, h):
            hr, hi = h
            r0 = pl.multiple_of(t * bt + base, SUBLANES)
            nr, ni = [], []
            for j in range(S5_BLOCKS):
                ar = ar_ref[:, half * j:half * (j + 1)]
                ai = ai_ref[:, half * j:half * (j + 1)]
                re_cols = pl.ds(2 * half * j, half)
                im_cols = pl.ds(2 * half * j + half, half)
                b_re = bu_ref[pl.ds(r0, SUBLANES), re_cols]
                b_im = bu_ref[pl.ds(r0, SUBLANES), im_cols]
                h_re = ar * hr[j] - ai * hi[j] + b_re
                h_im = ar * hi[j] + ai * hr[j] + b_im
                bu_ref[pl.ds(r0, SUBLANES), re_cols] = h_re
                bu_ref[pl.ds(r0, SUBLANES), im_cols] = h_im
                nr.append(h_re)
                ni.append(h_im)
            return tuple(nr), tuple(ni)

        hr, hi = lax.fori_loop(0, tc, t_loop, (hr0, hi0))
        for j in range(S5_BLOCKS):
            st_ref[pl.ds(base, SUBLANES), pl.ds(2 * half * j, half)] = hr[j]
            st_ref[pl.ds(base, SUBLANES), pl.ds(2 * half * j + half, half)] = hi[j]
        return carry

    lax.fori_loop(0, bt // SUBLANES, sb_loop, 0)
    sout_ref[...] = st_ref[...]

    y = jnp.concatenate(
        [jnp.dot(bu_ref[:, 2 * half * j:2 * half * (j + 1)].astype(BF16), cc_ref[j], preferred_element_type=F32)
         for j in range(S5_BLOCKS)], axis=-1)
    y = y + d_ref[...] * u
    g = _gelu(y)
    z = jnp.dot(g.astype(BF16), wg_ref[...], preferred_element_type=F32) + bg_ref[...]
    out = g * jax.nn.sigmoid(z)
    y_ref[...] = _rms(out, go_ref[...]).astype(BF16)


def _s5_mixer(u, s0, ar8, ai8, bb, cc, d, w_glu_bf, b_glu, g_out, bt, tile0, tiles):
    tc = ROWS // bt
    return pl.pallas_call(
        functools.partial(_s5_body, tc, bt),
        grid=(tiles,),
        in_specs=[pl.BlockSpec((ROWS, D_S5), lambda i: (i + tile0, 0)),
                  _full((bt, S5_COLS)), _full(ar8.shape), _full(ai8.shape), _full(bb.shape), _full(cc.shape),
                  _full((1, D_S5)), _full((D_S5, D_S5)), _full((1, D_S5)), _full((1, D_S5))],
        out_specs=[pl.BlockSpec((ROWS, D_S5), lambda i: (i, 0)), _full((bt, S5_COLS))],
        out_shape=[jax.ShapeDtypeStruct((tiles * ROWS, D_S5), BF16), jax.ShapeDtypeStruct((bt, S5_COLS), F32)],
        scratch_shapes=[pltpu.VMEM((ROWS, S5_COLS), F32), pltpu.VMEM((bt, S5_COLS), F32)],
        compiler_params=_params("arbitrary"),
        name="s5_mixer",
    )(u, s0, ar8, ai8, bb, cc, d, w_glu_bf, b_glu, g_out)


def _lru_body(tc, bt, xr_ref, gr_ref, h0_ref, cv0_ref, cw_ref, cb_ref, wa_ref, ba_ref, wx_ref, bx_ref,
              lam_ref, go_ref, y_ref, hout_ref, cvout_ref, xp_ref, a_ref, b_ref, h_ref):
    halo = (CONV_WIDTH - 1) * bt

    @pl.when(pl.program_id(0) == 0)
    def _():
        xp_ref[0:halo, :] = cv0_ref[...]
        h_ref[...] = h0_ref[...]

    xp_ref[halo:, :] = xr_ref[...]
    xc = cb_ref[...] + sum(xp_ref[k * bt:k * bt + ROWS, :] * cw_ref[k:k + 1, :] for k in range(CONV_WIDTH))
    xcb = xc.astype(BF16)
    r = jax.nn.sigmoid(jnp.dot(xcb, wa_ref[...], preferred_element_type=F32) + ba_ref[...])
    i = jax.nn.sigmoid(jnp.dot(xcb, wx_ref[...], preferred_element_type=F32) + bx_ref[...])
    lam = lam_ref[...]
    softplus_neg_lam = jnp.maximum(-lam, 0.0) + jnp.log1p(jnp.exp(-jnp.abs(lam)))
    log_a = -LRU_C * r * softplus_neg_lam
    a_ref[...] = jnp.exp(log_a)
    b_ref[...] = jnp.sqrt(-_expm1(2.0 * log_a)) * (i * xc)

    def sb_loop(sb, carry):
        base = pl.multiple_of(sb * SUBLANES, SUBLANES)

        def t_loop(t, h):
            r0 = pl.multiple_of(t * bt + base, SUBLANES)
            h = a_ref[pl.ds(r0, SUBLANES), :] * h + b_ref[pl.ds(r0, SUBLANES), :]
            b_ref[pl.ds(r0, SUBLANES), :] = h
            return h

        h_ref[pl.ds(base, SUBLANES), :] = lax.fori_loop(0, tc, t_loop, h_ref[pl.ds(base, SUBLANES), :])
        return carry

    lax.fori_loop(0, bt // SUBLANES, sb_loop, 0)

    y = b_ref[...] * _gelu(gr_ref[...])
    y_ref[...] = _rms(y, go_ref[...]).astype(BF16)
    tail = xp_ref[ROWS:ROWS + halo, :]
    xp_ref[0:halo, :] = tail
    cvout_ref[...] = tail
    hout_ref[...] = h_ref[...]


def _lru_mixer(xr, gr, h0, conv0_tm, conv_w, conv_b, wa_bd, b_a, wx_bd, b_x, lam, g_out, bt, tile0, tiles):
    tc = ROWS // bt
    halo = (CONV_WIDTH - 1) * bt
    in_rows = pl.BlockSpec((ROWS, D_LRU), lambda i: (i + tile0, 0))
    vec = _full((1, D_LRU))
    return pl.pallas_call(
        functools.partial(_lru_body, tc, bt),
        grid=(tiles,),
        in_specs=[in_rows, in_rows, _full((bt, D_LRU)), _full((halo, D_LRU)), _full((CONV_WIDTH, D_LRU)), vec,
                  _full((D_LRU, D_LRU)), vec, _full((D_LRU, D_LRU)), vec, vec, vec],
        out_specs=[pl.BlockSpec((ROWS, D_LRU), lambda i: (i, 0)), _full((bt, D_LRU)), _full((halo, D_LRU))],
        out_shape=[jax.ShapeDtypeStruct((tiles * ROWS, D_LRU), BF16), jax.ShapeDtypeStruct((bt, D_LRU), F32),
                   jax.ShapeDtypeStruct((halo, D_LRU), F32)],
        scratch_shapes=[pltpu.VMEM((ROWS + halo, D_LRU), F32), pltpu.VMEM((ROWS, D_LRU), F32),
                        pltpu.VMEM((ROWS, D_LRU), F32), pltpu.VMEM((bt, D_LRU), F32)],
        compiler_params=_params("arbitrary"),
        name="lru_mixer",
    )(xr, gr, h0, conv0_tm, conv_w, conv_b, wa_bd, b_a, wx_bd, b_x, lam, g_out)


def _out_body(tiles_p, ysp_ref, yss_ref, ylp_ref, yls_ref, x_ref, g1_ref, sc_ref, sh_ref, wo_ref, gf_ref,
              rwh_ref, rwl_ref, rb_ref, h_ref, xn_ref, pg_ref, cnt_ref):
    is_prompt = pl.program_id(0) < tiles_p
    ys = jnp.where(is_prompt, ysp_ref[...], yss_ref[...])
    yl = jnp.where(is_prompt, ylp_ref[...], yls_ref[...])
    mixed = (jnp.dot(ys, wo_ref[0], preferred_element_type=F32) + jnp.dot(yl, wo_ref[1], preferred_element_type=F32))
    h = x_ref[...] + g1_ref[0] * mixed
    h_ref[...] = h
    xn = _rms(h, gf_ref[...]) * (1.0 + sc_ref[0]) + sh_ref[0]
    x_hi = xn.astype(BF16)
    xn_ref[...] = x_hi

    x_lo = (xn - x_hi.astype(F32)).astype(BF16)
    nt = (((1,), (1,)), ((), ()))
    logits = (lax.dot_general(rwh_ref[...], x_hi, nt, preferred_element_type=F32)
              + lax.dot_general(rwh_ref[...], x_lo, nt, preferred_element_type=F32)
              + lax.dot_general(rwl_ref[...], x_hi, nt, preferred_element_type=F32)) + rb_ref[...]

    e_iota = lax.broadcasted_iota(jnp.int32, (N_EXPERTS, ROWS), 0).astype(F32)
    work = logits
    sels, vals = [], []
    for _ in range(TOP_K):
        m = jnp.max(work, axis=0, keepdims=True)
        idx = jnp.min(jnp.where(work == m, e_iota, float(N_EXPERTS)), axis=0, keepdims=True)
        sel = e_iota == idx
        work = jnp.where(sel, -jnp.inf, work)
        sels.append(sel)
        vals.append(m)
    exps = [jnp.exp(v - vals[0]) for v in vals]
    denom = exps[0] + exps[1] + exps[2] + exps[3]
    gates = [e / denom for e in exps]

    onehot = sels[0] | sels[1] | sels[2] | sels[3]
    rr = lax.broadcasted_iota(jnp.int32, (ROWS, ROWS), 0)
    cc = lax.broadcasted_iota(jnp.int32, (ROWS, ROWS), 1)
    before = (rr < cc).astype(BF16)
    prefix = jnp.dot(onehot.astype(BF16), before, preferred_element_type=F32)
    cnt = jnp.sum(onehot.astype(F32), axis=1, keepdims=True)
    cnt_pad = jnp.floor((cnt + (SEG_ALIGN - 1)) * (1.0 / SEG_ALIGN)) * SEG_ALIGN
    er = lax.broadcasted_iota(jnp.int32, (N_EXPERTS, N_EXPERTS), 0)
    ec = lax.broadcasted_iota(jnp.int32, (N_EXPERTS, N_EXPERTS), 1)
    seg_start = jnp.dot((ec < er).astype(BF16), jnp.broadcast_to(cnt_pad, (N_EXPERTS, LANES)).astype(BF16),
                        preferred_element_type=F32)[:, 0:1]
    where_to = prefix + seg_start
    poss = [jnp.sum(jnp.where(s, where_to, 0.0), axis=0, keepdims=True) for s in sels]

    s_iota = lax.broadcasted_iota(jnp.int32, (2 * TOP_K, ROWS), 0)
    pg = jnp.zeros((2 * TOP_K, ROWS), F32)
    for k in range(TOP_K):
        pg = jnp.where(s_iota == k, poss[k], pg)
        pg = jnp.where(s_iota == TOP_K + k, gates[k], pg)
    pg_ref[...] = pg
    cnt_ref[0] = jnp.broadcast_to(cnt, (N_EXPERTS, LANES))


def _out_proj(ys5_p, ys5_s, ylru_p, ylru_s, x_rows, modpat, w_out_bf, g_ffn, rw_hi, rw_lo, router_b, tiles_p):
    t_rows = x_rows.shape[0]
    tiles = t_rows // ROWS
    tiles_s = tiles - tiles_p
    p_spec = pl.BlockSpec((ROWS, D_S5), lambda i: (jnp.minimum(i, tiles_p - 1), 0))
    s_spec = pl.BlockSpec((ROWS, D_S5), lambda i: (jnp.clip(i - tiles_p, 0, tiles_s - 1), 0))
    row_spec = pl.BlockSpec((ROWS, D_MODEL), lambda i: (i, 0))
    return pl.pallas_call(
        functools.partial(_out_body, tiles_p),
        grid=(tiles,),
        in_specs=[p_spec, s_spec, p_spec, s_spec, row_spec,
                  _mod_spec(2, tiles_p), _mod_spec(4, tiles_p), _mod_spec(3, tiles_p),
                  _full((2, D_S5, D_MODEL)), _full((1, D_MODEL)),
                  _full((N_EXPERTS, D_MODEL)), _full((N_EXPERTS, D_MODEL)), _full((N_EXPERTS, 1))],
        out_specs=[row_spec, row_spec, pl.BlockSpec((2 * TOP_K, ROWS), lambda i: (0, i)),
                   pl.BlockSpec((1, N_EXPERTS, LANES), lambda i: (i, 0, 0))],
        out_shape=[jax.ShapeDtypeStruct((t_rows, D_MODEL), F32), jax.ShapeDtypeStruct((t_rows, D_MODEL), BF16),
                   jax.ShapeDtypeStruct((2 * TOP_K, t_rows), F32),
                   jax.ShapeDtypeStruct((tiles, N_EXPERTS, LANES), F32)],
        compiler_params=_params("arbitrary"),
        name="out_proj_router",
    )(ys5_p, ys5_s, ylru_p, ylru_s, x_rows, modpat, modpat, modpat, w_out_bf, g_ffn, rw_hi, rw_lo, router_b)


def _segment_copies(n_ref, hbm_ref, vmem_ref, hbm, vmem_slot, sem, step, to_hbm, act):
    def seg(e, carry):
        n = n_ref[step * N_EXPERTS + e]
        h0 = hbm_ref[step * N_EXPERTS + e]
        v0 = vmem_ref[step * N_EXPERTS + e]

        def piece(off, size):
            h = hbm.at[pl.ds(pl.multiple_of(h0 + off, SEG_ALIGN), size)]
            v = vmem_slot.at[pl.ds(pl.multiple_of(v0 + off, SEG_ALIGN), size)]
            act(pltpu.make_async_copy(v, h, sem) if to_hbm else pltpu.make_async_copy(h, v, sem))

        _pow2_chunks(n, ROWS, SEG_ALIGN, piece)
        return carry

    lax.fori_loop(0, N_EXPERTS, seg, 0)


def _dispatch_body(n_ref, glob_ref, local_ref, nb_ref, pg_ref, x_ref, xs_hbm, stage, sem):
    j = pl.program_id(0)
    last = pl.num_programs(0) - 1
    slot = j % 2

    def unused_blocks(act):
        def blk(b, carry):
            act(pltpu.make_async_copy(stage.at[slot, pl.ds(0, MOE_TM)],
                                      xs_hbm.at[pl.ds(pl.multiple_of(b * MOE_TM, MOE_TM), MOE_TM)], sem.at[slot]))
            return carry
        lax.fori_loop(nb_ref[0], xs_hbm.shape[0] // MOE_TM, blk, 0)

    def copies(step, s, act):
        _segment_copies(n_ref, glob_ref, local_ref, xs_hbm, stage.at[s], sem.at[s], step, True, act)

    @pl.when(j >= 2)
    def _():
        copies(jnp.maximum(j - 2, 0), slot, lambda c: c.wait())

    @pl.when(j < last)
    def _():
        x = x_ref[...]
        pos = pg_ref[0:TOP_K, :]
        for c in range(SORT_ROWS // SORT_CHUNK):
            r = (lax.broadcasted_iota(jnp.int32, (SORT_CHUNK, ROWS), 0) + c * SORT_CHUNK).astype(F32)
            pick = (r == pos[0:1]) | (r == pos[1:2]) | (r == pos[2:3]) | (r == pos[3:4])
            stage[slot, c * SORT_CHUNK:(c + 1) * SORT_CHUNK, :] = jnp.dot(
                pick.astype(BF16), x, preferred_element_type=F32).astype(BF16)

    @pl.when(j == last)
    def _():
        stage[slot, 0:MOE_TM, :] = jnp.zeros((MOE_TM, D_MODEL), BF16)

    copies(j, slot, lambda c: c.start())

    @pl.when(j == last)
    def _():
        unused_blocks(lambda c: c.start())

        @pl.when(j >= 1)
        def _():
            copies(jnp.maximum(j - 1, 0), 1 - slot, lambda c: c.wait())
        copies(j, slot, lambda c: c.wait())
        unused_blocks(lambda c: c.wait())


def _dispatch(seg_n, seg_glob, seg_local, nb_used, pg, xn, n_rows):
    tiles = xn.shape[0] // ROWS
    grid_spec = pltpu.PrefetchScalarGridSpec(
        num_scalar_prefetch=4,
        grid=(tiles + 1,),
        in_specs=[pl.BlockSpec((2 * TOP_K, ROWS), lambda j, *_: (0, jnp.minimum(j, tiles - 1))),
                  pl.BlockSpec((ROWS, D_MODEL), lambda j, *_: (jnp.minimum(j, tiles - 1), 0))],
        out_specs=pl.BlockSpec(memory_space=pl.ANY),
        scratch_shapes=[pltpu.VMEM((2, SORT_ROWS, D_MODEL), BF16), pltpu.SemaphoreType.DMA((2,))],
    )
    return pl.pallas_call(
        _dispatch_body,
        grid_spec=grid_spec,
        out_shape=jax.ShapeDtypeStruct((n_rows, D_MODEL), BF16),
        compiler_params=_params("arbitrary"),
        name="moe_dispatch",
    )(seg_n, seg_glob, seg_local, nb_used, pg, xn)


def _moe_body(be_ref, nb_ref, xs_ref, wgu_ref, bgu_ref, wd_ref, bd_ref, ys_ref, wgu_bf, wd_bf):
    i = pl.program_id(0)

    @pl.when(i >= nb_ref[0])
    def _():
        ys_ref[...] = jnp.zeros_like(ys_ref)

    @pl.when(i < nb_ref[0])
    def _():
        prev = be_ref[jnp.maximum(i - 1, 0)]

        @pl.when(jnp.logical_or(i == 0, be_ref[i] != prev))
        def _():
            wgu_bf[...] = wgu_ref[0].astype(BF16)
            wd_bf[...] = wd_ref[0].astype(BF16)

        hg = jnp.dot(xs_ref[...], wgu_bf[...], preferred_element_type=F32) + bgu_ref[0]
        gate = jnp.minimum(hg[:, :D_FF], SWIGLU_LIMIT)
        up = jnp.clip(hg[:, D_FF:], -SWIGLU_LIMIT, SWIGLU_LIMIT)
        act = (up + 1.0) * (gate * jax.nn.sigmoid(SWIGLU_ALPHA * gate))
        ys_ref[...] = (jnp.dot(act.astype(BF16), wd_bf[...], preferred_element_type=F32) + bd_ref[0]).astype(BF16)


def _moe(xs, block_expert, nb_used, w_gu, b_gu, w_down, b_down):
    n_blocks = xs.shape[0] // MOE_TM
    in_rows = pl.BlockSpec((MOE_TM, D_MODEL), lambda i, be, nb: (jnp.minimum(i, nb[0] - 1), 0))
    grid_spec = pltpu.PrefetchScalarGridSpec(
        num_scalar_prefetch=2,
        grid=(n_blocks,),
        in_specs=[in_rows,
                  pl.BlockSpec((1, D_MODEL, 2 * D_FF), lambda i, be, nb: (be[i], 0, 0)),
                  pl.BlockSpec((1, 1, 2 * D_FF), lambda i, be, nb: (be[i], 0, 0)),
                  pl.BlockSpec((1, D_FF, D_MODEL), lambda i, be, nb: (be[i], 0, 0)),
                  pl.BlockSpec((1, 1, D_MODEL), lambda i, be, nb: (be[i], 0, 0))],
        out_specs=pl.BlockSpec((MOE_TM, D_MODEL), lambda i, be, nb: (i, 0)),
        scratch_shapes=[pltpu.VMEM((D_MODEL, 2 * D_FF), BF16), pltpu.VMEM((D_FF, D_MODEL), BF16)],
    )
    return pl.pallas_call(
        _moe_body,
        grid_spec=grid_spec,
        out_shape=jax.ShapeDtypeStruct(xs.shape, BF16),
        compiler_params=_params("arbitrary"),
        name="moe_experts",
    )(block_expert, nb_used, xs, w_gu, b_gu, w_down, b_down)


def _fin_body(n_ref, glob_ref, local_ref, h_ref, pg_ref, g2_ref, gf_ref, ys_hbm, o_ref, ybuf, sem):
    j = pl.program_id(0)
    tiles = pl.num_programs(0)
    slot = j % 2

    def copies(step, s, act):
        _segment_copies(n_ref, glob_ref, local_ref, ys_hbm, ybuf.at[s], sem.at[s], step, False, act)

    @pl.when(j == 0)
    def _():
        ybuf[...] = jnp.zeros_like(ybuf)
        copies(0, 0, lambda c: c.start())

    copies(j, slot, lambda c: c.wait())

    @pl.when(j + 1 < tiles)
    def _():
        copies(jnp.minimum(j + 1, tiles - 1), 1 - slot, lambda c: c.start())

    pg = pg_ref[...]
    ff = jnp.zeros((ROWS, D_MODEL), F32)
    for c in range(SORT_ROWS // SORT_CHUNK):
        r = (lax.broadcasted_iota(jnp.int32, (ROWS, SORT_CHUNK), 1) + c * SORT_CHUNK).astype(F32)
        w = jnp.zeros((ROWS, SORT_CHUNK), F32)
        for k in range(TOP_K):
            w = w + jnp.where(r == pg[:, k:k + 1], pg[:, TOP_K + k:TOP_K + k + 1], 0.0)
        ff = ff + jnp.dot(w.astype(BF16), ybuf[slot, c * SORT_CHUNK:(c + 1) * SORT_CHUNK, :],
                          preferred_element_type=F32)
    o_ref[...] = _rms(h_ref[...] + g2_ref[0] * ff, gf_ref[...])


def _final(seg_n, seg_glob, seg_local, h, pg_t, modpat, g_final, ys, tiles_p):
    t_rows = h.shape[0]
    grid_spec = pltpu.PrefetchScalarGridSpec(
        num_scalar_prefetch=3,
        grid=(t_rows // ROWS,),
        in_specs=[pl.BlockSpec((ROWS, D_MODEL), lambda j, *_: (j, 0)),
                  pl.BlockSpec((ROWS, 2 * TOP_K), lambda j, *_: (j, 0)),
                  _mod_spec(5, tiles_p), _full((1, D_MODEL)),
                  pl.BlockSpec(memory_space=pl.ANY)],
        out_specs=pl.BlockSpec((ROWS, D_MODEL), lambda j, *_: (j, 0)),
        scratch_shapes=[pltpu.VMEM((2, SORT_ROWS, D_MODEL), BF16), pltpu.SemaphoreType.DMA((2,))],
    )
    return pl.pallas_call(
        _fin_body,
        grid_spec=grid_spec,
        out_shape=jax.ShapeDtypeStruct((t_rows, D_MODEL), F32),
        compiler_params=_params("arbitrary"),
        name="combine_final",
    )(seg_n, seg_glob, seg_local, h, pg_t, modpat, g_final, ys)


def _block_diag(w):
    h, i, j = w.shape
    return jnp.einsum('hij,hk->hikj', w, jnp.eye(h, dtype=w.dtype)).reshape(h * i, h * j)


def _s5_cols(re, im):
    b = re.shape[0]
    stack = jnp.stack([re.reshape(b, S5_BLOCKS, -1), im.reshape(b, S5_BLOCKS, -1)], axis=2)
    return stack.reshape(b, S5_COLS)


def _s5_uncols(cols):
    b = cols.shape[0]
    c = cols.reshape(b, S5_BLOCKS, 2, S5_GROUPS // S5_BLOCKS, S5_STATE)
    return (c[:, :, 0].reshape(b, S5_GROUPS, S5_STATE), c[:, :, 1].reshape(b, S5_GROUPS, S5_STATE))


def _moe_rows_bound(tiles):
    worst = tiles * (TOP_K * ROWS + N_EXPERTS * (SEG_ALIGN - 1)) + N_EXPERTS * (MOE_TM - SEG_ALIGN)
    return (worst + MOE_TM - 1) // MOE_TM * MOE_TM


def _plan(cnt):
    cnt = cnt.astype(jnp.int32)
    tiles = cnt.shape[0]
    cp = (cnt + SEG_ALIGN - 1) // SEG_ALIGN * SEG_ALIGN
    local = jnp.cumsum(cp, axis=1) - cp
    group = jnp.sum(cp, axis=0)
    group_pad = (group + MOE_TM - 1) // MOE_TM * MOE_TM
    pend = jnp.cumsum(group_pad)
    pstart = pend - group_pad
    glob = pstart[None, :] + jnp.cumsum(cp, axis=0) - cp
    gap = group_pad - group
    seg_n = jnp.concatenate([cp, gap[None]], axis=0).reshape(-1)
    seg_local = jnp.concatenate([local, jnp.zeros((1, N_EXPERTS), jnp.int32)], axis=0).reshape(-1)
    seg_glob = jnp.concatenate([glob, (pstart + group)[None]], axis=0).reshape(-1)
    n_blocks = _moe_rows_bound(tiles) // MOE_TM
    block_row0 = jnp.arange(n_blocks, dtype=jnp.int32) * MOE_TM
    block_expert = jnp.minimum(jnp.sum(block_row0[:, None] >= pend[None, :], axis=1), N_EXPERTS - 1).astype(jnp.int32)
    nb_used = (pend[-1] // MOE_TM).astype(jnp.int32).reshape(1)
    return seg_n, seg_glob, seg_local, block_expert, nb_used


def kernel(x_prompt, x_sample, state_s5_re, state_s5_im, state_lru_h, state_conv, c_prompt, c_sample, w_ada, b_ada, g_mix, w_in, s5_a_re, s5_a_im, s5_log_dt, s5_b_re, s5_b_im, s5_c_re, s5_c_im, s5_d, s5_w_glu, s5_b_glu, lru_conv_w, lru_conv_b, lru_w_a, lru_b_a, lru_w_x, lru_b_x, lru_lambda, g_out_s5, g_out_lru, w_out, g_ffn, router_w, router_b, moe_w_gu, moe_b_gu, moe_w_down, moe_b_down, g_final):
    assert w_ada.shape[0] == 1, "one layer"
    bp, lp, _ = x_prompt.shape
    bs, ls, _ = x_sample.shape
    assert ROWS % bp == 0 and ROWS % bs == 0 and (bp * lp) % ROWS == 0 and (bs * ls) % ROWS == 0
    tiles_p = bp * lp // ROWS
    tiles_s = bs * ls // ROWS
    row = lambda v: v.reshape(1, -1)

    ab_re, ab_im, bb_re, bb_im = _s5_prep(s5_a_re[0], s5_a_im[0], s5_log_dt[0].reshape(S5_GROUPS, 1),
                                          jnp.swapaxes(s5_b_re[0], 1, 2), jnp.swapaxes(s5_b_im[0], 1, 2))
    gpb = S5_GROUPS // S5_BLOCKS
    eye = jnp.eye(gpb, dtype=F32)

    def in_blocks(b):
        b = b.reshape(S5_BLOCKS, gpb, S5_GROUP_CH, S5_STATE)
        return jnp.einsum('bgjn,gh->bgjhn', b, eye).reshape(S5_BLOCKS, gpb * S5_GROUP_CH, gpb * S5_STATE)

    def out_blocks(c):
        c = c.reshape(S5_BLOCKS, gpb, S5_GROUP_CH, S5_STATE)
        return jnp.einsum('bgjn,gh->bgnhj', c, eye).reshape(S5_BLOCKS, gpb * S5_STATE, gpb * S5_GROUP_CH)

    ar8 = jnp.broadcast_to(ab_re.reshape(1, -1), (SUBLANES, S5_GROUPS * S5_STATE))
    ai8 = jnp.broadcast_to(ab_im.reshape(1, -1), (SUBLANES, S5_GROUPS * S5_STATE))
    bb = jnp.concatenate([in_blocks(bb_re), in_blocks(bb_im)], axis=-1).astype(BF16)
    cc = jnp.concatenate([out_blocks(s5_c_re[0]), -out_blocks(s5_c_im[0])], axis=1).astype(BF16)
    wa_bd = _block_diag(lru_w_a[0]).astype(BF16)
    wx_bd = _block_diag(lru_w_x[0]).astype(BF16)
    rw_t = router_w[0].T
    rw_hi = rw_t.astype(BF16)
    rw_lo = (rw_t - rw_hi.astype(F32)).astype(BF16)

    mod = _adaln(jnp.concatenate([c_prompt, c_sample], axis=0), w_ada[0], row(b_ada[0]))
    modpat = jnp.stack([jnp.tile(mod[:bp], (ROWS // bp, 1)), jnp.tile(mod[bp:], (ROWS // bs, 1))])

    x_rows = jnp.concatenate([jnp.swapaxes(x_prompt, 0, 1).reshape(bp * lp, D_MODEL),
                              jnp.swapaxes(x_sample, 0, 1).reshape(bs * ls, D_MODEL)], axis=0)
    u, xr, gr = _in_proj(x_rows, modpat, row(g_mix[0]), w_in[0].astype(BF16), tiles_p)

    def conv_tm(cv):
        return jnp.swapaxes(cv, 0, 1).reshape(-1, D_LRU)

    def conv_bm(cv, b):
        return jnp.swapaxes(cv.reshape(CONV_WIDTH - 1, b, D_LRU), 0, 1)

    s5_args = (ar8, ai8, bb, cc, row(s5_d[0]), s5_w_glu[0].astype(BF16), row(s5_b_glu[0]), row(g_out_s5[0]))
    lru_args = (lru_conv_w[0], row(lru_conv_b[0]), wa_bd, row(lru_b_a[0]), wx_bd, row(lru_b_x[0]),
                row(lru_lambda[0]), row(g_out_lru[0]))
    ys5_p, s5p = _s5_mixer(u, jnp.zeros((bp, S5_COLS), F32), *s5_args, bp, 0, tiles_p)
    ys5_s, s5s = _s5_mixer(u, _s5_cols(state_s5_re[0], state_s5_im[0]), *s5_args, bs, tiles_p, tiles_s)
    ylru_p, hp, cvp = _lru_mixer(xr, gr, jnp.zeros((bp, D_LRU), F32),
                                 jnp.zeros(((CONV_WIDTH - 1) * bp, D_LRU), F32), *lru_args, bp, 0, tiles_p)
    ylru_s, hs, cvs = _lru_mixer(xr, gr, state_lru_h[0], conv_tm(state_conv[0]), *lru_args, bs, tiles_p, tiles_s)

    h, xn, pg, cnt = _out_proj(ys5_p, ys5_s, ylru_p, ylru_s, x_rows, modpat,
                               w_out[0].astype(BF16).reshape(2, D_S5, D_MODEL), row(g_ffn[0]), rw_hi, rw_lo,
                               router_b[0].reshape(N_EXPERTS, 1), tiles_p)

    seg_n, seg_glob, seg_local, block_expert, nb_used = _plan(cnt[:, :, 0])
    xs = _dispatch(seg_n, seg_glob, seg_local, nb_used, pg, xn, _moe_rows_bound(tiles_p + tiles_s))
    ys = _moe(xs, block_expert, nb_used, moe_w_gu[0], moe_b_gu[0].reshape(N_EXPERTS, 1, 2 * D_FF),
              moe_w_down[0], moe_b_down[0].reshape(N_EXPERTS, 1, D_MODEL))
    y = _final(seg_n, seg_glob, seg_local, h, pg.T, modpat, row(g_final), ys, tiles_p)

    y_prompt = jnp.swapaxes(y[:bp * lp].reshape(lp, bp, D_MODEL), 0, 1)
    y_sample = jnp.swapaxes(y[bp * lp:].reshape(ls, bs, D_MODEL), 0, 1)
    s5p_re, s5p_im = _s5_uncols(s5p)
    s5s_re, s5s_im = _s5_uncols(s5s)
    return (y_prompt, y_sample,
            s5p_re[None], s5p_im[None], hp[None], conv_bm(cvp, bp)[None],
            s5s_re[None], s5s_im[None], hs[None], conv_bm(cvs, bs)[None])
```

```python
import functools

import jax
import jax.numpy as jnp
from jax import lax
from jax.experimental import pallas as pl
from jax.experimental.pallas import tpu as pltpu

D_MODEL = 1024
D_S5 = 512
D_LRU = 512
S5_GROUPS = 32
S5_GROUP_CH = 16
S5_STATE = 64
S5_COLS = 2 * S5_GROUPS * S5_STATE
S5_BLOCKS = 4
LRU_HEADS = 8
LRU_HEAD_DIM = 64
LRU_C = 8.0
CONV_WIDTH = 4
N_EXPERTS = 32
TOP_K = 4
D_FF = 1024
SWIGLU_LIMIT = 7.0
SWIGLU_ALPHA = 1.702
N_MOD = 6
EPS = 1e-6

ROWS = 512
MOE_TM = 256
SEG_ALIGN = 16
SORT_ROWS = 2560
SORT_CHUNK = 512
SUBLANES = 8
LANES = 128
VMEM_LIMIT = 48 * 1024 * 1024

BF16 = jnp.bfloat16
F32 = jnp.float32


def _params(*sem):
    return pltpu.CompilerParams(dimension_semantics=sem, vmem_limit_bytes=VMEM_LIMIT)


def _full(shape):
    return pl.BlockSpec(shape, lambda *_: (0,) * len(shape))


def _rms(x, g):
    return x * lax.rsqrt(jnp.mean(x * x, axis=-1, keepdims=True) + EPS) * g


def _gelu(x):
    return 0.5 * x * (1.0 + lax.erf(x * (2.0 ** -0.5)))


def _expm1(x):
    u = jnp.exp(x)
    d = u - 1.0
    return jnp.where(d == 0.0, x, jnp.where(d == -1.0, -1.0, d * x / jnp.log(u)))


def _pow2_chunks(n, largest, smallest, fn):
    off = 0
    bit = largest
    while bit >= smallest:
        @pl.when((n & bit) != 0)
        def _(off=off, bit=bit):
            fn(off, bit)
        off = off + (n & bit)
        bit //= 2


def _mod_body(c_ref, w_ref, b_ref, o_ref):
    c = c_ref[...]
    s = (c * jax.nn.sigmoid(c)).astype(BF16)
    o_ref[...] = jnp.dot(s, w_ref[...].astype(BF16), preferred_element_type=F32) + b_ref[...]


def _adaln(c, w_ada, b_ada):
    m = c.shape[0]
    return pl.pallas_call(
        _mod_body,
        grid=(N_MOD,),
        in_specs=[pl.BlockSpec((m, D_MODEL), lambda j: (0, 0)),
                  pl.BlockSpec((D_MODEL, D_MODEL), lambda j: (0, j)),
                  pl.BlockSpec((1, D_MODEL), lambda j: (0, j))],
        out_specs=pl.BlockSpec((m, D_MODEL), lambda j: (0, j)),
        out_shape=jax.ShapeDtypeStruct((m, N_MOD * D_MODEL), F32),
        compiler_params=_params("arbitrary"),
        name="adaln",
    )(c, w_ada, b_ada)


def _mod_spec(k, tiles_p):
    return pl.BlockSpec((1, ROWS, D_MODEL), lambda i, *_: (jnp.where(i < tiles_p, 0, 1), 0, k))


def _perm_spec(tiles_p):
    return pl.BlockSpec((1, ROWS, ROWS), lambda i, *_: (jnp.where(i < tiles_p, 0, 1), 0, 0))


def _tile_specs(bp, tiles_p, tiles_s):
    tc = ROWS // bp
    p_spec = pl.BlockSpec((bp, tc, D_MODEL), lambda i, *_: (0, jnp.minimum(i, tiles_p - 1), 0))
    s_spec = pl.BlockSpec((ROWS, D_MODEL), lambda i, *_: (jnp.clip(i - tiles_p, 0, tiles_s - 1), 0))
    return p_spec, s_spec


def _in_body(tiles_p, xp_ref, xs_ref, sc_ref, sh_ref, perm_ref, g_ref, w_ref, u_ref, xr_ref, gr_ref):
    x = jnp.where(pl.program_id(0) < tiles_p, xp_ref[...].reshape(ROWS, D_MODEL), xs_ref[...])
    xn = _rms(x, g_ref[...]) * (1.0 + sc_ref[0]) + sh_ref[0]
    xn = jnp.dot(perm_ref[0], xn.astype(BF16), preferred_element_type=F32).astype(BF16)
    p = jnp.dot(xn, w_ref[...], preferred_element_type=F32)
    u_ref[...] = p[:, :D_S5]
    xr_ref[...] = p[:, D_S5:D_S5 + D_LRU]
    gr_ref[...] = p[:, D_S5 + D_LRU:]


def _in_proj(x_p, x_s, modpat, to_tm, g_mix, w_in_bf, tiles_p, tiles_s):
    t_rows = (tiles_p + tiles_s) * ROWS
    row_spec = pl.BlockSpec((ROWS, D_S5), lambda i: (i, 0))
    return pl.pallas_call(
        functools.partial(_in_body, tiles_p),
        grid=(tiles_p + tiles_s,),
        in_specs=[*_tile_specs(x_p.shape[0], tiles_p, tiles_s),
                  _mod_spec(1, tiles_p), _mod_spec(0, tiles_p), _perm_spec(tiles_p), _full((1, D_MODEL)),
                  _full((D_MODEL, D_S5 + 2 * D_LRU))],
        out_specs=[row_spec, row_spec, row_spec],
        out_shape=[jax.ShapeDtypeStruct((t_rows, D_S5), F32)] * 3,
        compiler_params=_params("arbitrary"),
        name="in_proj",
    )(x_p, x_s, modpat, modpat, to_tm, g_mix, w_in_bf)


def _s5_prep_body(are_ref, aim_ref, ldt_ref, bre_ref, bim_ref, abre_ref, abim_ref, bbre_ref, bbim_ref):
    a_re = are_ref[...]
    a_im = aim_ref[...]
    dt = jnp.exp(ldt_ref[...])
    mag = jnp.exp(dt * a_re)
    ang = dt * a_im
    ab_re = mag * jnp.cos(ang)
    ab_im = mag * jnp.sin(ang)
    den = a_re * a_re + a_im * a_im
    q_re = ((ab_re - 1.0) * a_re + ab_im * a_im) / den
    q_im = (ab_im * a_re - (ab_re - 1.0) * a_im) / den
    abre_ref[...] = ab_re
    abim_ref[...] = ab_im
    b_re = bre_ref[...]
    b_im = bim_ref[...]
    bbre_ref[...] = q_re[:, None, :] * b_re - q_im[:, None, :] * b_im
    bbim_ref[...] = q_re[:, None, :] * b_im + q_im[:, None, :] * b_re


def _s5_prep(a_re, a_im, log_dt, b_re_t, b_im_t):
    gn = jax.ShapeDtypeStruct((S5_GROUPS, S5_STATE), F32)
    gjn = jax.ShapeDtypeStruct((S5_GROUPS, S5_GROUP_CH, S5_STATE), F32)
    return pl.pallas_call(_s5_prep_body, out_shape=[gn, gn, gjn, gjn], name="s5_prep")(
        a_re, a_im, log_dt, b_re_t, b_im_t)


def _s5_body(tc, bt, u_ref, s0_ref, ar_ref, ai_ref, bb_ref, cc_ref, d_ref, wg_ref, bg_ref, go_ref,
             y_ref, sout_ref, bu_ref, st_ref):
    half = S5_COLS // S5_BLOCKS // 2

    @pl.when(pl.program_id(0) == 0)
    def _():
        st_ref[...] = s0_ref[...]

    u = u_ref[...]
    ub = u.astype(BF16)
    for j in range(S5_BLOCKS):
        bu_ref[:, 2 * half * j:2 * half * (j + 1)] = jnp.dot(
            ub[:, LANES * j:LANES * (j + 1)], bb_ref[j], preferred_element_type=F32)

    def sb_loop(sb, carry):
        base = pl.multiple_of(sb * SUBLANES, SUBLANES)
        h0 = st_ref[pl.ds(base, SUBLANES), :]
        hr0 = tuple(h0[:, 2 * half * j:2 * half * j + half] for j in range(S5_BLOCKS))
        hi0 = tuple(h0[:, 2 * half * j + half:2 * half * (j + 1)] for j in range(S5_BLOCKS))

        def t_loop(t, h):
            hr, hi = h
            r0 = pl.multiple_of(t * bt + base, SUBLANES)
            nr, ni = [], []
            for j in range(S5_BLOCKS):
                ar = ar_ref[:, half * j:half * (j + 1)]
                ai = ai_ref[:, half * j:half * (j + 1)]
                re_cols = pl.ds(2 * half * j, half)
                im_cols = pl.ds(2 * half * j + half, half)
                b_re = bu_ref[pl.ds(r0, SUBLANES), re_cols]
                b_im = bu_ref[pl.ds(r0, SUBLANES), im_cols]
                h_re = ar * hr[j] - ai * hi[j] + b_re
                h_im = ar * hi[j] + ai * hr[j] + b_im
                bu_ref[pl.ds(r0, SUBLANES), re_cols] = h_re
                bu_ref[pl.ds(r0, SUBLANES), im_cols] = h_im
                nr.append(h_re)
                ni.append(h_im)
            return tuple(nr), tuple(ni)

        hr, hi = lax.fori_loop(0, tc, t_loop, (hr0, hi0))
        for j in range(S5_BLOCKS):
            st_ref[pl.ds(base, SUBLANES), pl.ds(2 * half * j, half)] = hr[j]
            st_ref[pl.ds(base, SUBLANES), pl.ds(2 * half * j + half, half)] = hi[j]
        return carry

    lax.fori_loop(0, bt // SUBLANES, sb_loop, 0)
    sout_ref[...] = st_ref[...]

    y = jnp.concatenate(
        [jnp.dot(bu_ref[:, 2 * half * j:2 * half * (j + 1)].astype(BF16), cc_ref[j], preferred_element_type=F32)
         for j in range(S5_BLOCKS)], axis=-1)
    y = y + d_ref[...] * u
    g = _gelu(y)
    z = jnp.dot(g.astype(BF16), wg_ref[...], preferred_element_type=F32) + bg_ref[...]
    out = g * jax.nn.sigmoid(z)
    y_ref[...] = _rms(out, go_ref[...]).astype(BF16)


def _s5_mixer(u, s0, ar8, ai8, bb, cc, d, w_glu_bf, b_glu, g_out, bt, tile0, tiles):
    tc = ROWS // bt
    return pl.pallas_call(
        functools.partial(_s5_body, tc, bt),
        grid=(tiles,),
        in_specs=[pl.BlockSpec((ROWS, D_S5), lambda i: (i + tile0, 0)),
                  _full((bt, S5_COLS)), _full(ar8.shape), _full(ai8.shape), _full(bb.shape), _full(cc.shape),
                  _full((1, D_S5)), _full((D_S5, D_S5)), _full((1, D_S5)), _full((1, D_S5))],
        out_specs=[pl.BlockSpec((ROWS, D_S5), lambda i: (i, 0)), _full((bt, S5_COLS))],
        out_shape=[jax.ShapeDtypeStruct((tiles * ROWS, D_S5), BF16), jax.ShapeDtypeStruct((bt, S5_COLS), F32)],
        scratch_shapes=[pltpu.VMEM((ROWS, S5_COLS), F32), pltpu.VMEM((bt, S5_COLS), F32)],
        compiler_params=_params("arbitrary"),
        name="s5_mixer",
    )(u, s0, ar8, ai8, bb, cc, d, w_glu_bf, b_glu, g_out)


def _lru_body(tc, bt, xr_ref, gr_ref, h0_ref, cv0_ref, cw_ref, cb_ref, wa_ref, ba_ref, wx_ref, bx_ref,
              lam_ref, go_ref, y_ref, hout_ref, cvout_ref, xp_ref, a_ref, b_ref, h_ref):
    halo = (CONV_WIDTH - 1) * bt

    @pl.when(pl.program_id(0) == 0)
    def _():
        xp_ref[0:halo, :] = cv0_ref[...]
        h_ref[...] = h0_ref[...]

    xp_ref[halo:, :] = xr_ref[...]
    xc = cb_ref[...] + sum(xp_ref[k * bt:k * bt + ROWS, :] * cw_ref[k:k + 1, :] for k in range(CONV_WIDTH))
    xcb = xc.astype(BF16)
    r = jax.nn.sigmoid(jnp.dot(xcb, wa_ref[...], preferred_element_type=F32) + ba_ref[...])
    i = jax.nn.sigmoid(jnp.dot(xcb, wx_ref[...], preferred_element_type=F32) + bx_ref[...])
    lam = lam_ref[...]
    softplus_neg_lam = jnp.maximum(-lam, 0.0) + jnp.log1p(jnp.exp(-jnp.abs(lam)))
    log_a = -LRU_C * r * softplus_neg_lam
    a_ref[...] = jnp.exp(log_a)
    b_ref[...] = jnp.sqrt(-_expm1(2.0 * log_a)) * (i * xc)

    def sb_loop(sb, carry):
        base = pl.multiple_of(sb * SUBLANES, SUBLANES)

        def t_loop(t, h):
            r0 = pl.multiple_of(t * bt + base, SUBLANES)
            h = a_ref[pl.ds(r0, SUBLANES), :] * h + b_ref[pl.ds(r0, SUBLANES), :]
            b_ref[pl.ds(r0, SUBLANES), :] = h
            return h

        h_ref[pl.ds(base, SUBLANES), :] = lax.fori_loop(0, tc, t_loop, h_ref[pl.ds(base, SUBLANES), :])
        return carry

    lax.fori_loop(0, bt // SUBLANES, sb_loop, 0)

    y = b_ref[...] * _gelu(gr_ref[...])
    y_ref[...] = _rms(y, go_ref[...]).astype(BF16)
    tail = xp_ref[ROWS:ROWS + halo, :]
    xp_ref[0:halo, :] = tail
    cvout_ref[...] = tail
    hout_ref[...] = h_ref[...]


def _lru_mixer(xr, gr, h0, conv0_tm, conv_w, conv_b, wa_bd, b_a, wx_bd, b_x, lam, g_out, bt, tile0, tiles):
    tc = ROWS // bt
    halo = (CONV_WIDTH - 1) * bt
    in_rows = pl.BlockSpec((ROWS, D_LRU), lambda i: (i + tile0, 0))
    vec = _full((1, D_LRU))
    return pl.pallas_call(
        functools.partial(_lru_body, tc, bt),
        grid=(tiles,),
        in_specs=[in_rows, in_rows, _full((bt, D_LRU)), _full((halo, D_LRU)), _full((CONV_WIDTH, D_LRU)), vec,
                  _full((D_LRU, D_LRU)), vec, _full((D_LRU, D_LRU)), vec, vec, vec],
        out_specs=[pl.BlockSpec((ROWS, D_LRU), lambda i: (i, 0)), _full((bt, D_LRU)), _full((halo, D_LRU))],
        out_shape=[jax.ShapeDtypeStruct((tiles * ROWS, D_LRU), BF16), jax.ShapeDtypeStruct((bt, D_LRU), F32),
                   jax.ShapeDtypeStruct((halo, D_LRU), F32)],
        scratch_shapes=[pltpu.VMEM((ROWS + halo, D_LRU), F32), pltpu.VMEM((ROWS, D_LRU), F32),
                        pltpu.VMEM((ROWS, D_LRU), F32), pltpu.VMEM((bt, D_LRU), F32)],
        compiler_params=_params("arbitrary"),
        name="lru_mixer",
    )(xr, gr, h0, conv0_tm, conv_w, conv_b, wa_bd, b_a, wx_bd, b_x, lam, g_out)


def _out_body(tiles_p, ysp_ref, yss_ref, ylp_ref, yls_ref, xp_ref, xs_ref, g1_ref, sc_ref, sh_ref, perm_ref,
              wo_ref, gf_ref, rwh_ref, rwl_ref, rb_ref, hp_ref, hs_ref, xn_ref, pg_ref, cnt_ref):
    is_prompt = pl.program_id(0) < tiles_p
    ys = jnp.where(is_prompt, ysp_ref[...], yss_ref[...])
    yl = jnp.where(is_prompt, ylp_ref[...], yls_ref[...])
    ys = jnp.dot(perm_ref[0], ys, preferred_element_type=F32).astype(BF16)
    yl = jnp.dot(perm_ref[0], yl, preferred_element_type=F32).astype(BF16)
    mixed = (jnp.dot(ys, wo_ref[0], preferred_element_type=F32) + jnp.dot(yl, wo_ref[1], preferred_element_type=F32))
    x = jnp.where(is_prompt, xp_ref[...].reshape(ROWS, D_MODEL), xs_ref[...])
    h = x + g1_ref[0] * mixed

    @pl.when(is_prompt)
    def _():
        hp_ref[...] = h.reshape(hp_ref.shape)

    @pl.when(jnp.logical_not(is_prompt))
    def _():
        hs_ref[...] = h

    xn = _rms(h, gf_ref[...]) * (1.0 + sc_ref[0]) + sh_ref[0]
    x_hi = xn.astype(BF16)
    xn_ref[...] = x_hi

    x_lo = (xn - x_hi.astype(F32)).astype(BF16)
    nt = (((1,), (1,)), ((), ()))
    logits = (lax.dot_general(rwh_ref[...], x_hi, nt, preferred_element_type=F32)
              + lax.dot_general(rwh_ref[...], x_lo, nt, preferred_element_type=F32)
              + lax.dot_general(rwl_ref[...], x_hi, nt, preferred_element_type=F32)) + rb_ref[...]

    e_iota = lax.broadcasted_iota(jnp.int32, (N_EXPERTS, ROWS), 0).astype(F32)
    work = logits
    sels, vals = [], []
    for _ in range(TOP_K):
        m = jnp.max(work, axis=0, keepdims=True)
        idx = jnp.min(jnp.where(work == m, e_iota, float(N_EXPERTS)), axis=0, keepdims=True)
        sel = e_iota == idx
        work = jnp.where(sel, -jnp.inf, work)
        sels.append(sel)
        vals.append(m)
    exps = [jnp.exp(v - vals[0]) for v in vals]
    denom = exps[0] + exps[1] + exps[2] + exps[3]
    gates = [e / denom for e in exps]

    onehot = sels[0] | sels[1] | sels[2] | sels[3]
    rr = lax.broadcasted_iota(jnp.int32, (ROWS, ROWS), 0)
    cc = lax.broadcasted_iota(jnp.int32, (ROWS, ROWS), 1)
    before = (rr < cc).astype(BF16)
    prefix = jnp.dot(onehot.astype(BF16), before, preferred_element_type=F32)
    cnt = jnp.sum(onehot.astype(F32), axis=1, keepdims=True)
    cnt_pad = jnp.floor((cnt + (SEG_ALIGN - 1)) * (1.0 / SEG_ALIGN)) * SEG_ALIGN
    er = lax.broadcasted_iota(jnp.int32, (N_EXPERTS, N_EXPERTS), 0)
    ec = lax.broadcasted_iota(jnp.int32, (N_EXPERTS, N_EXPERTS), 1)
    seg_start = jnp.dot((ec < er).astype(BF16), jnp.broadcast_to(cnt_pad, (N_EXPERTS, LANES)).astype(BF16),
                        preferred_element_type=F32)[:, 0:1]
    where_to = prefix + seg_start
    poss = [jnp.sum(jnp.where(s, where_to, 0.0), axis=0, keepdims=True) for s in sels]

    s_iota = lax.broadcasted_iota(jnp.int32, (2 * TOP_K, ROWS), 0)
    pg = jnp.zeros((2 * TOP_K, ROWS), F32)
    for k in range(TOP_K):
        pg = jnp.where(s_iota == k, poss[k], pg)
        pg = jnp.where(s_iota == TOP_K + k, gates[k], pg)
    pg_ref[...] = pg
    cnt_ref[0] = jnp.broadcast_to(cnt, (N_EXPERTS, LANES))


def _out_proj(ys5_p, ys5_s, ylru_p, ylru_s, x_p, x_s, modpat, from_tm, w_out_bf, g_ffn, rw_hi, rw_lo, router_b,
              tiles_p, tiles_s):
    tiles = tiles_p + tiles_s
    t_rows = tiles * ROWS
    p_half = pl.BlockSpec((ROWS, D_S5), lambda i: (jnp.minimum(i, tiles_p - 1), 0))
    s_half = pl.BlockSpec((ROWS, D_S5), lambda i: (jnp.clip(i - tiles_p, 0, tiles_s - 1), 0))
    xp_spec, xs_spec = _tile_specs(x_p.shape[0], tiles_p, tiles_s)
    return pl.pallas_call(
        functools.partial(_out_body, tiles_p),
        grid=(tiles,),
        in_specs=[p_half, s_half, p_half, s_half, xp_spec, xs_spec,
                  _mod_spec(2, tiles_p), _mod_spec(4, tiles_p), _mod_spec(3, tiles_p), _perm_spec(tiles_p),
                  _full((2, D_S5, D_MODEL)), _full((1, D_MODEL)),
                  _full((N_EXPERTS, D_MODEL)), _full((N_EXPERTS, D_MODEL)), _full((N_EXPERTS, 1))],
        out_specs=[xp_spec, xs_spec, pl.BlockSpec((ROWS, D_MODEL), lambda i: (i, 0)),
                   pl.BlockSpec((2 * TOP_K, ROWS), lambda i: (0, i)),
                   pl.BlockSpec((1, N_EXPERTS, LANES), lambda i: (i, 0, 0))],
        out_shape=[jax.ShapeDtypeStruct(x_p.shape, F32), jax.ShapeDtypeStruct(x_s.shape, F32),
                   jax.ShapeDtypeStruct((t_rows, D_MODEL), BF16),
                   jax.ShapeDtypeStruct((2 * TOP_K, t_rows), F32),
                   jax.ShapeDtypeStruct((tiles, N_EXPERTS, LANES), F32)],
        compiler_params=_params("arbitrary"),
        name="out_proj_router",
    )(ys5_p, ys5_s, ylru_p, ylru_s, x_p, x_s, modpat, modpat, modpat, from_tm, w_out_bf, g_ffn, rw_hi, rw_lo,
      router_b)


def _segment_copies(n_ref, hbm_ref, vmem_ref, hbm, vmem_slot, sem, step, to_hbm, act):
    def seg(e, carry):
        n = n_ref[step * N_EXPERTS + e]
        h0 = hbm_ref[step * N_EXPERTS + e]
        v0 = vmem_ref[step * N_EXPERTS + e]

        def piece(off, size):
            h = hbm.at[pl.ds(pl.multiple_of(h0 + off, SEG_ALIGN), size)]
            v = vmem_slot.at[pl.ds(pl.multiple_of(v0 + off, SEG_ALIGN), size)]
            act(pltpu.make_async_copy(v, h, sem) if to_hbm else pltpu.make_async_copy(h, v, sem))

        _pow2_chunks(n, ROWS, SEG_ALIGN, piece)
        return carry

    lax.fori_loop(0, N_EXPERTS, seg, 0)


def _dispatch_body(n_ref, glob_ref, local_ref, nb_ref, pg_ref, x_ref, xs_hbm, stage, sem):
    j = pl.program_id(0)
    last = pl.num_programs(0) - 1
    slot = j % 2

    def unused_blocks(act):
        def blk(b, carry):
            act(pltpu.make_async_copy(stage.at[slot, pl.ds(0, MOE_TM)],
                                      xs_hbm.at[pl.ds(pl.multiple_of(b * MOE_TM, MOE_TM), MOE_TM)], sem.at[slot]))
            return carry
        lax.fori_loop(nb_ref[0], xs_hbm.shape[0] // MOE_TM, blk, 0)

    def copies(step, s, act):
        _segment_copies(n_ref, glob_ref, local_ref, xs_hbm, stage.at[s], sem.at[s], step, True, act)

    @pl.when(j >= 2)
    def _():
        copies(jnp.maximum(j - 2, 0), slot, lambda c: c.wait())

    @pl.when(j < last)
    def _():
        x = x_ref[...]
        pos = pg_ref[0:TOP_K, :]
        for c in range(SORT_ROWS // SORT_CHUNK):
            r = (lax.broadcasted_iota(jnp.int32, (SORT_CHUNK, ROWS), 0) + c * SORT_CHUNK).astype(F32)
            pick = (r == pos[0:1]) | (r == pos[1:2]) | (r == pos[2:3]) | (r == pos[3:4])
            stage[slot, c * SORT_CHUNK:(c + 1) * SORT_CHUNK, :] = jnp.dot(
                pick.astype(BF16), x, preferred_element_type=F32).astype(BF16)

    @pl.when(j == last)
    def _():
        stage[slot, 0:MOE_TM, :] = jnp.zeros((MOE_TM, D_MODEL), BF16)

    copies(j, slot, lambda c: c.start())

    @pl.when(j == last)
    def _():
        unused_blocks(lambda c: c.start())

        @pl.when(j >= 1)
        def _():
            copies(jnp.maximum(j - 1, 0), 1 - slot, lambda c: c.wait())
        copies(j, slot, lambda c: c.wait())
        unused_blocks(lambda c: c.wait())


def _dispatch(seg_n, seg_glob, seg_local, nb_used, pg, xn, n_rows):
    tiles = xn.shape[0] // ROWS
    grid_spec = pltpu.PrefetchScalarGridSpec(
        num_scalar_prefetch=4,
        grid=(tiles + 1,),
        in_specs=[pl.BlockSpec((2 * TOP_K, ROWS), lambda j, *_: (0, jnp.minimum(j, tiles - 1))),
                  pl.BlockSpec((ROWS, D_MODEL), lambda j, *_: (jnp.minimum(j, tiles - 1), 0))],
        out_specs=pl.BlockSpec(memory_space=pl.ANY),
        scratch_shapes=[pltpu.VMEM((2, SORT_ROWS, D_MODEL), BF16), pltpu.SemaphoreType.DMA((2,))],
    )
    return pl.pallas_call(
        _dispatch_body,
        grid_spec=grid_spec,
        out_shape=jax.ShapeDtypeStruct((n_rows, D_MODEL), BF16),
        compiler_params=_params("arbitrary"),
        name="moe_dispatch",
    )(seg_n, seg_glob, seg_local, nb_used, pg, xn)


def _moe_body(be_ref, nxt_ref, nb_ref, xs_ref, wgu_hbm, bgu_ref, wd_hbm, bd_ref, ys_ref,
              wgu_f32, wd_f32, wgu_bf, wd_bf, sem):
    i = pl.program_id(0)

    def weight_copies(e):
        return (pltpu.make_async_copy(wgu_hbm.at[e], wgu_f32, sem.at[0]),
                pltpu.make_async_copy(wd_hbm.at[e], wd_f32, sem.at[1]))

    @pl.when(i >= nb_ref[0])
    def _():
        ys_ref[...] = jnp.zeros_like(ys_ref)

    @pl.when(i < nb_ref[0])
    def _():
        e = be_ref[i]

        @pl.when(i == 0)
        def _():
            for c in weight_copies(e):
                c.start()

        @pl.when(jnp.logical_or(i == 0, e != be_ref[jnp.maximum(i - 1, 0)]))
        def _():
            for c in weight_copies(e):
                c.wait()
            wgu_bf[...] = wgu_f32[...].astype(BF16)
            wd_bf[...] = wd_f32[...].astype(BF16)

            @pl.when(nxt_ref[i] >= 0)
            def _():
                for c in weight_copies(nxt_ref[i]):
                    c.start()

        hg = jnp.dot(xs_ref[...], wgu_bf[...], preferred_element_type=F32) + bgu_ref[0]
        gate = jnp.minimum(hg[:, :D_FF], SWIGLU_LIMIT)
        up = jnp.clip(hg[:, D_FF:], -SWIGLU_LIMIT, SWIGLU_LIMIT)
        act = (up + 1.0) * (gate * jax.nn.sigmoid(SWIGLU_ALPHA * gate))
        ys_ref[...] = (jnp.dot(act.astype(BF16), wd_bf[...], preferred_element_type=F32) + bd_ref[0]).astype(BF16)


def _moe(xs, block_expert, block_next, nb_used, w_gu, b_gu, w_down, b_down):
    n_blocks = xs.shape[0] // MOE_TM
    in_rows = pl.BlockSpec((MOE_TM, D_MODEL), lambda i, be, nx, nb: (jnp.minimum(i, nb[0] - 1), 0))
    grid_spec = pltpu.PrefetchScalarGridSpec(
        num_scalar_prefetch=3,
        grid=(n_blocks,),
        in_specs=[in_rows,
                  pl.BlockSpec(memory_space=pl.ANY),
                  pl.BlockSpec((1, 1, 2 * D_FF), lambda i, be, nx, nb: (be[i], 0, 0)),
                  pl.BlockSpec(memory_space=pl.ANY),
                  pl.BlockSpec((1, 1, D_MODEL), lambda i, be, nx, nb: (be[i], 0, 0))],
        out_specs=pl.BlockSpec((MOE_TM, D_MODEL), lambda i, be, nx, nb: (i, 0)),
        scratch_shapes=[pltpu.VMEM((D_MODEL, 2 * D_FF), F32), pltpu.VMEM((D_FF, D_MODEL), F32),
                        pltpu.VMEM((D_MODEL, 2 * D_FF), BF16), pltpu.VMEM((D_FF, D_MODEL), BF16),
                        pltpu.SemaphoreType.DMA((2,))],
    )
    return pl.pallas_call(
        _moe_body,
        grid_spec=grid_spec,
        out_shape=jax.ShapeDtypeStruct(xs.shape, BF16),
        compiler_params=_params("arbitrary"),
        name="moe_experts",
    )(block_expert, block_next, nb_used, xs, w_gu, b_gu, w_down, b_down)


def _fin_body(tiles_p, n_ref, glob_ref, local_ref, hp_ref, hs_ref, pg_ref, g2_ref, gf_ref, ys_hbm, op_ref, os_ref,
              ybuf, sem):
    j = pl.program_id(0)
    tiles = pl.num_programs(0)
    slot = j % 2
    is_prompt = j < tiles_p

    def copies(step, s, act):
        _segment_copies(n_ref, glob_ref, local_ref, ys_hbm, ybuf.at[s], sem.at[s], step, False, act)

    @pl.when(j == 0)
    def _():
        ybuf[...] = jnp.zeros_like(ybuf)
        copies(0, 0, lambda c: c.start())

    copies(j, slot, lambda c: c.wait())

    @pl.when(j + 1 < tiles)
    def _():
        copies(jnp.minimum(j + 1, tiles - 1), 1 - slot, lambda c: c.start())

    pg = pg_ref[...]
    ff = jnp.zeros((ROWS, D_MODEL), F32)
    for c in range(SORT_ROWS // SORT_CHUNK):
        r = (lax.broadcasted_iota(jnp.int32, (ROWS, SORT_CHUNK), 1) + c * SORT_CHUNK).astype(F32)
        w = jnp.zeros((ROWS, SORT_CHUNK), F32)
        for k in range(TOP_K):
            w = w + jnp.where(r == pg[:, k:k + 1], pg[:, TOP_K + k:TOP_K + k + 1], 0.0)
        ff = ff + jnp.dot(w.astype(BF16), ybuf[slot, c * SORT_CHUNK:(c + 1) * SORT_CHUNK, :],
                          preferred_element_type=F32)
    h = jnp.where(is_prompt, hp_ref[...].reshape(ROWS, D_MODEL), hs_ref[...])
    y = _rms(h + g2_ref[0] * ff, gf_ref[...])

    @pl.when(is_prompt)
    def _():
        op_ref[...] = y.reshape(op_ref.shape)

    @pl.when(jnp.logical_not(is_prompt))
    def _():
        os_ref[...] = y


def _final(seg_n, seg_glob, seg_local, h_p, h_s, pg_t, modpat, g_final, ys, tiles_p, tiles_s):
    hp_spec, hs_spec = _tile_specs(h_p.shape[0], tiles_p, tiles_s)
    grid_spec = pltpu.PrefetchScalarGridSpec(
        num_scalar_prefetch=3,
        grid=(tiles_p + tiles_s,),
        in_specs=[hp_spec, hs_spec,
                  pl.BlockSpec((ROWS, 2 * TOP_K), lambda j, *_: (j, 0)),
                  _mod_spec(5, tiles_p), _full((1, D_MODEL)),
                  pl.BlockSpec(memory_space=pl.ANY)],
        out_specs=[hp_spec, hs_spec],
        scratch_shapes=[pltpu.VMEM((2, SORT_ROWS, D_MODEL), BF16), pltpu.SemaphoreType.DMA((2,))],
    )
    return pl.pallas_call(
        functools.partial(_fin_body, tiles_p),
        grid_spec=grid_spec,
        out_shape=[jax.ShapeDtypeStruct(h_p.shape, F32), jax.ShapeDtypeStruct(h_s.shape, F32)],
        compiler_params=_params("arbitrary"),
        name="combine_final",
    )(seg_n, seg_glob, seg_local, h_p, h_s, pg_t, modpat, g_final, ys)


def _block_diag(w):
    h, i, j = w.shape
    return jnp.einsum('hij,hk->hikj', w, jnp.eye(h, dtype=w.dtype)).reshape(h * i, h * j)


def _s5_cols(re, im):
    b = re.shape[0]
    stack = jnp.stack([re.reshape(b, S5_BLOCKS, -1), im.reshape(b, S5_BLOCKS, -1)], axis=2)
    return stack.reshape(b, S5_COLS)


def _s5_uncols(cols):
    b = cols.shape[0]
    c = cols.reshape(b, S5_BLOCKS, 2, S5_GROUPS // S5_BLOCKS, S5_STATE)
    return (c[:, :, 0].reshape(b, S5_GROUPS, S5_STATE), c[:, :, 1].reshape(b, S5_GROUPS, S5_STATE))


def _moe_rows_bound(tiles):
    worst = tiles * (TOP_K * ROWS + N_EXPERTS * (SEG_ALIGN - 1)) + N_EXPERTS * (MOE_TM - SEG_ALIGN)
    return (worst + MOE_TM - 1) // MOE_TM * MOE_TM


def _plan(cnt):
    cnt = cnt.astype(jnp.int32)
    tiles = cnt.shape[0]
    cp = (cnt + SEG_ALIGN - 1) // SEG_ALIGN * SEG_ALIGN
    local = jnp.cumsum(cp, axis=1) - cp
    group = jnp.sum(cp, axis=0)
    group_pad = (group + MOE_TM - 1) // MOE_TM * MOE_TM
    pend = jnp.cumsum(group_pad)
    pstart = pend - group_pad
    glob = pstart[None, :] + jnp.cumsum(cp, axis=0) - cp
    gap = group_pad - group
    seg_n = jnp.concatenate([cp, gap[None]], axis=0).reshape(-1)
    seg_local = jnp.concatenate([local, jnp.zeros((1, N_EXPERTS), jnp.int32)], axis=0).reshape(-1)
    seg_glob = jnp.concatenate([glob, (pstart + group)[None]], axis=0).reshape(-1)
    n_blocks = _moe_rows_bound(tiles) // MOE_TM
    block_row0 = jnp.arange(n_blocks, dtype=jnp.int32) * MOE_TM
    block_expert = jnp.minimum(jnp.sum(block_row0[:, None] >= pend[None, :], axis=1), N_EXPERTS - 1).astype(jnp.int32)
    nb_used = (pend[-1] // MOE_TM).astype(jnp.int32).reshape(1)
    experts = jnp.arange(N_EXPERTS, dtype=jnp.int32)
    later_owner = jnp.where((experts[None, :] > experts[:, None]) & (group_pad[None, :] > 0), experts[None, :],
                            N_EXPERTS)
    next_owner = jnp.min(later_owner, axis=1)
    next_owner = jnp.where(next_owner == N_EXPERTS, -1, next_owner).astype(jnp.int32)
    return seg_n, seg_glob, seg_local, block_expert, next_owner[block_expert], nb_used


def kernel(x_prompt, x_sample, state_s5_re, state_s5_im, state_lru_h, state_conv, c_prompt, c_sample, w_ada, b_ada, g_mix, w_in, s5_a_re, s5_a_im, s5_log_dt, s5_b_re, s5_b_im, s5_c_re, s5_c_im, s5_d, s5_w_glu, s5_b_glu, lru_conv_w, lru_conv_b, lru_w_a, lru_b_a, lru_w_x, lru_b_x, lru_lambda, g_out_s5, g_out_lru, w_out, g_ffn, router_w, router_b, moe_w_gu, moe_b_gu, moe_w_down, moe_b_down, g_final):
    assert w_ada.shape[0] == 1, "one layer"
    bp, lp, _ = x_prompt.shape
    bs, ls, _ = x_sample.shape
    assert ROWS % bp == 0 and ROWS % bs == 0 and (bp * lp) % ROWS == 0 and (bs * ls) % ROWS == 0
    tiles_p = bp * lp // ROWS
    tiles_s = bs * ls // ROWS
    row = lambda v: v.reshape(1, -1)

    ab_re, ab_im, bb_re, bb_im = _s5_prep(s5_a_re[0], s5_a_im[0], s5_log_dt[0].reshape(S5_GROUPS, 1),
                                          jnp.swapaxes(s5_b_re[0], 1, 2), jnp.swapaxes(s5_b_im[0], 1, 2))
    gpb = S5_GROUPS // S5_BLOCKS
    eye = jnp.eye(gpb, dtype=F32)

    def in_blocks(b):
        b = b.reshape(S5_BLOCKS, gpb, S5_GROUP_CH, S5_STATE)
        return jnp.einsum('bgjn,gh->bgjhn', b, eye).reshape(S5_BLOCKS, gpb * S5_GROUP_CH, gpb * S5_STATE)

    def out_blocks(c):
        c = c.reshape(S5_BLOCKS, gpb, S5_GROUP_CH, S5_STATE)
        return jnp.einsum('bgjn,gh->bgnhj', c, eye).reshape(S5_BLOCKS, gpb * S5_STATE, gpb * S5_GROUP_CH)

    ar8 = jnp.broadcast_to(ab_re.reshape(1, -1), (SUBLANES, S5_GROUPS * S5_STATE))
    ai8 = jnp.broadcast_to(ab_im.reshape(1, -1), (SUBLANES, S5_GROUPS * S5_STATE))
    bb = jnp.concatenate([in_blocks(bb_re), in_blocks(bb_im)], axis=-1).astype(BF16)
    cc = jnp.concatenate([out_blocks(s5_c_re[0]), -out_blocks(s5_c_im[0])], axis=1).astype(BF16)
    wa_bd = _block_diag(lru_w_a[0]).astype(BF16)
    wx_bd = _block_diag(lru_w_x[0]).astype(BF16)
    rw_t = router_w[0].T
    rw_hi = rw_t.astype(BF16)
    rw_lo = (rw_t - rw_hi.astype(F32)).astype(BF16)

    tc = ROWS // bp
    mod = _adaln(jnp.concatenate([c_prompt, c_sample], axis=0), w_ada[0], row(b_ada[0]))
    modpat = jnp.stack([jnp.repeat(mod[:bp], tc, axis=0), jnp.tile(mod[bp:], (ROWS // bs, 1))])
    r = jnp.arange(ROWS)
    tm_of = (r % tc) * bp + r // tc
    to_tm_p = (r[:, None] == tm_of[None, :]).astype(BF16)
    ident = jnp.eye(ROWS, dtype=BF16)
    to_tm = jnp.stack([to_tm_p, ident])
    from_tm = jnp.stack([to_tm_p.T, ident])

    x_s = jnp.swapaxes(x_sample, 0, 1).reshape(bs * ls, D_MODEL)
    u, xr, gr = _in_proj(x_prompt, x_s, modpat, to_tm, row(g_mix[0]), w_in[0].astype(BF16), tiles_p, tiles_s)

    def conv_tm(cv):
        return jnp.swapaxes(cv, 0, 1).reshape(-1, D_LRU)

    def conv_bm(cv, b):
        return jnp.swapaxes(cv.reshape(CONV_WIDTH - 1, b, D_LRU), 0, 1)

    s5_args = (ar8, ai8, bb, cc, row(s5_d[0]), s5_w_glu[0].astype(BF16), row(s5_b_glu[0]), row(g_out_s5[0]))
    lru_args = (lru_conv_w[0], row(lru_conv_b[0]), wa_bd, row(lru_b_a[0]), wx_bd, row(lru_b_x[0]),
                row(lru_lambda[0]), row(g_out_lru[0]))
    ys5_p, s5p = _s5_mixer(u, jnp.zeros((bp, S5_COLS), F32), *s5_args, bp, 0, tiles_p)
    ys5_s, s5s = _s5_mixer(u, _s5_cols(state_s5_re[0], state_s5_im[0]), *s5_args, bs, tiles_p, tiles_s)
    ylru_p, hp, cvp = _lru_mixer(xr, gr, jnp.zeros((bp, D_LRU), F32),
                                 jnp.zeros(((CONV_WIDTH - 1) * bp, D_LRU), F32), *lru_args, bp, 0, tiles_p)
    ylru_s, hs, cvs = _lru_mixer(xr, gr, state_lru_h[0], conv_tm(state_conv[0]), *lru_args, bs, tiles_p, tiles_s)

    h_p, h_s, xn, pg, cnt = _out_proj(ys5_p, ys5_s, ylru_p, ylru_s, x_prompt, x_s, modpat, from_tm,
                                      w_out[0].astype(BF16).reshape(2, D_S5, D_MODEL), row(g_ffn[0]), rw_hi, rw_lo,
                                      router_b[0].reshape(N_EXPERTS, 1), tiles_p, tiles_s)

    seg_n, seg_glob, seg_local, block_expert, block_next, nb_used = _plan(cnt[:, :, 0])
    xs = _dispatch(seg_n, seg_glob, seg_local, nb_used, pg, xn, _moe_rows_bound(tiles_p + tiles_s))
    ys = _moe(xs, block_expert, block_next, nb_used, moe_w_gu[0], moe_b_gu[0].reshape(N_EXPERTS, 1, 2 * D_FF),
              moe_w_down[0], moe_b_down[0].reshape(N_EXPERTS, 1, D_MODEL))
    y_prompt, y_s = _final(seg_n, seg_glob, seg_local, h_p, h_s, pg.T, modpat, row(g_final), ys, tiles_p, tiles_s)
    y_sample = jnp.swapaxes(y_s.reshape(ls, bs, D_MODEL), 0, 1)
    s5p_re, s5p_im = _s5_uncols(s5p)
    s5s_re, s5s_im = _s5_uncols(s5s)
    return (y_prompt, y_sample,
            s5p_re[None], s5p_im[None], hp[None], conv_bm(cvp, bp)[None],
            s5s_re[None], s5s_im[None], hs[None], conv_bm(cvs, bs)[None])
```

```python
import functools

import jax
import jax.numpy as jnp
from jax import lax
from jax.experimental import pallas as pl
from jax.experimental.pallas import tpu as pltpu

D_MODEL = 1024
D_S5 = 512
D_LRU = 512
S5_GROUPS = 32
S5_GROUP_CH = 16
S5_STATE = 64
S5_COLS = 2 * S5_GROUPS * S5_STATE
S5_BLOCKS = 4
LRU_HEADS = 8
LRU_HEAD_DIM = 64
LRU_C = 8.0
CONV_WIDTH = 4
N_EXPERTS = 32
TOP_K = 4
D_FF = 1024
SWIGLU_LIMIT = 7.0
SWIGLU_ALPHA = 1.702
N_MOD = 6
EPS = 1e-6

ROWS = 512
MOE_TM = 512
SEG_ALIGN = 16
SORT_ROWS = 2560
SORT_CHUNK = 512
SUBLANES = 8
LANES = 128
VMEM_LIMIT = 48 * 1024 * 1024

BF16 = jnp.bfloat16
F32 = jnp.float32


def _params(*sem):
    return pltpu.CompilerParams(dimension_semantics=sem, vmem_limit_bytes=VMEM_LIMIT)


def _full(shape):
    return pl.BlockSpec(shape, lambda *_: (0,) * len(shape))


def _rms(x, g):
    return x * lax.rsqrt(jnp.mean(x * x, axis=-1, keepdims=True) + EPS) * g


def _gelu(x):
    return 0.5 * x * (1.0 + lax.erf(x * (2.0 ** -0.5)))


def _expm1(x):
    u = jnp.exp(x)
    d = u - 1.0
    return jnp.where(d == 0.0, x, jnp.where(d == -1.0, -1.0, d * x / jnp.log(u)))


def _row_pieces(n, fn):
    off = 0
    bit = ROWS
    while bit >= SEG_ALIGN:
        @pl.when((n & bit) != 0)
        def _(off=off, bit=bit):
            fn(off, bit)
        off = off + (n & bit)
        bit //= 2


def _mod_body(c_ref, w_ref, b_ref, o_ref):
    c = c_ref[...]
    s = (c * jax.nn.sigmoid(c)).astype(BF16)
    o_ref[...] = jnp.dot(s, w_ref[...].astype(BF16), preferred_element_type=F32) + b_ref[...]


def _adaln(c, w_ada, b_ada):
    m = c.shape[0]
    return pl.pallas_call(
        _mod_body,
        grid=(N_MOD,),
        in_specs=[pl.BlockSpec((m, D_MODEL), lambda j: (0, 0)),
                  pl.BlockSpec((D_MODEL, D_MODEL), lambda j: (0, j)),
                  pl.BlockSpec((1, D_MODEL), lambda j: (0, j))],
        out_specs=pl.BlockSpec((m, D_MODEL), lambda j: (0, j)),
        out_shape=jax.ShapeDtypeStruct((m, N_MOD * D_MODEL), F32),
        compiler_params=_params("arbitrary"),
        name="adaln",
    )(c, w_ada, b_ada)


def _mod_spec(k, tiles_p):
    return pl.BlockSpec((1, ROWS, D_MODEL), lambda i, *_: (jnp.where(i < tiles_p, 0, 1), 0, k))


def _perm_spec(tiles_p):
    return pl.BlockSpec((1, ROWS, ROWS), lambda i, *_: (jnp.where(i < tiles_p, 0, 1), 0, 0))


def _tile_specs(bp, tiles_p, tiles_s):
    tc = ROWS // bp
    p_spec = pl.BlockSpec((bp, tc, D_MODEL), lambda i, *_: (0, jnp.minimum(i, tiles_p - 1), 0))
    s_spec = pl.BlockSpec((ROWS, D_MODEL), lambda i, *_: (jnp.clip(i - tiles_p, 0, tiles_s - 1), 0))
    return p_spec, s_spec


def _in_body(tiles_p, xp_ref, xs_ref, sc_ref, sh_ref, perm_ref, g_ref, w_ref, u_ref, xr_ref, gr_ref):
    x = jnp.where(pl.program_id(0) < tiles_p, xp_ref[...].reshape(ROWS, D_MODEL), xs_ref[...])
    xn = _rms(x, g_ref[...]) * (1.0 + sc_ref[0]) + sh_ref[0]
    xn = jnp.dot(perm_ref[0], xn.astype(BF16), preferred_element_type=F32).astype(BF16)
    p = jnp.dot(xn, w_ref[...], preferred_element_type=F32)
    u_ref[...] = p[:, :D_S5]
    xr_ref[...] = p[:, D_S5:D_S5 + D_LRU]
    gr_ref[...] = p[:, D_S5 + D_LRU:]


def _in_proj(x_p, x_s, modpat, to_tm, g_mix, w_in_bf, tiles_p, tiles_s):
    t_rows = (tiles_p + tiles_s) * ROWS
    row_spec = pl.BlockSpec((ROWS, D_S5), lambda i: (i, 0))
    return pl.pallas_call(
        functools.partial(_in_body, tiles_p),
        grid=(tiles_p + tiles_s,),
        in_specs=[*_tile_specs(x_p.shape[0], tiles_p, tiles_s),
                  _mod_spec(1, tiles_p), _mod_spec(0, tiles_p), _perm_spec(tiles_p), _full((1, D_MODEL)),
                  _full((D_MODEL, D_S5 + 2 * D_LRU))],
        out_specs=[row_spec, row_spec, row_spec],
        out_shape=[jax.ShapeDtypeStruct((t_rows, D_S5), F32)] * 3,
        compiler_params=_params("arbitrary"),
        name="in_proj",
    )(x_p, x_s, modpat, modpat, to_tm, g_mix, w_in_bf)


def _s5_prep_body(are_ref, aim_ref, ldt_ref, bre_ref, bim_ref, abre_ref, abim_ref, bbre_ref, bbim_ref):
    a_re = are_ref[...]
    a_im = aim_ref[...]
    dt = jnp.exp(ldt_ref[...])
    mag = jnp.exp(dt * a_re)
    ang = dt * a_im
    ab_re = mag * jnp.cos(ang)
    ab_im = mag * jnp.sin(ang)
    den = a_re * a_re + a_im * a_im
    q_re = ((ab_re - 1.0) * a_re + ab_im * a_im) / den
    q_im = (ab_im * a_re - (ab_re - 1.0) * a_im) / den
    abre_ref[...] = ab_re
    abim_ref[...] = ab_im
    b_re = bre_ref[...]
    b_im = bim_ref[...]
    bbre_ref[...] = q_re[:, None, :] * b_re - q_im[:, None, :] * b_im
    bbim_ref[...] = q_re[:, None, :] * b_im + q_im[:, None, :] * b_re


def _s5_prep(a_re, a_im, log_dt, b_re_t, b_im_t):
    gn = jax.ShapeDtypeStruct((S5_GROUPS, S5_STATE), F32)
    gjn = jax.ShapeDtypeStruct((S5_GROUPS, S5_GROUP_CH, S5_STATE), F32)
    return pl.pallas_call(_s5_prep_body, out_shape=[gn, gn, gjn, gjn], name="s5_prep")(
        a_re, a_im, log_dt, b_re_t, b_im_t)


def _s5_body(tc, bt, chunks, u_ref, s0_ref, ar_ref, ai_ref, bb_ref, cc_ref, d_ref, wg_ref, bg_ref, go_ref,
             y_ref, sout_ref, bu_ref, st_ref):
    half = S5_COLS // S5_BLOCKS // 2

    @pl.when(pl.program_id(0) == 0)
    def _():
        st_ref[...] = s0_ref[...]

    for c in range(chunks):
        ub = u_ref[c * ROWS:(c + 1) * ROWS, :].astype(BF16)
        for j in range(S5_BLOCKS):
            bu_ref[c, :, 2 * half * j:2 * half * (j + 1)] = jnp.dot(
                ub[:, LANES * j:LANES * (j + 1)], bb_ref[j], preferred_element_type=F32)

    for sb in range(bt // SUBLANES):
        base = sb * SUBLANES
        hr = [st_ref[base:base + SUBLANES, 2 * half * j:2 * half * j + half] for j in range(S5_BLOCKS)]
        hi = [st_ref[base:base + SUBLANES, 2 * half * j + half:2 * half * (j + 1)] for j in range(S5_BLOCKS)]
        for c in range(chunks):
            for t in range(tc):
                r0 = t * bt + base
                for j in range(S5_BLOCKS):
                    ar = ar_ref[:, half * j:half * (j + 1)]
                    ai = ai_ref[:, half * j:half * (j + 1)]
                    b_re = bu_ref[c, r0:r0 + SUBLANES, 2 * half * j:2 * half * j + half]
                    b_im = bu_ref[c, r0:r0 + SUBLANES, 2 * half * j + half:2 * half * (j + 1)]
                    h_re = ar * hr[j] - ai * hi[j] + b_re
                    h_im = ar * hi[j] + ai * hr[j] + b_im
                    bu_ref[c, r0:r0 + SUBLANES, 2 * half * j:2 * half * j + half] = h_re
                    bu_ref[c, r0:r0 + SUBLANES, 2 * half * j + half:2 * half * (j + 1)] = h_im
                    hr[j], hi[j] = h_re, h_im
        for j in range(S5_BLOCKS):
            st_ref[base:base + SUBLANES, 2 * half * j:2 * half * j + half] = hr[j]
            st_ref[base:base + SUBLANES, 2 * half * j + half:2 * half * (j + 1)] = hi[j]
    sout_ref[...] = st_ref[...]

    for c in range(chunks):
        y = jnp.concatenate(
            [jnp.dot(bu_ref[c, :, 2 * half * j:2 * half * (j + 1)].astype(BF16), cc_ref[j],
                     preferred_element_type=F32) for j in range(S5_BLOCKS)], axis=-1)
        y = y + d_ref[...] * u_ref[c * ROWS:(c + 1) * ROWS, :]
        g = _gelu(y)
        z = jnp.dot(g.astype(BF16), wg_ref[...], preferred_element_type=F32) + bg_ref[...]
        out = g * jax.nn.sigmoid(z)
        y_ref[c * ROWS:(c + 1) * ROWS, :] = _rms(out, go_ref[...]).astype(BF16)


def _s5_mixer(u, s0, ar8, ai8, bb, cc, d, w_glu_bf, b_glu, g_out, bt, tile0, tiles, chunks):
    tc = ROWS // bt
    assert tiles % chunks == 0 and tile0 % chunks == 0
    step_rows = chunks * ROWS
    return pl.pallas_call(
        functools.partial(_s5_body, tc, bt, chunks),
        grid=(tiles // chunks,),
        in_specs=[pl.BlockSpec((step_rows, D_S5), lambda i: (i + tile0 // chunks, 0)),
                  _full((bt, S5_COLS)), _full(ar8.shape), _full(ai8.shape), _full(bb.shape), _full(cc.shape),
                  _full((1, D_S5)), _full((D_S5, D_S5)), _full((1, D_S5)), _full((1, D_S5))],
        out_specs=[pl.BlockSpec((step_rows, D_S5), lambda i: (i, 0)), _full((bt, S5_COLS))],
        out_shape=[jax.ShapeDtypeStruct((tiles * ROWS, D_S5), BF16), jax.ShapeDtypeStruct((bt, S5_COLS), F32)],
        scratch_shapes=[pltpu.VMEM((chunks, ROWS, S5_COLS), F32), pltpu.VMEM((bt, S5_COLS), F32)],
        compiler_params=_params("arbitrary"),
        name="s5_mixer",
    )(u, s0, ar8, ai8, bb, cc, d, w_glu_bf, b_glu, g_out)


def _lru_body(tc, bt, xr_ref, gr_ref, h0_ref, cv0_ref, cw_ref, cb_ref, wa_ref, ba_ref, wx_ref, bx_ref,
              lam_ref, go_ref, y_ref, hout_ref, cvout_ref, xp_ref, a_ref, b_ref, h_ref):
    halo = (CONV_WIDTH - 1) * bt

    @pl.when(pl.program_id(0) == 0)
    def _():
        xp_ref[0:halo, :] = cv0_ref[...]
        h_ref[...] = h0_ref[...]

    xp_ref[halo:, :] = xr_ref[...]
    xc = cb_ref[...] + sum(xp_ref[k * bt:k * bt + ROWS, :] * cw_ref[k:k + 1, :] for k in range(CONV_WIDTH))
    xcb = xc.astype(BF16)
    r = jax.nn.sigmoid(jnp.dot(xcb, wa_ref[...], preferred_element_type=F32) + ba_ref[...])
    i = jax.nn.sigmoid(jnp.dot(xcb, wx_ref[...], preferred_element_type=F32) + bx_ref[...])
    lam = lam_ref[...]
    softplus_neg_lam = jnp.maximum(-lam, 0.0) + jnp.log1p(jnp.exp(-jnp.abs(lam)))
    log_a = -LRU_C * r * softplus_neg_lam
    a_ref[...] = jnp.exp(log_a)
    b_ref[...] = jnp.sqrt(-_expm1(2.0 * log_a)) * (i * xc)

    def sb_loop(sb, carry):
        base = pl.multiple_of(sb * SUBLANES, SUBLANES)

        def t_loop(t, h):
            r0 = pl.multiple_of(t * bt + base, SUBLANES)
            h = a_ref[pl.ds(r0, SUBLANES), :] * h + b_ref[pl.ds(r0, SUBLANES), :]
            b_ref[pl.ds(r0, SUBLANES), :] = h
            return h

        h_ref[pl.ds(base, SUBLANES), :] = lax.fori_loop(0, tc, t_loop, h_ref[pl.ds(base, SUBLANES), :])
        return carry

    lax.fori_loop(0, bt // SUBLANES, sb_loop, 0)

    y = b_ref[...] * _gelu(gr_ref[...])
    y_ref[...] = _rms(y, go_ref[...]).astype(BF16)
    tail = xp_ref[ROWS:ROWS + halo, :]
    xp_ref[0:halo, :] = tail
    cvout_ref[...] = tail
    hout_ref[...] = h_ref[...]


def _lru_mixer(xr, gr, h0, conv0_tm, conv_w, conv_b, wa_bd, b_a, wx_bd, b_x, lam, g_out, bt, tile0, tiles):
    tc = ROWS // bt
    halo = (CONV_WIDTH - 1) * bt
    in_rows = pl.BlockSpec((ROWS, D_LRU), lambda i: (i + tile0, 0))
    vec = _full((1, D_LRU))
    return pl.pallas_call(
        functools.partial(_lru_body, tc, bt),
        grid=(tiles,),
        in_specs=[in_rows, in_rows, _full((bt, D_LRU)), _full((halo, D_LRU)), _full((CONV_WIDTH, D_LRU)), vec,
                  _full((D_LRU, D_LRU)), vec, _full((D_LRU, D_LRU)), vec, vec, vec],
        out_specs=[pl.BlockSpec((ROWS, D_LRU), lambda i: (i, 0)), _full((bt, D_LRU)), _full((halo, D_LRU))],
        out_shape=[jax.ShapeDtypeStruct((tiles * ROWS, D_LRU), BF16), jax.ShapeDtypeStruct((bt, D_LRU), F32),
                   jax.ShapeDtypeStruct((halo, D_LRU), F32)],
        scratch_shapes=[pltpu.VMEM((ROWS + halo, D_LRU), F32), pltpu.VMEM((ROWS, D_LRU), F32),
                        pltpu.VMEM((ROWS, D_LRU), F32), pltpu.VMEM((bt, D_LRU), F32)],
        compiler_params=_params("arbitrary"),
        name="lru_mixer",
    )(xr, gr, h0, conv0_tm, conv_w, conv_b, wa_bd, b_a, wx_bd, b_x, lam, g_out)


def _out_body(tiles_p, ysp_ref, yss_ref, ylp_ref, yls_ref, xp_ref, xs_ref, g1_ref, sc_ref, sh_ref, perm_ref,
              wo_ref, gf_ref, rwh_ref, rwl_ref, rb_ref, hp_ref, hs_ref, xn_ref, pg_ref, cnt_ref):
    is_prompt = pl.program_id(0) < tiles_p
    ys = jnp.where(is_prompt, ysp_ref[...], yss_ref[...])
    yl = jnp.where(is_prompt, ylp_ref[...], yls_ref[...])
    ys = jnp.dot(perm_ref[0], ys, preferred_element_type=F32).astype(BF16)
    yl = jnp.dot(perm_ref[0], yl, preferred_element_type=F32).astype(BF16)
    mixed = (jnp.dot(ys, wo_ref[0], preferred_element_type=F32) + jnp.dot(yl, wo_ref[1], preferred_element_type=F32))
    x = jnp.where(is_prompt, xp_ref[...].reshape(ROWS, D_MODEL), xs_ref[...])
    h = x + g1_ref[0] * mixed

    @pl.when(is_prompt)
    def _():
        hp_ref[...] = h.reshape(hp_ref.shape)

    @pl.when(jnp.logical_not(is_prompt))
    def _():
        hs_ref[...] = h

    xn = _rms(h, gf_ref[...]) * (1.0 + sc_ref[0]) + sh_ref[0]
    x_hi = xn.astype(BF16)
    xn_ref[...] = x_hi

    x_lo = (xn - x_hi.astype(F32)).astype(BF16)
    nt = (((1,), (1,)), ((), ()))
    logits = (lax.dot_general(rwh_ref[...], x_hi, nt, preferred_element_type=F32)
              + lax.dot_general(rwh_ref[...], x_lo, nt, preferred_element_type=F32)
              + lax.dot_general(rwl_ref[...], x_hi, nt, preferred_element_type=F32)) + rb_ref[...]

    e_iota = lax.broadcasted_iota(jnp.int32, (N_EXPERTS, ROWS), 0).astype(F32)
    work = logits
    sels, vals = [], []
    for _ in range(TOP_K):
        m = jnp.max(work, axis=0, keepdims=True)
        idx = jnp.min(jnp.where(work == m, e_iota, float(N_EXPERTS)), axis=0, keepdims=True)
        sel = e_iota == idx
        work = jnp.where(sel, -jnp.inf, work)
        sels.append(sel)
        vals.append(m)
    exps = [jnp.exp(v - vals[0]) for v in vals]
    denom = exps[0] + exps[1] + exps[2] + exps[3]
    gates = [e / denom for e in exps]

    onehot = sels[0] | sels[1] | sels[2] | sels[3]
    rr = lax.broadcasted_iota(jnp.int32, (ROWS, ROWS), 0)
    cc = lax.broadcasted_iota(jnp.int32, (ROWS, ROWS), 1)
    before = (rr < cc).astype(BF16)
    prefix = jnp.dot(onehot.astype(BF16), before, preferred_element_type=F32)
    cnt = jnp.sum(onehot.astype(F32), axis=1, keepdims=True)
    cnt_pad = jnp.floor((cnt + (SEG_ALIGN - 1)) * (1.0 / SEG_ALIGN)) * SEG_ALIGN
    er = lax.broadcasted_iota(jnp.int32, (N_EXPERTS, N_EXPERTS), 0)
    ec = lax.broadcasted_iota(jnp.int32, (N_EXPERTS, N_EXPERTS), 1)
    seg_start = jnp.dot((ec < er).astype(BF16), jnp.broadcast_to(cnt_pad, (N_EXPERTS, LANES)).astype(BF16),
                        preferred_element_type=F32)[:, 0:1]
    where_to = prefix + seg_start
    poss = [jnp.sum(jnp.where(s, where_to, 0.0), axis=0, keepdims=True) for s in sels]

    s_iota = lax.broadcasted_iota(jnp.int32, (2 * TOP_K, ROWS), 0)
    pg = jnp.zeros((2 * TOP_K, ROWS), F32)
    for k in range(TOP_K):
        pg = jnp.where(s_iota == k, poss[k], pg)
        pg = jnp.where(s_iota == TOP_K + k, gates[k], pg)
    pg_ref[...] = pg
    cnt_ref[0] = jnp.broadcast_to(cnt, (N_EXPERTS, LANES))


def _out_proj(ys5_p, ys5_s, ylru_p, ylru_s, x_p, x_s, modpat, from_tm, w_out_bf, g_ffn, rw_hi, rw_lo, router_b,
              tiles_p, tiles_s):
    tiles = tiles_p + tiles_s
    t_rows = tiles * ROWS
    p_half = pl.BlockSpec((ROWS, D_S5), lambda i: (jnp.minimum(i, tiles_p - 1), 0))
    s_half = pl.BlockSpec((ROWS, D_S5), lambda i: (jnp.clip(i - tiles_p, 0, tiles_s - 1), 0))
    xp_spec, xs_spec = _tile_specs(x_p.shape[0], tiles_p, tiles_s)
    return pl.pallas_call(
        functools.partial(_out_body, tiles_p),
        grid=(tiles,),
        in_specs=[p_half, s_half, p_half, s_half, xp_spec, xs_spec,
                  _mod_spec(2, tiles_p), _mod_spec(4, tiles_p), _mod_spec(3, tiles_p), _perm_spec(tiles_p),
                  _full((2, D_S5, D_MODEL)), _full((1, D_MODEL)),
                  _full((N_EXPERTS, D_MODEL)), _full((N_EXPERTS, D_MODEL)), _full((N_EXPERTS, 1))],
        out_specs=[xp_spec, xs_spec, pl.BlockSpec((ROWS, D_MODEL), lambda i: (i, 0)),
                   pl.BlockSpec((2 * TOP_K, ROWS), lambda i: (0, i)),
                   pl.BlockSpec((1, N_EXPERTS, LANES), lambda i: (i, 0, 0))],
        out_shape=[jax.ShapeDtypeStruct(x_p.shape, F32), jax.ShapeDtypeStruct(x_s.shape, F32),
                   jax.ShapeDtypeStruct((t_rows, D_MODEL), BF16),
                   jax.ShapeDtypeStruct((2 * TOP_K, t_rows), F32),
                   jax.ShapeDtypeStruct((tiles, N_EXPERTS, LANES), F32)],
        compiler_params=_params("arbitrary"),
        name="out_proj_router",
    )(ys5_p, ys5_s, ylru_p, ylru_s, x_p, x_s, modpat, modpat, modpat, from_tm, w_out_bf, g_ffn, rw_hi, rw_lo,
      router_b)


def _segment_copies(n_ref, hbm_ref, vmem_ref, hbm, vmem_slot, sem, step, to_hbm, act):
    def seg(e, carry):
        n = n_ref[step * N_EXPERTS + e]
        h0 = hbm_ref[step * N_EXPERTS + e]
        v0 = vmem_ref[step * N_EXPERTS + e]

        def piece(off, size):
            h = hbm.at[pl.ds(pl.multiple_of(h0 + off, SEG_ALIGN), size)]
            v = vmem_slot.at[pl.ds(pl.multiple_of(v0 + off, SEG_ALIGN), size)]
            act(pltpu.make_async_copy(v, h, sem) if to_hbm else pltpu.make_async_copy(h, v, sem))

        _row_pieces(n, piece)
        return carry

    lax.fori_loop(0, N_EXPERTS, seg, 0)


def _dispatch_body(n_ref, glob_ref, local_ref, nb_ref, pg_ref, x_ref, xs_hbm, stage, sem):
    j = pl.program_id(0)
    last = pl.num_programs(0) - 1
    slot = j % 2

    def unused_blocks(act):
        def blk(b, carry):
            act(pltpu.make_async_copy(stage.at[slot, pl.ds(0, MOE_TM)],
                                      xs_hbm.at[pl.ds(pl.multiple_of(b * MOE_TM, MOE_TM), MOE_TM)], sem.at[slot]))
            return carry
        lax.fori_loop(nb_ref[0], xs_hbm.shape[0] // MOE_TM, blk, 0)

    def copies(step, s, act):
        _segment_copies(n_ref, glob_ref, local_ref, xs_hbm, stage.at[s], sem.at[s], step, True, act)

    @pl.when(j >= 2)
    def _():
        copies(jnp.maximum(j - 2, 0), slot, lambda c: c.wait())

    @pl.when(j < last)
    def _():
        x = x_ref[...]
        pos = pg_ref[0:TOP_K, :]
        for c in range(SORT_ROWS // SORT_CHUNK):
            r = (lax.broadcasted_iota(jnp.int32, (SORT_CHUNK, ROWS), 0) + c * SORT_CHUNK).astype(F32)
            pick = (r == pos[0:1]) | (r == pos[1:2]) | (r == pos[2:3]) | (r == pos[3:4])
            stage[slot, c * SORT_CHUNK:(c + 1) * SORT_CHUNK, :] = jnp.dot(
                pick.astype(BF16), x, preferred_element_type=F32).astype(BF16)

    @pl.when(j == last)
    def _():
        stage[slot, 0:MOE_TM, :] = jnp.zeros((MOE_TM, D_MODEL), BF16)

    copies(j, slot, lambda c: c.start())

    @pl.when(j == last)
    def _():
        unused_blocks(lambda c: c.start())

        @pl.when(j >= 1)
        def _():
            copies(jnp.maximum(j - 1, 0), 1 - slot, lambda c: c.wait())
        copies(j, slot, lambda c: c.wait())
        unused_blocks(lambda c: c.wait())


def _dispatch(seg_n, seg_glob, seg_local, nb_used, pg, xn, n_rows):
    tiles = xn.shape[0] // ROWS
    grid_spec = pltpu.PrefetchScalarGridSpec(
        num_scalar_prefetch=4,
        grid=(tiles + 1,),
        in_specs=[pl.BlockSpec((2 * TOP_K, ROWS), lambda j, *_: (0, jnp.minimum(j, tiles - 1))),
                  pl.BlockSpec((ROWS, D_MODEL), lambda j, *_: (jnp.minimum(j, tiles - 1), 0))],
        out_specs=pl.BlockSpec(memory_space=pl.ANY),
        scratch_shapes=[pltpu.VMEM((2, SORT_ROWS, D_MODEL), BF16), pltpu.SemaphoreType.DMA((2,))],
    )
    return pl.pallas_call(
        _dispatch_body,
        grid_spec=grid_spec,
        out_shape=jax.ShapeDtypeStruct((n_rows, D_MODEL), BF16),
        compiler_params=_params("arbitrary"),
        name="moe_dispatch",
    )(seg_n, seg_glob, seg_local, nb_used, pg, xn)


def _moe_body(be_ref, nxt_ref, nv_ref, nb_ref, xs_ref, wgu_hbm, bgu_ref, wd_hbm, bd_ref, ys_ref,
              wgu_f32, wd_f32, wgu_bf, wd_bf, sem):
    i = pl.program_id(0)

    def weight_copies(e):
        return (pltpu.make_async_copy(wgu_hbm.at[e], wgu_f32, sem.at[0]),
                pltpu.make_async_copy(wd_hbm.at[e], wd_f32, sem.at[1]))

    @pl.when(i >= nb_ref[0])
    def _():
        ys_ref[...] = jnp.zeros_like(ys_ref)

    @pl.when(i < nb_ref[0])
    def _():
        e = be_ref[i]

        @pl.when(i == 0)
        def _():
            for c in weight_copies(e):
                c.start()

        @pl.when(jnp.logical_or(i == 0, e != be_ref[jnp.maximum(i - 1, 0)]))
        def _():
            for c in weight_copies(e):
                c.wait()
            wgu_bf[...] = wgu_f32[...].astype(BF16)
            wd_bf[...] = wd_f32[...].astype(BF16)

            @pl.when(nxt_ref[i] >= 0)
            def _():
                for c in weight_copies(nxt_ref[i]):
                    c.start()

        def ffn(x):
            hg = jnp.dot(x, wgu_bf[...], preferred_element_type=F32) + bgu_ref[0]
            gate = jnp.minimum(hg[:, :D_FF], SWIGLU_LIMIT)
            up = jnp.clip(hg[:, D_FF:], -SWIGLU_LIMIT, SWIGLU_LIMIT)
            act = (up + 1.0) * (gate * jax.nn.sigmoid(SWIGLU_ALPHA * gate))
            return (jnp.dot(act.astype(BF16), wd_bf[...], preferred_element_type=F32) + bd_ref[0]).astype(BF16)

        half = MOE_TM // 2

        @pl.when(nv_ref[i] > half)
        def _():
            ys_ref[...] = ffn(xs_ref[...])

        @pl.when(nv_ref[i] <= half)
        def _():
            ys_ref[0:half, :] = ffn(xs_ref[0:half, :])
            ys_ref[half:, :] = jnp.zeros((half, D_MODEL), BF16)


def _moe(xs, block_expert, block_next, block_rows, nb_used, w_gu, b_gu, w_down, b_down):
    n_blocks = xs.shape[0] // MOE_TM
    in_rows = pl.BlockSpec((MOE_TM, D_MODEL), lambda i, be, nx, nv, nb: (jnp.minimum(i, nb[0] - 1), 0))
    grid_spec = pltpu.PrefetchScalarGridSpec(
        num_scalar_prefetch=4,
        grid=(n_blocks,),
        in_specs=[in_rows,
                  pl.BlockSpec(memory_space=pl.ANY),
                  pl.BlockSpec((1, 1, 2 * D_FF), lambda i, be, nx, nv, nb: (be[i], 0, 0)),
                  pl.BlockSpec(memory_space=pl.ANY),
                  pl.BlockSpec((1, 1, D_MODEL), lambda i, be, nx, nv, nb: (be[i], 0, 0))],
        out_specs=pl.BlockSpec((MOE_TM, D_MODEL), lambda i, be, nx, nv, nb: (i, 0)),
        scratch_shapes=[pltpu.VMEM((D_MODEL, 2 * D_FF), F32), pltpu.VMEM((D_FF, D_MODEL), F32),
                        pltpu.VMEM((D_MODEL, 2 * D_FF), BF16), pltpu.VMEM((D_FF, D_MODEL), BF16),
                        pltpu.SemaphoreType.DMA((2,))],
    )
    return pl.pallas_call(
        _moe_body,
        grid_spec=grid_spec,
        out_shape=jax.ShapeDtypeStruct(xs.shape, BF16),
        compiler_params=_params("arbitrary"),
        name="moe_experts",
    )(block_expert, block_next, block_rows, nb_used, xs, w_gu, b_gu, w_down, b_down)


def _fin_body(tiles_p, n_ref, glob_ref, local_ref, hp_ref, hs_ref, pg_ref, g2_ref, gf_ref, ys_hbm, op_ref, os_ref,
              ybuf, sem):
    j = pl.program_id(0)
    tiles = pl.num_programs(0)
    slot = j % 2
    is_prompt = j < tiles_p

    def copies(step, s, act):
        _segment_copies(n_ref, glob_ref, local_ref, ys_hbm, ybuf.at[s], sem.at[s], step, False, act)

    @pl.when(j == 0)
    def _():
        ybuf[...] = jnp.zeros_like(ybuf)
        copies(0, 0, lambda c: c.start())

    copies(j, slot, lambda c: c.wait())

    @pl.when(j + 1 < tiles)
    def _():
        copies(jnp.minimum(j + 1, tiles - 1), 1 - slot, lambda c: c.start())

    pg = pg_ref[...]
    ff = jnp.zeros((ROWS, D_MODEL), F32)
    for c in range(SORT_ROWS // SORT_CHUNK):
        r = (lax.broadcasted_iota(jnp.int32, (ROWS, SORT_CHUNK), 1) + c * SORT_CHUNK).astype(F32)
        w = jnp.zeros((ROWS, SORT_CHUNK), F32)
        for k in range(TOP_K):
            w = w + jnp.where(r == pg[:, k:k + 1], pg[:, TOP_K + k:TOP_K + k + 1], 0.0)
        ff = ff + jnp.dot(w.astype(BF16), ybuf[slot, c * SORT_CHUNK:(c + 1) * SORT_CHUNK, :],
                          preferred_element_type=F32)
    h = jnp.where(is_prompt, hp_ref[...].reshape(ROWS, D_MODEL), hs_ref[...])
    y = _rms(h + g2_ref[0] * ff, gf_ref[...])

    @pl.when(is_prompt)
    def _():
        op_ref[...] = y.reshape(op_ref.shape)

    @pl.when(jnp.logical_not(is_prompt))
    def _():
        os_ref[...] = y


def _final(seg_n, seg_glob, seg_local, h_p, h_s, pg_t, modpat, g_final, ys, tiles_p, tiles_s):
    hp_spec, hs_spec = _tile_specs(h_p.shape[0], tiles_p, tiles_s)
    grid_spec = pltpu.PrefetchScalarGridSpec(
        num_scalar_prefetch=3,
        grid=(tiles_p + tiles_s,),
        in_specs=[hp_spec, hs_spec,
                  pl.BlockSpec((ROWS, 2 * TOP_K), lambda j, *_: (j, 0)),
                  _mod_spec(5, tiles_p), _full((1, D_MODEL)),
                  pl.BlockSpec(memory_space=pl.ANY)],
        out_specs=[hp_spec, hs_spec],
        scratch_shapes=[pltpu.VMEM((2, SORT_ROWS, D_MODEL), BF16), pltpu.SemaphoreType.DMA((2,))],
    )
    return pl.pallas_call(
        functools.partial(_fin_body, tiles_p),
        grid_spec=grid_spec,
        out_shape=[jax.ShapeDtypeStruct(h_p.shape, F32), jax.ShapeDtypeStruct(h_s.shape, F32)],
        compiler_params=_params("arbitrary"),
        name="combine_final",
    )(seg_n, seg_glob, seg_local, h_p, h_s, pg_t, modpat, g_final, ys)


def _block_diag(w):
    h, i, j = w.shape
    return jnp.einsum('hij,hk->hikj', w, jnp.eye(h, dtype=w.dtype)).reshape(h * i, h * j)


def _s5_cols(re, im):
    b = re.shape[0]
    stack = jnp.stack([re.reshape(b, S5_BLOCKS, -1), im.reshape(b, S5_BLOCKS, -1)], axis=2)
    return stack.reshape(b, S5_COLS)


def _s5_uncols(cols):
    b = cols.shape[0]
    c = cols.reshape(b, S5_BLOCKS, 2, S5_GROUPS // S5_BLOCKS, S5_STATE)
    return (c[:, :, 0].reshape(b, S5_GROUPS, S5_STATE), c[:, :, 1].reshape(b, S5_GROUPS, S5_STATE))


def _moe_rows_bound(tiles):
    worst = tiles * (TOP_K * ROWS + N_EXPERTS * (SEG_ALIGN - 1)) + N_EXPERTS * (MOE_TM - SEG_ALIGN)
    return (worst + MOE_TM - 1) // MOE_TM * MOE_TM


def _plan(cnt):
    cnt = cnt.astype(jnp.int32)
    tiles = cnt.shape[0]
    cp = (cnt + SEG_ALIGN - 1) // SEG_ALIGN * SEG_ALIGN
    local = jnp.cumsum(cp, axis=1) - cp
    group = jnp.sum(cp, axis=0)
    group_pad = (group + MOE_TM - 1) // MOE_TM * MOE_TM
    pend = jnp.cumsum(group_pad)
    pstart = pend - group_pad
    glob = pstart[None, :] + jnp.cumsum(cp, axis=0) - cp
    gap = group_pad - group
    seg_n = jnp.concatenate([cp, gap[None]], axis=0).reshape(-1)
    seg_local = jnp.concatenate([local, jnp.zeros((1, N_EXPERTS), jnp.int32)], axis=0).reshape(-1)
    seg_glob = jnp.concatenate([glob, (pstart + group)[None]], axis=0).reshape(-1)
    n_blocks = _moe_rows_bound(tiles) // MOE_TM
    block_row0 = jnp.arange(n_blocks, dtype=jnp.int32) * MOE_TM
    block_expert = jnp.minimum(jnp.sum(block_row0[:, None] >= pend[None, :], axis=1), N_EXPERTS - 1).astype(jnp.int32)
    nb_used = (pend[-1] // MOE_TM).astype(jnp.int32).reshape(1)
    experts = jnp.arange(N_EXPERTS, dtype=jnp.int32)
    later_owner = jnp.where((experts[None, :] > experts[:, None]) & (group_pad[None, :] > 0), experts[None, :],
                            N_EXPERTS)
    next_owner = jnp.min(later_owner, axis=1)
    next_owner = jnp.where(next_owner == N_EXPERTS, -1, next_owner).astype(jnp.int32)
    block_rows = jnp.clip((pstart + group)[block_expert] - block_row0, 0, MOE_TM).astype(jnp.int32)
    return seg_n, seg_glob, seg_local, block_expert, next_owner[block_expert], block_rows, nb_used


def kernel(x_prompt, x_sample, state_s5_re, state_s5_im, state_lru_h, state_conv, c_prompt, c_sample, w_ada, b_ada, g_mix, w_in, s5_a_re, s5_a_im, s5_log_dt, s5_b_re, s5_b_im, s5_c_re, s5_c_im, s5_d, s5_w_glu, s5_b_glu, lru_conv_w, lru_conv_b, lru_w_a, lru_b_a, lru_w_x, lru_b_x, lru_lambda, g_out_s5, g_out_lru, w_out, g_ffn, router_w, router_b, moe_w_gu, moe_b_gu, moe_w_down, moe_b_down, g_final):
    assert w_ada.shape[0] == 1, "one layer"
    bp, lp, _ = x_prompt.shape
    bs, ls, _ = x_sample.shape
    assert ROWS % bp == 0 and ROWS % bs == 0 and (bp * lp) % ROWS == 0 and (bs * ls) % ROWS == 0
    tiles_p = bp * lp // ROWS
    tiles_s = bs * ls // ROWS
    row = lambda v: v.reshape(1, -1)

    ab_re, ab_im, bb_re, bb_im = _s5_prep(s5_a_re[0], s5_a_im[0], s5_log_dt[0].reshape(S5_GROUPS, 1),
                                          jnp.swapaxes(s5_b_re[0], 1, 2), jnp.swapaxes(s5_b_im[0], 1, 2))
    gpb = S5_GROUPS // S5_BLOCKS
    eye = jnp.eye(gpb, dtype=F32)

    def in_blocks(b):
        b = b.reshape(S5_BLOCKS, gpb, S5_GROUP_CH, S5_STATE)
        return jnp.einsum('bgjn,gh->bgjhn', b, eye).reshape(S5_BLOCKS, gpb * S5_GROUP_CH, gpb * S5_STATE)

    def out_blocks(c):
        c = c.reshape(S5_BLOCKS, gpb, S5_GROUP_CH, S5_STATE)
        return jnp.einsum('bgjn,gh->bgnhj', c, eye).reshape(S5_BLOCKS, gpb * S5_STATE, gpb * S5_GROUP_CH)

    ar8 = jnp.broadcast_to(ab_re.reshape(1, -1), (SUBLANES, S5_GROUPS * S5_STATE))
    ai8 = jnp.broadcast_to(ab_im.reshape(1, -1), (SUBLANES, S5_GROUPS * S5_STATE))
    bb = jnp.concatenate([in_blocks(bb_re), in_blocks(bb_im)], axis=-1).astype(BF16)
    cc = jnp.concatenate([out_blocks(s5_c_re[0]), -out_blocks(s5_c_im[0])], axis=1).astype(BF16)
    wa_bd = _block_diag(lru_w_a[0]).astype(BF16)
    wx_bd = _block_diag(lru_w_x[0]).astype(BF16)
    rw_t = router_w[0].T
    rw_hi = rw_t.astype(BF16)
    rw_lo = (rw_t - rw_hi.astype(F32)).astype(BF16)

    tc = ROWS // bp
    mod = _adaln(jnp.concatenate([c_prompt, c_sample], axis=0), w_ada[0], row(b_ada[0]))
    modpat = jnp.stack([jnp.repeat(mod[:bp], tc, axis=0), jnp.tile(mod[bp:], (ROWS // bs, 1))])
    r = jnp.arange(ROWS)
    tm_of = (r % tc) * bp + r // tc
    to_tm_p = (r[:, None] == tm_of[None, :]).astype(BF16)
    ident = jnp.eye(ROWS, dtype=BF16)
    to_tm = jnp.stack([to_tm_p, ident])
    from_tm = jnp.stack([to_tm_p.T, ident])

    x_s = jnp.swapaxes(x_sample, 0, 1).reshape(bs * ls, D_MODEL)
    u, xr, gr = _in_proj(x_prompt, x_s, modpat, to_tm, row(g_mix[0]), w_in[0].astype(BF16), tiles_p, tiles_s)

    def conv_tm(cv):
        return jnp.swapaxes(cv, 0, 1).reshape(-1, D_LRU)

    def conv_bm(cv, b):
        return jnp.swapaxes(cv.reshape(CONV_WIDTH - 1, b, D_LRU), 0, 1)

    s5_args = (ar8, ai8, bb, cc, row(s5_d[0]), s5_w_glu[0].astype(BF16), row(s5_b_glu[0]), row(g_out_s5[0]))
    lru_args = (lru_conv_w[0], row(lru_conv_b[0]), wa_bd, row(lru_b_a[0]), wx_bd, row(lru_b_x[0]),
                row(lru_lambda[0]), row(g_out_lru[0]))
    s5_chunks_p = 2 if tiles_p % 2 == 0 else 1
    ys5_p, s5p = _s5_mixer(u, jnp.zeros((bp, S5_COLS), F32), *s5_args, bp, 0, tiles_p, s5_chunks_p)
    ys5_s, s5s = _s5_mixer(u, _s5_cols(state_s5_re[0], state_s5_im[0]), *s5_args, bs, tiles_p, tiles_s, 1)
    ylru_p, hp, cvp = _lru_mixer(xr, gr, jnp.zeros((bp, D_LRU), F32),
                                 jnp.zeros(((CONV_WIDTH - 1) * bp, D_LRU), F32), *lru_args, bp, 0, tiles_p)
    ylru_s, hs, cvs = _lru_mixer(xr, gr, state_lru_h[0], conv_tm(state_conv[0]), *lru_args, bs, tiles_p, tiles_s)

    h_p, h_s, xn, pg, cnt = _out_proj(ys5_p, ys5_s, ylru_p, ylru_s, x_prompt, x_s, modpat, from_tm,
                                      w_out[0].astype(BF16).reshape(2, D_S5, D_MODEL), row(g_ffn[0]), rw_hi, rw_lo,
                                      router_b[0].reshape(N_EXPERTS, 1), tiles_p, tiles_s)

    seg_n, seg_glob, seg_local, block_expert, block_next, block_rows, nb_used = _plan(cnt[:, :, 0])
    xs = _dispatch(seg_n, seg_glob, seg_local, nb_used, pg, xn, _moe_rows_bound(tiles_p + tiles_s))
    ys = _moe(xs, block_expert, block_next, block_rows, nb_used, moe_w_gu[0], moe_b_gu[0].reshape(N_EXPERTS, 1, 2 * D_FF),
              moe_w_down[0], moe_b_down[0].reshape(N_EXPERTS, 1, D_MODEL))
    y_prompt, y_s = _final(seg_n, seg_glob, seg_local, h_p, h_s, pg.T, modpat, row(g_final), ys, tiles_p, tiles_s)
    y_sample = jnp.swapaxes(y_s.reshape(ls, bs, D_MODEL), 0, 1)
    s5p_re, s5p_im = _s5_uncols(s5p)
    s5s_re, s5s_im = _s5_uncols(s5s)
    return (y_prompt, y_sample,
            s5p_re[None], s5p_im[None], hp[None], conv_bm(cvp, bp)[None],
            s5s_re[None], s5s_im[None], hs[None], conv_bm(cvs, bs)[None])
```

```python
import functools

import jax
import jax.numpy as jnp
from jax import lax
from jax.experimental import pallas as pl
from jax.experimental.pallas import tpu as pltpu

D_MODEL = 1024
D_S5 = 512
D_LRU = 512
S5_GROUPS = 32
S5_GROUP_CH = 16
S5_STATE = 64
S5_COLS = 2 * S5_GROUPS * S5_STATE
S5_BLOCKS = 4
LRU_HEADS = 8
LRU_HEAD_DIM = 64
LRU_C = 8.0
CONV_WIDTH = 4
N_EXPERTS = 32
TOP_K = 4
D_FF = 1024
SWIGLU_LIMIT = 7.0
SWIGLU_ALPHA = 1.702
N_MOD = 6
EPS = 1e-6

ROWS = 512
MOE_TM = 512
BIG_PIECE = 128
WAIT_MAX_PIECE = 8192
SEG_ALIGN = 16
SORT_ROWS = 2560
SORT_CHUNK = 512
SUBLANES = 8
LANES = 128
VMEM_LIMIT = 48 * 1024 * 1024

BF16 = jnp.bfloat16
F32 = jnp.float32


def _params(*sem):
    return pltpu.CompilerParams(dimension_semantics=sem, vmem_limit_bytes=VMEM_LIMIT)


def _full(shape):
    return pl.BlockSpec(shape, lambda *_: (0,) * len(shape))


def _rms(x, g):
    return x * lax.rsqrt(jnp.mean(x * x, axis=-1, keepdims=True) + EPS) * g


def _gelu(x):
    return 0.5 * x * (1.0 + lax.erf(x * (2.0 ** -0.5)))


def _expm1(x):
    u = jnp.exp(x)
    d = u - 1.0
    return jnp.where(d == 0.0, x, jnp.where(d == -1.0, -1.0, d * x / jnp.log(u)))


def _row_pieces(n, largest, fn):
    off = 0
    bit = largest
    while bit >= SEG_ALIGN:
        @pl.when((n & bit) != 0)
        def _(off=off, bit=bit):
            fn(off, bit)
        off = off + (n & bit)
        bit //= 2


def _mod_body(bp, c_ref, w_ref, b_ref, o_ref):
    c = c_ref[...]
    s = (c * jax.nn.sigmoid(c)).astype(BF16)
    mod = jnp.dot(s, w_ref[...].astype(BF16), preferred_element_type=F32) + b_ref[...]
    bs = mod.shape[0] - bp
    o_ref[0] = jnp.broadcast_to(mod[:bp][:, None, :], (bp, ROWS // bp, D_MODEL)).reshape(ROWS, D_MODEL)
    o_ref[1] = jnp.broadcast_to(mod[bp:][None], (ROWS // bs, bs, D_MODEL)).reshape(ROWS, D_MODEL)


def _adaln(c, w_ada, b_ada, bp):
    m = c.shape[0]
    return pl.pallas_call(
        functools.partial(_mod_body, bp),
        grid=(N_MOD,),
        in_specs=[pl.BlockSpec((m, D_MODEL), lambda j: (0, 0)),
                  pl.BlockSpec((D_MODEL, D_MODEL), lambda j: (0, j)),
                  pl.BlockSpec((1, D_MODEL), lambda j: (0, j))],
        out_specs=pl.BlockSpec((2, ROWS, D_MODEL), lambda j: (0, 0, j)),
        out_shape=jax.ShapeDtypeStruct((2, ROWS, N_MOD * D_MODEL), F32),
        compiler_params=_params("arbitrary"),
        name="adaln",
    )(c, w_ada, b_ada)


def _mod_spec(k, tiles_p):
    return pl.BlockSpec((1, ROWS, D_MODEL), lambda i, *_: (jnp.where(i < tiles_p, 0, 1), 0, k))


def _perm_spec(tiles_p):
    return pl.BlockSpec((1, ROWS, ROWS), lambda i, *_: (jnp.where(i < tiles_p, 0, 1), 0, 0))


def _tile_specs(bp, tiles_p, tiles_s):
    tc = ROWS // bp
    p_spec = pl.BlockSpec((bp, tc, D_MODEL), lambda i, *_: (0, jnp.minimum(i, tiles_p - 1), 0))
    s_spec = pl.BlockSpec((ROWS, D_MODEL), lambda i, *_: (jnp.clip(i - tiles_p, 0, tiles_s - 1), 0))
    return p_spec, s_spec


def _in_body(tiles_p, xp_ref, xs_ref, sc_ref, sh_ref, perm_ref, g_ref, w_ref, u_ref, xr_ref, gr_ref):
    x = jnp.where(pl.program_id(0) < tiles_p, xp_ref[...].reshape(ROWS, D_MODEL), xs_ref[...])
    xn = _rms(x, g_ref[...]) * (1.0 + sc_ref[0]) + sh_ref[0]
    xn = jnp.dot(perm_ref[0], xn.astype(BF16), preferred_element_type=F32).astype(BF16)
    p = jnp.dot(xn, w_ref[...], preferred_element_type=F32)
    u_ref[...] = p[:, :D_S5]
    xr_ref[...] = p[:, D_S5:D_S5 + D_LRU]
    gr_ref[...] = p[:, D_S5 + D_LRU:]


def _in_proj(x_p, x_s, modpat, to_tm, g_mix, w_in_bf, tiles_p, tiles_s):
    t_rows = (tiles_p + tiles_s) * ROWS
    row_spec = pl.BlockSpec((ROWS, D_S5), lambda i: (i, 0))
    return pl.pallas_call(
        functools.partial(_in_body, tiles_p),
        grid=(tiles_p + tiles_s,),
        in_specs=[*_tile_specs(x_p.shape[0], tiles_p, tiles_s),
                  _mod_spec(1, tiles_p), _mod_spec(0, tiles_p), _perm_spec(tiles_p), _full((1, D_MODEL)),
                  _full((D_MODEL, D_S5 + 2 * D_LRU))],
        out_specs=[row_spec, row_spec, row_spec],
        out_shape=[jax.ShapeDtypeStruct((t_rows, D_S5), F32)] * 3,
        compiler_params=_params("arbitrary"),
        name="in_proj",
    )(x_p, x_s, modpat, modpat, to_tm, g_mix, w_in_bf)


def _s5_prep_body(are_ref, aim_ref, ldt_ref, bre_ref, bim_ref, abre_ref, abim_ref, bbre_ref, bbim_ref):
    a_re = are_ref[...]
    a_im = aim_ref[...]
    dt = jnp.exp(ldt_ref[...])
    mag = jnp.exp(dt * a_re)
    ang = dt * a_im
    ab_re = mag * jnp.cos(ang)
    ab_im = mag * jnp.sin(ang)
    den = a_re * a_re + a_im * a_im
    q_re = ((ab_re - 1.0) * a_re + ab_im * a_im) / den
    q_im = (ab_im * a_re - (ab_re - 1.0) * a_im) / den
    abre_ref[...] = ab_re
    abim_ref[...] = ab_im
    b_re = bre_ref[...]
    b_im = bim_ref[...]
    bbre_ref[...] = q_re[:, None, :] * b_re - q_im[:, None, :] * b_im
    bbim_ref[...] = q_re[:, None, :] * b_im + q_im[:, None, :] * b_re


def _s5_prep(a_re, a_im, log_dt, b_re_t, b_im_t):
    gn = jax.ShapeDtypeStruct((S5_GROUPS, S5_STATE), F32)
    gjn = jax.ShapeDtypeStruct((S5_GROUPS, S5_GROUP_CH, S5_STATE), F32)
    return pl.pallas_call(_s5_prep_body, out_shape=[gn, gn, gjn, gjn], name="s5_prep")(
        a_re, a_im, log_dt, b_re_t, b_im_t)


def _s5_body(tc, bt, chunks, u_ref, s0_ref, ar_ref, ai_ref, bb_ref, cc_ref, d_ref, wg_ref, bg_ref, go_ref,
             y_ref, sout_ref, bu_ref, st_ref):
    half = S5_COLS // S5_BLOCKS // 2

    @pl.when(pl.program_id(0) == 0)
    def _():
        st_ref[...] = s0_ref[...]

    for c in range(chunks):
        ub = u_ref[c * ROWS:(c + 1) * ROWS, :].astype(BF16)
        for j in range(S5_BLOCKS):
            bu_ref[c, :, 2 * half * j:2 * half * (j + 1)] = jnp.dot(
                ub[:, LANES * j:LANES * (j + 1)], bb_ref[j], preferred_element_type=F32)

    for sb in range(bt // SUBLANES):
        base = sb * SUBLANES
        hr = [st_ref[base:base + SUBLANES, 2 * half * j:2 * half * j + half] for j in range(S5_BLOCKS)]
        hi = [st_ref[base:base + SUBLANES, 2 * half * j + half:2 * half * (j + 1)] for j in range(S5_BLOCKS)]
        for c in range(chunks):
            for t in range(tc):
                r0 = t * bt + base
                for j in range(S5_BLOCKS):
                    ar = ar_ref[:, half * j:half * (j + 1)]
                    ai = ai_ref[:, half * j:half * (j + 1)]
                    b_re = bu_ref[c, r0:r0 + SUBLANES, 2 * half * j:2 * half * j + half]
                    b_im = bu_ref[c, r0:r0 + SUBLANES, 2 * half * j + half:2 * half * (j + 1)]
                    h_re = ar * hr[j] - ai * hi[j] + b_re
                    h_im = ar * hi[j] + ai * hr[j] + b_im
                    bu_ref[c, r0:r0 + SUBLANES, 2 * half * j:2 * half * j + half] = h_re
                    bu_ref[c, r0:r0 + SUBLANES, 2 * half * j + half:2 * half * (j + 1)] = h_im
                    hr[j], hi[j] = h_re, h_im
        for j in range(S5_BLOCKS):
            st_ref[base:base + SUBLANES, 2 * half * j:2 * half * j + half] = hr[j]
            st_ref[base:base + SUBLANES, 2 * half * j + half:2 * half * (j + 1)] = hi[j]
    sout_ref[...] = st_ref[...]

    for c in range(chunks):
        y = jnp.concatenate(
            [jnp.dot(bu_ref[c, :, 2 * half * j:2 * half * (j + 1)].astype(BF16), cc_ref[j],
                     preferred_element_type=F32) for j in range(S5_BLOCKS)], axis=-1)
        y = y + d_ref[...] * u_ref[c * ROWS:(c + 1) * ROWS, :]
        g = _gelu(y)
        z = jnp.dot(g.astype(BF16), wg_ref[...], preferred_element_type=F32) + bg_ref[...]
        out = g * jax.nn.sigmoid(z)
        y_ref[c * ROWS:(c + 1) * ROWS, :] = _rms(out, go_ref[...]).astype(BF16)


def _s5_mixer(u, s0, ar8, ai8, bb, cc, d, w_glu_bf, b_glu, g_out, bt, tile0, tiles, chunks):
    tc = ROWS // bt
    assert tiles % chunks == 0 and tile0 % chunks == 0
    step_rows = chunks * ROWS
    return pl.pallas_call(
        functools.partial(_s5_body, tc, bt, chunks),
        grid=(tiles // chunks,),
        in_specs=[pl.BlockSpec((step_rows, D_S5), lambda i: (i + tile0 // chunks, 0)),
                  _full((bt, S5_COLS)), _full(ar8.shape), _full(ai8.shape), _full(bb.shape), _full(cc.shape),
                  _full((1, D_S5)), _full((D_S5, D_S5)), _full((1, D_S5)), _full((1, D_S5))],
        out_specs=[pl.BlockSpec((step_rows, D_S5), lambda i: (i, 0)), _full((bt, S5_COLS))],
        out_shape=[jax.ShapeDtypeStruct((tiles * ROWS, D_S5), BF16), jax.ShapeDtypeStruct((bt, S5_COLS), F32)],
        scratch_shapes=[pltpu.VMEM((chunks, ROWS, S5_COLS), F32), pltpu.VMEM((bt, S5_COLS), F32)],
        compiler_params=_params("arbitrary"),
        name="s5_mixer",
    )(u, s0, ar8, ai8, bb, cc, d, w_glu_bf, b_glu, g_out)


def _lru_body(tc, bt, xr_ref, gr_ref, h0_ref, cv0_ref, cw_ref, cb_ref, wa_ref, ba_ref, wx_ref, bx_ref,
              lam_ref, go_ref, y_ref, hout_ref, cvout_ref, xp_ref, a_ref, b_ref, h_ref):
    halo = (CONV_WIDTH - 1) * bt

    @pl.when(pl.program_id(0) == 0)
    def _():
        xp_ref[0:halo, :] = cv0_ref[...]
        h_ref[...] = h0_ref[...]

    xp_ref[halo:, :] = xr_ref[...]
    xc = cb_ref[...] + sum(xp_ref[k * bt:k * bt + ROWS, :] * cw_ref[k:k + 1, :] for k in range(CONV_WIDTH))
    xcb = xc.astype(BF16)
    r = jax.nn.sigmoid(jnp.dot(xcb, wa_ref[...], preferred_element_type=F32) + ba_ref[...])
    i = jax.nn.sigmoid(jnp.dot(xcb, wx_ref[...], preferred_element_type=F32) + bx_ref[...])
    lam = lam_ref[...]
    softplus_neg_lam = jnp.maximum(-lam, 0.0) + jnp.log1p(jnp.exp(-jnp.abs(lam)))
    log_a = -LRU_C * r * softplus_neg_lam
    a_ref[...] = jnp.exp(log_a)
    b_ref[...] = jnp.sqrt(-_expm1(2.0 * log_a)) * (i * xc)

    def sb_loop(sb, carry):
        base = pl.multiple_of(sb * SUBLANES, SUBLANES)

        def t_loop(t, h):
            r0 = pl.multiple_of(t * bt + base, SUBLANES)
            h = a_ref[pl.ds(r0, SUBLANES), :] * h + b_ref[pl.ds(r0, SUBLANES), :]
            b_ref[pl.ds(r0, SUBLANES), :] = h
            return h

        h_ref[pl.ds(base, SUBLANES), :] = lax.fori_loop(0, tc, t_loop, h_ref[pl.ds(base, SUBLANES), :])
        return carry

    lax.fori_loop(0, bt // SUBLANES, sb_loop, 0)

    y = b_ref[...] * _gelu(gr_ref[...])
    y_ref[...] = _rms(y, go_ref[...]).astype(BF16)
    tail = xp_ref[ROWS:ROWS + halo, :]
    xp_ref[0:halo, :] = tail
    cvout_ref[...] = tail
    hout_ref[...] = h_ref[...]


def _lru_mixer(xr, gr, h0, conv0_tm, conv_w, conv_b, wa_bd, b_a, wx_bd, b_x, lam, g_out, bt, tile0, tiles):
    tc = ROWS // bt
    halo = (CONV_WIDTH - 1) * bt
    in_rows = pl.BlockSpec((ROWS, D_LRU), lambda i: (i + tile0, 0))
    vec = _full((1, D_LRU))
    return pl.pallas_call(
        functools.partial(_lru_body, tc, bt),
        grid=(tiles,),
        in_specs=[in_rows, in_rows, _full((bt, D_LRU)), _full((halo, D_LRU)), _full((CONV_WIDTH, D_LRU)), vec,
                  _full((D_LRU, D_LRU)), vec, _full((D_LRU, D_LRU)), vec, vec, vec],
        out_specs=[pl.BlockSpec((ROWS, D_LRU), lambda i: (i, 0)), _full((bt, D_LRU)), _full((halo, D_LRU))],
        out_shape=[jax.ShapeDtypeStruct((tiles * ROWS, D_LRU), BF16), jax.ShapeDtypeStruct((bt, D_LRU), F32),
                   jax.ShapeDtypeStruct((halo, D_LRU), F32)],
        scratch_shapes=[pltpu.VMEM((ROWS + halo, D_LRU), F32), pltpu.VMEM((ROWS, D_LRU), F32),
                        pltpu.VMEM((ROWS, D_LRU), F32), pltpu.VMEM((bt, D_LRU), F32)],
        compiler_params=_params("arbitrary"),
        name="lru_mixer",
    )(xr, gr, h0, conv0_tm, conv_w, conv_b, wa_bd, b_a, wx_bd, b_x, lam, g_out)


def _route_tile(xn, x_hi, rw_hi, rw_lo, rb):
    x_lo = (xn - x_hi.astype(F32)).astype(BF16)
    nt = (((1,), (1,)), ((), ()))
    logits = (lax.dot_general(rw_hi, x_hi, nt, preferred_element_type=F32)
              + lax.dot_general(rw_hi, x_lo, nt, preferred_element_type=F32)
              + lax.dot_general(rw_lo, x_hi, nt, preferred_element_type=F32)) + rb

    e_iota = lax.broadcasted_iota(jnp.int32, (N_EXPERTS, ROWS), 0).astype(F32)
    work = logits
    sels, vals = [], []
    for _ in range(TOP_K):
        m = jnp.max(work, axis=0, keepdims=True)
        idx = jnp.min(jnp.where(work == m, e_iota, float(N_EXPERTS)), axis=0, keepdims=True)
        sel = e_iota == idx
        work = jnp.where(sel, -jnp.inf, work)
        sels.append(sel)
        vals.append(m)
    exps = [jnp.exp(v - vals[0]) for v in vals]
    denom = exps[0] + exps[1] + exps[2] + exps[3]
    gates = [e / denom for e in exps]

    onehot = sels[0] | sels[1] | sels[2] | sels[3]
    rr = lax.broadcasted_iota(jnp.int32, (ROWS, ROWS), 0)
    cc = lax.broadcasted_iota(jnp.int32, (ROWS, ROWS), 1)
    before = (rr < cc).astype(BF16)
    prefix = jnp.dot(onehot.astype(BF16), before, preferred_element_type=F32)
    cnt = jnp.sum(onehot.astype(F32), axis=1, keepdims=True)
    cnt_pad = jnp.floor((cnt + (SEG_ALIGN - 1)) * (1.0 / SEG_ALIGN)) * SEG_ALIGN
    er = lax.broadcasted_iota(jnp.int32, (N_EXPERTS, N_EXPERTS), 0)
    ec = lax.broadcasted_iota(jnp.int32, (N_EXPERTS, N_EXPERTS), 1)
    seg_start = jnp.dot((ec < er).astype(BF16), jnp.broadcast_to(cnt_pad, (N_EXPERTS, LANES)).astype(BF16),
                        preferred_element_type=F32)[:, 0:1]
    where_to = prefix + seg_start
    poss = [jnp.sum(jnp.where(s, where_to, 0.0), axis=0, keepdims=True) for s in sels]

    s_iota = lax.broadcasted_iota(jnp.int32, (2 * TOP_K, ROWS), 0)
    pg = jnp.zeros((2 * TOP_K, ROWS), F32)
    for k in range(TOP_K):
        pg = jnp.where(s_iota == k, poss[k], pg)
        pg = jnp.where(s_iota == TOP_K + k, gates[k], pg)
    return pg, cnt


def _out_body(subtiles, tm_rows, *refs):
    if tm_rows:
        ys_ref, yl_ref, x_ref, g1_ref, sc_ref, sh_ref, perm_ref, *refs = refs
    else:
        ys_ref, yl_ref, x_ref, g1_ref, sc_ref, sh_ref, *refs = refs
    wo_ref, gf_ref, rwh_ref, rwl_ref, rb_ref, h_ref, xn_ref, pg_ref, cnt_ref = refs
    for c in range(subtiles):
        rows = slice(c * ROWS, (c + 1) * ROWS)
        ys = ys_ref[rows, :]
        yl = yl_ref[rows, :]
        if tm_rows:
            ys = jnp.dot(perm_ref[...], ys, preferred_element_type=F32).astype(BF16)
            yl = jnp.dot(perm_ref[...], yl, preferred_element_type=F32).astype(BF16)
            tc = x_ref.shape[1] // subtiles
            x = x_ref[:, c * tc:(c + 1) * tc, :].reshape(ROWS, D_MODEL)
        else:
            x = x_ref[rows, :]
        mixed = (jnp.dot(ys, wo_ref[0], preferred_element_type=F32)
                 + jnp.dot(yl, wo_ref[1], preferred_element_type=F32))
        h = x + g1_ref[0] * mixed
        if tm_rows:
            h_ref[:, c * tc:(c + 1) * tc, :] = h.reshape(x_ref.shape[0], tc, D_MODEL)
        else:
            h_ref[rows, :] = h
        xn = _rms(h, gf_ref[...]) * (1.0 + sc_ref[0]) + sh_ref[0]
        x_hi = xn.astype(BF16)
        xn_ref[rows, :] = x_hi
        pg, cnt = _route_tile(xn, x_hi, rwh_ref[...], rwl_ref[...], rb_ref[...])
        pg_ref[:, rows] = pg
        cnt_ref[c] = jnp.broadcast_to(cnt, (N_EXPERTS, LANES))


def _out_proj(ys5, ylru, x, modpat, pattern, from_tm, w_out_bf, g_ffn, rw_hi, rw_lo, router_b, subtiles):
    tm_rows = from_tm is not None
    t_rows = ys5.shape[0]
    step_rows = subtiles * ROWS
    half_spec = pl.BlockSpec((step_rows, D_S5), lambda i: (i, 0))
    if tm_rows:
        x_spec = pl.BlockSpec((x.shape[0], step_rows // x.shape[0], D_MODEL), lambda i: (0, i, 0))
    else:
        x_spec = pl.BlockSpec((step_rows, D_MODEL), lambda i: (i, 0))
    mod_spec = lambda k: pl.BlockSpec((1, ROWS, D_MODEL), lambda i: (pattern, 0, k))
    perm = [_full((ROWS, ROWS))] if tm_rows else []
    return pl.pallas_call(
        functools.partial(_out_body, subtiles, tm_rows),
        grid=(t_rows // step_rows,),
        in_specs=[half_spec, half_spec, x_spec, mod_spec(2), mod_spec(4), mod_spec(3), *perm,
                  _full((2, D_S5, D_MODEL)), _full((1, D_MODEL)),
                  _full((N_EXPERTS, D_MODEL)), _full((N_EXPERTS, D_MODEL)), _full((N_EXPERTS, 1))],
        out_specs=[x_spec, pl.BlockSpec((step_rows, D_MODEL), lambda i: (i, 0)),
                   pl.BlockSpec((2 * TOP_K, step_rows), lambda i: (0, i)),
                   pl.BlockSpec((subtiles, N_EXPERTS, LANES), lambda i: (i, 0, 0))],
        out_shape=[jax.ShapeDtypeStruct(x.shape, F32), jax.ShapeDtypeStruct((t_rows, D_MODEL), BF16),
                   jax.ShapeDtypeStruct((2 * TOP_K, t_rows), F32),
                   jax.ShapeDtypeStruct((t_rows // ROWS, N_EXPERTS, LANES), F32)],
        compiler_params=_params("arbitrary"),
        name="out_proj_router",
    )(ys5, ylru, x, modpat, modpat, modpat, *([from_tm] if tm_rows else []), w_out_bf, g_ffn, rw_hi, rw_lo,
      router_b)


def _start_segments(n_ref, hbm_ref, vmem_ref, hbm, vmem_slot, sem, step, to_hbm):
    def seg(e, carry):
        n = n_ref[step * N_EXPERTS + e]
        h0 = hbm_ref[step * N_EXPERTS + e]
        v0 = vmem_ref[step * N_EXPERTS + e]

        def piece(off, size):
            h = hbm.at[pl.ds(pl.multiple_of(h0 + off, SEG_ALIGN), size)]
            v = vmem_slot.at[pl.ds(pl.multiple_of(v0 + off, SEG_ALIGN), size)]
            (pltpu.make_async_copy(v, h, sem) if to_hbm else pltpu.make_async_copy(h, v, sem)).start()

        def big(k, c):
            piece(k * BIG_PIECE, BIG_PIECE)
            return c

        lax.fori_loop(0, n // BIG_PIECE, big, 0)
        _row_pieces(n % BIG_PIECE, BIG_PIECE // 2, lambda off, size: piece(n // BIG_PIECE * BIG_PIECE + off, size))
        return carry

    lax.fori_loop(0, N_EXPERTS, seg, 0)


def _wait_rows(total, largest, hbm, sem):
    _row_pieces(total, largest, lambda off, size: pltpu.make_async_copy(
        hbm.at[pl.ds(0, size)], hbm.at[pl.ds(0, size)], sem).wait())


def _dispatch_body(tiles_p, n_ref, glob_ref, local_ref, tot_ref, nb_ref, pg_ref, xp_ref, xsm_ref, xs_hbm, stage,
                   sem):
    j = pl.program_id(0)
    last = pl.num_programs(0) - 1
    slot = j % 2

    def unused_blocks(act):
        def blk(b, carry):
            act(pltpu.make_async_copy(stage.at[slot, pl.ds(0, MOE_TM)],
                                      xs_hbm.at[pl.ds(pl.multiple_of(b * MOE_TM, MOE_TM), MOE_TM)], sem.at[slot]))
            return carry
        lax.fori_loop(nb_ref[0], xs_hbm.shape[0] // MOE_TM, blk, 0)

    def wait_step(step, s):
        _wait_rows(tot_ref[step], WAIT_MAX_PIECE, xs_hbm, sem.at[s])

    @pl.when(j >= 2)
    def _():
        wait_step(jnp.maximum(j - 2, 0), slot)

    @pl.when(j < last)
    def _():
        x = jnp.where(j < tiles_p, xp_ref[...], xsm_ref[...])
        pos = pg_ref[0:TOP_K, :]
        for c in range(SORT_ROWS // SORT_CHUNK):
            r = (lax.broadcasted_iota(jnp.int32, (SORT_CHUNK, ROWS), 0) + c * SORT_CHUNK).astype(F32)
            pick = (r == pos[0:1]) | (r == pos[1:2]) | (r == pos[2:3]) | (r == pos[3:4])
            stage[slot, c * SORT_CHUNK:(c + 1) * SORT_CHUNK, :] = jnp.dot(
                pick.astype(BF16), x, preferred_element_type=F32).astype(BF16)

    @pl.when(j == last)
    def _():
        stage[slot, 0:MOE_TM, :] = jnp.zeros((MOE_TM, D_MODEL), BF16)

    _start_segments(n_ref, glob_ref, local_ref, xs_hbm, stage.at[slot], sem.at[slot], j, True)

    @pl.when(j == last)
    def _():
        unused_blocks(lambda c: c.start())

        @pl.when(j >= 1)
        def _():
            wait_step(jnp.maximum(j - 1, 0), 1 - slot)
        wait_step(j, slot)
        unused_blocks(lambda c: c.wait())


def _dispatch(seg_n, seg_glob, seg_local, seg_tot, nb_used, pg, xn_p, xn_s, n_rows):
    tiles_p = xn_p.shape[0] // ROWS
    tiles_s = xn_s.shape[0] // ROWS
    tiles = tiles_p + tiles_s
    grid_spec = pltpu.PrefetchScalarGridSpec(
        num_scalar_prefetch=5,
        grid=(tiles + 1,),
        in_specs=[pl.BlockSpec((2 * TOP_K, ROWS), lambda j, *_: (0, jnp.minimum(j, tiles - 1))),
                  pl.BlockSpec((ROWS, D_MODEL), lambda j, *_: (jnp.minimum(j, tiles_p - 1), 0)),
                  pl.BlockSpec((ROWS, D_MODEL), lambda j, *_: (jnp.clip(j - tiles_p, 0, tiles_s - 1), 0))],
        out_specs=pl.BlockSpec(memory_space=pl.ANY),
        scratch_shapes=[pltpu.VMEM((2, SORT_ROWS, D_MODEL), BF16), pltpu.SemaphoreType.DMA((2,))],
    )
    return pl.pallas_call(
        functools.partial(_dispatch_body, tiles_p),
        grid_spec=grid_spec,
        out_shape=jax.ShapeDtypeStruct((n_rows, D_MODEL), BF16),
        compiler_params=_params("arbitrary"),
        name="moe_dispatch",
    )(seg_n, seg_glob, seg_local, seg_tot, nb_used, pg, xn_p, xn_s)


def _moe_body(be_ref, nxt_ref, nv_ref, nb_ref, xs_ref, wgu_hbm, bgu_ref, wd_hbm, bd_ref, ys_ref,
              wgu_f32, wd_f32, wgu_bf, wd_bf, sem):
    i = pl.program_id(0)

    def weight_copies(e):
        return (pltpu.make_async_copy(wgu_hbm.at[e], wgu_f32, sem.at[0]),
                pltpu.make_async_copy(wd_hbm.at[e], wd_f32, sem.at[1]))

    @pl.when(i >= nb_ref[0])
    def _():
        ys_ref[...] = jnp.zeros_like(ys_ref)

    @pl.when(i < nb_ref[0])
    def _():
        e = be_ref[i]

        @pl.when(i == 0)
        def _():
            for c in weight_copies(e):
                c.start()

        @pl.when(jnp.logical_or(i == 0, e != be_ref[jnp.maximum(i - 1, 0)]))
        def _():
            for c in weight_copies(e):
                c.wait()
            wgu_bf[...] = wgu_f32[...].astype(BF16)
            wd_bf[...] = wd_f32[...].astype(BF16)

            @pl.when(nxt_ref[i] >= 0)
            def _():
                for c in weight_copies(nxt_ref[i]):
                    c.start()

        def ffn(x):
            hg = jnp.dot(x, wgu_bf[...], preferred_element_type=F32) + bgu_ref[0]
            gate = jnp.minimum(hg[:, :D_FF], SWIGLU_LIMIT)
            up = jnp.clip(hg[:, D_FF:], -SWIGLU_LIMIT, SWIGLU_LIMIT)
            act = (up + 1.0) * (gate * jax.nn.sigmoid(SWIGLU_ALPHA * gate))
            return (jnp.dot(act.astype(BF16), wd_bf[...], preferred_element_type=F32) + bd_ref[0]).astype(BF16)

        half = MOE_TM // 2

        @pl.when(nv_ref[i] > half)
        def _():
            ys_ref[...] = ffn(xs_ref[...])

        @pl.when(nv_ref[i] <= half)
        def _():
            ys_ref[0:half, :] = ffn(xs_ref[0:half, :])
            ys_ref[half:, :] = jnp.zeros((half, D_MODEL), BF16)


def _moe(xs, block_expert, block_next, block_rows, nb_used, w_gu, b_gu, w_down, b_down):
    n_blocks = xs.shape[0] // MOE_TM
    in_rows = pl.BlockSpec((MOE_TM, D_MODEL), lambda i, be, nx, nv, nb: (jnp.minimum(i, nb[0] - 1), 0))
    grid_spec = pltpu.PrefetchScalarGridSpec(
        num_scalar_prefetch=4,
        grid=(n_blocks,),
        in_specs=[in_rows,
                  pl.BlockSpec(memory_space=pl.ANY),
                  pl.BlockSpec((1, 1, 2 * D_FF), lambda i, be, nx, nv, nb: (be[i], 0, 0)),
                  pl.BlockSpec(memory_space=pl.ANY),
                  pl.BlockSpec((1, 1, D_MODEL), lambda i, be, nx, nv, nb: (be[i], 0, 0))],
        out_specs=pl.BlockSpec((MOE_TM, D_MODEL), lambda i, be, nx, nv, nb: (i, 0)),
        scratch_shapes=[pltpu.VMEM((D_MODEL, 2 * D_FF), F32), pltpu.VMEM((D_FF, D_MODEL), F32),
                        pltpu.VMEM((D_MODEL, 2 * D_FF), BF16), pltpu.VMEM((D_FF, D_MODEL), BF16),
                        pltpu.SemaphoreType.DMA((2,))],
    )
    return pl.pallas_call(
        _moe_body,
        grid_spec=grid_spec,
        out_shape=jax.ShapeDtypeStruct(xs.shape, BF16),
        compiler_params=_params("arbitrary"),
        name="moe_experts",
    )(block_expert, block_next, block_rows, nb_used, xs, w_gu, b_gu, w_down, b_down)


def _fin_body(tiles_p, n_ref, glob_ref, local_ref, tot_ref, hp_ref, hs_ref, pg_ref, g2_ref, gf_ref, ys_hbm,
              op_ref, os_ref, ybuf, sem):
    j = pl.program_id(0)
    tiles = pl.num_programs(0)
    slot = j % 2
    is_prompt = j < tiles_p

    def start_gather(step, s):
        _start_segments(n_ref, glob_ref, local_ref, ys_hbm, ybuf.at[s], sem.at[s], step, False)

    @pl.when(j == 0)
    def _():
        ybuf[...] = jnp.zeros_like(ybuf)
        start_gather(0, 0)

    _wait_rows(tot_ref[j], WAIT_MAX_PIECE, ys_hbm, sem.at[slot])

    @pl.when(j + 1 < tiles)
    def _():
        start_gather(jnp.minimum(j + 1, tiles - 1), 1 - slot)

    pg = pg_ref[...]
    ff = jnp.zeros((ROWS, D_MODEL), F32)
    for c in range(SORT_ROWS // SORT_CHUNK):
        r = (lax.broadcasted_iota(jnp.int32, (ROWS, SORT_CHUNK), 1) + c * SORT_CHUNK).astype(F32)
        w = jnp.zeros((ROWS, SORT_CHUNK), F32)
        for k in range(TOP_K):
            w = jnp.where(r == pg[:, k:k + 1], pg[:, TOP_K + k:TOP_K + k + 1], w)
        ff = ff + jnp.dot(w.astype(BF16), ybuf[slot, c * SORT_CHUNK:(c + 1) * SORT_CHUNK, :],
                          preferred_element_type=F32)
    h = jnp.where(is_prompt, hp_ref[...].reshape(ROWS, D_MODEL), hs_ref[...])
    y = _rms(h + g2_ref[0] * ff, gf_ref[...])

    @pl.when(is_prompt)
    def _():
        op_ref[...] = y.reshape(op_ref.shape)

    @pl.when(jnp.logical_not(is_prompt))
    def _():
        os_ref[...] = y


def _final(seg_n, seg_glob, seg_local, seg_tot, h_p, h_s, pg_t, modpat, g_final, ys, tiles_p, tiles_s):
    hp_spec, hs_spec = _tile_specs(h_p.shape[0], tiles_p, tiles_s)
    grid_spec = pltpu.PrefetchScalarGridSpec(
        num_scalar_prefetch=4,
        grid=(tiles_p + tiles_s,),
        in_specs=[hp_spec, hs_spec,
                  pl.BlockSpec((ROWS, 2 * TOP_K), lambda j, *_: (j, 0)),
                  _mod_spec(5, tiles_p), _full((1, D_MODEL)),
                  pl.BlockSpec(memory_space=pl.ANY)],
        out_specs=[hp_spec, hs_spec],
        scratch_shapes=[pltpu.VMEM((2, SORT_ROWS, D_MODEL), BF16), pltpu.SemaphoreType.DMA((2,))],
    )
    return pl.pallas_call(
        functools.partial(_fin_body, tiles_p),
        grid_spec=grid_spec,
        out_shape=[jax.ShapeDtypeStruct(h_p.shape, F32), jax.ShapeDtypeStruct(h_s.shape, F32)],
        compiler_params=_params("arbitrary"),
        name="combine_final",
    )(seg_n, seg_glob, seg_local, seg_tot, h_p, h_s, pg_t, modpat, g_final, ys)


def _block_diag(w):
    h, i, j = w.shape
    return jnp.einsum('hij,hk->hikj', w, jnp.eye(h, dtype=w.dtype)).reshape(h * i, h * j)


def _s5_cols(re, im):
    b = re.shape[0]
    stack = jnp.stack([re.reshape(b, S5_BLOCKS, -1), im.reshape(b, S5_BLOCKS, -1)], axis=2)
    return stack.reshape(b, S5_COLS)


def _s5_uncols(cols):
    b = cols.shape[0]
    c = cols.reshape(b, S5_BLOCKS, 2, S5_GROUPS // S5_BLOCKS, S5_STATE)
    return (c[:, :, 0].reshape(b, S5_GROUPS, S5_STATE), c[:, :, 1].reshape(b, S5_GROUPS, S5_STATE))


def _moe_rows_bound(tiles):
    worst = tiles * (TOP_K * ROWS + N_EXPERTS * (SEG_ALIGN - 1)) + N_EXPERTS * (MOE_TM - SEG_ALIGN)
    return (worst + MOE_TM - 1) // MOE_TM * MOE_TM


def _plan(cnt):
    cnt = cnt.astype(jnp.int32)
    tiles = cnt.shape[0]
    cp = (cnt + SEG_ALIGN - 1) // SEG_ALIGN * SEG_ALIGN
    local = jnp.cumsum(cp, axis=1) - cp
    group = jnp.sum(cp, axis=0)
    group_pad = (group + MOE_TM - 1) // MOE_TM * MOE_TM
    pend = jnp.cumsum(group_pad)
    pstart = pend - group_pad
    glob = pstart[None, :] + jnp.cumsum(cp, axis=0) - cp
    gap = group_pad - group
    seg_n = jnp.concatenate([cp, gap[None]], axis=0).reshape(-1)
    seg_local = jnp.concatenate([local, jnp.zeros((1, N_EXPERTS), jnp.int32)], axis=0).reshape(-1)
    seg_glob = jnp.concatenate([glob, (pstart + group)[None]], axis=0).reshape(-1)
    seg_tot = jnp.concatenate([jnp.sum(cp, axis=1), jnp.sum(gap)[None]]).astype(jnp.int32)
    n_blocks = _moe_rows_bound(tiles) // MOE_TM
    block_row0 = jnp.arange(n_blocks, dtype=jnp.int32) * MOE_TM
    block_expert = jnp.minimum(jnp.sum(block_row0[:, None] >= pend[None, :], axis=1), N_EXPERTS - 1).astype(jnp.int32)
    nb_used = (pend[-1] // MOE_TM).astype(jnp.int32).reshape(1)
    experts = jnp.arange(N_EXPERTS, dtype=jnp.int32)
    later_owner = jnp.where((experts[None, :] > experts[:, None]) & (group_pad[None, :] > 0), experts[None, :],
                            N_EXPERTS)
    next_owner = jnp.min(later_owner, axis=1)
    next_owner = jnp.where(next_owner == N_EXPERTS, -1, next_owner).astype(jnp.int32)
    block_rows = jnp.clip((pstart + group)[block_expert] - block_row0, 0, MOE_TM).astype(jnp.int32)
    return seg_n, seg_glob, seg_local, seg_tot, block_expert, next_owner[block_expert], block_rows, nb_used


def kernel(x_prompt, x_sample, state_s5_re, state_s5_im, state_lru_h, state_conv, c_prompt, c_sample, w_ada, b_ada, g_mix, w_in, s5_a_re, s5_a_im, s5_log_dt, s5_b_re, s5_b_im, s5_c_re, s5_c_im, s5_d, s5_w_glu, s5_b_glu, lru_conv_w, lru_conv_b, lru_w_a, lru_b_a, lru_w_x, lru_b_x, lru_lambda, g_out_s5, g_out_lru, w_out, g_ffn, router_w, router_b, moe_w_gu, moe_b_gu, moe_w_down, moe_b_down, g_final):
    assert w_ada.shape[0] == 1, "one layer"
    bp, lp, _ = x_prompt.shape
    bs, ls, _ = x_sample.shape
    assert ROWS % bp == 0 and ROWS % bs == 0 and (bp * lp) % ROWS == 0 and (bs * ls) % ROWS == 0
    tiles_p = bp * lp // ROWS
    tiles_s = bs * ls // ROWS
    row = lambda v: v.reshape(1, -1)

    ab_re, ab_im, bb_re, bb_im = _s5_prep(s5_a_re[0], s5_a_im[0], s5_log_dt[0].reshape(S5_GROUPS, 1),
                                          jnp.swapaxes(s5_b_re[0], 1, 2), jnp.swapaxes(s5_b_im[0], 1, 2))
    gpb = S5_GROUPS // S5_BLOCKS
    eye = jnp.eye(gpb, dtype=F32)

    def in_blocks(b):
        b = b.reshape(S5_BLOCKS, gpb, S5_GROUP_CH, S5_STATE)
        return jnp.einsum('bgjn,gh->bgjhn', b, eye).reshape(S5_BLOCKS, gpb * S5_GROUP_CH, gpb * S5_STATE)

    def out_blocks(c):
        c = c.reshape(S5_BLOCKS, gpb, S5_GROUP_CH, S5_STATE)
        return jnp.einsum('bgjn,gh->bgnhj', c, eye).reshape(S5_BLOCKS, gpb * S5_STATE, gpb * S5_GROUP_CH)

    ar8 = jnp.broadcast_to(ab_re.reshape(1, -1), (SUBLANES, S5_GROUPS * S5_STATE))
    ai8 = jnp.broadcast_to(ab_im.reshape(1, -1), (SUBLANES, S5_GROUPS * S5_STATE))
    bb = jnp.concatenate([in_blocks(bb_re), in_blocks(bb_im)], axis=-1).astype(BF16)
    cc = jnp.concatenate([out_blocks(s5_c_re[0]), -out_blocks(s5_c_im[0])], axis=1).astype(BF16)
    wa_bd = _block_diag(lru_w_a[0]).astype(BF16)
    wx_bd = _block_diag(lru_w_x[0]).astype(BF16)
    rw_t = router_w[0].T
    rw_hi = rw_t.astype(BF16)
    rw_lo = (rw_t - rw_hi.astype(F32)).astype(BF16)

    tc = ROWS // bp
    modpat = _adaln(jnp.concatenate([c_prompt, c_sample], axis=0), w_ada[0], row(b_ada[0]), bp)
    r = jnp.arange(ROWS)
    tm_of = (r % tc) * bp + r // tc
    to_tm_p = (r[:, None] == tm_of[None, :]).astype(BF16)
    ident = jnp.eye(ROWS, dtype=BF16)
    to_tm = jnp.stack([to_tm_p, ident])
    from_tm = to_tm_p.T

    x_s = jnp.swapaxes(x_sample, 0, 1).reshape(bs * ls, D_MODEL)
    u, xr, gr = _in_proj(x_prompt, x_s, modpat, to_tm, row(g_mix[0]), w_in[0].astype(BF16), tiles_p, tiles_s)

    def conv_tm(cv):
        return jnp.swapaxes(cv, 0, 1).reshape(-1, D_LRU)

    def conv_bm(cv, b):
        return jnp.swapaxes(cv.reshape(CONV_WIDTH - 1, b, D_LRU), 0, 1)

    s5_args = (ar8, ai8, bb, cc, row(s5_d[0]), s5_w_glu[0].astype(BF16), row(s5_b_glu[0]), row(g_out_s5[0]))
    lru_args = (lru_conv_w[0], row(lru_conv_b[0]), wa_bd, row(lru_b_a[0]), wx_bd, row(lru_b_x[0]),
                row(lru_lambda[0]), row(g_out_lru[0]))
    s5_chunks_p = 2 if tiles_p % 2 == 0 else 1
    ys5_p, s5p = _s5_mixer(u, jnp.zeros((bp, S5_COLS), F32), *s5_args, bp, 0, tiles_p, s5_chunks_p)
    ys5_s, s5s = _s5_mixer(u, _s5_cols(state_s5_re[0], state_s5_im[0]), *s5_args, bs, tiles_p, tiles_s, 1)
    ylru_p, hp, cvp = _lru_mixer(xr, gr, jnp.zeros((bp, D_LRU), F32),
                                 jnp.zeros(((CONV_WIDTH - 1) * bp, D_LRU), F32), *lru_args, bp, 0, tiles_p)
    ylru_s, hs, cvs = _lru_mixer(xr, gr, state_lru_h[0], conv_tm(state_conv[0]), *lru_args, bs, tiles_p, tiles_s)

    out_args = (w_out[0].astype(BF16).reshape(2, D_S5, D_MODEL), row(g_ffn[0]), rw_hi, rw_lo,
                router_b[0].reshape(N_EXPERTS, 1))
    h_p, xn_p, pg_p, cnt_p = _out_proj(ys5_p, ylru_p, x_prompt, modpat, 0, from_tm, *out_args,
                                       2 if tiles_p % 2 == 0 else 1)
    h_s, xn_s, pg_s, cnt_s = _out_proj(ys5_s, ylru_s, x_s, modpat, 1, None, *out_args, 1)
    pg = jnp.concatenate([pg_p, pg_s], axis=1)
    cnt = jnp.concatenate([cnt_p, cnt_s], axis=0)

    seg_n, seg_glob, seg_local, seg_tot, block_expert, block_next, block_rows, nb_used = _plan(cnt[:, :, 0])
    xs = _dispatch(seg_n, seg_glob, seg_local, seg_tot, nb_used, pg, xn_p, xn_s,
                   _moe_rows_bound(tiles_p + tiles_s))
    ys = _moe(xs, block_expert, block_next, block_rows, nb_used, moe_w_gu[0], moe_b_gu[0].reshape(N_EXPERTS, 1, 2 * D_FF),
              moe_w_down[0], moe_b_down[0].reshape(N_EXPERTS, 1, D_MODEL))
    y_prompt, y_s = _final(seg_n, seg_glob, seg_local, seg_tot, h_p, h_s, pg.T, modpat, row(g_final), ys,
                           tiles_p, tiles_s)
    y_sample = jnp.swapaxes(y_s.reshape(ls, bs, D_MODEL), 0, 1)
    s5p_re, s5p_im = _s5_uncols(s5p)
    s5s_re, s5s_im = _s5_uncols(s5s)
    return (y_prompt, y_sample,
            s5p_re[None], s5p_im[None], hp[None], conv_bm(cvp, bp)[None],
            s5s_re[None], s5s_im[None], hs[None], conv_bm(cvs, bs)[None])
```

```python
import functools

import jax
import jax.numpy as jnp
from jax import lax
from jax.experimental import pallas as pl
from jax.experimental.pallas import tpu as pltpu

D_MODEL = 1024
D_S5 = 512
D_LRU = 512
S5_GROUPS = 32
S5_GROUP_CH = 16
S5_STATE = 64
S5_COLS = 2 * S5_GROUPS * S5_STATE
S5_BLOCKS = 4
LRU_HEADS = 8
LRU_HEAD_DIM = 64
LRU_C = 8.0
CONV_WIDTH = 4
N_EXPERTS = 32
TOP_K = 4
D_FF = 1024
SWIGLU_LIMIT = 7.0
SWIGLU_ALPHA = 1.702
N_MOD = 6
EPS = 1e-6

ROWS = 512
MOE_TM = 512
WAIT_MAX_PIECE = 8192
SEG_ALIGN = 16
SORT_ROWS = 2560
SORT_CHUNK = 512
SUBLANES = 8
LANES = 128
VMEM_LIMIT = 48 * 1024 * 1024

BF16 = jnp.bfloat16
F32 = jnp.float32


def _params(*sem):
    return pltpu.CompilerParams(dimension_semantics=sem, vmem_limit_bytes=VMEM_LIMIT)


def _full(shape):
    return pl.BlockSpec(shape, lambda *_: (0,) * len(shape))


def _rms(x, g):
    return x * lax.rsqrt(jnp.mean(x * x, axis=-1, keepdims=True) + EPS) * g


def _gelu(x):
    return 0.5 * x * (1.0 + lax.erf(x * (2.0 ** -0.5)))


def _expm1(x):
    u = jnp.exp(x)
    d = u - 1.0
    return jnp.where(d == 0.0, x, jnp.where(d == -1.0, -1.0, d * x / jnp.log(u)))


def _row_pieces(n, largest, fn):
    off = 0
    bit = largest
    while bit >= SEG_ALIGN:
        @pl.when((n & bit) != 0)
        def _(off=off, bit=bit):
            fn(off, bit)
        off = off + (n & bit)
        bit //= 2


def _mod_body(bp, c_ref, w_ref, b_ref, o_ref):
    c = c_ref[...]
    s = (c * jax.nn.sigmoid(c)).astype(BF16)
    mod = jnp.dot(s, w_ref[...].astype(BF16), preferred_element_type=F32) + b_ref[...]
    bs = mod.shape[0] - bp
    o_ref[0] = jnp.broadcast_to(mod[:bp][:, None, :], (bp, ROWS // bp, D_MODEL)).reshape(ROWS, D_MODEL)
    o_ref[1] = jnp.broadcast_to(mod[bp:][None], (ROWS // bs, bs, D_MODEL)).reshape(ROWS, D_MODEL)


def _adaln(c, w_ada, b_ada, bp):
    m = c.shape[0]
    return pl.pallas_call(
        functools.partial(_mod_body, bp),
        grid=(N_MOD,),
        in_specs=[pl.BlockSpec((m, D_MODEL), lambda j: (0, 0)),
                  pl.BlockSpec((D_MODEL, D_MODEL), lambda j: (0, j)),
                  pl.BlockSpec((1, D_MODEL), lambda j: (0, j))],
        out_specs=pl.BlockSpec((2, ROWS, D_MODEL), lambda j: (0, 0, j)),
        out_shape=jax.ShapeDtypeStruct((2, ROWS, N_MOD * D_MODEL), F32),
        compiler_params=_params("arbitrary"),
        name="adaln",
    )(c, w_ada, b_ada)


def _mod_spec(k, tiles_p):
    return pl.BlockSpec((1, ROWS, D_MODEL), lambda i, *_: (jnp.where(i < tiles_p, 0, 1), 0, k))


def _perm_spec(tiles_p):
    return pl.BlockSpec((1, ROWS, ROWS), lambda i, *_: (jnp.where(i < tiles_p, 0, 1), 0, 0))


def _tile_specs(bp, tiles_p, tiles_s):
    tc = ROWS // bp
    p_spec = pl.BlockSpec((bp, tc, D_MODEL), lambda i, *_: (0, jnp.minimum(i, tiles_p - 1), 0))
    s_spec = pl.BlockSpec((ROWS, D_MODEL), lambda i, *_: (jnp.clip(i - tiles_p, 0, tiles_s - 1), 0))
    return p_spec, s_spec


def _in_body(tiles_p, xp_ref, xs_ref, sc_ref, sh_ref, perm_ref, g_ref, w_ref, u_ref, xr_ref, gr_ref):
    x = jnp.where(pl.program_id(0) < tiles_p, xp_ref[...].reshape(ROWS, D_MODEL), xs_ref[...])
    xn = _rms(x, g_ref[...]) * (1.0 + sc_ref[0]) + sh_ref[0]
    xn = jnp.dot(perm_ref[0], xn.astype(BF16), preferred_element_type=F32).astype(BF16)
    p = jnp.dot(xn, w_ref[...], preferred_element_type=F32)
    u_ref[...] = p[:, :D_S5]
    xr_ref[...] = p[:, D_S5:D_S5 + D_LRU]
    gr_ref[...] = p[:, D_S5 + D_LRU:]


def _in_proj(x_p, x_s, modpat, to_tm, g_mix, w_in_bf, tiles_p, tiles_s):
    t_rows = (tiles_p + tiles_s) * ROWS
    row_spec = pl.BlockSpec((ROWS, D_S5), lambda i: (i, 0))
    return pl.pallas_call(
        functools.partial(_in_body, tiles_p),
        grid=(tiles_p + tiles_s,),
        in_specs=[*_tile_specs(x_p.shape[0], tiles_p, tiles_s),
                  _mod_spec(1, tiles_p), _mod_spec(0, tiles_p), _perm_spec(tiles_p), _full((1, D_MODEL)),
                  _full((D_MODEL, D_S5 + 2 * D_LRU))],
        out_specs=[row_spec, row_spec, row_spec],
        out_shape=[jax.ShapeDtypeStruct((t_rows, D_S5), F32)] * 3,
        compiler_params=_params("arbitrary"),
        name="in_proj",
    )(x_p, x_s, modpat, modpat, to_tm, g_mix, w_in_bf)


def _s5_prep_body(are_ref, aim_ref, ldt_ref, bre_ref, bim_ref, abre_ref, abim_ref, bbre_ref, bbim_ref):
    a_re = are_ref[...]
    a_im = aim_ref[...]
    dt = jnp.exp(ldt_ref[...])
    mag = jnp.exp(dt * a_re)
    ang = dt * a_im
    ab_re = mag * jnp.cos(ang)
    ab_im = mag * jnp.sin(ang)
    den = a_re * a_re + a_im * a_im
    q_re = ((ab_re - 1.0) * a_re + ab_im * a_im) / den
    q_im = (ab_im * a_re - (ab_re - 1.0) * a_im) / den
    abre_ref[...] = ab_re
    abim_ref[...] = ab_im
    b_re = bre_ref[...]
    b_im = bim_ref[...]
    bbre_ref[...] = q_re[:, None, :] * b_re - q_im[:, None, :] * b_im
    bbim_ref[...] = q_re[:, None, :] * b_im + q_im[:, None, :] * b_re


def _s5_prep(a_re, a_im, log_dt, b_re_t, b_im_t):
    gn = jax.ShapeDtypeStruct((S5_GROUPS, S5_STATE), F32)
    gjn = jax.ShapeDtypeStruct((S5_GROUPS, S5_GROUP_CH, S5_STATE), F32)
    return pl.pallas_call(_s5_prep_body, out_shape=[gn, gn, gjn, gjn], name="s5_prep")(
        a_re, a_im, log_dt, b_re_t, b_im_t)


def _s5_body(tc, bt, chunks, u_ref, s0_ref, ar_ref, ai_ref, bb_ref, cc_ref, d_ref, wg_ref, bg_ref, go_ref,
             y_ref, sout_ref, bu_ref, st_ref):
    half = S5_COLS // S5_BLOCKS // 2

    @pl.when(pl.program_id(0) == 0)
    def _():
        st_ref[...] = s0_ref[...]

    for c in range(chunks):
        ub = u_ref[c * ROWS:(c + 1) * ROWS, :].astype(BF16)
        for j in range(S5_BLOCKS):
            bu_ref[c, :, 2 * half * j:2 * half * (j + 1)] = jnp.dot(
                ub[:, LANES * j:LANES * (j + 1)], bb_ref[j], preferred_element_type=F32)

    for sb in range(bt // SUBLANES):
        base = sb * SUBLANES
        hr = [st_ref[base:base + SUBLANES, 2 * half * j:2 * half * j + half] for j in range(S5_BLOCKS)]
        hi = [st_ref[base:base + SUBLANES, 2 * half * j + half:2 * half * (j + 1)] for j in range(S5_BLOCKS)]
        for c in range(chunks):
            for t in range(tc):
                r0 = t * bt + base
                for j in range(S5_BLOCKS):
                    ar = ar_ref[:, half * j:half * (j + 1)]
                    ai = ai_ref[:, half * j:half * (j + 1)]
                    b_re = bu_ref[c, r0:r0 + SUBLANES, 2 * half * j:2 * half * j + half]
                    b_im = bu_ref[c, r0:r0 + SUBLANES, 2 * half * j + half:2 * half * (j + 1)]
                    h_re = ar * hr[j] - ai * hi[j] + b_re
                    h_im = ar * hi[j] + ai * hr[j] + b_im
                    bu_ref[c, r0:r0 + SUBLANES, 2 * half * j:2 * half * j + half] = h_re
                    bu_ref[c, r0:r0 + SUBLANES, 2 * half * j + half:2 * half * (j + 1)] = h_im
                    hr[j], hi[j] = h_re, h_im
        for j in range(S5_BLOCKS):
            st_ref[base:base + SUBLANES, 2 * half * j:2 * half * j + half] = hr[j]
            st_ref[base:base + SUBLANES, 2 * half * j + half:2 * half * (j + 1)] = hi[j]
    sout_ref[...] = st_ref[...]

    for c in range(chunks):
        y = jnp.concatenate(
            [jnp.dot(bu_ref[c, :, 2 * half * j:2 * half * (j + 1)].astype(BF16), cc_ref[j],
                     preferred_element_type=F32) for j in range(S5_BLOCKS)], axis=-1)
        y = y + d_ref[...] * u_ref[c * ROWS:(c + 1) * ROWS, :]
        g = _gelu(y)
        z = jnp.dot(g.astype(BF16), wg_ref[...], preferred_element_type=F32) + bg_ref[...]
        out = g * jax.nn.sigmoid(z)
        y_ref[c * ROWS:(c + 1) * ROWS, :] = _rms(out, go_ref[...]).astype(BF16)


def _s5_mixer(u, s0, ar8, ai8, bb, cc, d, w_glu_bf, b_glu, g_out, bt, tile0, tiles, chunks):
    tc = ROWS // bt
    assert tiles % chunks == 0 and tile0 % chunks == 0
    step_rows = chunks * ROWS
    return pl.pallas_call(
        functools.partial(_s5_body, tc, bt, chunks),
        grid=(tiles // chunks,),
        in_specs=[pl.BlockSpec((step_rows, D_S5), lambda i: (i + tile0 // chunks, 0)),
                  _full((bt, S5_COLS)), _full(ar8.shape), _full(ai8.shape), _full(bb.shape), _full(cc.shape),
                  _full((1, D_S5)), _full((D_S5, D_S5)), _full((1, D_S5)), _full((1, D_S5))],
        out_specs=[pl.BlockSpec((step_rows, D_S5), lambda i: (i, 0)), _full((bt, S5_COLS))],
        out_shape=[jax.ShapeDtypeStruct((tiles * ROWS, D_S5), BF16), jax.ShapeDtypeStruct((bt, S5_COLS), F32)],
        scratch_shapes=[pltpu.VMEM((chunks, ROWS, S5_COLS), F32), pltpu.VMEM((bt, S5_COLS), F32)],
        compiler_params=_params("arbitrary"),
        name="s5_mixer",
    )(u, s0, ar8, ai8, bb, cc, d, w_glu_bf, b_glu, g_out)


def _lru_body(tc, bt, xr_ref, gr_ref, h0_ref, cv0_ref, cw_ref, cb_ref, wa_ref, ba_ref, wx_ref, bx_ref,
              lam_ref, go_ref, y_ref, hout_ref, cvout_ref, xp_ref, a_ref, b_ref, h_ref):
    halo = (CONV_WIDTH - 1) * bt

    @pl.when(pl.program_id(0) == 0)
    def _():
        xp_ref[0:halo, :] = cv0_ref[...]
        h_ref[...] = h0_ref[...]

    xp_ref[halo:, :] = xr_ref[...]
    xc = cb_ref[...] + sum(xp_ref[k * bt:k * bt + ROWS, :] * cw_ref[k:k + 1, :] for k in range(CONV_WIDTH))
    xcb = xc.astype(BF16)
    r = jax.nn.sigmoid(jnp.dot(xcb, wa_ref[...], preferred_element_type=F32) + ba_ref[...])
    i = jax.nn.sigmoid(jnp.dot(xcb, wx_ref[...], preferred_element_type=F32) + bx_ref[...])
    lam = lam_ref[...]
    softplus_neg_lam = jnp.maximum(-lam, 0.0) + jnp.log1p(jnp.exp(-jnp.abs(lam)))
    log_a = -LRU_C * r * softplus_neg_lam
    a_ref[...] = jnp.exp(log_a)
    b_ref[...] = jnp.sqrt(-_expm1(2.0 * log_a)) * (i * xc)

    def sb_loop(sb, carry):
        base = pl.multiple_of(sb * SUBLANES, SUBLANES)

        def t_loop(t, h):
            r0 = pl.multiple_of(t * bt + base, SUBLANES)
            h = a_ref[pl.ds(r0, SUBLANES), :] * h + b_ref[pl.ds(r0, SUBLANES), :]
            b_ref[pl.ds(r0, SUBLANES), :] = h
            return h

        h_ref[pl.ds(base, SUBLANES), :] = lax.fori_loop(0, tc, t_loop, h_ref[pl.ds(base, SUBLANES), :])
        return carry

    lax.fori_loop(0, bt // SUBLANES, sb_loop, 0)

    y = b_ref[...] * _gelu(gr_ref[...])
    y_ref[...] = _rms(y, go_ref[...]).astype(BF16)
    tail = xp_ref[ROWS:ROWS + halo, :]
    xp_ref[0:halo, :] = tail
    cvout_ref[...] = tail
    hout_ref[...] = h_ref[...]


def _lru_mixer(xr, gr, h0, conv0_tm, conv_w, conv_b, wa_bd, b_a, wx_bd, b_x, lam, g_out, bt, tile0, tiles):
    tc = ROWS // bt
    halo = (CONV_WIDTH - 1) * bt
    in_rows = pl.BlockSpec((ROWS, D_LRU), lambda i: (i + tile0, 0))
    vec = _full((1, D_LRU))
    return pl.pallas_call(
        functools.partial(_lru_body, tc, bt),
        grid=(tiles,),
        in_specs=[in_rows, in_rows, _full((bt, D_LRU)), _full((halo, D_LRU)), _full((CONV_WIDTH, D_LRU)), vec,
                  _full((D_LRU, D_LRU)), vec, _full((D_LRU, D_LRU)), vec, vec, vec],
        out_specs=[pl.BlockSpec((ROWS, D_LRU), lambda i: (i, 0)), _full((bt, D_LRU)), _full((halo, D_LRU))],
        out_shape=[jax.ShapeDtypeStruct((tiles * ROWS, D_LRU), BF16), jax.ShapeDtypeStruct((bt, D_LRU), F32),
                   jax.ShapeDtypeStruct((halo, D_LRU), F32)],
        scratch_shapes=[pltpu.VMEM((ROWS + halo, D_LRU), F32), pltpu.VMEM((ROWS, D_LRU), F32),
                        pltpu.VMEM((ROWS, D_LRU), F32), pltpu.VMEM((bt, D_LRU), F32)],
        compiler_params=_params("arbitrary"),
        name="lru_mixer",
    )(xr, gr, h0, conv0_tm, conv_w, conv_b, wa_bd, b_a, wx_bd, b_x, lam, g_out)


def _route_tile(xn, x_hi, rw_hi, rw_lo, rb):
    x_lo = (xn - x_hi.astype(F32)).astype(BF16)
    nt = (((1,), (1,)), ((), ()))
    logits = (lax.dot_general(rw_hi, x_hi, nt, preferred_element_type=F32)
              + lax.dot_general(rw_hi, x_lo, nt, preferred_element_type=F32)
              + lax.dot_general(rw_lo, x_hi, nt, preferred_element_type=F32)) + rb

    e_iota = lax.broadcasted_iota(jnp.int32, (N_EXPERTS, ROWS), 0).astype(F32)
    work = logits
    sels, vals = [], []
    for _ in range(TOP_K):
        m = jnp.max(work, axis=0, keepdims=True)
        idx = jnp.min(jnp.where(work == m, e_iota, float(N_EXPERTS)), axis=0, keepdims=True)
        sel = e_iota == idx
        work = jnp.where(sel, -jnp.inf, work)
        sels.append(sel)
        vals.append(m)
    exps = [jnp.exp(v - vals[0]) for v in vals]
    denom = exps[0] + exps[1] + exps[2] + exps[3]
    gates = [e / denom for e in exps]

    onehot = sels[0] | sels[1] | sels[2] | sels[3]
    rr = lax.broadcasted_iota(jnp.int32, (ROWS, ROWS), 0)
    cc = lax.broadcasted_iota(jnp.int32, (ROWS, ROWS), 1)
    before = (rr < cc).astype(BF16)
    prefix = jnp.dot(onehot.astype(BF16), before, preferred_element_type=F32)
    cnt = jnp.sum(onehot.astype(F32), axis=1, keepdims=True)
    cnt_pad = jnp.floor((cnt + (SEG_ALIGN - 1)) * (1.0 / SEG_ALIGN)) * SEG_ALIGN
    er = lax.broadcasted_iota(jnp.int32, (N_EXPERTS, N_EXPERTS), 0)
    ec = lax.broadcasted_iota(jnp.int32, (N_EXPERTS, N_EXPERTS), 1)
    seg_start = jnp.dot((ec < er).astype(BF16), jnp.broadcast_to(cnt_pad, (N_EXPERTS, LANES)).astype(BF16),
                        preferred_element_type=F32)[:, 0:1]
    where_to = prefix + seg_start
    poss = [jnp.sum(jnp.where(s, where_to, 0.0), axis=0, keepdims=True) for s in sels]

    s_iota = lax.broadcasted_iota(jnp.int32, (2 * TOP_K, ROWS), 0)
    pg = jnp.zeros((2 * TOP_K, ROWS), F32)
    for k in range(TOP_K):
        pg = jnp.where(s_iota == k, poss[k], pg)
        pg = jnp.where(s_iota == TOP_K + k, gates[k], pg)
    return pg, cnt


def _out_body(subtiles, tm_rows, *refs):
    if tm_rows:
        ys_ref, yl_ref, x_ref, g1_ref, sc_ref, sh_ref, perm_ref, *refs = refs
    else:
        ys_ref, yl_ref, x_ref, g1_ref, sc_ref, sh_ref, *refs = refs
    wo_ref, gf_ref, rwh_ref, rwl_ref, rb_ref, h_ref, xn_ref, pg_ref, cnt_ref = refs
    for c in range(subtiles):
        rows = slice(c * ROWS, (c + 1) * ROWS)
        ys = ys_ref[rows, :]
        yl = yl_ref[rows, :]
        if tm_rows:
            ys = jnp.dot(perm_ref[...], ys, preferred_element_type=F32).astype(BF16)
            yl = jnp.dot(perm_ref[...], yl, preferred_element_type=F32).astype(BF16)
            tc = x_ref.shape[1] // subtiles
            x = x_ref[:, c * tc:(c + 1) * tc, :].reshape(ROWS, D_MODEL)
        else:
            x = x_ref[rows, :]
        mixed = (jnp.dot(ys, wo_ref[0], preferred_element_type=F32)
                 + jnp.dot(yl, wo_ref[1], preferred_element_type=F32))
        h = x + g1_ref[0] * mixed
        if tm_rows:
            h_ref[:, c * tc:(c + 1) * tc, :] = h.reshape(x_ref.shape[0], tc, D_MODEL)
        else:
            h_ref[rows, :] = h
        xn = _rms(h, gf_ref[...]) * (1.0 + sc_ref[0]) + sh_ref[0]
        x_hi = xn.astype(BF16)
        xn_ref[rows, :] = x_hi
        pg, cnt = _route_tile(xn, x_hi, rwh_ref[...], rwl_ref[...], rb_ref[...])
        pg_ref[:, rows] = pg
        cnt_ref[c] = jnp.broadcast_to(cnt, (N_EXPERTS, LANES))


def _out_proj(ys5, ylru, x, modpat, pattern, from_tm, w_out_bf, g_ffn, rw_hi, rw_lo, router_b, subtiles):
    tm_rows = from_tm is not None
    t_rows = ys5.shape[0]
    step_rows = subtiles * ROWS
    half_spec = pl.BlockSpec((step_rows, D_S5), lambda i: (i, 0))
    if tm_rows:
        x_spec = pl.BlockSpec((x.shape[0], step_rows // x.shape[0], D_MODEL), lambda i: (0, i, 0))
    else:
        x_spec = pl.BlockSpec((step_rows, D_MODEL), lambda i: (i, 0))
    mod_spec = lambda k: pl.BlockSpec((1, ROWS, D_MODEL), lambda i: (pattern, 0, k))
    perm = [_full((ROWS, ROWS))] if tm_rows else []
    return pl.pallas_call(
        functools.partial(_out_body, subtiles, tm_rows),
        grid=(t_rows // step_rows,),
        in_specs=[half_spec, half_spec, x_spec, mod_spec(2), mod_spec(4), mod_spec(3), *perm,
                  _full((2, D_S5, D_MODEL)), _full((1, D_MODEL)),
                  _full((N_EXPERTS, D_MODEL)), _full((N_EXPERTS, D_MODEL)), _full((N_EXPERTS, 1))],
        out_specs=[x_spec, pl.BlockSpec((step_rows, D_MODEL), lambda i: (i, 0)),
                   pl.BlockSpec((2 * TOP_K, step_rows), lambda i: (0, i)),
                   pl.BlockSpec((subtiles, N_EXPERTS, LANES), lambda i: (i, 0, 0))],
        out_shape=[jax.ShapeDtypeStruct(x.shape, F32), jax.ShapeDtypeStruct((t_rows, D_MODEL), BF16),
                   jax.ShapeDtypeStruct((2 * TOP_K, t_rows), F32),
                   jax.ShapeDtypeStruct((t_rows // ROWS, N_EXPERTS, LANES), F32)],
        compiler_params=_params("arbitrary"),
        name="out_proj_router",
    )(ys5, ylru, x, modpat, modpat, modpat, *([from_tm] if tm_rows else []), w_out_bf, g_ffn, rw_hi, rw_lo,
      router_b)


def _start_segments(n_ref, hbm_ref, vmem_ref, hbm, vmem_buf, sem, step, to_hbm):
    for e in range(N_EXPERTS):
        n = n_ref[step * N_EXPERTS + e]
        h0 = hbm_ref[step * N_EXPERTS + e]
        v0 = vmem_ref[step * N_EXPERTS + e]

        def piece(off, size, h0=h0, v0=v0):
            h = hbm.at[pl.ds(pl.multiple_of(h0 + off, SEG_ALIGN), size)]
            v = vmem_buf.at[pl.ds(pl.multiple_of(v0 + off, SEG_ALIGN), size)]
            (pltpu.make_async_copy(v, h, sem) if to_hbm else pltpu.make_async_copy(h, v, sem)).start()

        _row_pieces(n, ROWS, piece)


def _wait_rows(total, largest, hbm, sem):
    _row_pieces(total, largest, lambda off, size: pltpu.make_async_copy(
        hbm.at[pl.ds(0, size)], hbm.at[pl.ds(0, size)], sem).wait())


def _dispatch_body(tiles_p, n_ref, glob_ref, local_ref, tot_ref, nb_ref, pg_ref, xp_ref, xsm_ref, xs_hbm, stage,
                   sem):
    j = pl.program_id(0)
    last = pl.num_programs(0) - 1
    slot = j % 2

    def unused_blocks(act):
        def blk(b, carry):
            act(pltpu.make_async_copy(stage.at[slot, pl.ds(0, MOE_TM)],
                                      xs_hbm.at[pl.ds(pl.multiple_of(b * MOE_TM, MOE_TM), MOE_TM)], sem.at[slot]))
            return carry
        lax.fori_loop(nb_ref[0], xs_hbm.shape[0] // MOE_TM, blk, 0)

    def wait_step(step, s):
        _wait_rows(tot_ref[step], WAIT_MAX_PIECE, xs_hbm, sem.at[s])

    @pl.when(j >= 2)
    def _():
        wait_step(jnp.maximum(j - 2, 0), slot)

    @pl.when(j < last)
    def _():
        x = jnp.where(j < tiles_p, xp_ref[...], xsm_ref[...])
        pos = pg_ref[0:TOP_K, :]
        for c in range(SORT_ROWS // SORT_CHUNK):
            r = (lax.broadcasted_iota(jnp.int32, (SORT_CHUNK, ROWS), 0) + c * SORT_CHUNK).astype(F32)
            pick = (r == pos[0:1]) | (r == pos[1:2]) | (r == pos[2:3]) | (r == pos[3:4])
            stage[slot, c * SORT_CHUNK:(c + 1) * SORT_CHUNK, :] = jnp.dot(
                pick.astype(BF16), x, preferred_element_type=F32).astype(BF16)

    @pl.when(j == last)
    def _():
        stage[slot, 0:MOE_TM, :] = jnp.zeros((MOE_TM, D_MODEL), BF16)

    _start_segments(n_ref, glob_ref, local_ref, xs_hbm, stage.at[slot], sem.at[slot], j, True)

    @pl.when(j == last)
    def _():
        unused_blocks(lambda c: c.start())

        @pl.when(j >= 1)
        def _():
            wait_step(jnp.maximum(j - 1, 0), 1 - slot)
        wait_step(j, slot)
        unused_blocks(lambda c: c.wait())


def _dispatch(seg_n, seg_glob, seg_local, seg_tot, nb_used, pg, xn_p, xn_s, n_rows):
    tiles_p = xn_p.shape[0] // ROWS
    tiles_s = xn_s.shape[0] // ROWS
    tiles = tiles_p + tiles_s
    grid_spec = pltpu.PrefetchScalarGridSpec(
        num_scalar_prefetch=5,
        grid=(tiles + 1,),
        in_specs=[pl.BlockSpec((2 * TOP_K, ROWS), lambda j, *_: (0, jnp.minimum(j, tiles - 1))),
                  pl.BlockSpec((ROWS, D_MODEL), lambda j, *_: (jnp.minimum(j, tiles_p - 1), 0)),
                  pl.BlockSpec((ROWS, D_MODEL), lambda j, *_: (jnp.clip(j - tiles_p, 0, tiles_s - 1), 0))],
        out_specs=pl.BlockSpec(memory_space=pl.ANY),
        scratch_shapes=[pltpu.VMEM((2, SORT_ROWS, D_MODEL), BF16), pltpu.SemaphoreType.DMA((2,))],
    )
    return pl.pallas_call(
        functools.partial(_dispatch_body, tiles_p),
        grid_spec=grid_spec,
        out_shape=jax.ShapeDtypeStruct((n_rows, D_MODEL), BF16),
        compiler_params=_params("arbitrary"),
        name="moe_dispatch",
    )(seg_n, seg_glob, seg_local, seg_tot, nb_used, pg, xn_p, xn_s)


def _moe_body(be_ref, nxt_ref, nv_ref, nb_ref, xs_ref, wgu_hbm, bgu_ref, wd_hbm, bd_ref, ys_ref,
              wgu_f32, wd_f32, wgu_bf, wd_bf, sem):
    i = pl.program_id(0)

    def weight_copies(e):
        return (pltpu.make_async_copy(wgu_hbm.at[e], wgu_f32, sem.at[0]),
                pltpu.make_async_copy(wd_hbm.at[e], wd_f32, sem.at[1]))

    @pl.when(i >= nb_ref[0])
    def _():
        ys_ref[...] = jnp.zeros_like(ys_ref)

    @pl.when(i < nb_ref[0])
    def _():
        e = be_ref[i]

        @pl.when(i == 0)
        def _():
            for c in weight_copies(e):
                c.start()

        @pl.when(jnp.logical_or(i == 0, e != be_ref[jnp.maximum(i - 1, 0)]))
        def _():
            for c in weight_copies(e):
                c.wait()
            wgu_bf[...] = wgu_f32[...].astype(BF16)
            wd_bf[...] = wd_f32[...].astype(BF16)

            @pl.when(nxt_ref[i] >= 0)
            def _():
                for c in weight_copies(nxt_ref[i]):
                    c.start()

        def ffn(x):
            hg = jnp.dot(x, wgu_bf[...], preferred_element_type=F32) + bgu_ref[0]
            gate = jnp.minimum(hg[:, :D_FF], SWIGLU_LIMIT)
            up = jnp.clip(hg[:, D_FF:], -SWIGLU_LIMIT, SWIGLU_LIMIT)
            act = (up + 1.0) * (gate * jax.nn.sigmoid(SWIGLU_ALPHA * gate))
            return (jnp.dot(act.astype(BF16), wd_bf[...], preferred_element_type=F32) + bd_ref[0]).astype(BF16)

        half = MOE_TM // 2

        @pl.when(nv_ref[i] > half)
        def _():
            ys_ref[...] = ffn(xs_ref[...])

        @pl.when(nv_ref[i] <= half)
        def _():
            ys_ref[0:half, :] = ffn(xs_ref[0:half, :])
            ys_ref[half:, :] = jnp.zeros((half, D_MODEL), BF16)


def _moe(xs, block_expert, block_next, block_rows, nb_used, w_gu, b_gu, w_down, b_down):
    n_blocks = xs.shape[0] // MOE_TM
    in_rows = pl.BlockSpec((MOE_TM, D_MODEL), lambda i, be, nx, nv, nb: (jnp.minimum(i, nb[0] - 1), 0))
    grid_spec = pltpu.PrefetchScalarGridSpec(
        num_scalar_prefetch=4,
        grid=(n_blocks,),
        in_specs=[in_rows,
                  pl.BlockSpec(memory_space=pl.ANY),
                  pl.BlockSpec((1, 1, 2 * D_FF), lambda i, be, nx, nv, nb: (be[i], 0, 0)),
                  pl.BlockSpec(memory_space=pl.ANY),
                  pl.BlockSpec((1, 1, D_MODEL), lambda i, be, nx, nv, nb: (be[i], 0, 0))],
        out_specs=pl.BlockSpec((MOE_TM, D_MODEL), lambda i, be, nx, nv, nb: (i, 0)),
        scratch_shapes=[pltpu.VMEM((D_MODEL, 2 * D_FF), F32), pltpu.VMEM((D_FF, D_MODEL), F32),
                        pltpu.VMEM((D_MODEL, 2 * D_FF), BF16), pltpu.VMEM((D_FF, D_MODEL), BF16),
                        pltpu.SemaphoreType.DMA((2,))],
    )
    return pl.pallas_call(
        _moe_body,
        grid_spec=grid_spec,
        out_shape=jax.ShapeDtypeStruct(xs.shape, BF16),
        compiler_params=_params("arbitrary"),
        name="moe_experts",
    )(block_expert, block_next, block_rows, nb_used, xs, w_gu, b_gu, w_down, b_down)


def _fin_body(tiles_p, tiles, n_ref, glob_ref, local_ref, tot_ref, hp_ref, hs_ref, pg_ref, g2_ref, gf_ref, ys_hbm,
              op_ref, os_ref, ybuf0, ybuf1, sem):
    j = pl.program_id(0)
    is_prompt = j < tiles_p
    bufs = (ybuf0, ybuf1)

    def fetch(p, buf, s):
        _start_segments(n_ref, glob_ref, local_ref, ys_hbm, buf, sem.at[s], p, False)

    def combine(buf):
        pg = pg_ref[...]
        ff = jnp.zeros((ROWS, D_MODEL), F32)
        for c in range(SORT_ROWS // SORT_CHUNK):
            r = (lax.broadcasted_iota(jnp.int32, (ROWS, SORT_CHUNK), 1) + c * SORT_CHUNK).astype(F32)
            w = jnp.zeros((ROWS, SORT_CHUNK), F32)
            for k in range(TOP_K):
                w = jnp.where(r == pg[:, k:k + 1], pg[:, TOP_K + k:TOP_K + k + 1], w)
            ff = ff + jnp.dot(w.astype(BF16), buf[c * SORT_CHUNK:(c + 1) * SORT_CHUNK, :],
                              preferred_element_type=F32)
        h = jnp.where(is_prompt, hp_ref[...].reshape(ROWS, D_MODEL), hs_ref[...])
        y = _rms(h + g2_ref[0] * ff, gf_ref[...])

        @pl.when(is_prompt)
        def _():
            op_ref[...] = y.reshape(op_ref.shape)

        @pl.when(jnp.logical_not(is_prompt))
        def _():
            os_ref[...] = y

    @pl.when(j == 0)
    def _():
        ybuf0[...] = jnp.zeros_like(ybuf0)
        ybuf1[...] = jnp.zeros_like(ybuf1)
        fetch(0, ybuf0, 0)

    for v in range(2):
        @pl.when(j % 2 == v)
        def _(v=v):
            @pl.when(j + 1 < tiles)
            def _():
                _wait_rows(tot_ref[j], WAIT_MAX_PIECE, ys_hbm, sem.at[v])
                fetch(jnp.minimum(j + 1, tiles - 1), bufs[1 - v], 1 - v)
                combine(bufs[v])

            @pl.when(j + 1 == tiles)
            def _():
                _wait_rows(tot_ref[j], WAIT_MAX_PIECE, ys_hbm, sem.at[v])
                combine(bufs[v])


def _final(seg_n, seg_glob, seg_local, seg_tot, h_p, h_s, pg_t, modpat, g_final, ys, tiles_p, tiles_s):
    hp_spec, hs_spec = _tile_specs(h_p.shape[0], tiles_p, tiles_s)
    grid_spec = pltpu.PrefetchScalarGridSpec(
        num_scalar_prefetch=4,
        grid=(tiles_p + tiles_s,),
        in_specs=[hp_spec, hs_spec,
                  pl.BlockSpec((ROWS, 2 * TOP_K), lambda j, *_: (j, 0)),
                  _mod_spec(5, tiles_p), _full((1, D_MODEL)),
                  pl.BlockSpec(memory_space=pl.ANY)],
        out_specs=[hp_spec, hs_spec],
        scratch_shapes=[pltpu.VMEM((SORT_ROWS, D_MODEL), BF16)] * 2 + [pltpu.SemaphoreType.DMA((2,))],
    )
    return pl.pallas_call(
        functools.partial(_fin_body, tiles_p, tiles_p + tiles_s),
        grid_spec=grid_spec,
        out_shape=[jax.ShapeDtypeStruct(h_p.shape, F32), jax.ShapeDtypeStruct(h_s.shape, F32)],
        compiler_params=_params("arbitrary"),
        name="combine_final",
    )(seg_n, seg_glob, seg_local, seg_tot, h_p, h_s, pg_t, modpat, g_final, ys)


def _block_diag(w):
    h, i, j = w.shape
    return jnp.einsum('hij,hk->hikj', w, jnp.eye(h, dtype=w.dtype)).reshape(h * i, h * j)


def _s5_cols(re, im):
    b = re.shape[0]
    stack = jnp.stack([re.reshape(b, S5_BLOCKS, -1), im.reshape(b, S5_BLOCKS, -1)], axis=2)
    return stack.reshape(b, S5_COLS)


def _s5_uncols(cols):
    b = cols.shape[0]
    c = cols.reshape(b, S5_BLOCKS, 2, S5_GROUPS // S5_BLOCKS, S5_STATE)
    return (c[:, :, 0].reshape(b, S5_GROUPS, S5_STATE), c[:, :, 1].reshape(b, S5_GROUPS, S5_STATE))


def _moe_rows_bound(tiles):
    worst = tiles * (TOP_K * ROWS + N_EXPERTS * (SEG_ALIGN - 1)) + N_EXPERTS * (MOE_TM - SEG_ALIGN)
    return (worst + MOE_TM - 1) // MOE_TM * MOE_TM


def _plan(cnt):
    cnt = cnt.astype(jnp.int32)
    tiles = cnt.shape[0]
    cp = (cnt + SEG_ALIGN - 1) // SEG_ALIGN * SEG_ALIGN
    local = jnp.cumsum(cp, axis=1) - cp
    group = jnp.sum(cp, axis=0)
    group_pad = (group + MOE_TM - 1) // MOE_TM * MOE_TM
    pend = jnp.cumsum(group_pad)
    pstart = pend - group_pad
    glob = pstart[None, :] + jnp.cumsum(cp, axis=0) - cp
    gap = group_pad - group
    seg_n = jnp.concatenate([cp, gap[None]], axis=0).reshape(-1)
    seg_local = jnp.concatenate([local, jnp.zeros((1, N_EXPERTS), jnp.int32)], axis=0).reshape(-1)
    seg_glob = jnp.concatenate([glob, (pstart + group)[None]], axis=0).reshape(-1)
    seg_tot = jnp.concatenate([jnp.sum(cp, axis=1), jnp.sum(gap)[None]]).astype(jnp.int32)
    n_blocks = _moe_rows_bound(tiles) // MOE_TM
    block_row0 = jnp.arange(n_blocks, dtype=jnp.int32) * MOE_TM
    block_expert = jnp.minimum(jnp.sum(block_row0[:, None] >= pend[None, :], axis=1), N_EXPERTS - 1).astype(jnp.int32)
    nb_used = (pend[-1] // MOE_TM).astype(jnp.int32).reshape(1)
    experts = jnp.arange(N_EXPERTS, dtype=jnp.int32)
    later_owner = jnp.where((experts[None, :] > experts[:, None]) & (group_pad[None, :] > 0), experts[None, :],
                            N_EXPERTS)
    next_owner = jnp.min(later_owner, axis=1)
    next_owner = jnp.where(next_owner == N_EXPERTS, -1, next_owner).astype(jnp.int32)
    owner = block_expert[:, None] == experts[None, :]
    block_next = jnp.sum(jnp.where(owner, next_owner[None, :], 0), axis=1).astype(jnp.int32)
    group_end = jnp.sum(jnp.where(owner, (pstart + group)[None, :], 0), axis=1)
    block_rows = jnp.clip(group_end - block_row0, 0, MOE_TM).astype(jnp.int32)
    return seg_n, seg_glob, seg_local, seg_tot, block_expert, block_next, block_rows, nb_used


def kernel(x_prompt, x_sample, state_s5_re, state_s5_im, state_lru_h, state_conv, c_prompt, c_sample, w_ada, b_ada, g_mix, w_in, s5_a_re, s5_a_im, s5_log_dt, s5_b_re, s5_b_im, s5_c_re, s5_c_im, s5_d, s5_w_glu, s5_b_glu, lru_conv_w, lru_conv_b, lru_w_a, lru_b_a, lru_w_x, lru_b_x, lru_lambda, g_out_s5, g_out_lru, w_out, g_ffn, router_w, router_b, moe_w_gu, moe_b_gu, moe_w_down, moe_b_down, g_final):
    assert w_ada.shape[0] == 1, "one layer"
    bp, lp, _ = x_prompt.shape
    bs, ls, _ = x_sample.shape
    assert ROWS % bp == 0 and ROWS % bs == 0 and (bp * lp) % ROWS == 0 and (bs * ls) % ROWS == 0
    tiles_p = bp * lp // ROWS
    tiles_s = bs * ls // ROWS
    row = lambda v: v.reshape(1, -1)

    ab_re, ab_im, bb_re, bb_im = _s5_prep(s5_a_re[0], s5_a_im[0], s5_log_dt[0].reshape(S5_GROUPS, 1),
                                          jnp.swapaxes(s5_b_re[0], 1, 2), jnp.swapaxes(s5_b_im[0], 1, 2))
    gpb = S5_GROUPS // S5_BLOCKS
    eye = jnp.eye(gpb, dtype=F32)

    def in_blocks(b):
        b = b.reshape(S5_BLOCKS, gpb, S5_GROUP_CH, S5_STATE)
        return jnp.einsum('bgjn,gh->bgjhn', b, eye).reshape(S5_BLOCKS, gpb * S5_GROUP_CH, gpb * S5_STATE)

    def out_blocks(c):
        c = c.reshape(S5_BLOCKS, gpb, S5_GROUP_CH, S5_STATE)
        return jnp.einsum('bgjn,gh->bgnhj', c, eye).reshape(S5_BLOCKS, gpb * S5_STATE, gpb * S5_GROUP_CH)

    ar8 = jnp.broadcast_to(ab_re.reshape(1, -1), (SUBLANES, S5_GROUPS * S5_STATE))
    ai8 = jnp.broadcast_to(ab_im.reshape(1, -1), (SUBLANES, S5_GROUPS * S5_STATE))
    bb = jnp.concatenate([in_blocks(bb_re), in_blocks(bb_im)], axis=-1).astype(BF16)
    cc = jnp.concatenate([out_blocks(s5_c_re[0]), -out_blocks(s5_c_im[0])], axis=1).astype(BF16)
    wa_bd = _block_diag(lru_w_a[0]).astype(BF16)
    wx_bd = _block_diag(lru_w_x[0]).astype(BF16)
    rw_t = router_w[0].T
    rw_hi = rw_t.astype(BF16)
    rw_lo = (rw_t - rw_hi.astype(F32)).astype(BF16)

    tc = ROWS // bp
    modpat = _adaln(jnp.concatenate([c_prompt, c_sample], axis=0), w_ada[0], row(b_ada[0]), bp)
    r = jnp.arange(ROWS)
    tm_of = (r % tc) * bp + r // tc
    to_tm_p = (r[:, None] == tm_of[None, :]).astype(BF16)
    ident = jnp.eye(ROWS, dtype=BF16)
    to_tm = jnp.stack([to_tm_p, ident])
    from_tm = to_tm_p.T

    x_s = jnp.swapaxes(x_sample, 0, 1).reshape(bs * ls, D_MODEL)
    u, xr, gr = _in_proj(x_prompt, x_s, modpat, to_tm, row(g_mix[0]), w_in[0].astype(BF16), tiles_p, tiles_s)

    def conv_tm(cv):
        return jnp.swapaxes(cv, 0, 1).reshape(-1, D_LRU)

    def conv_bm(cv, b):
        return jnp.swapaxes(cv.reshape(CONV_WIDTH - 1, b, D_LRU), 0, 1)

    s5_args = (ar8, ai8, bb, cc, row(s5_d[0]), s5_w_glu[0].astype(BF16), row(s5_b_glu[0]), row(g_out_s5[0]))
    lru_args = (lru_conv_w[0], row(lru_conv_b[0]), wa_bd, row(lru_b_a[0]), wx_bd, row(lru_b_x[0]),
                row(lru_lambda[0]), row(g_out_lru[0]))
    s5_chunks_p = 2 if tiles_p % 2 == 0 else 1
    ys5_p, s5p = _s5_mixer(u, jnp.zeros((bp, S5_COLS), F32), *s5_args, bp, 0, tiles_p, s5_chunks_p)
    ys5_s, s5s = _s5_mixer(u, _s5_cols(state_s5_re[0], state_s5_im[0]), *s5_args, bs, tiles_p, tiles_s, 1)
    ylru_p, hp, cvp = _lru_mixer(xr, gr, jnp.zeros((bp, D_LRU), F32),
                                 jnp.zeros(((CONV_WIDTH - 1) * bp, D_LRU), F32), *lru_args, bp, 0, tiles_p)
    ylru_s, hs, cvs = _lru_mixer(xr, gr, state_lru_h[0], conv_tm(state_conv[0]), *lru_args, bs, tiles_p, tiles_s)

    out_args = (w_out[0].astype(BF16).reshape(2, D_S5, D_MODEL), row(g_ffn[0]), rw_hi, rw_lo,
                router_b[0].reshape(N_EXPERTS, 1))
    h_p, xn_p, pg_p, cnt_p = _out_proj(ys5_p, ylru_p, x_prompt, modpat, 0, from_tm, *out_args,
                                       2 if tiles_p % 2 == 0 else 1)
    h_s, xn_s, pg_s, cnt_s = _out_proj(ys5_s, ylru_s, x_s, modpat, 1, None, *out_args, 1)
    pg = jnp.concatenate([pg_p, pg_s], axis=1)
    cnt = jnp.concatenate([cnt_p, cnt_s], axis=0)

    seg_n, seg_glob, seg_local, seg_tot, block_expert, block_next, block_rows, nb_used = _plan(cnt[:, :, 0])
    xs = _dispatch(seg_n, seg_glob, seg_local, seg_tot, nb_used, pg, xn_p, xn_s,
                   _moe_rows_bound(tiles_p + tiles_s))
    ys = _moe(xs, block_expert, block_next, block_rows, nb_used, moe_w_gu[0], moe_b_gu[0].reshape(N_EXPERTS, 1, 2 * D_FF),
              moe_w_down[0], moe_b_down[0].reshape(N_EXPERTS, 1, D_MODEL))
    y_prompt, y_s = _final(seg_n, seg_glob, seg_local, seg_tot, h_p, h_s, pg.T, modpat, row(g_final), ys,
                           tiles_p, tiles_s)
    y_sample = jnp.swapaxes(y_s.reshape(ls, bs, D_MODEL), 0, 1)
    s5p_re, s5p_im = _s5_uncols(s5p)
    s5s_re, s5s_im = _s5_uncols(s5s)
    return (y_prompt, y_sample,
            s5p_re[None], s5p_im[None], hp[None], conv_bm(cvp, bp)[None],
            s5s_re[None], s5s_im[None], hs[None], conv_bm(cvs, bs)[None])
```

```python
import functools

import jax
import jax.numpy as jnp
from jax import lax
from jax.experimental import pallas as pl
from jax.experimental.pallas import tpu as pltpu

D_MODEL = 1024
D_S5 = 512
D_LRU = 512
S5_GROUPS = 32
S5_GROUP_CH = 16
S5_STATE = 64
S5_COLS = 2 * S5_GROUPS * S5_STATE
S5_BLOCKS = 4
LRU_HEADS = 8
LRU_HEAD_DIM = 64
LRU_C = 8.0
CONV_WIDTH = 4
N_EXPERTS = 32
TOP_K = 4
D_FF = 1024
SWIGLU_LIMIT = 7.0
SWIGLU_ALPHA = 1.702
N_MOD = 6
EPS = 1e-6

ROWS = 512
MOE_TM = 512
BIG_PIECE = 128
WAIT_MAX_PIECE = 8192
SEG_ALIGN = 16
SORT_ROWS = 2560
SORT_CHUNK = 512
SUBLANES = 8
LANES = 128
VMEM_LIMIT = 48 * 1024 * 1024

BF16 = jnp.bfloat16
F32 = jnp.float32


def _params(*sem):
    return pltpu.CompilerParams(dimension_semantics=sem, vmem_limit_bytes=VMEM_LIMIT)


def _full(shape):
    return pl.BlockSpec(shape, lambda *_: (0,) * len(shape))


def _rms(x, g):
    return x * lax.rsqrt(jnp.mean(x * x, axis=-1, keepdims=True) + EPS) * g


def _gelu(x):
    return 0.5 * x * (1.0 + lax.erf(x * (2.0 ** -0.5)))


def _expm1(x):
    u = jnp.exp(x)
    d = u - 1.0
    return jnp.where(d == 0.0, x, jnp.where(d == -1.0, -1.0, d * x / jnp.log(u)))


def _row_pieces(n, largest, fn):
    off = 0
    bit = largest
    while bit >= SEG_ALIGN:
        @pl.when((n & bit) != 0)
        def _(off=off, bit=bit):
            fn(off, bit)
        off = off + (n & bit)
        bit //= 2


def _mod_body(bp, c_ref, w_ref, b_ref, o_ref):
    c = c_ref[...]
    s = (c * jax.nn.sigmoid(c)).astype(BF16)
    mod = jnp.dot(s, w_ref[...].astype(BF16), preferred_element_type=F32) + b_ref[...]
    bs = mod.shape[0] - bp
    o_ref[0] = jnp.broadcast_to(mod[:bp][:, None, :], (bp, ROWS // bp, D_MODEL)).reshape(ROWS, D_MODEL)
    o_ref[1] = jnp.broadcast_to(mod[bp:][None], (ROWS // bs, bs, D_MODEL)).reshape(ROWS, D_MODEL)


def _adaln(c, w_ada, b_ada, bp):
    m = c.shape[0]
    return pl.pallas_call(
        functools.partial(_mod_body, bp),
        grid=(N_MOD,),
        in_specs=[pl.BlockSpec((m, D_MODEL), lambda j: (0, 0)),
                  pl.BlockSpec((D_MODEL, D_MODEL), lambda j: (0, j)),
                  pl.BlockSpec((1, D_MODEL), lambda j: (0, j))],
        out_specs=pl.BlockSpec((2, ROWS, D_MODEL), lambda j: (0, 0, j)),
        out_shape=jax.ShapeDtypeStruct((2, ROWS, N_MOD * D_MODEL), F32),
        compiler_params=_params("arbitrary"),
        name="adaln",
    )(c, w_ada, b_ada)


def _mod_spec(k, tiles_p):
    return pl.BlockSpec((1, ROWS, D_MODEL), lambda i, *_: (jnp.where(i < tiles_p, 0, 1), 0, k))


def _tile_specs(bp, tiles_p, tiles_s):
    tc = ROWS // bp
    p_spec = pl.BlockSpec((bp, tc, D_MODEL), lambda i, *_: (0, jnp.minimum(i, tiles_p - 1), 0))
    s_spec = pl.BlockSpec((ROWS, D_MODEL), lambda i, *_: (jnp.clip(i - tiles_p, 0, tiles_s - 1), 0))
    return p_spec, s_spec


def _in_body(subtiles, to_tm, *refs):
    if to_tm:
        x_ref, sc_ref, sh_ref, perm_ref, g_ref, w_ref, u_ref, xr_ref, gr_ref = refs
    else:
        x_ref, sc_ref, sh_ref, g_ref, w_ref, u_ref, xr_ref, gr_ref = refs
    for c in range(subtiles):
        rows = slice(c * ROWS, (c + 1) * ROWS)
        if to_tm:
            tc = x_ref.shape[1] // subtiles
            x = x_ref[:, c * tc:(c + 1) * tc, :].reshape(ROWS, D_MODEL)
        else:
            x = x_ref[rows, :]
        xn = (_rms(x, g_ref[...]) * (1.0 + sc_ref[0]) + sh_ref[0]).astype(BF16)
        if to_tm:
            xn = jnp.dot(perm_ref[...], xn, preferred_element_type=F32).astype(BF16)
        p = jnp.dot(xn, w_ref[...], preferred_element_type=F32)
        u_ref[rows, :] = p[:, :D_S5]
        xr_ref[rows, :] = p[:, D_S5:D_S5 + D_LRU]
        gr_ref[rows, :] = p[:, D_S5 + D_LRU:]


def _in_proj(x, modpat, pattern, to_tm, g_mix, w_in_bf, subtiles):
    tm = to_tm is not None
    t_rows = x.shape[0] * x.shape[1] if tm else x.shape[0]
    step_rows = subtiles * ROWS
    if tm:
        x_spec = pl.BlockSpec((x.shape[0], step_rows // x.shape[0], D_MODEL), lambda i: (0, i, 0))
    else:
        x_spec = pl.BlockSpec((step_rows, D_MODEL), lambda i: (i, 0))
    mod_spec = lambda k: pl.BlockSpec((1, ROWS, D_MODEL), lambda i: (pattern, 0, k))
    row_spec = pl.BlockSpec((step_rows, D_S5), lambda i: (i, 0))
    return pl.pallas_call(
        functools.partial(_in_body, subtiles, tm),
        grid=(t_rows // step_rows,),
        in_specs=[x_spec, mod_spec(1), mod_spec(0), *([_full((ROWS, ROWS))] if tm else []), _full((1, D_MODEL)),
                  _full((D_MODEL, D_S5 + 2 * D_LRU))],
        out_specs=[row_spec, row_spec, row_spec],
        out_shape=[jax.ShapeDtypeStruct((t_rows, D_S5), F32)] * 3,
        compiler_params=_params("arbitrary"),
        name="in_proj",
    )(x, modpat, modpat, *([to_tm] if tm else []), g_mix, w_in_bf)


def _s5_prep_body(are_ref, aim_ref, ldt_ref, bre_ref, bim_ref, abre_ref, abim_ref, bbre_ref, bbim_ref):
    a_re = are_ref[...]
    a_im = aim_ref[...]
    dt = jnp.exp(ldt_ref[...])
    mag = jnp.exp(dt * a_re)
    ang = dt * a_im
    ab_re = mag * jnp.cos(ang)
    ab_im = mag * jnp.sin(ang)
    den = a_re * a_re + a_im * a_im
    q_re = ((ab_re - 1.0) * a_re + ab_im * a_im) / den
    q_im = (ab_im * a_re - (ab_re - 1.0) * a_im) / den
    abre_ref[...] = ab_re
    abim_ref[...] = ab_im
    b_re = bre_ref[...]
    b_im = bim_ref[...]
    bbre_ref[...] = q_re[:, None, :] * b_re - q_im[:, None, :] * b_im
    bbim_ref[...] = q_re[:, None, :] * b_im + q_im[:, None, :] * b_re


def _s5_prep(a_re, a_im, log_dt, b_re_t, b_im_t):
    gn = jax.ShapeDtypeStruct((S5_GROUPS, S5_STATE), F32)
    gjn = jax.ShapeDtypeStruct((S5_GROUPS, S5_GROUP_CH, S5_STATE), F32)
    return pl.pallas_call(_s5_prep_body, out_shape=[gn, gn, gjn, gjn], name="s5_prep")(
        a_re, a_im, log_dt, b_re_t, b_im_t)


def _s5_body(tc, bt, chunks, u_ref, s0_ref, ar_ref, ai_ref, bb_ref, cc_ref, d_ref, wg_ref, bg_ref, go_ref,
             y_ref, sout_ref, bu_ref, st_ref):
    half = S5_COLS // S5_BLOCKS // 2

    @pl.when(pl.program_id(0) == 0)
    def _():
        st_ref[...] = s0_ref[...]

    for c in range(chunks):
        ub = u_ref[c * ROWS:(c + 1) * ROWS, :].astype(BF16)
        for j in range(S5_BLOCKS):
            bu_ref[c, :, 2 * half * j:2 * half * (j + 1)] = jnp.dot(
                ub[:, LANES * j:LANES * (j + 1)], bb_ref[j], preferred_element_type=F32)

    for sb in range(bt // SUBLANES):
        base = sb * SUBLANES
        hr = [st_ref[base:base + SUBLANES, 2 * half * j:2 * half * j + half] for j in range(S5_BLOCKS)]
        hi = [st_ref[base:base + SUBLANES, 2 * half * j + half:2 * half * (j + 1)] for j in range(S5_BLOCKS)]
        for c in range(chunks):
            for t in range(tc):
                r0 = t * bt + base
                for j in range(S5_BLOCKS):
                    ar = ar_ref[:, half * j:half * (j + 1)]
                    ai = ai_ref[:, half * j:half * (j + 1)]
                    b_re = bu_ref[c, r0:r0 + SUBLANES, 2 * half * j:2 * half * j + half]
                    b_im = bu_ref[c, r0:r0 + SUBLANES, 2 * half * j + half:2 * half * (j + 1)]
                    h_re = ar * hr[j] - ai * hi[j] + b_re
                    h_im = ar * hi[j] + ai * hr[j] + b_im
                    bu_ref[c, r0:r0 + SUBLANES, 2 * half * j:2 * half * j + half] = h_re
                    bu_ref[c, r0:r0 + SUBLANES, 2 * half * j + half:2 * half * (j + 1)] = h_im
                    hr[j], hi[j] = h_re, h_im
        for j in range(S5_BLOCKS):
            st_ref[base:base + SUBLANES, 2 * half * j:2 * half * j + half] = hr[j]
            st_ref[base:base + SUBLANES, 2 * half * j + half:2 * half * (j + 1)] = hi[j]
    sout_ref[...] = st_ref[...]

    for c in range(chunks):
        y = jnp.concatenate(
            [jnp.dot(bu_ref[c, :, 2 * half * j:2 * half * (j + 1)].astype(BF16), cc_ref[j],
                     preferred_element_type=F32) for j in range(S5_BLOCKS)], axis=-1)
        y = y + d_ref[...] * u_ref[c * ROWS:(c + 1) * ROWS, :]
        g = _gelu(y)
        z = jnp.dot(g.astype(BF16), wg_ref[...], preferred_element_type=F32) + bg_ref[...]
        out = g * jax.nn.sigmoid(z)
        y_ref[c * ROWS:(c + 1) * ROWS, :] = _rms(out, go_ref[...]).astype(BF16)


def _s5_mixer(u, s0, ar8, ai8, bb, cc, d, w_glu_bf, b_glu, g_out, bt, tile0, tiles, chunks):
    tc = ROWS // bt
    assert tiles % chunks == 0 and tile0 % chunks == 0
    step_rows = chunks * ROWS
    return pl.pallas_call(
        functools.partial(_s5_body, tc, bt, chunks),
        grid=(tiles // chunks,),
        in_specs=[pl.BlockSpec((step_rows, D_S5), lambda i: (i + tile0 // chunks, 0)),
                  _full((bt, S5_COLS)), _full(ar8.shape), _full(ai8.shape), _full(bb.shape), _full(cc.shape),
                  _full((1, D_S5)), _full((D_S5, D_S5)), _full((1, D_S5)), _full((1, D_S5))],
        out_specs=[pl.BlockSpec((step_rows, D_S5), lambda i: (i, 0)), _full((bt, S5_COLS))],
        out_shape=[jax.ShapeDtypeStruct((tiles * ROWS, D_S5), BF16), jax.ShapeDtypeStruct((bt, S5_COLS), F32)],
        scratch_shapes=[pltpu.VMEM((chunks, ROWS, S5_COLS), F32), pltpu.VMEM((bt, S5_COLS), F32)],
        compiler_params=_params("arbitrary"),
        name="s5_mixer",
    )(u, s0, ar8, ai8, bb, cc, d, w_glu_bf, b_glu, g_out)


def _lru_body(tc, bt, xr_ref, gr_ref, h0_ref, cv0_ref, cw_ref, cb_ref, wa_ref, ba_ref, wx_ref, bx_ref,
              lam_ref, go_ref, y_ref, hout_ref, cvout_ref, xp_ref, a_ref, b_ref, h_ref):
    halo = (CONV_WIDTH - 1) * bt

    @pl.when(pl.program_id(0) == 0)
    def _():
        xp_ref[0:halo, :] = cv0_ref[...]
        h_ref[...] = h0_ref[...]

    xp_ref[halo:, :] = xr_ref[...]
    xc = cb_ref[...] + sum(xp_ref[k * bt:k * bt + ROWS, :] * cw_ref[k:k + 1, :] for k in range(CONV_WIDTH))
    xcb = xc.astype(BF16)
    r = jax.nn.sigmoid(jnp.dot(xcb, wa_ref[...], preferred_element_type=F32) + ba_ref[...])
    i = jax.nn.sigmoid(jnp.dot(xcb, wx_ref[...], preferred_element_type=F32) + bx_ref[...])
    lam = lam_ref[...]
    softplus_neg_lam = jnp.maximum(-lam, 0.0) + jnp.log1p(jnp.exp(-jnp.abs(lam)))
    log_a = -LRU_C * r * softplus_neg_lam
    a_ref[...] = jnp.exp(log_a)
    b_ref[...] = jnp.sqrt(-_expm1(2.0 * log_a)) * (i * xc)

    def sb_loop(sb, carry):
        base = pl.multiple_of(sb * SUBLANES, SUBLANES)

        def t_loop(t, h):
            r0 = pl.multiple_of(t * bt + base, SUBLANES)
            h = a_ref[pl.ds(r0, SUBLANES), :] * h + b_ref[pl.ds(r0, SUBLANES), :]
            b_ref[pl.ds(r0, SUBLANES), :] = h
            return h

        h_ref[pl.ds(base, SUBLANES), :] = lax.fori_loop(0, tc, t_loop, h_ref[pl.ds(base, SUBLANES), :])
        return carry

    lax.fori_loop(0, bt // SUBLANES, sb_loop, 0)

    y = b_ref[...] * _gelu(gr_ref[...])
    y_ref[...] = _rms(y, go_ref[...]).astype(BF16)
    tail = xp_ref[ROWS:ROWS + halo, :]
    xp_ref[0:halo, :] = tail
    cvout_ref[...] = tail
    hout_ref[...] = h_ref[...]


def _lru_mixer(xr, gr, h0, conv0_tm, conv_w, conv_b, wa_bd, b_a, wx_bd, b_x, lam, g_out, bt, tile0, tiles):
    tc = ROWS // bt
    halo = (CONV_WIDTH - 1) * bt
    in_rows = pl.BlockSpec((ROWS, D_LRU), lambda i: (i + tile0, 0))
    vec = _full((1, D_LRU))
    return pl.pallas_call(
        functools.partial(_lru_body, tc, bt),
        grid=(tiles,),
        in_specs=[in_rows, in_rows, _full((bt, D_LRU)), _full((halo, D_LRU)), _full((CONV_WIDTH, D_LRU)), vec,
                  _full((D_LRU, D_LRU)), vec, _full((D_LRU, D_LRU)), vec, vec, vec],
        out_specs=[pl.BlockSpec((ROWS, D_LRU), lambda i: (i, 0)), _full((bt, D_LRU)), _full((halo, D_LRU))],
        out_shape=[jax.ShapeDtypeStruct((tiles * ROWS, D_LRU), BF16), jax.ShapeDtypeStruct((bt, D_LRU), F32),
                   jax.ShapeDtypeStruct((halo, D_LRU), F32)],
        scratch_shapes=[pltpu.VMEM((ROWS + halo, D_LRU), F32), pltpu.VMEM((ROWS, D_LRU), F32),
                        pltpu.VMEM((ROWS, D_LRU), F32), pltpu.VMEM((bt, D_LRU), F32)],
        compiler_params=_params("arbitrary"),
        name="lru_mixer",
    )(xr, gr, h0, conv0_tm, conv_w, conv_b, wa_bd, b_a, wx_bd, b_x, lam, g_out)


def _route_tile(xn, x_hi, rw_hi, rw_lo, rb):
    x_lo = (xn - x_hi.astype(F32)).astype(BF16)
    nt = (((1,), (1,)), ((), ()))
    logits = (lax.dot_general(rw_hi, x_hi, nt, preferred_element_type=F32)
              + lax.dot_general(rw_hi, x_lo, nt, preferred_element_type=F32)
              + lax.dot_general(rw_lo, x_hi, nt, preferred_element_type=F32)) + rb

    e_iota = lax.broadcasted_iota(jnp.int32, (N_EXPERTS, ROWS), 0).astype(F32)
    work = logits
    sels, vals = [], []
    for _ in range(TOP_K):
        m = jnp.max(work, axis=0, keepdims=True)
        idx = jnp.min(jnp.where(work == m, e_iota, float(N_EXPERTS)), axis=0, keepdims=True)
        sel = e_iota == idx
        work = jnp.where(sel, -jnp.inf, work)
        sels.append(sel)
        vals.append(m)
    exps = [jnp.exp(v - vals[0]) for v in vals]
    denom = exps[0] + exps[1] + exps[2] + exps[3]
    gates = [e / denom for e in exps]

    onehot = sels[0] | sels[1] | sels[2] | sels[3]
    rr = lax.broadcasted_iota(jnp.int32, (ROWS, ROWS), 0)
    cc = lax.broadcasted_iota(jnp.int32, (ROWS, ROWS), 1)
    before = (rr < cc).astype(BF16)
    prefix = jnp.dot(onehot.astype(BF16), before, preferred_element_type=F32)
    cnt = jnp.sum(onehot.astype(F32), axis=1, keepdims=True)
    cnt_pad = jnp.floor((cnt + (SEG_ALIGN - 1)) * (1.0 / SEG_ALIGN)) * SEG_ALIGN
    er = lax.broadcasted_iota(jnp.int32, (N_EXPERTS, N_EXPERTS), 0)
    ec = lax.broadcasted_iota(jnp.int32, (N_EXPERTS, N_EXPERTS), 1)
    seg_start = jnp.dot((ec < er).astype(BF16), jnp.broadcast_to(cnt_pad, (N_EXPERTS, LANES)).astype(BF16),
                        preferred_element_type=F32)[:, 0:1]
    where_to = prefix + seg_start
    poss = [jnp.sum(jnp.where(s, where_to, 0.0), axis=0, keepdims=True) for s in sels]

    s_iota = lax.broadcasted_iota(jnp.int32, (2 * TOP_K, ROWS), 0)
    pg = jnp.zeros((2 * TOP_K, ROWS), F32)
    for k in range(TOP_K):
        pg = jnp.where(s_iota == k, poss[k], pg)
        pg = jnp.where(s_iota == TOP_K + k, gates[k], pg)
    return pg, cnt


def _out_body(subtiles, tm_rows, *refs):
    if tm_rows:
        ys_ref, yl_ref, x_ref, g1_ref, sc_ref, sh_ref, perm_ref, *refs = refs
    else:
        ys_ref, yl_ref, x_ref, g1_ref, sc_ref, sh_ref, *refs = refs
    wo_ref, gf_ref, rwh_ref, rwl_ref, rb_ref, h_ref, xn_ref, pg_ref, cnt_ref = refs
    for c in range(subtiles):
        rows = slice(c * ROWS, (c + 1) * ROWS)
        ys = ys_ref[rows, :]
        yl = yl_ref[rows, :]
        if tm_rows:
            ys = jnp.dot(perm_ref[...], ys, preferred_element_type=F32).astype(BF16)
            yl = jnp.dot(perm_ref[...], yl, preferred_element_type=F32).astype(BF16)
            tc = x_ref.shape[1] // subtiles
            x = x_ref[:, c * tc:(c + 1) * tc, :].reshape(ROWS, D_MODEL)
        else:
            x = x_ref[rows, :]
        mixed = (jnp.dot(ys, wo_ref[0], preferred_element_type=F32)
                 + jnp.dot(yl, wo_ref[1], preferred_element_type=F32))
        h = x + g1_ref[0] * mixed
        if tm_rows:
            h_ref[:, c * tc:(c + 1) * tc, :] = h.reshape(x_ref.shape[0], tc, D_MODEL)
        else:
            h_ref[rows, :] = h
        xn = _rms(h, gf_ref[...]) * (1.0 + sc_ref[0]) + sh_ref[0]
        x_hi = xn.astype(BF16)
        xn_ref[rows, :] = x_hi
        pg, cnt = _route_tile(xn, x_hi, rwh_ref[...], rwl_ref[...], rb_ref[...])
        pg_ref[:, rows] = pg
        cnt_ref[c] = jnp.broadcast_to(cnt, (N_EXPERTS, LANES))


def _out_proj(ys5, ylru, x, modpat, pattern, from_tm, w_out_bf, g_ffn, rw_hi, rw_lo, router_b, subtiles):
    tm_rows = from_tm is not None
    t_rows = ys5.shape[0]
    step_rows = subtiles * ROWS
    half_spec = pl.BlockSpec((step_rows, D_S5), lambda i: (i, 0))
    if tm_rows:
        x_spec = pl.BlockSpec((x.shape[0], step_rows // x.shape[0], D_MODEL), lambda i: (0, i, 0))
    else:
        x_spec = pl.BlockSpec((step_rows, D_MODEL), lambda i: (i, 0))
    mod_spec = lambda k: pl.BlockSpec((1, ROWS, D_MODEL), lambda i: (pattern, 0, k))
    perm = [_full((ROWS, ROWS))] if tm_rows else []
    return pl.pallas_call(
        functools.partial(_out_body, subtiles, tm_rows),
        grid=(t_rows // step_rows,),
        in_specs=[half_spec, half_spec, x_spec, mod_spec(2), mod_spec(4), mod_spec(3), *perm,
                  _full((2, D_S5, D_MODEL)), _full((1, D_MODEL)),
                  _full((N_EXPERTS, D_MODEL)), _full((N_EXPERTS, D_MODEL)), _full((N_EXPERTS, 1))],
        out_specs=[x_spec, pl.BlockSpec((step_rows, D_MODEL), lambda i: (i, 0)),
                   pl.BlockSpec((2 * TOP_K, step_rows), lambda i: (0, i)),
                   pl.BlockSpec((subtiles, N_EXPERTS, LANES), lambda i: (i, 0, 0))],
        out_shape=[jax.ShapeDtypeStruct(x.shape, F32), jax.ShapeDtypeStruct((t_rows, D_MODEL), BF16),
                   jax.ShapeDtypeStruct((2 * TOP_K, t_rows), F32),
                   jax.ShapeDtypeStruct((t_rows // ROWS, N_EXPERTS, LANES), F32)],
        compiler_params=_params("arbitrary"),
        name="out_proj_router",
    )(ys5, ylru, x, modpat, modpat, modpat, *([from_tm] if tm_rows else []), w_out_bf, g_ffn, rw_hi, rw_lo,
      router_b)


def _start_segments(n_ref, hbm_ref, vmem_ref, hbm, vmem_buf, sem, step, to_hbm, unroll):
    def seg(e):
        n = n_ref[step * N_EXPERTS + e]
        h0 = hbm_ref[step * N_EXPERTS + e]
        v0 = vmem_ref[step * N_EXPERTS + e]

        def piece(off, size):
            h = hbm.at[pl.ds(pl.multiple_of(h0 + off, SEG_ALIGN), size)]
            v = vmem_buf.at[pl.ds(pl.multiple_of(v0 + off, SEG_ALIGN), size)]
            (pltpu.make_async_copy(v, h, sem) if to_hbm else pltpu.make_async_copy(h, v, sem)).start()

        if unroll:
            _row_pieces(n, ROWS, piece)
        else:
            def big(k, carry):
                piece(k * BIG_PIECE, BIG_PIECE)
                return carry

            lax.fori_loop(0, n // BIG_PIECE, big, 0)
            _row_pieces(n % BIG_PIECE, BIG_PIECE // 2, lambda off, size: piece(n // BIG_PIECE * BIG_PIECE + off, size))

    if unroll:
        for e in range(N_EXPERTS):
            seg(e)
    else:
        def body(e, carry):
            seg(e)
            return carry

        lax.fori_loop(0, N_EXPERTS, body, 0)


def _wait_rows(total, largest, hbm, sem):
    _row_pieces(total, largest, lambda off, size: pltpu.make_async_copy(
        hbm.at[pl.ds(0, size)], hbm.at[pl.ds(0, size)], sem).wait())


def _dispatch_body(tiles_p, n_ref, glob_ref, local_ref, tot_ref, nb_ref, pg_ref, xp_ref, xsm_ref, xs_hbm, stage,
                   sem):
    j = pl.program_id(0)
    last = pl.num_programs(0) - 1
    slot = j % 2

    def unused_blocks(act):
        def blk(b, carry):
            act(pltpu.make_async_copy(stage.at[slot, pl.ds(0, MOE_TM)],
                                      xs_hbm.at[pl.ds(pl.multiple_of(b * MOE_TM, MOE_TM), MOE_TM)], sem.at[slot]))
            return carry
        lax.fori_loop(nb_ref[0], xs_hbm.shape[0] // MOE_TM, blk, 0)

    def wait_step(step, s):
        _wait_rows(tot_ref[step], WAIT_MAX_PIECE, xs_hbm, sem.at[s])

    @pl.when(j >= 2)
    def _():
        wait_step(jnp.maximum(j - 2, 0), slot)

    @pl.when(j < last)
    def _():
        x = jnp.where(j < tiles_p, xp_ref[...], xsm_ref[...])
        pos = pg_ref[0:TOP_K, :]
        for c in range(SORT_ROWS // SORT_CHUNK):
            r = (lax.broadcasted_iota(jnp.int32, (SORT_CHUNK, ROWS), 0) + c * SORT_CHUNK).astype(F32)
            pick = (r == pos[0:1]) | (r == pos[1:2]) | (r == pos[2:3]) | (r == pos[3:4])
            stage[slot, c * SORT_CHUNK:(c + 1) * SORT_CHUNK, :] = jnp.dot(
                pick.astype(BF16), x, preferred_element_type=F32).astype(BF16)

    @pl.when(j == last)
    def _():
        stage[slot, 0:MOE_TM, :] = jnp.zeros((MOE_TM, D_MODEL), BF16)

    _start_segments(n_ref, glob_ref, local_ref, xs_hbm, stage.at[slot], sem.at[slot], j, True, unroll=True)

    @pl.when(j == last)
    def _():
        unused_blocks(lambda c: c.start())

        @pl.when(j >= 1)
        def _():
            wait_step(jnp.maximum(j - 1, 0), 1 - slot)
        wait_step(j, slot)
        unused_blocks(lambda c: c.wait())


def _dispatch(seg_n, seg_glob, seg_local, seg_tot, nb_used, pg, xn_p, xn_s, n_rows):
    tiles_p = xn_p.shape[0] // ROWS
    tiles_s = xn_s.shape[0] // ROWS
    tiles = tiles_p + tiles_s
    grid_spec = pltpu.PrefetchScalarGridSpec(
        num_scalar_prefetch=5,
        grid=(tiles + 1,),
        in_specs=[pl.BlockSpec((2 * TOP_K, ROWS), lambda j, *_: (0, jnp.minimum(j, tiles - 1))),
                  pl.BlockSpec((ROWS, D_MODEL), lambda j, *_: (jnp.minimum(j, tiles_p - 1), 0)),
                  pl.BlockSpec((ROWS, D_MODEL), lambda j, *_: (jnp.clip(j - tiles_p, 0, tiles_s - 1), 0))],
        out_specs=pl.BlockSpec(memory_space=pl.ANY),
        scratch_shapes=[pltpu.VMEM((2, SORT_ROWS, D_MODEL), BF16), pltpu.SemaphoreType.DMA((2,))],
    )
    return pl.pallas_call(
        functools.partial(_dispatch_body, tiles_p),
        grid_spec=grid_spec,
        out_shape=jax.ShapeDtypeStruct((n_rows, D_MODEL), BF16),
        compiler_params=_params("arbitrary"),
        name="moe_dispatch",
    )(seg_n, seg_glob, seg_local, seg_tot, nb_used, pg, xn_p, xn_s)


def _moe_body(be_ref, nxt_ref, nv_ref, nb_ref, xs_ref, wgu_hbm, bgu_ref, wd_hbm, bd_ref, ys_ref,
              wgu_f32, wd_f32, wgu_bf, wd_bf, sem):
    i = pl.program_id(0)

    def weight_copies(e):
        return (pltpu.make_async_copy(wgu_hbm.at[e], wgu_f32, sem.at[0]),
                pltpu.make_async_copy(wd_hbm.at[e], wd_f32, sem.at[1]))

    @pl.when(i >= nb_ref[0])
    def _():
        ys_ref[...] = jnp.zeros_like(ys_ref)

    @pl.when(i < nb_ref[0])
    def _():
        e = be_ref[i]

        @pl.when(i == 0)
        def _():
            for c in weight_copies(e):
                c.start()

        @pl.when(jnp.logical_or(i == 0, e != be_ref[jnp.maximum(i - 1, 0)]))
        def _():
            for c in weight_copies(e):
                c.wait()
            wgu_bf[...] = wgu_f32[...].astype(BF16)
            wd_bf[...] = wd_f32[...].astype(BF16)

            @pl.when(nxt_ref[i] >= 0)
            def _():
                for c in weight_copies(nxt_ref[i]):
                    c.start()

        def ffn(x):
            hg = jnp.dot(x, wgu_bf[...], preferred_element_type=F32) + bgu_ref[0]
            gate = jnp.minimum(hg[:, :D_FF], SWIGLU_LIMIT)
            up = jnp.clip(hg[:, D_FF:], -SWIGLU_LIMIT, SWIGLU_LIMIT)
            act = (up + 1.0) * (gate * jax.nn.sigmoid(SWIGLU_ALPHA * gate))
            return (jnp.dot(act.astype(BF16), wd_bf[...], preferred_element_type=F32) + bd_ref[0]).astype(BF16)

        half = MOE_TM // 2

        @pl.when(nv_ref[i] > half)
        def _():
            ys_ref[...] = ffn(xs_ref[...])

        @pl.when(nv_ref[i] <= half)
        def _():
            ys_ref[0:half, :] = ffn(xs_ref[0:half, :])
            ys_ref[half:, :] = jnp.zeros((half, D_MODEL), BF16)


def _moe(xs, block_expert, block_next, block_rows, nb_used, w_gu, b_gu, w_down, b_down):
    n_blocks = xs.shape[0] // MOE_TM
    in_rows = pl.BlockSpec((MOE_TM, D_MODEL), lambda i, be, nx, nv, nb: (jnp.minimum(i, nb[0] - 1), 0))
    grid_spec = pltpu.PrefetchScalarGridSpec(
        num_scalar_prefetch=4,
        grid=(n_blocks,),
        in_specs=[in_rows,
                  pl.BlockSpec(memory_space=pl.ANY),
                  pl.BlockSpec((1, 1, 2 * D_FF), lambda i, be, nx, nv, nb: (be[i], 0, 0)),
                  pl.BlockSpec(memory_space=pl.ANY),
                  pl.BlockSpec((1, 1, D_MODEL), lambda i, be, nx, nv, nb: (be[i], 0, 0))],
        out_specs=pl.BlockSpec((MOE_TM, D_MODEL), lambda i, be, nx, nv, nb: (i, 0)),
        scratch_shapes=[pltpu.VMEM((D_MODEL, 2 * D_FF), F32), pltpu.VMEM((D_FF, D_MODEL), F32),
                        pltpu.VMEM((D_MODEL, 2 * D_FF), BF16), pltpu.VMEM((D_FF, D_MODEL), BF16),
                        pltpu.SemaphoreType.DMA((2,))],
    )
    return pl.pallas_call(
        _moe_body,
        grid_spec=grid_spec,
        out_shape=jax.ShapeDtypeStruct(xs.shape, BF16),
        compiler_params=_params("arbitrary"),
        name="moe_experts",
    )(block_expert, block_next, block_rows, nb_used, xs, w_gu, b_gu, w_down, b_down)


def _fin_body(tiles_p, n_ref, glob_ref, local_ref, tot_ref, hp_ref, hs_ref, pg_ref, g2_ref, gf_ref, ys_hbm,
              op_ref, os_ref, ybuf, sem):
    j = pl.program_id(0)
    tiles = pl.num_programs(0)
    slot = j % 2
    is_prompt = j < tiles_p

    def fetch(step, s):
        _start_segments(n_ref, glob_ref, local_ref, ys_hbm, ybuf.at[s], sem.at[s], step, False, unroll=False)

    @pl.when(j == 0)
    def _():
        ybuf[...] = jnp.zeros_like(ybuf)
        fetch(0, 0)

    _wait_rows(tot_ref[j], WAIT_MAX_PIECE, ys_hbm, sem.at[slot])

    @pl.when(j + 1 < tiles)
    def _():
        fetch(jnp.minimum(j + 1, tiles - 1), 1 - slot)

    pg = pg_ref[...]
    ff = jnp.zeros((ROWS, D_MODEL), F32)
    for c in range(SORT_ROWS // SORT_CHUNK):
        r = (lax.broadcasted_iota(jnp.int32, (ROWS, SORT_CHUNK), 1) + c * SORT_CHUNK).astype(F32)
        w = jnp.zeros((ROWS, SORT_CHUNK), F32)
        for k in range(TOP_K):
            w = jnp.where(r == pg[:, k:k + 1], pg[:, TOP_K + k:TOP_K + k + 1], w)
        ff = ff + jnp.dot(w.astype(BF16), ybuf[slot, c * SORT_CHUNK:(c + 1) * SORT_CHUNK, :],
                          preferred_element_type=F32)
    h = jnp.where(is_prompt, hp_ref[...].reshape(ROWS, D_MODEL), hs_ref[...])
    y = _rms(h + g2_ref[0] * ff, gf_ref[...])

    @pl.when(is_prompt)
    def _():
        op_ref[...] = y.reshape(op_ref.shape)

    @pl.when(jnp.logical_not(is_prompt))
    def _():
        os_ref[...] = y


def _final(seg_n, seg_glob, seg_local, seg_tot, h_p, h_s, pg_t, modpat, g_final, ys, tiles_p, tiles_s):
    hp_spec, hs_spec = _tile_specs(h_p.shape[0], tiles_p, tiles_s)
    grid_spec = pltpu.PrefetchScalarGridSpec(
        num_scalar_prefetch=4,
        grid=(tiles_p + tiles_s,),
        in_specs=[hp_spec, hs_spec,
                  pl.BlockSpec((ROWS, 2 * TOP_K), lambda j, *_: (j, 0)),
                  _mod_spec(5, tiles_p), _full((1, D_MODEL)),
                  pl.BlockSpec(memory_space=pl.ANY)],
        out_specs=[hp_spec, hs_spec],
        scratch_shapes=[pltpu.VMEM((2, SORT_ROWS, D_MODEL), BF16), pltpu.SemaphoreType.DMA((2,))],
    )
    return pl.pallas_call(
        functools.partial(_fin_body, tiles_p),
        grid_spec=grid_spec,
        out_shape=[jax.ShapeDtypeStruct(h_p.shape, F32), jax.ShapeDtypeStruct(h_s.shape, F32)],
        compiler_params=_params("arbitrary"),
        name="combine_final",
    )(seg_n, seg_glob, seg_local, seg_tot, h_p, h_s, pg_t, modpat, g_final, ys)


def _block_diag(w):
    h, i, j = w.shape
    return jnp.einsum('hij,hk->hikj', w, jnp.eye(h, dtype=w.dtype)).reshape(h * i, h * j)


def _s5_cols(re, im):
    b = re.shape[0]
    stack = jnp.stack([re.reshape(b, S5_BLOCKS, -1), im.reshape(b, S5_BLOCKS, -1)], axis=2)
    return stack.reshape(b, S5_COLS)


def _s5_uncols(cols):
    b = cols.shape[0]
    c = cols.reshape(b, S5_BLOCKS, 2, S5_GROUPS // S5_BLOCKS, S5_STATE)
    return (c[:, :, 0].reshape(b, S5_GROUPS, S5_STATE), c[:, :, 1].reshape(b, S5_GROUPS, S5_STATE))


def _moe_rows_bound(tiles):
    worst = tiles * (TOP_K * ROWS + N_EXPERTS * (SEG_ALIGN - 1)) + N_EXPERTS * (MOE_TM - SEG_ALIGN)
    return (worst + MOE_TM - 1) // MOE_TM * MOE_TM


def _plan(cnt):
    cnt = cnt.astype(jnp.int32)
    tiles = cnt.shape[0]
    cp = (cnt + SEG_ALIGN - 1) // SEG_ALIGN * SEG_ALIGN
    local = jnp.cumsum(cp, axis=1) - cp
    group = jnp.sum(cp, axis=0)
    group_pad = (group + MOE_TM - 1) // MOE_TM * MOE_TM
    pend = jnp.cumsum(group_pad)
    pstart = pend - group_pad
    glob = pstart[None, :] + jnp.cumsum(cp, axis=0) - cp
    gap = group_pad - group
    seg_n = jnp.concatenate([cp, gap[None]], axis=0).reshape(-1)
    seg_local = jnp.concatenate([local, jnp.zeros((1, N_EXPERTS), jnp.int32)], axis=0).reshape(-1)
    seg_glob = jnp.concatenate([glob, (pstart + group)[None]], axis=0).reshape(-1)
    seg_tot = jnp.concatenate([jnp.sum(cp, axis=1), jnp.sum(gap)[None]]).astype(jnp.int32)
    n_blocks = _moe_rows_bound(tiles) // MOE_TM
    block_row0 = jnp.arange(n_blocks, dtype=jnp.int32) * MOE_TM
    block_expert = jnp.minimum(jnp.sum(block_row0[:, None] >= pend[None, :], axis=1), N_EXPERTS - 1).astype(jnp.int32)
    nb_used = (pend[-1] // MOE_TM).astype(jnp.int32).reshape(1)
    experts = jnp.arange(N_EXPERTS, dtype=jnp.int32)
    later_owner = jnp.where((experts[None, :] > experts[:, None]) & (group_pad[None, :] > 0), experts[None, :],
                            N_EXPERTS)
    next_owner = jnp.min(later_owner, axis=1)
    next_owner = jnp.where(next_owner == N_EXPERTS, -1, next_owner).astype(jnp.int32)
    owner = block_expert[:, None] == experts[None, :]
    block_next = jnp.sum(jnp.where(owner, next_owner[None, :], 0), axis=1).astype(jnp.int32)
    group_end = jnp.sum(jnp.where(owner, (pstart + group)[None, :], 0), axis=1)
    block_rows = jnp.clip(group_end - block_row0, 0, MOE_TM).astype(jnp.int32)
    return seg_n, seg_glob, seg_local, seg_tot, block_expert, block_next, block_rows, nb_used


def kernel(x_prompt, x_sample, state_s5_re, state_s5_im, state_lru_h, state_conv, c_prompt, c_sample, w_ada, b_ada, g_mix, w_in, s5_a_re, s5_a_im, s5_log_dt, s5_b_re, s5_b_im, s5_c_re, s5_c_im, s5_d, s5_w_glu, s5_b_glu, lru_conv_w, lru_conv_b, lru_w_a, lru_b_a, lru_w_x, lru_b_x, lru_lambda, g_out_s5, g_out_lru, w_out, g_ffn, router_w, router_b, moe_w_gu, moe_b_gu, moe_w_down, moe_b_down, g_final):
    assert w_ada.shape[0] == 1, "one layer"
    bp, lp, _ = x_prompt.shape
    bs, ls, _ = x_sample.shape
    assert ROWS % bp == 0 and ROWS % bs == 0 and (bp * lp) % ROWS == 0 and (bs * ls) % ROWS == 0
    tiles_p = bp * lp // ROWS
    tiles_s = bs * ls // ROWS
    row = lambda v: v.reshape(1, -1)

    ab_re, ab_im, bb_re, bb_im = _s5_prep(s5_a_re[0], s5_a_im[0], s5_log_dt[0].reshape(S5_GROUPS, 1),
                                          jnp.swapaxes(s5_b_re[0], 1, 2), jnp.swapaxes(s5_b_im[0], 1, 2))
    gpb = S5_GROUPS // S5_BLOCKS
    eye = jnp.eye(gpb, dtype=F32)

    def in_blocks(b):
        b = b.reshape(S5_BLOCKS, gpb, S5_GROUP_CH, S5_STATE)
        return jnp.einsum('bgjn,gh->bgjhn', b, eye).reshape(S5_BLOCKS, gpb * S5_GROUP_CH, gpb * S5_STATE)

    def out_blocks(c):
        c = c.reshape(S5_BLOCKS, gpb, S5_GROUP_CH, S5_STATE)
        return jnp.einsum('bgjn,gh->bgnhj', c, eye).reshape(S5_BLOCKS, gpb * S5_STATE, gpb * S5_GROUP_CH)

    ar8 = jnp.broadcast_to(ab_re.reshape(1, -1), (SUBLANES, S5_GROUPS * S5_STATE))
    ai8 = jnp.broadcast_to(ab_im.reshape(1, -1), (SUBLANES, S5_GROUPS * S5_STATE))
    bb = jnp.concatenate([in_blocks(bb_re), in_blocks(bb_im)], axis=-1).astype(BF16)
    cc = jnp.concatenate([out_blocks(s5_c_re[0]), -out_blocks(s5_c_im[0])], axis=1).astype(BF16)
    wa_bd = _block_diag(lru_w_a[0]).astype(BF16)
    wx_bd = _block_diag(lru_w_x[0]).astype(BF16)
    rw_t = router_w[0].T
    rw_hi = rw_t.astype(BF16)
    rw_lo = (rw_t - rw_hi.astype(F32)).astype(BF16)

    tc = ROWS // bp
    modpat = _adaln(jnp.concatenate([c_prompt, c_sample], axis=0), w_ada[0], row(b_ada[0]), bp)
    r = jnp.arange(ROWS)
    tm_of = (r % tc) * bp + r // tc
    to_tm = (r[:, None] == tm_of[None, :]).astype(BF16)
    from_tm = to_tm.T
    pair = 2 if tiles_p % 2 == 0 else 1

    x_s = jnp.swapaxes(x_sample, 0, 1).reshape(bs * ls, D_MODEL)
    w_in_bf = w_in[0].astype(BF16)
    u_p, xr_p, gr_p = _in_proj(x_prompt, modpat, 0, to_tm, row(g_mix[0]), w_in_bf, pair)
    u_s, xr_s, gr_s = _in_proj(x_s, modpat, 1, None, row(g_mix[0]), w_in_bf, 1)

    def conv_tm(cv):
        return jnp.swapaxes(cv, 0, 1).reshape(-1, D_LRU)

    def conv_bm(cv, b):
        return jnp.swapaxes(cv.reshape(CONV_WIDTH - 1, b, D_LRU), 0, 1)

    s5_args = (ar8, ai8, bb, cc, row(s5_d[0]), s5_w_glu[0].astype(BF16), row(s5_b_glu[0]), row(g_out_s5[0]))
    lru_args = (lru_conv_w[0], row(lru_conv_b[0]), wa_bd, row(lru_b_a[0]), wx_bd, row(lru_b_x[0]),
                row(lru_lambda[0]), row(g_out_lru[0]))
    ys5_p, s5p = _s5_mixer(u_p, jnp.zeros((bp, S5_COLS), F32), *s5_args, bp, 0, tiles_p, pair)
    ys5_s, s5s = _s5_mixer(u_s, _s5_cols(state_s5_re[0], state_s5_im[0]), *s5_args, bs, 0, tiles_s, 1)
    ylru_p, hp, cvp = _lru_mixer(xr_p, gr_p, jnp.zeros((bp, D_LRU), F32),
                                 jnp.zeros(((CONV_WIDTH - 1) * bp, D_LRU), F32), *lru_args, bp, 0, tiles_p)
    ylru_s, hs, cvs = _lru_mixer(xr_s, gr_s, state_lru_h[0], conv_tm(state_conv[0]), *lru_args, bs, 0, tiles_s)

    out_args = (w_out[0].astype(BF16).reshape(2, D_S5, D_MODEL), row(g_ffn[0]), rw_hi, rw_lo,
                router_b[0].reshape(N_EXPERTS, 1))
    h_p, xn_p, pg_p, cnt_p = _out_proj(ys5_p, ylru_p, x_prompt, modpat, 0, from_tm, *out_args, pair)
    h_s, xn_s, pg_s, cnt_s = _out_proj(ys5_s, ylru_s, x_s, modpat, 1, None, *out_args, 1)
    pg = jnp.concatenate([pg_p, pg_s], axis=1)
    cnt = jnp.concatenate([cnt_p, cnt_s], axis=0)

    seg_n, seg_glob, seg_local, seg_tot, block_expert, block_next, block_rows, nb_used = _plan(cnt[:, :, 0])
    xs = _dispatch(seg_n, seg_glob, seg_local, seg_tot, nb_used, pg, xn_p, xn_s,
                   _moe_rows_bound(tiles_p + tiles_s))
    ys = _moe(xs, block_expert, block_next, block_rows, nb_used, moe_w_gu[0], moe_b_gu[0].reshape(N_EXPERTS, 1, 2 * D_FF),
              moe_w_down[0], moe_b_down[0].reshape(N_EXPERTS, 1, D_MODEL))
    y_prompt, y_s = _final(seg_n, seg_glob, seg_local, seg_tot, h_p, h_s, pg.T, modpat, row(g_final), ys,
                           tiles_p, tiles_s)
    y_sample = jnp.swapaxes(y_s.reshape(ls, bs, D_MODEL), 0, 1)
    s5p_re, s5p_im = _s5_uncols(s5p)
    s5s_re, s5s_im = _s5_uncols(s5s)
    return (y_prompt, y_sample,
            s5p_re[None], s5p_im[None], hp[None], conv_bm(cvp, bp)[None],
            s5s_re[None], s5s_im[None], hs[None], conv_bm(cvs, bs)[None])
```

```python
import functools

import jax
import jax.numpy as jnp
from jax import lax
from jax.experimental import pallas as pl
from jax.experimental.pallas import tpu as pltpu

D_MODEL = 1024
D_S5 = 512
D_LRU = 512
S5_GROUPS = 32
S5_GROUP_CH = 16
S5_STATE = 64
S5_COLS = 2 * S5_GROUPS * S5_STATE
S5_BLOCKS = 4
LRU_HEADS = 8
LRU_HEAD_DIM = 64
LRU_C = 8.0
CONV_WIDTH = 4
N_EXPERTS = 32
TOP_K = 4
D_FF = 1024
SWIGLU_LIMIT = 7.0
SWIGLU_ALPHA = 1.702
N_MOD = 6
EPS = 1e-6

ROWS = 512
MOE_TM = 512
BIG_PIECE = 128
WAIT_MAX_PIECE = 8192
SEG_ALIGN = 16
SORT_ROWS = 2560
SORT_CHUNK = 512
SUBLANES = 8
LANES = 128
VMEM_LIMIT = 48 * 1024 * 1024

BF16 = jnp.bfloat16
F32 = jnp.float32


def _params(*sem):
    return pltpu.CompilerParams(dimension_semantics=sem, vmem_limit_bytes=VMEM_LIMIT)


def _full(shape):
    return pl.BlockSpec(shape, lambda *_: (0,) * len(shape))


def _rms(x, g):
    return x * lax.rsqrt(jnp.mean(x * x, axis=-1, keepdims=True) + EPS) * g


def _gelu(x):
    return 0.5 * x * (1.0 + lax.erf(x * (2.0 ** -0.5)))


def _expm1(x):
    u = jnp.exp(x)
    d = u - 1.0
    return jnp.where(d == 0.0, x, jnp.where(d == -1.0, -1.0, d * x / jnp.log(u)))


def _row_pieces(n, largest, fn):
    off = 0
    bit = largest
    while bit >= SEG_ALIGN:
        @pl.when((n & bit) != 0)
        def _(off=off, bit=bit):
            fn(off, bit)
        off = off + (n & bit)
        bit //= 2


def _mod_body(bp, c_ref, w_ref, b_ref, o_ref):
    c = c_ref[...]
    s = (c * jax.nn.sigmoid(c)).astype(BF16)
    mod = jnp.dot(s, w_ref[...].astype(BF16), preferred_element_type=F32) + b_ref[...]
    bs = mod.shape[0] - bp
    o_ref[0] = jnp.broadcast_to(mod[:bp][:, None, :], (bp, ROWS // bp, D_MODEL)).reshape(ROWS, D_MODEL)
    o_ref[1] = jnp.broadcast_to(mod[bp:][None], (ROWS // bs, bs, D_MODEL)).reshape(ROWS, D_MODEL)


def _adaln(c, w_ada, b_ada, bp):
    m = c.shape[0]
    return pl.pallas_call(
        functools.partial(_mod_body, bp),
        grid=(N_MOD,),
        in_specs=[pl.BlockSpec((m, D_MODEL), lambda j: (0, 0)),
                  pl.BlockSpec((D_MODEL, D_MODEL), lambda j: (0, j)),
                  pl.BlockSpec((1, D_MODEL), lambda j: (0, j))],
        out_specs=pl.BlockSpec((2, ROWS, D_MODEL), lambda j: (0, 0, j)),
        out_shape=jax.ShapeDtypeStruct((2, ROWS, N_MOD * D_MODEL), F32),
        compiler_params=_params("arbitrary"),
        name="adaln",
    )(c, w_ada, b_ada)


def _mod_spec(k, tiles_p):
    return pl.BlockSpec((1, ROWS, D_MODEL), lambda i, *_: (jnp.where(i < tiles_p, 0, 1), 0, k))


def _tile_specs(bp, tiles_p, tiles_s):
    tc = ROWS // bp
    p_spec = pl.BlockSpec((bp, tc, D_MODEL), lambda i, *_: (0, jnp.minimum(i, tiles_p - 1), 0))
    s_spec = pl.BlockSpec((ROWS, D_MODEL), lambda i, *_: (jnp.clip(i - tiles_p, 0, tiles_s - 1), 0))
    return p_spec, s_spec


def _in_body(subtiles, bt, to_tm, *refs):
    if to_tm:
        x_ref, sc_ref, sh_ref, perm_ref, *refs = refs
    else:
        x_ref, sc_ref, sh_ref, *refs = refs
    (g_ref, w_ref, cv0_ref, cw_ref, cb_ref, wa_ref, ba_ref, wx_ref, bx_ref, lam_ref,
     u_ref, a_ref, b_ref, gg_ref, cvout_ref, xp_ref) = refs
    halo = (CONV_WIDTH - 1) * bt

    @pl.when(pl.program_id(0) == 0)
    def _():
        xp_ref[0:halo, :] = cv0_ref[...]

    lam = lam_ref[...]
    softplus_neg_lam = jnp.maximum(-lam, 0.0) + jnp.log1p(jnp.exp(-jnp.abs(lam)))
    for c in range(subtiles):
        rows = slice(c * ROWS, (c + 1) * ROWS)
        if to_tm:
            tc = x_ref.shape[1] // subtiles
            x = x_ref[:, c * tc:(c + 1) * tc, :].reshape(ROWS, D_MODEL)
        else:
            x = x_ref[rows, :]
        xn = (_rms(x, g_ref[...]) * (1.0 + sc_ref[0]) + sh_ref[0]).astype(BF16)
        if to_tm:
            xn = jnp.dot(perm_ref[...], xn, preferred_element_type=F32).astype(BF16)
        p = jnp.dot(xn, w_ref[...], preferred_element_type=F32)
        u_ref[rows, :] = p[:, :D_S5]
        gg_ref[rows, :] = _gelu(p[:, D_S5 + D_LRU:])

        xp_ref[halo + c * ROWS:halo + (c + 1) * ROWS, :] = p[:, D_S5:D_S5 + D_LRU]
        xc = cb_ref[...] + sum(xp_ref[c * ROWS + k * bt:(c + 1) * ROWS + k * bt, :] * cw_ref[k:k + 1, :]
                               for k in range(CONV_WIDTH))
        xcb = xc.astype(BF16)
        r = jax.nn.sigmoid(jnp.dot(xcb, wa_ref[...], preferred_element_type=F32) + ba_ref[...])
        i = jax.nn.sigmoid(jnp.dot(xcb, wx_ref[...], preferred_element_type=F32) + bx_ref[...])
        log_a = -LRU_C * r * softplus_neg_lam
        a_ref[rows, :] = jnp.exp(log_a)
        b_ref[rows, :] = jnp.sqrt(-_expm1(2.0 * log_a)) * (i * xc)

    tail = xp_ref[subtiles * ROWS:subtiles * ROWS + halo, :]
    xp_ref[0:halo, :] = tail
    cvout_ref[...] = tail


def _in_proj(x, modpat, pattern, to_tm, g_mix, w_in_bf, conv0_tm, conv_w, conv_b, wa_bd, b_a, wx_bd, b_x, lam, bt,
             subtiles):
    tm = to_tm is not None
    t_rows = x.shape[0] * x.shape[1] if tm else x.shape[0]
    step_rows = subtiles * ROWS
    halo = (CONV_WIDTH - 1) * bt
    if tm:
        x_spec = pl.BlockSpec((x.shape[0], step_rows // x.shape[0], D_MODEL), lambda i: (0, i, 0))
    else:
        x_spec = pl.BlockSpec((step_rows, D_MODEL), lambda i: (i, 0))
    mod_spec = lambda k: pl.BlockSpec((1, ROWS, D_MODEL), lambda i: (pattern, 0, k))
    row_spec = pl.BlockSpec((step_rows, D_S5), lambda i: (i, 0))
    vec = _full((1, D_LRU))
    return pl.pallas_call(
        functools.partial(_in_body, subtiles, bt, tm),
        grid=(t_rows // step_rows,),
        in_specs=[x_spec, mod_spec(1), mod_spec(0), *([_full((ROWS, ROWS))] if tm else []), _full((1, D_MODEL)),
                  _full((D_MODEL, D_S5 + 2 * D_LRU)), _full((halo, D_LRU)), _full((CONV_WIDTH, D_LRU)), vec,
                  _full((D_LRU, D_LRU)), vec, _full((D_LRU, D_LRU)), vec, vec],
        out_specs=[row_spec, row_spec, row_spec, row_spec, _full((halo, D_LRU))],
        out_shape=[jax.ShapeDtypeStruct((t_rows, D_S5), F32)] * 4 + [jax.ShapeDtypeStruct((halo, D_LRU), F32)],
        scratch_shapes=[pltpu.VMEM((step_rows + halo, D_LRU), F32)],
        compiler_params=_params("arbitrary"),
        name="in_proj",
    )(x, modpat, modpat, *([to_tm] if tm else []), g_mix, w_in_bf, conv0_tm, conv_w, conv_b, wa_bd, b_a, wx_bd, b_x,
      lam)


def _s5_prep_body(are_ref, aim_ref, ldt_ref, bre_ref, bim_ref, abre_ref, abim_ref, bbre_ref, bbim_ref):
    a_re = are_ref[...]
    a_im = aim_ref[...]
    dt = jnp.exp(ldt_ref[...])
    mag = jnp.exp(dt * a_re)
    ang = dt * a_im
    ab_re = mag * jnp.cos(ang)
    ab_im = mag * jnp.sin(ang)
    den = a_re * a_re + a_im * a_im
    q_re = ((ab_re - 1.0) * a_re + ab_im * a_im) / den
    q_im = (ab_im * a_re - (ab_re - 1.0) * a_im) / den
    abre_ref[...] = ab_re
    abim_ref[...] = ab_im
    b_re = bre_ref[...]
    b_im = bim_ref[...]
    bbre_ref[...] = q_re[:, None, :] * b_re - q_im[:, None, :] * b_im
    bbim_ref[...] = q_re[:, None, :] * b_im + q_im[:, None, :] * b_re


def _s5_prep(a_re, a_im, log_dt, b_re_t, b_im_t):
    gn = jax.ShapeDtypeStruct((S5_GROUPS, S5_STATE), F32)
    gjn = jax.ShapeDtypeStruct((S5_GROUPS, S5_GROUP_CH, S5_STATE), F32)
    return pl.pallas_call(_s5_prep_body, out_shape=[gn, gn, gjn, gjn], name="s5_prep")(
        a_re, a_im, log_dt, b_re_t, b_im_t)


def _s5_body(tc, bt, chunks, u_ref, s0_ref, ar_ref, ai_ref, bb_ref, cc_ref, d_ref, wg_ref, bg_ref, go_ref,
             la_ref, lb_ref, lg_ref, lh0_ref, lgo_ref, y_ref, sout_ref, yl_ref, lhout_ref,
             bu_ref, st_ref, lh_ref, lst_ref):
    half = S5_COLS // S5_BLOCKS // 2

    @pl.when(pl.program_id(0) == 0)
    def _():
        st_ref[...] = s0_ref[...]
        lst_ref[...] = lh0_ref[...]

    for c in range(chunks):
        ub = u_ref[c * ROWS:(c + 1) * ROWS, :].astype(BF16)
        for j in range(S5_BLOCKS):
            bu_ref[c, :, 2 * half * j:2 * half * (j + 1)] = jnp.dot(
                ub[:, LANES * j:LANES * (j + 1)], bb_ref[j], preferred_element_type=F32)

    for sb in range(bt // SUBLANES):
        base = sb * SUBLANES
        hr = [st_ref[base:base + SUBLANES, 2 * half * j:2 * half * j + half] for j in range(S5_BLOCKS)]
        hi = [st_ref[base:base + SUBLANES, 2 * half * j + half:2 * half * (j + 1)] for j in range(S5_BLOCKS)]
        hl = lst_ref[base:base + SUBLANES, :]
        for c in range(chunks):
            for t in range(tc):
                r0 = t * bt + base
                row = c * ROWS + r0
                hl = la_ref[row:row + SUBLANES, :] * hl + lb_ref[row:row + SUBLANES, :]
                lh_ref[row:row + SUBLANES, :] = hl
                for j in range(S5_BLOCKS):
                    ar = ar_ref[:, half * j:half * (j + 1)]
                    ai = ai_ref[:, half * j:half * (j + 1)]
                    b_re = bu_ref[c, r0:r0 + SUBLANES, 2 * half * j:2 * half * j + half]
                    b_im = bu_ref[c, r0:r0 + SUBLANES, 2 * half * j + half:2 * half * (j + 1)]
                    h_re = ar * hr[j] - ai * hi[j] + b_re
                    h_im = ar * hi[j] + ai * hr[j] + b_im
                    bu_ref[c, r0:r0 + SUBLANES, 2 * half * j:2 * half * j + half] = h_re
                    bu_ref[c, r0:r0 + SUBLANES, 2 * half * j + half:2 * half * (j + 1)] = h_im
                    hr[j], hi[j] = h_re, h_im
        for j in range(S5_BLOCKS):
            st_ref[base:base + SUBLANES, 2 * half * j:2 * half * j + half] = hr[j]
            st_ref[base:base + SUBLANES, 2 * half * j + half:2 * half * (j + 1)] = hi[j]
        lst_ref[base:base + SUBLANES, :] = hl
    sout_ref[...] = st_ref[...]
    lhout_ref[...] = lst_ref[...]
    yl_ref[...] = _rms(lh_ref[...] * lg_ref[...], lgo_ref[...]).astype(BF16)

    for c in range(chunks):
        y = jnp.concatenate(
            [jnp.dot(bu_ref[c, :, 2 * half * j:2 * half * (j + 1)].astype(BF16), cc_ref[j],
                     preferred_element_type=F32) for j in range(S5_BLOCKS)], axis=-1)
        y = y + d_ref[...] * u_ref[c * ROWS:(c + 1) * ROWS, :]
        g = _gelu(y)
        z = jnp.dot(g.astype(BF16), wg_ref[...], preferred_element_type=F32) + bg_ref[...]
        out = g * jax.nn.sigmoid(z)
        y_ref[c * ROWS:(c + 1) * ROWS, :] = _rms(out, go_ref[...]).astype(BF16)


def _mixers(u, s0, ar8, ai8, bb, cc, d, w_glu_bf, b_glu, g_out_s5, lru_a, lru_b, lru_gg, lru_h0, g_out_lru, bt,
            chunks):
    tc = ROWS // bt
    t_rows = u.shape[0]
    step_rows = chunks * ROWS
    assert t_rows % step_rows == 0
    rows_f32 = pl.BlockSpec((step_rows, D_S5), lambda i: (i, 0))
    vec = _full((1, D_S5))
    return pl.pallas_call(
        functools.partial(_s5_body, tc, bt, chunks),
        grid=(t_rows // step_rows,),
        in_specs=[rows_f32, _full((bt, S5_COLS)), _full(ar8.shape), _full(ai8.shape), _full(bb.shape),
                  _full(cc.shape), vec, _full((D_S5, D_S5)), vec, vec,
                  rows_f32, rows_f32, rows_f32, _full((bt, D_LRU)), vec],
        out_specs=[rows_f32, _full((bt, S5_COLS)), rows_f32, _full((bt, D_LRU))],
        out_shape=[jax.ShapeDtypeStruct((t_rows, D_S5), BF16), jax.ShapeDtypeStruct((bt, S5_COLS), F32),
                   jax.ShapeDtypeStruct((t_rows, D_LRU), BF16), jax.ShapeDtypeStruct((bt, D_LRU), F32)],
        scratch_shapes=[pltpu.VMEM((chunks, ROWS, S5_COLS), F32), pltpu.VMEM((bt, S5_COLS), F32),
                        pltpu.VMEM((step_rows, D_LRU), F32), pltpu.VMEM((bt, D_LRU), F32)],
        compiler_params=_params("arbitrary"),
        name="s5_lru_mixers",
    )(u, s0, ar8, ai8, bb, cc, d, w_glu_bf, b_glu, g_out_s5, lru_a, lru_b, lru_gg, lru_h0, g_out_lru)


def _route_tile(xn, x_hi, rw_hi, rw_lo, rb):
    x_lo = (xn - x_hi.astype(F32)).astype(BF16)
    nt = (((1,), (1,)), ((), ()))
    logits = (lax.dot_general(rw_hi, x_hi, nt, preferred_element_type=F32)
              + lax.dot_general(rw_hi, x_lo, nt, preferred_element_type=F32)
              + lax.dot_general(rw_lo, x_hi, nt, preferred_element_type=F32)) + rb

    e_iota = lax.broadcasted_iota(jnp.int32, (N_EXPERTS, ROWS), 0).astype(F32)
    work = logits
    sels, vals = [], []
    for _ in range(TOP_K):
        m = jnp.max(work, axis=0, keepdims=True)
        idx = jnp.min(jnp.where(work == m, e_iota, float(N_EXPERTS)), axis=0, keepdims=True)
        sel = e_iota == idx
        work = jnp.where(sel, -jnp.inf, work)
        sels.append(sel)
        vals.append(m)
    exps = [jnp.exp(v - vals[0]) for v in vals]
    denom = exps[0] + exps[1] + exps[2] + exps[3]
    gates = [e / denom for e in exps]

    onehot = sels[0] | sels[1] | sels[2] | sels[3]
    rr = lax.broadcasted_iota(jnp.int32, (ROWS, ROWS), 0)
    cc = lax.broadcasted_iota(jnp.int32, (ROWS, ROWS), 1)
    before = (rr < cc).astype(BF16)
    prefix = jnp.dot(onehot.astype(BF16), before, preferred_element_type=F32)
    cnt = jnp.sum(onehot.astype(F32), axis=1, keepdims=True)
    cnt_pad = jnp.floor((cnt + (SEG_ALIGN - 1)) * (1.0 / SEG_ALIGN)) * SEG_ALIGN
    er = lax.broadcasted_iota(jnp.int32, (N_EXPERTS, N_EXPERTS), 0)
    ec = lax.broadcasted_iota(jnp.int32, (N_EXPERTS, N_EXPERTS), 1)
    seg_start = jnp.dot((ec < er).astype(BF16), jnp.broadcast_to(cnt_pad, (N_EXPERTS, LANES)).astype(BF16),
                        preferred_element_type=F32)[:, 0:1]
    where_to = prefix + seg_start
    poss = [jnp.sum(jnp.where(s, where_to, 0.0), axis=0, keepdims=True) for s in sels]

    s_iota = lax.broadcasted_iota(jnp.int32, (2 * TOP_K, ROWS), 0)
    pg = jnp.zeros((2 * TOP_K, ROWS), F32)
    for k in range(TOP_K):
        pg = jnp.where(s_iota == k, poss[k], pg)
        pg = jnp.where(s_iota == TOP_K + k, gates[k], pg)
    return pg, cnt


def _out_body(subtiles, tm_rows, *refs):
    if tm_rows:
        ys_ref, yl_ref, x_ref, g1_ref, sc_ref, sh_ref, perm_ref, *refs = refs
    else:
        ys_ref, yl_ref, x_ref, g1_ref, sc_ref, sh_ref, *refs = refs
    wo_ref, gf_ref, rwh_ref, rwl_ref, rb_ref, h_ref, xn_ref, pg_ref, cnt_ref = refs
    for c in range(subtiles):
        rows = slice(c * ROWS, (c + 1) * ROWS)
        ys = ys_ref[rows, :]
        yl = yl_ref[rows, :]
        if tm_rows:
            ys = jnp.dot(perm_ref[...], ys, preferred_element_type=F32).astype(BF16)
            yl = jnp.dot(perm_ref[...], yl, preferred_element_type=F32).astype(BF16)
            tc = x_ref.shape[1] // subtiles
            x = x_ref[:, c * tc:(c + 1) * tc, :].reshape(ROWS, D_MODEL)
        else:
            x = x_ref[rows, :]
        mixed = (jnp.dot(ys, wo_ref[0], preferred_element_type=F32)
                 + jnp.dot(yl, wo_ref[1], preferred_element_type=F32))
        h = x + g1_ref[0] * mixed
        if tm_rows:
            h_ref[:, c * tc:(c + 1) * tc, :] = h.reshape(x_ref.shape[0], tc, D_MODEL)
        else:
            h_ref[rows, :] = h
        xn = _rms(h, gf_ref[...]) * (1.0 + sc_ref[0]) + sh_ref[0]
        x_hi = xn.astype(BF16)
        xn_ref[rows, :] = x_hi
        pg, cnt = _route_tile(xn, x_hi, rwh_ref[...], rwl_ref[...], rb_ref[...])
        pg_ref[:, rows] = pg
        cnt_ref[c] = jnp.broadcast_to(cnt, (N_EXPERTS, LANES))


def _out_proj(ys5, ylru, x, modpat, pattern, from_tm, w_out_bf, g_ffn, rw_hi, rw_lo, router_b, subtiles):
    tm_rows = from_tm is not None
    t_rows = ys5.shape[0]
    step_rows = subtiles * ROWS
    half_spec = pl.BlockSpec((step_rows, D_S5), lambda i: (i, 0))
    if tm_rows:
        x_spec = pl.BlockSpec((x.shape[0], step_rows // x.shape[0], D_MODEL), lambda i: (0, i, 0))
    else:
        x_spec = pl.BlockSpec((step_rows, D_MODEL), lambda i: (i, 0))
    mod_spec = lambda k: pl.BlockSpec((1, ROWS, D_MODEL), lambda i: (pattern, 0, k))
    perm = [_full((ROWS, ROWS))] if tm_rows else []
    return pl.pallas_call(
        functools.partial(_out_body, subtiles, tm_rows),
        grid=(t_rows // step_rows,),
        in_specs=[half_spec, half_spec, x_spec, mod_spec(2), mod_spec(4), mod_spec(3), *perm,
                  _full((2, D_S5, D_MODEL)), _full((1, D_MODEL)),
                  _full((N_EXPERTS, D_MODEL)), _full((N_EXPERTS, D_MODEL)), _full((N_EXPERTS, 1))],
        out_specs=[x_spec, pl.BlockSpec((step_rows, D_MODEL), lambda i: (i, 0)),
                   pl.BlockSpec((2 * TOP_K, step_rows), lambda i: (0, i)),
                   pl.BlockSpec((subtiles, N_EXPERTS, LANES), lambda i: (i, 0, 0))],
        out_shape=[jax.ShapeDtypeStruct(x.shape, F32), jax.ShapeDtypeStruct((t_rows, D_MODEL), BF16),
                   jax.ShapeDtypeStruct((2 * TOP_K, t_rows), F32),
                   jax.ShapeDtypeStruct((t_rows // ROWS, N_EXPERTS, LANES), F32)],
        compiler_params=_params("arbitrary"),
        name="out_proj_router",
    )(ys5, ylru, x, modpat, modpat, modpat, *([from_tm] if tm_rows else []), w_out_bf, g_ffn, rw_hi, rw_lo,
      router_b)


def _start_segments(n_ref, hbm_ref, vmem_ref, hbm, vmem_buf, sem, step, to_hbm, unroll):
    def seg(e):
        n = n_ref[step * N_EXPERTS + e]
        h0 = hbm_ref[step * N_EXPERTS + e]
        v0 = vmem_ref[step * N_EXPERTS + e]

        def piece(off, size):
            h = hbm.at[pl.ds(pl.multiple_of(h0 + off, SEG_ALIGN), size)]
            v = vmem_buf.at[pl.ds(pl.multiple_of(v0 + off, SEG_ALIGN), size)]
            (pltpu.make_async_copy(v, h, sem) if to_hbm else pltpu.make_async_copy(h, v, sem)).start()

        if unroll:
            _row_pieces(n, ROWS, piece)
        else:
            def big(k, carry):
                piece(k * BIG_PIECE, BIG_PIECE)
                return carry

            lax.fori_loop(0, n // BIG_PIECE, big, 0)
            _row_pieces(n % BIG_PIECE, BIG_PIECE // 2, lambda off, size: piece(n // BIG_PIECE * BIG_PIECE + off, size))

    if unroll:
        for e in range(N_EXPERTS):
            seg(e)
    else:
        def body(e, carry):
            seg(e)
            return carry

        lax.fori_loop(0, N_EXPERTS, body, 0)


def _wait_rows(total, largest, hbm, sem):
    _row_pieces(total, largest, lambda off, size: pltpu.make_async_copy(
        hbm.at[pl.ds(0, size)], hbm.at[pl.ds(0, size)], sem).wait())


def _dispatch_body(tiles_p, n_ref, glob_ref, local_ref, tot_ref, nb_ref, pg_ref, xp_ref, xsm_ref, xs_hbm, stage,
                   sem):
    j = pl.program_id(0)
    last = pl.num_programs(0) - 1
    slot = j % 2

    def unused_blocks(act):
        def blk(b, carry):
            act(pltpu.make_async_copy(stage.at[slot, pl.ds(0, MOE_TM)],
                                      xs_hbm.at[pl.ds(pl.multiple_of(b * MOE_TM, MOE_TM), MOE_TM)], sem.at[slot]))
            return carry
        lax.fori_loop(nb_ref[0], xs_hbm.shape[0] // MOE_TM, blk, 0)

    def wait_step(step, s):
        _wait_rows(tot_ref[step], WAIT_MAX_PIECE, xs_hbm, sem.at[s])

    @pl.when(j >= 2)
    def _():
        wait_step(jnp.maximum(j - 2, 0), slot)

    @pl.when(j < last)
    def _():
        x = jnp.where(j < tiles_p, xp_ref[...], xsm_ref[...])
        pos = pg_ref[0:TOP_K, :]
        for c in range(SORT_ROWS // SORT_CHUNK):
            r = (lax.broadcasted_iota(jnp.int32, (SORT_CHUNK, ROWS), 0) + c * SORT_CHUNK).astype(F32)
            pick = (r == pos[0:1]) | (r == pos[1:2]) | (r == pos[2:3]) | (r == pos[3:4])
            stage[slot, c * SORT_CHUNK:(c + 1) * SORT_CHUNK, :] = jnp.dot(
                pick.astype(BF16), x, preferred_element_type=F32).astype(BF16)

    @pl.when(j == last)
    def _():
        stage[slot, 0:MOE_TM, :] = jnp.zeros((MOE_TM, D_MODEL), BF16)

    _start_segments(n_ref, glob_ref, local_ref, xs_hbm, stage.at[slot], sem.at[slot], j, True, unroll=True)

    @pl.when(j == last)
    def _():
        unused_blocks(lambda c: c.start())

        @pl.when(j >= 1)
        def _():
            wait_step(jnp.maximum(j - 1, 0), 1 - slot)
        wait_step(j, slot)
        unused_blocks(lambda c: c.wait())


def _dispatch(seg_n, seg_glob, seg_local, seg_tot, nb_used, pg, xn_p, xn_s, n_rows):
    tiles_p = xn_p.shape[0] // ROWS
    tiles_s = xn_s.shape[0] // ROWS
    tiles = tiles_p + tiles_s
    grid_spec = pltpu.PrefetchScalarGridSpec(
        num_scalar_prefetch=5,
        grid=(tiles + 1,),
        in_specs=[pl.BlockSpec((2 * TOP_K, ROWS), lambda j, *_: (0, jnp.minimum(j, tiles - 1))),
                  pl.BlockSpec((ROWS, D_MODEL), lambda j, *_: (jnp.minimum(j, tiles_p - 1), 0)),
                  pl.BlockSpec((ROWS, D_MODEL), lambda j, *_: (jnp.clip(j - tiles_p, 0, tiles_s - 1), 0))],
        out_specs=pl.BlockSpec(memory_space=pl.ANY),
        scratch_shapes=[pltpu.VMEM((2, SORT_ROWS, D_MODEL), BF16), pltpu.SemaphoreType.DMA((2,))],
    )
    return pl.pallas_call(
        functools.partial(_dispatch_body, tiles_p),
        grid_spec=grid_spec,
        out_shape=jax.ShapeDtypeStruct((n_rows, D_MODEL), BF16),
        compiler_params=_params("arbitrary"),
        name="moe_dispatch",
    )(seg_n, seg_glob, seg_local, seg_tot, nb_used, pg, xn_p, xn_s)


def _moe_body(be_ref, nxt_ref, nv_ref, nb_ref, xs_ref, wgu_hbm, bgu_ref, wd_hbm, bd_ref, ys_ref,
              wgu_f32, wd_f32, wgu_bf, wd_bf, sem):
    i = pl.program_id(0)

    def weight_copies(e):
        return (pltpu.make_async_copy(wgu_hbm.at[e], wgu_f32, sem.at[0]),
                pltpu.make_async_copy(wd_hbm.at[e], wd_f32, sem.at[1]))

    @pl.when(i >= nb_ref[0])
    def _():
        ys_ref[...] = jnp.zeros_like(ys_ref)

    @pl.when(i < nb_ref[0])
    def _():
        e = be_ref[i]

        @pl.when(i == 0)
        def _():
            for c in weight_copies(e):
                c.start()

        @pl.when(jnp.logical_or(i == 0, e != be_ref[jnp.maximum(i - 1, 0)]))
        def _():
            for c in weight_copies(e):
                c.wait()
            wgu_bf[...] = wgu_f32[...].astype(BF16)
            wd_bf[...] = wd_f32[...].astype(BF16)

            @pl.when(nxt_ref[i] >= 0)
            def _():
                for c in weight_copies(nxt_ref[i]):
                    c.start()

        def ffn(x):
            hg = jnp.dot(x, wgu_bf[...], preferred_element_type=F32) + bgu_ref[0]
            gate = jnp.minimum(hg[:, :D_FF], SWIGLU_LIMIT)
            up = jnp.clip(hg[:, D_FF:], -SWIGLU_LIMIT, SWIGLU_LIMIT)
            act = (up + 1.0) * (gate * jax.nn.sigmoid(SWIGLU_ALPHA * gate))
            return (jnp.dot(act.astype(BF16), wd_bf[...], preferred_element_type=F32) + bd_ref[0]).astype(BF16)

        half = MOE_TM // 2

        @pl.when(nv_ref[i] > half)
        def _():
            ys_ref[...] = ffn(xs_ref[...])

        @pl.when(nv_ref[i] <= half)
        def _():
            ys_ref[0:half, :] = ffn(xs_ref[0:half, :])
            ys_ref[half:, :] = jnp.zeros((half, D_MODEL), BF16)


def _moe(xs, block_expert, block_next, block_rows, nb_used, w_gu, b_gu, w_down, b_down):
    n_blocks = xs.shape[0] // MOE_TM
    in_rows = pl.BlockSpec((MOE_TM, D_MODEL), lambda i, be, nx, nv, nb: (jnp.minimum(i, nb[0] - 1), 0))
    grid_spec = pltpu.PrefetchScalarGridSpec(
        num_scalar_prefetch=4,
        grid=(n_blocks,),
        in_specs=[in_rows,
                  pl.BlockSpec(memory_space=pl.ANY),
                  pl.BlockSpec((1, 1, 2 * D_FF), lambda i, be, nx, nv, nb: (be[i], 0, 0)),
                  pl.BlockSpec(memory_space=pl.ANY),
                  pl.BlockSpec((1, 1, D_MODEL), lambda i, be, nx, nv, nb: (be[i], 0, 0))],
        out_specs=pl.BlockSpec((MOE_TM, D_MODEL), lambda i, be, nx, nv, nb: (i, 0)),
        scratch_shapes=[pltpu.VMEM((D_MODEL, 2 * D_FF), F32), pltpu.VMEM((D_FF, D_MODEL), F32),
                        pltpu.VMEM((D_MODEL, 2 * D_FF), BF16), pltpu.VMEM((D_FF, D_MODEL), BF16),
                        pltpu.SemaphoreType.DMA((2,))],
    )
    return pl.pallas_call(
        _moe_body,
        grid_spec=grid_spec,
        out_shape=jax.ShapeDtypeStruct(xs.shape, BF16),
        compiler_params=_params("arbitrary"),
        name="moe_experts",
    )(block_expert, block_next, block_rows, nb_used, xs, w_gu, b_gu, w_down, b_down)


def _fin_body(tiles_p, n_ref, glob_ref, local_ref, tot_ref, hp_ref, hs_ref, pg_ref, g2_ref, gf_ref, ys_hbm,
              op_ref, os_ref, ybuf, sem):
    j = pl.program_id(0)
    tiles = pl.num_programs(0)
    slot = j % 2
    is_prompt = j < tiles_p

    def fetch(step, s):
        _start_segments(n_ref, glob_ref, local_ref, ys_hbm, ybuf.at[s], sem.at[s], step, False, unroll=False)

    @pl.when(j == 0)
    def _():
        ybuf[...] = jnp.zeros_like(ybuf)
        fetch(0, 0)

    _wait_rows(tot_ref[j], WAIT_MAX_PIECE, ys_hbm, sem.at[slot])

    @pl.when(j + 1 < tiles)
    def _():
        fetch(jnp.minimum(j + 1, tiles - 1), 1 - slot)

    pg = pg_ref[...]
    ff = jnp.zeros((ROWS, D_MODEL), F32)
    for c in range(SORT_ROWS // SORT_CHUNK):
        r = (lax.broadcasted_iota(jnp.int32, (ROWS, SORT_CHUNK), 1) + c * SORT_CHUNK).astype(F32)
        w = jnp.zeros((ROWS, SORT_CHUNK), F32)
        for k in range(TOP_K):
            w = jnp.where(r == pg[:, k:k + 1], pg[:, TOP_K + k:TOP_K + k + 1], w)
        ff = ff + jnp.dot(w.astype(BF16), ybuf[slot, c * SORT_CHUNK:(c + 1) * SORT_CHUNK, :],
                          preferred_element_type=F32)
    h = jnp.where(is_prompt, hp_ref[...].reshape(ROWS, D_MODEL), hs_ref[...])
    y = _rms(h + g2_ref[0] * ff, gf_ref[...])

    @pl.when(is_prompt)
    def _():
        op_ref[...] = y.reshape(op_ref.shape)

    @pl.when(jnp.logical_not(is_prompt))
    def _():
        os_ref[...] = y


def _final(seg_n, seg_glob, seg_local, seg_tot, h_p, h_s, pg_t, modpat, g_final, ys, tiles_p, tiles_s):
    hp_spec, hs_spec = _tile_specs(h_p.shape[0], tiles_p, tiles_s)
    grid_spec = pltpu.PrefetchScalarGridSpec(
        num_scalar_prefetch=4,
        grid=(tiles_p + tiles_s,),
        in_specs=[hp_spec, hs_spec,
                  pl.BlockSpec((ROWS, 2 * TOP_K), lambda j, *_: (j, 0)),
                  _mod_spec(5, tiles_p), _full((1, D_MODEL)),
                  pl.BlockSpec(memory_space=pl.ANY)],
        out_specs=[hp_spec, hs_spec],
        scratch_shapes=[pltpu.VMEM((2, SORT_ROWS, D_MODEL), BF16), pltpu.SemaphoreType.DMA((2,))],
    )
    return pl.pallas_call(
        functools.partial(_fin_body, tiles_p),
        grid_spec=grid_spec,
        out_shape=[jax.ShapeDtypeStruct(h_p.shape, F32), jax.ShapeDtypeStruct(h_s.shape, F32)],
        compiler_params=_params("arbitrary"),
        name="combine_final",
    )(seg_n, seg_glob, seg_local, seg_tot, h_p, h_s, pg_t, modpat, g_final, ys)


def _block_diag(w):
    h, i, j = w.shape
    return jnp.einsum('hij,hk->hikj', w, jnp.eye(h, dtype=w.dtype)).reshape(h * i, h * j)


def _s5_cols(re, im):
    b = re.shape[0]
    stack = jnp.stack([re.reshape(b, S5_BLOCKS, -1), im.reshape(b, S5_BLOCKS, -1)], axis=2)
    return stack.reshape(b, S5_COLS)


def _s5_uncols(cols):
    b = cols.shape[0]
    c = cols.reshape(b, S5_BLOCKS, 2, S5_GROUPS // S5_BLOCKS, S5_STATE)
    return (c[:, :, 0].reshape(b, S5_GROUPS, S5_STATE), c[:, :, 1].reshape(b, S5_GROUPS, S5_STATE))


def _moe_rows_bound(tiles):
    worst = tiles * (TOP_K * ROWS + N_EXPERTS * (SEG_ALIGN - 1)) + N_EXPERTS * (MOE_TM - SEG_ALIGN)
    return (worst + MOE_TM - 1) // MOE_TM * MOE_TM


def _plan(cnt):
    cnt = cnt.astype(jnp.int32)
    tiles = cnt.shape[0]
    cp = (cnt + SEG_ALIGN - 1) // SEG_ALIGN * SEG_ALIGN
    local = jnp.cumsum(cp, axis=1) - cp
    group = jnp.sum(cp, axis=0)
    group_pad = (group + MOE_TM - 1) // MOE_TM * MOE_TM
    pend = jnp.cumsum(group_pad)
    pstart = pend - group_pad
    glob = pstart[None, :] + jnp.cumsum(cp, axis=0) - cp
    gap = group_pad - group
    seg_n = jnp.concatenate([cp, gap[None]], axis=0).reshape(-1)
    seg_local = jnp.concatenate([local, jnp.zeros((1, N_EXPERTS), jnp.int32)], axis=0).reshape(-1)
    seg_glob = jnp.concatenate([glob, (pstart + group)[None]], axis=0).reshape(-1)
    seg_tot = jnp.concatenate([jnp.sum(cp, axis=1), jnp.sum(gap)[None]]).astype(jnp.int32)
    n_blocks = _moe_rows_bound(tiles) // MOE_TM
    block_row0 = jnp.arange(n_blocks, dtype=jnp.int32) * MOE_TM
    block_expert = jnp.minimum(jnp.sum(block_row0[:, None] >= pend[None, :], axis=1), N_EXPERTS - 1).astype(jnp.int32)
    nb_used = (pend[-1] // MOE_TM).astype(jnp.int32).reshape(1)
    experts = jnp.arange(N_EXPERTS, dtype=jnp.int32)
    later_owner = jnp.where((experts[None, :] > experts[:, None]) & (group_pad[None, :] > 0), experts[None, :],
                            N_EXPERTS)
    next_owner = jnp.min(later_owner, axis=1)
    next_owner = jnp.where(next_owner == N_EXPERTS, -1, next_owner).astype(jnp.int32)
    owner = block_expert[:, None] == experts[None, :]
    block_next = jnp.sum(jnp.where(owner, next_owner[None, :], 0), axis=1).astype(jnp.int32)
    group_end = jnp.sum(jnp.where(owner, (pstart + group)[None, :], 0), axis=1)
    block_rows = jnp.clip(group_end - block_row0, 0, MOE_TM).astype(jnp.int32)
    return seg_n, seg_glob, seg_local, seg_tot, block_expert, block_next, block_rows, nb_used


def kernel(x_prompt, x_sample, state_s5_re, state_s5_im, state_lru_h, state_conv, c_prompt, c_sample, w_ada, b_ada, g_mix, w_in, s5_a_re, s5_a_im, s5_log_dt, s5_b_re, s5_b_im, s5_c_re, s5_c_im, s5_d, s5_w_glu, s5_b_glu, lru_conv_w, lru_conv_b, lru_w_a, lru_b_a, lru_w_x, lru_b_x, lru_lambda, g_out_s5, g_out_lru, w_out, g_ffn, router_w, router_b, moe_w_gu, moe_b_gu, moe_w_down, moe_b_down, g_final):
    assert w_ada.shape[0] == 1, "one layer"
    bp, lp, _ = x_prompt.shape
    bs, ls, _ = x_sample.shape
    assert ROWS % bp == 0 and ROWS % bs == 0 and (bp * lp) % ROWS == 0 and (bs * ls) % ROWS == 0
    tiles_p = bp * lp // ROWS
    tiles_s = bs * ls // ROWS
    row = lambda v: v.reshape(1, -1)

    ab_re, ab_im, bb_re, bb_im = _s5_prep(s5_a_re[0], s5_a_im[0], s5_log_dt[0].reshape(S5_GROUPS, 1),
                                          jnp.swapaxes(s5_b_re[0], 1, 2), jnp.swapaxes(s5_b_im[0], 1, 2))
    gpb = S5_GROUPS // S5_BLOCKS
    eye = jnp.eye(gpb, dtype=F32)

    def in_blocks(b):
        b = b.reshape(S5_BLOCKS, gpb, S5_GROUP_CH, S5_STATE)
        return jnp.einsum('bgjn,gh->bgjhn', b, eye).reshape(S5_BLOCKS, gpb * S5_GROUP_CH, gpb * S5_STATE)

    def out_blocks(c):
        c = c.reshape(S5_BLOCKS, gpb, S5_GROUP_CH, S5_STATE)
        return jnp.einsum('bgjn,gh->bgnhj', c, eye).reshape(S5_BLOCKS, gpb * S5_STATE, gpb * S5_GROUP_CH)

    ar8 = jnp.broadcast_to(ab_re.reshape(1, -1), (SUBLANES, S5_GROUPS * S5_STATE))
    ai8 = jnp.broadcast_to(ab_im.reshape(1, -1), (SUBLANES, S5_GROUPS * S5_STATE))
    bb = jnp.concatenate([in_blocks(bb_re), in_blocks(bb_im)], axis=-1).astype(BF16)
    cc = jnp.concatenate([out_blocks(s5_c_re[0]), -out_blocks(s5_c_im[0])], axis=1).astype(BF16)
    wa_bd = _block_diag(lru_w_a[0]).astype(BF16)
    wx_bd = _block_diag(lru_w_x[0]).astype(BF16)
    rw_t = router_w[0].T
    rw_hi = rw_t.astype(BF16)
    rw_lo = (rw_t - rw_hi.astype(F32)).astype(BF16)

    tc = ROWS // bp
    modpat = _adaln(jnp.concatenate([c_prompt, c_sample], axis=0), w_ada[0], row(b_ada[0]), bp)
    r = jnp.arange(ROWS)
    tm_of = (r % tc) * bp + r // tc
    to_tm = (r[:, None] == tm_of[None, :]).astype(BF16)
    from_tm = to_tm.T
    pair = 2 if tiles_p % 2 == 0 else 1

    def conv_tm(cv):
        return jnp.swapaxes(cv, 0, 1).reshape(-1, D_LRU)

    def conv_bm(cv, b):
        return jnp.swapaxes(cv.reshape(CONV_WIDTH - 1, b, D_LRU), 0, 1)

    x_s = jnp.swapaxes(x_sample, 0, 1).reshape(bs * ls, D_MODEL)
    in_args = (row(g_mix[0]), w_in[0].astype(BF16))
    lru_args = (lru_conv_w[0], row(lru_conv_b[0]), wa_bd, row(lru_b_a[0]), wx_bd, row(lru_b_x[0]),
                row(lru_lambda[0]))
    u_p, la_p, lb_p, lg_p, cvp = _in_proj(x_prompt, modpat, 0, to_tm, *in_args,
                                          jnp.zeros(((CONV_WIDTH - 1) * bp, D_LRU), F32), *lru_args, bp, pair)
    u_s, la_s, lb_s, lg_s, cvs = _in_proj(x_s, modpat, 1, None, *in_args, conv_tm(state_conv[0]), *lru_args, bs, 1)

    s5_args = (ar8, ai8, bb, cc, row(s5_d[0]), s5_w_glu[0].astype(BF16), row(s5_b_glu[0]), row(g_out_s5[0]))
    ys5_p, s5p, ylru_p, hp = _mixers(u_p, jnp.zeros((bp, S5_COLS), F32), *s5_args, la_p, lb_p, lg_p,
                                     jnp.zeros((bp, D_LRU), F32), row(g_out_lru[0]), bp, pair)
    ys5_s, s5s, ylru_s, hs = _mixers(u_s, _s5_cols(state_s5_re[0], state_s5_im[0]), *s5_args, la_s, lb_s, lg_s,
                                     state_lru_h[0], row(g_out_lru[0]), bs, 1)

    out_args = (w_out[0].astype(BF16).reshape(2, D_S5, D_MODEL), row(g_ffn[0]), rw_hi, rw_lo,
                router_b[0].reshape(N_EXPERTS, 1))
    h_p, xn_p, pg_p, cnt_p = _out_proj(ys5_p, ylru_p, x_prompt, modpat, 0, from_tm, *out_args, pair)
    h_s, xn_s, pg_s, cnt_s = _out_proj(ys5_s, ylru_s, x_s, modpat, 1, None, *out_args, 1)
    pg = jnp.concatenate([pg_p, pg_s], axis=1)
    cnt = jnp.concatenate([cnt_p, cnt_s], axis=0)

    seg_n, seg_glob, seg_local, seg_tot, block_expert, block_next, block_rows, nb_used = _plan(cnt[:, :, 0])
    xs = _dispatch(seg_n, seg_glob, seg_local, seg_tot, nb_used, pg, xn_p, xn_s,
                   _moe_rows_bound(tiles_p + tiles_s))
    ys = _moe(xs, block_expert, block_next, block_rows, nb_used, moe_w_gu[0], moe_b_gu[0].reshape(N_EXPERTS, 1, 2 * D_FF),
              moe_w_down[0], moe_b_down[0].reshape(N_EXPERTS, 1, D_MODEL))
    y_prompt, y_s = _final(seg_n, seg_glob, seg_local, seg_tot, h_p, h_s, pg.T, modpat, row(g_final), ys,
                           tiles_p, tiles_s)
    y_sample = jnp.swapaxes(y_s.reshape(ls, bs, D_MODEL), 0, 1)
    s5p_re, s5p_im = _s5_uncols(s5p)
    s5s_re, s5s_im = _s5_uncols(s5s)
    return (y_prompt, y_sample,
            s5p_re[None], s5p_im[None], hp[None], conv_bm(cvp, bp)[None],
            s5s_re[None], s5s_im[None], hs[None], conv_bm(cvs, bs)[None])
```

```python
import functools

import jax
import jax.numpy as jnp
from jax import lax
from jax.experimental import pallas as pl
from jax.experimental.pallas import tpu as pltpu

D_MODEL = 1024
D_S5 = 512
D_LRU = 512
S5_GROUPS = 32
S5_GROUP_CH = 16
S5_STATE = 64
S5_COLS = 2 * S5_GROUPS * S5_STATE
S5_BLOCKS = 4
LRU_HEADS = 8
LRU_HEAD_DIM = 64
LRU_C = 8.0
CONV_WIDTH = 4
N_EXPERTS = 32
TOP_K = 4
D_FF = 1024
SWIGLU_LIMIT = 7.0
SWIGLU_ALPHA = 1.702
N_MOD = 6
EPS = 1e-6

ROWS = 512
MOE_TM = 512
BIG_PIECE = 128
WAIT_MAX_PIECE = 8192
SEG_ALIGN = 16
SORT_ROWS = 2560
SORT_CHUNK = 256
SUBLANES = 8
LANES = 128
VMEM_LIMIT = 48 * 1024 * 1024

BF16 = jnp.bfloat16
F32 = jnp.float32


def _params(*sem):
    return pltpu.CompilerParams(dimension_semantics=sem, vmem_limit_bytes=VMEM_LIMIT)


def _full(shape):
    return pl.BlockSpec(shape, lambda *_: (0,) * len(shape))


def _rms(x, g):
    return x * lax.rsqrt(jnp.mean(x * x, axis=-1, keepdims=True) + EPS) * g


def _gelu(x):
    return 0.5 * x * (1.0 + lax.erf(x * (2.0 ** -0.5)))


def _expm1(x):
    u = jnp.exp(x)
    d = u - 1.0
    return jnp.where(d == 0.0, x, jnp.where(d == -1.0, -1.0, d * x / jnp.log(u)))


def _row_pieces(n, largest, fn):
    off = 0
    bit = largest
    while bit >= SEG_ALIGN:
        @pl.when((n & bit) != 0)
        def _(off=off, bit=bit):
            fn(off, bit)
        off = off + (n & bit)
        bit //= 2


def _mod_body(bp, c_ref, w_ref, b_ref, o_ref):
    c = c_ref[...]
    s = (c * jax.nn.sigmoid(c)).astype(BF16)
    mod = jnp.dot(s, w_ref[...].astype(BF16), preferred_element_type=F32) + b_ref[...]
    bs = mod.shape[0] - bp
    o_ref[0] = jnp.broadcast_to(mod[:bp][:, None, :], (bp, ROWS // bp, D_MODEL)).reshape(ROWS, D_MODEL)
    o_ref[1] = jnp.broadcast_to(mod[bp:][None], (ROWS // bs, bs, D_MODEL)).reshape(ROWS, D_MODEL)


def _adaln(c, w_ada, b_ada, bp):
    m = c.shape[0]
    return pl.pallas_call(
        functools.partial(_mod_body, bp),
        grid=(N_MOD,),
        in_specs=[pl.BlockSpec((m, D_MODEL), lambda j: (0, 0)),
                  pl.BlockSpec((D_MODEL, D_MODEL), lambda j: (0, j)),
                  pl.BlockSpec((1, D_MODEL), lambda j: (0, j))],
        out_specs=pl.BlockSpec((2, ROWS, D_MODEL), lambda j: (0, 0, j)),
        out_shape=jax.ShapeDtypeStruct((2, ROWS, N_MOD * D_MODEL), F32),
        compiler_params=_params("arbitrary"),
        name="adaln",
    )(c, w_ada, b_ada)


def _mod_spec(k, tiles_p):
    return pl.BlockSpec((1, ROWS, D_MODEL), lambda i, *_: (jnp.where(i < tiles_p, 0, 1), 0, k))


def _tile_specs(bp, tiles_p, tiles_s):
    tc = ROWS // bp
    p_spec = pl.BlockSpec((bp, tc, D_MODEL), lambda i, *_: (0, jnp.minimum(i, tiles_p - 1), 0))
    s_spec = pl.BlockSpec((ROWS, D_MODEL), lambda i, *_: (jnp.clip(i - tiles_p, 0, tiles_s - 1), 0))
    return p_spec, s_spec


def _in_body(subtiles, bt, to_tm, *refs):
    if to_tm:
        x_ref, sc_ref, sh_ref, perm_ref, *refs = refs
    else:
        x_ref, sc_ref, sh_ref, *refs = refs
    (g_ref, w_ref, cv0_ref, cw_ref, cb_ref, wa_ref, ba_ref, wx_ref, bx_ref, lam_ref,
     u_ref, a_ref, b_ref, gg_ref, cvout_ref, xp_ref) = refs
    halo = (CONV_WIDTH - 1) * bt

    @pl.when(pl.program_id(0) == 0)
    def _():
        xp_ref[0:halo, :] = cv0_ref[...]

    lam = lam_ref[...]
    softplus_neg_lam = jnp.maximum(-lam, 0.0) + jnp.log1p(jnp.exp(-jnp.abs(lam)))
    for c in range(subtiles):
        rows = slice(c * ROWS, (c + 1) * ROWS)
        if to_tm:
            tc = x_ref.shape[1] // subtiles
            x = x_ref[:, c * tc:(c + 1) * tc, :].reshape(ROWS, D_MODEL)
        else:
            x = x_ref[rows, :]
        xn = (_rms(x, g_ref[...]) * (1.0 + sc_ref[0]) + sh_ref[0]).astype(BF16)
        if to_tm:
            xn = jnp.dot(perm_ref[...], xn, preferred_element_type=F32).astype(BF16)
        p = jnp.dot(xn, w_ref[...], preferred_element_type=F32)
        u_ref[rows, :] = p[:, :D_S5]
        gg_ref[rows, :] = _gelu(p[:, D_S5 + D_LRU:])

        xp_ref[halo + c * ROWS:halo + (c + 1) * ROWS, :] = p[:, D_S5:D_S5 + D_LRU]
        xc = cb_ref[...] + sum(xp_ref[c * ROWS + k * bt:(c + 1) * ROWS + k * bt, :] * cw_ref[k:k + 1, :]
                               for k in range(CONV_WIDTH))
        xcb = xc.astype(BF16)
        r = jax.nn.sigmoid(jnp.dot(xcb, wa_ref[...], preferred_element_type=F32) + ba_ref[...])
        i = jax.nn.sigmoid(jnp.dot(xcb, wx_ref[...], preferred_element_type=F32) + bx_ref[...])
        log_a = -LRU_C * r * softplus_neg_lam
        a_ref[rows, :] = jnp.exp(log_a)
        b_ref[rows, :] = jnp.sqrt(-_expm1(2.0 * log_a)) * (i * xc)

    tail = xp_ref[subtiles * ROWS:subtiles * ROWS + halo, :]
    xp_ref[0:halo, :] = tail
    cvout_ref[...] = tail


def _in_proj(x, modpat, pattern, to_tm, g_mix, w_in_bf, conv0_tm, conv_w, conv_b, wa_bd, b_a, wx_bd, b_x, lam, bt,
             subtiles):
    tm = to_tm is not None
    t_rows = x.shape[0] * x.shape[1] if tm else x.shape[0]
    step_rows = subtiles * ROWS
    halo = (CONV_WIDTH - 1) * bt
    if tm:
        x_spec = pl.BlockSpec((x.shape[0], step_rows // x.shape[0], D_MODEL), lambda i: (0, i, 0))
    else:
        x_spec = pl.BlockSpec((step_rows, D_MODEL), lambda i: (i, 0))
    mod_spec = lambda k: pl.BlockSpec((1, ROWS, D_MODEL), lambda i: (pattern, 0, k))
    row_spec = pl.BlockSpec((step_rows, D_S5), lambda i: (i, 0))
    vec = _full((1, D_LRU))
    return pl.pallas_call(
        functools.partial(_in_body, subtiles, bt, tm),
        grid=(t_rows // step_rows,),
        in_specs=[x_spec, mod_spec(1), mod_spec(0), *([_full((ROWS, ROWS))] if tm else []), _full((1, D_MODEL)),
                  _full((D_MODEL, D_S5 + 2 * D_LRU)), _full((halo, D_LRU)), _full((CONV_WIDTH, D_LRU)), vec,
                  _full((D_LRU, D_LRU)), vec, _full((D_LRU, D_LRU)), vec, vec],
        out_specs=[row_spec, row_spec, row_spec, row_spec, _full((halo, D_LRU))],
        out_shape=[jax.ShapeDtypeStruct((t_rows, D_S5), F32)] * 4 + [jax.ShapeDtypeStruct((halo, D_LRU), F32)],
        scratch_shapes=[pltpu.VMEM((step_rows + halo, D_LRU), F32)],
        compiler_params=_params("arbitrary"),
        name="in_proj",
    )(x, modpat, modpat, *([to_tm] if tm else []), g_mix, w_in_bf, conv0_tm, conv_w, conv_b, wa_bd, b_a, wx_bd, b_x,
      lam)


def _s5_prep_body(are_ref, aim_ref, ldt_ref, bre_ref, bim_ref, abre_ref, abim_ref, bbre_ref, bbim_ref):
    a_re = are_ref[...]
    a_im = aim_ref[...]
    dt = jnp.exp(ldt_ref[...])
    mag = jnp.exp(dt * a_re)
    ang = dt * a_im
    ab_re = mag * jnp.cos(ang)
    ab_im = mag * jnp.sin(ang)
    den = a_re * a_re + a_im * a_im
    q_re = ((ab_re - 1.0) * a_re + ab_im * a_im) / den
    q_im = (ab_im * a_re - (ab_re - 1.0) * a_im) / den
    abre_ref[...] = ab_re
    abim_ref[...] = ab_im
    b_re = bre_ref[...]
    b_im = bim_ref[...]
    bbre_ref[...] = q_re[:, None, :] * b_re - q_im[:, None, :] * b_im
    bbim_ref[...] = q_re[:, None, :] * b_im + q_im[:, None, :] * b_re


def _s5_prep(a_re, a_im, log_dt, b_re_t, b_im_t):
    gn = jax.ShapeDtypeStruct((S5_GROUPS, S5_STATE), F32)
    gjn = jax.ShapeDtypeStruct((S5_GROUPS, S5_GROUP_CH, S5_STATE), F32)
    return pl.pallas_call(_s5_prep_body, out_shape=[gn, gn, gjn, gjn], name="s5_prep")(
        a_re, a_im, log_dt, b_re_t, b_im_t)


def _s5_body(tc, bt, chunks, u_ref, s0_ref, ar_ref, ai_ref, bb_ref, cc_ref, d_ref, wg_ref, bg_ref, go_ref,
             la_ref, lb_ref, lg_ref, lh0_ref, lgo_ref, y_ref, sout_ref, yl_ref, lhout_ref,
             bu_ref, st_ref, lh_ref, lst_ref):
    half = S5_COLS // S5_BLOCKS // 2

    @pl.when(pl.program_id(0) == 0)
    def _():
        st_ref[...] = s0_ref[...]
        lst_ref[...] = lh0_ref[...]

    for c in range(chunks):
        ub = u_ref[c * ROWS:(c + 1) * ROWS, :].astype(BF16)
        for j in range(S5_BLOCKS):
            bu_ref[c, :, 2 * half * j:2 * half * (j + 1)] = jnp.dot(
                ub[:, LANES * j:LANES * (j + 1)], bb_ref[j], preferred_element_type=F32)

    for sb in range(bt // SUBLANES):
        base = sb * SUBLANES
        hr = [st_ref[base:base + SUBLANES, 2 * half * j:2 * half * j + half] for j in range(S5_BLOCKS)]
        hi = [st_ref[base:base + SUBLANES, 2 * half * j + half:2 * half * (j + 1)] for j in range(S5_BLOCKS)]
        hl = lst_ref[base:base + SUBLANES, :]
        for c in range(chunks):
            for t in range(tc):
                r0 = t * bt + base
                row = c * ROWS + r0
                hl = la_ref[row:row + SUBLANES, :] * hl + lb_ref[row:row + SUBLANES, :]
                lh_ref[row:row + SUBLANES, :] = hl
                for j in range(S5_BLOCKS):
                    ar = ar_ref[:, half * j:half * (j + 1)]
                    ai = ai_ref[:, half * j:half * (j + 1)]
                    b_re = bu_ref[c, r0:r0 + SUBLANES, 2 * half * j:2 * half * j + half]
                    b_im = bu_ref[c, r0:r0 + SUBLANES, 2 * half * j + half:2 * half * (j + 1)]
                    h_re = ar * hr[j] - ai * hi[j] + b_re
                    h_im = ar * hi[j] + ai * hr[j] + b_im
                    bu_ref[c, r0:r0 + SUBLANES, 2 * half * j:2 * half * j + half] = h_re
                    bu_ref[c, r0:r0 + SUBLANES, 2 * half * j + half:2 * half * (j + 1)] = h_im
                    hr[j], hi[j] = h_re, h_im
        for j in range(S5_BLOCKS):
            st_ref[base:base + SUBLANES, 2 * half * j:2 * half * j + half] = hr[j]
            st_ref[base:base + SUBLANES, 2 * half * j + half:2 * half * (j + 1)] = hi[j]
        lst_ref[base:base + SUBLANES, :] = hl
    sout_ref[...] = st_ref[...]
    lhout_ref[...] = lst_ref[...]
    yl_ref[...] = _rms(lh_ref[...] * lg_ref[...], lgo_ref[...]).astype(BF16)

    for c in range(chunks):
        y = jnp.concatenate(
            [jnp.dot(bu_ref[c, :, 2 * half * j:2 * half * (j + 1)].astype(BF16), cc_ref[j],
                     preferred_element_type=F32) for j in range(S5_BLOCKS)], axis=-1)
        y = y + d_ref[...] * u_ref[c * ROWS:(c + 1) * ROWS, :]
        g = _gelu(y)
        z = jnp.dot(g.astype(BF16), wg_ref[...], preferred_element_type=F32) + bg_ref[...]
        out = g * jax.nn.sigmoid(z)
        y_ref[c * ROWS:(c + 1) * ROWS, :] = _rms(out, go_ref[...]).astype(BF16)


def _mixers(u, s0, ar8, ai8, bb, cc, d, w_glu_bf, b_glu, g_out_s5, lru_a, lru_b, lru_gg, lru_h0, g_out_lru, bt,
            chunks):
    tc = ROWS // bt
    t_rows = u.shape[0]
    step_rows = chunks * ROWS
    assert t_rows % step_rows == 0
    rows_f32 = pl.BlockSpec((step_rows, D_S5), lambda i: (i, 0))
    vec = _full((1, D_S5))
    return pl.pallas_call(
        functools.partial(_s5_body, tc, bt, chunks),
        grid=(t_rows // step_rows,),
        in_specs=[rows_f32, _full((bt, S5_COLS)), _full(ar8.shape), _full(ai8.shape), _full(bb.shape),
                  _full(cc.shape), vec, _full((D_S5, D_S5)), vec, vec,
                  rows_f32, rows_f32, rows_f32, _full((bt, D_LRU)), vec],
        out_specs=[rows_f32, _full((bt, S5_COLS)), rows_f32, _full((bt, D_LRU))],
        out_shape=[jax.ShapeDtypeStruct((t_rows, D_S5), BF16), jax.ShapeDtypeStruct((bt, S5_COLS), F32),
                   jax.ShapeDtypeStruct((t_rows, D_LRU), BF16), jax.ShapeDtypeStruct((bt, D_LRU), F32)],
        scratch_shapes=[pltpu.VMEM((chunks, ROWS, S5_COLS), F32), pltpu.VMEM((bt, S5_COLS), F32),
                        pltpu.VMEM((step_rows, D_LRU), F32), pltpu.VMEM((bt, D_LRU), F32)],
        compiler_params=_params("arbitrary"),
        name="s5_lru_mixers",
    )(u, s0, ar8, ai8, bb, cc, d, w_glu_bf, b_glu, g_out_s5, lru_a, lru_b, lru_gg, lru_h0, g_out_lru)


def _route_tile(xn, x_hi, rw_hi, rw_lo, rb):
    x_lo = (xn - x_hi.astype(F32)).astype(BF16)
    nt = (((1,), (1,)), ((), ()))
    logits = (lax.dot_general(rw_hi, x_hi, nt, preferred_element_type=F32)
              + lax.dot_general(rw_hi, x_lo, nt, preferred_element_type=F32)
              + lax.dot_general(rw_lo, x_hi, nt, preferred_element_type=F32)) + rb

    e_iota = lax.broadcasted_iota(jnp.int32, (N_EXPERTS, ROWS), 0).astype(F32)
    work = logits
    sels, vals = [], []
    for _ in range(TOP_K):
        m = jnp.max(work, axis=0, keepdims=True)
        idx = jnp.min(jnp.where(work == m, e_iota, float(N_EXPERTS)), axis=0, keepdims=True)
        sel = e_iota == idx
        work = jnp.where(sel, -jnp.inf, work)
        sels.append(sel)
        vals.append(m)
    exps = [jnp.exp(v - vals[0]) for v in vals]
    denom = exps[0] + exps[1] + exps[2] + exps[3]
    gates = [e / denom for e in exps]

    onehot = sels[0] | sels[1] | sels[2] | sels[3]
    rr = lax.broadcasted_iota(jnp.int32, (ROWS, ROWS), 0)
    cc = lax.broadcasted_iota(jnp.int32, (ROWS, ROWS), 1)
    before = (rr < cc).astype(BF16)
    prefix = jnp.dot(onehot.astype(BF16), before, preferred_element_type=F32)
    cnt = jnp.sum(onehot.astype(F32), axis=1, keepdims=True)
    cnt_pad = jnp.floor((cnt + (SEG_ALIGN - 1)) * (1.0 / SEG_ALIGN)) * SEG_ALIGN
    er = lax.broadcasted_iota(jnp.int32, (N_EXPERTS, N_EXPERTS), 0)
    ec = lax.broadcasted_iota(jnp.int32, (N_EXPERTS, N_EXPERTS), 1)
    seg_start = jnp.dot((ec < er).astype(BF16), jnp.broadcast_to(cnt_pad, (N_EXPERTS, LANES)).astype(BF16),
                        preferred_element_type=F32)[:, 0:1]
    where_to = prefix + seg_start
    poss = [jnp.sum(jnp.where(s, where_to, 0.0), axis=0, keepdims=True) for s in sels]

    s_iota = lax.broadcasted_iota(jnp.int32, (2 * TOP_K, ROWS), 0)
    pg = jnp.zeros((2 * TOP_K, ROWS), F32)
    for k in range(TOP_K):
        pg = jnp.where(s_iota == k, poss[k], pg)
        pg = jnp.where(s_iota == TOP_K + k, gates[k], pg)
    return pg, cnt


def _out_body(subtiles, tm_rows, *refs):
    if tm_rows:
        ys_ref, yl_ref, x_ref, g1_ref, sc_ref, sh_ref, perm_ref, *refs = refs
    else:
        ys_ref, yl_ref, x_ref, g1_ref, sc_ref, sh_ref, *refs = refs
    wo_ref, gf_ref, rwh_ref, rwl_ref, rb_ref, h_ref, xn_ref, pg_ref, cnt_ref = refs
    for c in range(subtiles):
        rows = slice(c * ROWS, (c + 1) * ROWS)
        ys = ys_ref[rows, :]
        yl = yl_ref[rows, :]
        if tm_rows:
            ys = jnp.dot(perm_ref[...], ys, preferred_element_type=F32).astype(BF16)
            yl = jnp.dot(perm_ref[...], yl, preferred_element_type=F32).astype(BF16)
            tc = x_ref.shape[1] // subtiles
            x = x_ref[:, c * tc:(c + 1) * tc, :].reshape(ROWS, D_MODEL)
        else:
            x = x_ref[rows, :]
        mixed = (jnp.dot(ys, wo_ref[0], preferred_element_type=F32)
                 + jnp.dot(yl, wo_ref[1], preferred_element_type=F32))
        h = x + g1_ref[0] * mixed
        if tm_rows:
            h_ref[:, c * tc:(c + 1) * tc, :] = h.reshape(x_ref.shape[0], tc, D_MODEL)
        else:
            h_ref[rows, :] = h
        xn = _rms(h, gf_ref[...]) * (1.0 + sc_ref[0]) + sh_ref[0]
        x_hi = xn.astype(BF16)
        xn_ref[rows, :] = x_hi
        pg, cnt = _route_tile(xn, x_hi, rwh_ref[...], rwl_ref[...], rb_ref[...])
        pg_ref[:, rows] = pg
        cnt_ref[c] = jnp.broadcast_to(cnt, (N_EXPERTS, LANES))


def _out_proj(ys5, ylru, x, modpat, pattern, from_tm, w_out_bf, g_ffn, rw_hi, rw_lo, router_b, subtiles):
    tm_rows = from_tm is not None
    t_rows = ys5.shape[0]
    step_rows = subtiles * ROWS
    half_spec = pl.BlockSpec((step_rows, D_S5), lambda i: (i, 0))
    if tm_rows:
        x_spec = pl.BlockSpec((x.shape[0], step_rows // x.shape[0], D_MODEL), lambda i: (0, i, 0))
    else:
        x_spec = pl.BlockSpec((step_rows, D_MODEL), lambda i: (i, 0))
    mod_spec = lambda k: pl.BlockSpec((1, ROWS, D_MODEL), lambda i: (pattern, 0, k))
    perm = [_full((ROWS, ROWS))] if tm_rows else []
    return pl.pallas_call(
        functools.partial(_out_body, subtiles, tm_rows),
        grid=(t_rows // step_rows,),
        in_specs=[half_spec, half_spec, x_spec, mod_spec(2), mod_spec(4), mod_spec(3), *perm,
                  _full((2, D_S5, D_MODEL)), _full((1, D_MODEL)),
                  _full((N_EXPERTS, D_MODEL)), _full((N_EXPERTS, D_MODEL)), _full((N_EXPERTS, 1))],
        out_specs=[x_spec, pl.BlockSpec((step_rows, D_MODEL), lambda i: (i, 0)),
                   pl.BlockSpec((2 * TOP_K, step_rows), lambda i: (0, i)),
                   pl.BlockSpec((subtiles, N_EXPERTS, LANES), lambda i: (i, 0, 0))],
        out_shape=[jax.ShapeDtypeStruct(x.shape, F32), jax.ShapeDtypeStruct((t_rows, D_MODEL), BF16),
                   jax.ShapeDtypeStruct((2 * TOP_K, t_rows), F32),
                   jax.ShapeDtypeStruct((t_rows // ROWS, N_EXPERTS, LANES), F32)],
        compiler_params=_params("arbitrary"),
        name="out_proj_router",
    )(ys5, ylru, x, modpat, modpat, modpat, *([from_tm] if tm_rows else []), w_out_bf, g_ffn, rw_hi, rw_lo,
      router_b)


def _start_segments(n_ref, hbm_ref, vmem_ref, hbm, vmem_buf, sem, step, to_hbm, unroll):
    def seg(e):
        n = n_ref[step * N_EXPERTS + e]
        h0 = hbm_ref[step * N_EXPERTS + e]
        v0 = vmem_ref[step * N_EXPERTS + e]

        def piece(off, size):
            h = hbm.at[pl.ds(pl.multiple_of(h0 + off, SEG_ALIGN), size)]
            v = vmem_buf.at[pl.ds(pl.multiple_of(v0 + off, SEG_ALIGN), size)]
            (pltpu.make_async_copy(v, h, sem) if to_hbm else pltpu.make_async_copy(h, v, sem)).start()

        if unroll:
            _row_pieces(n, ROWS, piece)
        else:
            def big(k, carry):
                piece(k * BIG_PIECE, BIG_PIECE)
                return carry

            lax.fori_loop(0, n // BIG_PIECE, big, 0)
            _row_pieces(n % BIG_PIECE, BIG_PIECE // 2, lambda off, size: piece(n // BIG_PIECE * BIG_PIECE + off, size))

    if unroll:
        for e in range(N_EXPERTS):
            seg(e)
    else:
        def body(e, carry):
            seg(e)
            return carry

        lax.fori_loop(0, N_EXPERTS, body, 0)


def _wait_rows(total, largest, hbm, sem):
    _row_pieces(total, largest, lambda off, size: pltpu.make_async_copy(
        hbm.at[pl.ds(0, size)], hbm.at[pl.ds(0, size)], sem).wait())


def _dispatch_body(tiles_p, n_ref, glob_ref, local_ref, tot_ref, nb_ref, pg_ref, xp_ref, xsm_ref, xs_hbm, stage,
                   sem):
    j = pl.program_id(0)
    last = pl.num_programs(0) - 1
    slot = j % 2

    def unused_blocks(act):
        def blk(b, carry):
            act(pltpu.make_async_copy(stage.at[slot, pl.ds(0, MOE_TM)],
                                      xs_hbm.at[pl.ds(pl.multiple_of(b * MOE_TM, MOE_TM), MOE_TM)], sem.at[slot]))
            return carry
        lax.fori_loop(nb_ref[0], xs_hbm.shape[0] // MOE_TM, blk, 0)

    def wait_step(step, s):
        _wait_rows(tot_ref[step], WAIT_MAX_PIECE, xs_hbm, sem.at[s])

    @pl.when(j >= 2)
    def _():
        wait_step(jnp.maximum(j - 2, 0), slot)

    @pl.when(j < last)
    def _():
        x = jnp.where(j < tiles_p, xp_ref[...], xsm_ref[...])
        pos = pg_ref[0:TOP_K, :]
        r = lax.broadcasted_iota(jnp.int32, (SORT_CHUNK, ROWS), 0).astype(F32).astype(BF16)
        for c in range(SORT_ROWS // SORT_CHUNK):
            rel = (pos - float(c * SORT_CHUNK)).astype(BF16)
            pick = (r == rel[0:1]) | (r == rel[1:2]) | (r == rel[2:3]) | (r == rel[3:4])
            stage[slot, c * SORT_CHUNK:(c + 1) * SORT_CHUNK, :] = jnp.dot(
                jnp.where(pick, jnp.ones((), BF16), jnp.zeros((), BF16)), x,
                preferred_element_type=F32).astype(BF16)

    @pl.when(j == last)
    def _():
        stage[slot, 0:MOE_TM, :] = jnp.zeros((MOE_TM, D_MODEL), BF16)

    _start_segments(n_ref, glob_ref, local_ref, xs_hbm, stage.at[slot], sem.at[slot], j, True, unroll=True)

    @pl.when(j == last)
    def _():
        unused_blocks(lambda c: c.start())

        @pl.when(j >= 1)
        def _():
            wait_step(jnp.maximum(j - 1, 0), 1 - slot)
        wait_step(j, slot)
        unused_blocks(lambda c: c.wait())


def _dispatch(seg_n, seg_glob, seg_local, seg_tot, nb_used, pg, xn_p, xn_s, n_rows):
    tiles_p = xn_p.shape[0] // ROWS
    tiles_s = xn_s.shape[0] // ROWS
    tiles = tiles_p + tiles_s
    grid_spec = pltpu.PrefetchScalarGridSpec(
        num_scalar_prefetch=5,
        grid=(tiles + 1,),
        in_specs=[pl.BlockSpec((2 * TOP_K, ROWS), lambda j, *_: (0, jnp.minimum(j, tiles - 1))),
                  pl.BlockSpec((ROWS, D_MODEL), lambda j, *_: (jnp.minimum(j, tiles_p - 1), 0)),
                  pl.BlockSpec((ROWS, D_MODEL), lambda j, *_: (jnp.clip(j - tiles_p, 0, tiles_s - 1), 0))],
        out_specs=pl.BlockSpec(memory_space=pl.ANY),
        scratch_shapes=[pltpu.VMEM((2, SORT_ROWS, D_MODEL), BF16), pltpu.SemaphoreType.DMA((2,))],
    )
    return pl.pallas_call(
        functools.partial(_dispatch_body, tiles_p),
        grid_spec=grid_spec,
        out_shape=jax.ShapeDtypeStruct((n_rows, D_MODEL), BF16),
        compiler_params=_params("arbitrary"),
        name="moe_dispatch",
    )(seg_n, seg_glob, seg_local, seg_tot, nb_used, pg, xn_p, xn_s)


def _moe_body(be_ref, nxt_ref, nv_ref, nb_ref, xs_ref, wgu_hbm, bgu_ref, wd_hbm, bd_ref, ys_ref,
              wgu_f32, wd_f32, wgu_bf, wd_bf, sem):
    i = pl.program_id(0)

    def weight_copies(e):
        return (pltpu.make_async_copy(wgu_hbm.at[e], wgu_f32, sem.at[0]),
                pltpu.make_async_copy(wd_hbm.at[e], wd_f32, sem.at[1]))

    @pl.when(i >= nb_ref[0])
    def _():
        ys_ref[...] = jnp.zeros_like(ys_ref)

    @pl.when(i < nb_ref[0])
    def _():
        e = be_ref[i]

        @pl.when(i == 0)
        def _():
            for c in weight_copies(e):
                c.start()

        @pl.when(jnp.logical_or(i == 0, e != be_ref[jnp.maximum(i - 1, 0)]))
        def _():
            for c in weight_copies(e):
                c.wait()
            wgu_bf[...] = wgu_f32[...].astype(BF16)
            wd_bf[...] = wd_f32[...].astype(BF16)

            @pl.when(nxt_ref[i] >= 0)
            def _():
                for c in weight_copies(nxt_ref[i]):
                    c.start()

        def ffn(x):
            hg = jnp.dot(x, wgu_bf[...], preferred_element_type=F32) + bgu_ref[0]
            gate = jnp.minimum(hg[:, :D_FF], SWIGLU_LIMIT)
            up = jnp.clip(hg[:, D_FF:], -SWIGLU_LIMIT, SWIGLU_LIMIT)
            act = (up + 1.0) * (gate * jax.nn.sigmoid(SWIGLU_ALPHA * gate))
            return (jnp.dot(act.astype(BF16), wd_bf[...], preferred_element_type=F32) + bd_ref[0]).astype(BF16)

        half = MOE_TM // 2

        @pl.when(nv_ref[i] > half)
        def _():
            ys_ref[...] = ffn(xs_ref[...])

        @pl.when(nv_ref[i] <= half)
        def _():
            ys_ref[0:half, :] = ffn(xs_ref[0:half, :])
            ys_ref[half:, :] = jnp.zeros((half, D_MODEL), BF16)


def _moe(xs, block_expert, block_next, block_rows, nb_used, w_gu, b_gu, w_down, b_down):
    n_blocks = xs.shape[0] // MOE_TM
    in_rows = pl.BlockSpec((MOE_TM, D_MODEL), lambda i, be, nx, nv, nb: (jnp.minimum(i, nb[0] - 1), 0))
    grid_spec = pltpu.PrefetchScalarGridSpec(
        num_scalar_prefetch=4,
        grid=(n_blocks,),
        in_specs=[in_rows,
                  pl.BlockSpec(memory_space=pl.ANY),
                  pl.BlockSpec((1, 1, 2 * D_FF), lambda i, be, nx, nv, nb: (be[i], 0, 0)),
                  pl.BlockSpec(memory_space=pl.ANY),
                  pl.BlockSpec((1, 1, D_MODEL), lambda i, be, nx, nv, nb: (be[i], 0, 0))],
        out_specs=pl.BlockSpec((MOE_TM, D_MODEL), lambda i, be, nx, nv, nb: (i, 0)),
        scratch_shapes=[pltpu.VMEM((D_MODEL, 2 * D_FF), F32), pltpu.VMEM((D_FF, D_MODEL), F32),
                        pltpu.VMEM((D_MODEL, 2 * D_FF), BF16), pltpu.VMEM((D_FF, D_MODEL), BF16),
                        pltpu.SemaphoreType.DMA((2,))],
    )
    return pl.pallas_call(
        _moe_body,
        grid_spec=grid_spec,
        out_shape=jax.ShapeDtypeStruct(xs.shape, BF16),
        compiler_params=_params("arbitrary"),
        name="moe_experts",
    )(block_expert, block_next, block_rows, nb_used, xs, w_gu, b_gu, w_down, b_down)


def _fin_body(tiles_p, n_ref, glob_ref, local_ref, tot_ref, hp_ref, hs_ref, pg_ref, g2_ref, gf_ref, ys_hbm,
              op_ref, os_ref, ybuf, sem):
    j = pl.program_id(0)
    tiles = pl.num_programs(0)
    slot = j % 2
    is_prompt = j < tiles_p

    def fetch(step, s):
        _start_segments(n_ref, glob_ref, local_ref, ys_hbm, ybuf.at[s], sem.at[s], step, False, unroll=False)

    @pl.when(j == 0)
    def _():
        ybuf[...] = jnp.zeros_like(ybuf)
        fetch(0, 0)

    _wait_rows(tot_ref[j], WAIT_MAX_PIECE, ys_hbm, sem.at[slot])

    @pl.when(j + 1 < tiles)
    def _():
        fetch(jnp.minimum(j + 1, tiles - 1), 1 - slot)

    pg = pg_ref[...]
    ff = jnp.zeros((ROWS, D_MODEL), F32)
    r = lax.broadcasted_iota(jnp.int32, (ROWS, SORT_CHUNK), 1).astype(F32).astype(BF16)
    gates = pg[:, TOP_K:].astype(BF16)
    for c in range(SORT_ROWS // SORT_CHUNK):
        rel = (pg[:, :TOP_K] - float(c * SORT_CHUNK)).astype(BF16)
        w = jnp.zeros((ROWS, SORT_CHUNK), BF16)
        for k in range(TOP_K):
            w = jnp.where(r == rel[:, k:k + 1], gates[:, k:k + 1], w)
        ff = ff + jnp.dot(w, ybuf[slot, c * SORT_CHUNK:(c + 1) * SORT_CHUNK, :], preferred_element_type=F32)
    h = jnp.where(is_prompt, hp_ref[...].reshape(ROWS, D_MODEL), hs_ref[...])
    y = _rms(h + g2_ref[0] * ff, gf_ref[...])

    @pl.when(is_prompt)
    def _():
        op_ref[...] = y.reshape(op_ref.shape)

    @pl.when(jnp.logical_not(is_prompt))
    def _():
        os_ref[...] = y


def _final(seg_n, seg_glob, seg_local, seg_tot, h_p, h_s, pg_t, modpat, g_final, ys, tiles_p, tiles_s):
    hp_spec, hs_spec = _tile_specs(h_p.shape[0], tiles_p, tiles_s)
    grid_spec = pltpu.PrefetchScalarGridSpec(
        num_scalar_prefetch=4,
        grid=(tiles_p + tiles_s,),
        in_specs=[hp_spec, hs_spec,
                  pl.BlockSpec((ROWS, 2 * TOP_K), lambda j, *_: (j, 0)),
                  _mod_spec(5, tiles_p), _full((1, D_MODEL)),
                  pl.BlockSpec(memory_space=pl.ANY)],
        out_specs=[hp_spec, hs_spec],
        scratch_shapes=[pltpu.VMEM((2, SORT_ROWS, D_MODEL), BF16), pltpu.SemaphoreType.DMA((2,))],
    )
    return pl.pallas_call(
        functools.partial(_fin_body, tiles_p),
        grid_spec=grid_spec,
        out_shape=[jax.ShapeDtypeStruct(h_p.shape, F32), jax.ShapeDtypeStruct(h_s.shape, F32)],
        compiler_params=_params("arbitrary"),
        name="combine_final",
    )(seg_n, seg_glob, seg_local, seg_tot, h_p, h_s, pg_t, modpat, g_final, ys)


def _block_diag(w):
    h, i, j = w.shape
    return jnp.einsum('hij,hk->hikj', w, jnp.eye(h, dtype=w.dtype)).reshape(h * i, h * j)


def _s5_cols(re, im):
    b = re.shape[0]
    stack = jnp.stack([re.reshape(b, S5_BLOCKS, -1), im.reshape(b, S5_BLOCKS, -1)], axis=2)
    return stack.reshape(b, S5_COLS)


def _s5_uncols(cols):
    b = cols.shape[0]
    c = cols.reshape(b, S5_BLOCKS, 2, S5_GROUPS // S5_BLOCKS, S5_STATE)
    return (c[:, :, 0].reshape(b, S5_GROUPS, S5_STATE), c[:, :, 1].reshape(b, S5_GROUPS, S5_STATE))


def _moe_rows_bound(tiles):
    worst = tiles * (TOP_K * ROWS + N_EXPERTS * (SEG_ALIGN - 1)) + N_EXPERTS * (MOE_TM - SEG_ALIGN)
    return (worst + MOE_TM - 1) // MOE_TM * MOE_TM


def _plan(cnt):
    cnt = cnt.astype(jnp.int32)
    tiles = cnt.shape[0]
    cp = (cnt + SEG_ALIGN - 1) // SEG_ALIGN * SEG_ALIGN
    local = jnp.cumsum(cp, axis=1) - cp
    group = jnp.sum(cp, axis=0)
    group_pad = (group + MOE_TM - 1) // MOE_TM * MOE_TM
    pend = jnp.cumsum(group_pad)
    pstart = pend - group_pad
    glob = pstart[None, :] + jnp.cumsum(cp, axis=0) - cp
    gap = group_pad - group
    seg_n = jnp.concatenate([cp, gap[None]], axis=0).reshape(-1)
    seg_local = jnp.concatenate([local, jnp.zeros((1, N_EXPERTS), jnp.int32)], axis=0).reshape(-1)
    seg_glob = jnp.concatenate([glob, (pstart + group)[None]], axis=0).reshape(-1)
    seg_tot = jnp.concatenate([jnp.sum(cp, axis=1), jnp.sum(gap)[None]]).astype(jnp.int32)
    n_blocks = _moe_rows_bound(tiles) // MOE_TM
    block_row0 = jnp.arange(n_blocks, dtype=jnp.int32) * MOE_TM
    block_expert = jnp.minimum(jnp.sum(block_row0[:, None] >= pend[None, :], axis=1), N_EXPERTS - 1).astype(jnp.int32)
    nb_used = (pend[-1] // MOE_TM).astype(jnp.int32).reshape(1)
    experts = jnp.arange(N_EXPERTS, dtype=jnp.int32)
    later_owner = jnp.where((experts[None, :] > experts[:, None]) & (group_pad[None, :] > 0), experts[None, :],
                            N_EXPERTS)
    next_owner = jnp.min(later_owner, axis=1)
    next_owner = jnp.where(next_owner == N_EXPERTS, -1, next_owner).astype(jnp.int32)
    owner = block_expert[:, None] == experts[None, :]
    block_next = jnp.sum(jnp.where(owner, next_owner[None, :], 0), axis=1).astype(jnp.int32)
    group_end = jnp.sum(jnp.where(owner, (pstart + group)[None, :], 0), axis=1)
    block_rows = jnp.clip(group_end - block_row0, 0, MOE_TM).astype(jnp.int32)
    return seg_n, seg_glob, seg_local, seg_tot, block_expert, block_next, block_rows, nb_used


def kernel(x_prompt, x_sample, state_s5_re, state_s5_im, state_lru_h, state_conv, c_prompt, c_sample, w_ada, b_ada, g_mix, w_in, s5_a_re, s5_a_im, s5_log_dt, s5_b_re, s5_b_im, s5_c_re, s5_c_im, s5_d, s5_w_glu, s5_b_glu, lru_conv_w, lru_conv_b, lru_w_a, lru_b_a, lru_w_x, lru_b_x, lru_lambda, g_out_s5, g_out_lru, w_out, g_ffn, router_w, router_b, moe_w_gu, moe_b_gu, moe_w_down, moe_b_down, g_final):
    assert w_ada.shape[0] == 1, "one layer"
    bp, lp, _ = x_prompt.shape
    bs, ls, _ = x_sample.shape
    assert ROWS % bp == 0 and ROWS % bs == 0 and (bp * lp) % ROWS == 0 and (bs * ls) % ROWS == 0
    tiles_p = bp * lp // ROWS
    tiles_s = bs * ls // ROWS
    row = lambda v: v.reshape(1, -1)

    ab_re, ab_im, bb_re, bb_im = _s5_prep(s5_a_re[0], s5_a_im[0], s5_log_dt[0].reshape(S5_GROUPS, 1),
                                          jnp.swapaxes(s5_b_re[0], 1, 2), jnp.swapaxes(s5_b_im[0], 1, 2))
    gpb = S5_GROUPS // S5_BLOCKS
    eye = jnp.eye(gpb, dtype=F32)

    def in_blocks(b):
        b = b.reshape(S5_BLOCKS, gpb, S5_GROUP_CH, S5_STATE)
        return jnp.einsum('bgjn,gh->bgjhn', b, eye).reshape(S5_BLOCKS, gpb * S5_GROUP_CH, gpb * S5_STATE)

    def out_blocks(c):
        c = c.reshape(S5_BLOCKS, gpb, S5_GROUP_CH, S5_STATE)
        return jnp.einsum('bgjn,gh->bgnhj', c, eye).reshape(S5_BLOCKS, gpb * S5_STATE, gpb * S5_GROUP_CH)

    ar8 = jnp.broadcast_to(ab_re.reshape(1, -1), (SUBLANES, S5_GROUPS * S5_STATE))
    ai8 = jnp.broadcast_to(ab_im.reshape(1, -1), (SUBLANES, S5_GROUPS * S5_STATE))
    bb = jnp.concatenate([in_blocks(bb_re), in_blocks(bb_im)], axis=-1).astype(BF16)
    cc = jnp.concatenate([out_blocks(s5_c_re[0]), -out_blocks(s5_c_im[0])], axis=1).astype(BF16)
    wa_bd = _block_diag(lru_w_a[0]).astype(BF16)
    wx_bd = _block_diag(lru_w_x[0]).astype(BF16)
    rw_t = router_w[0].T
    rw_hi = rw_t.astype(BF16)
    rw_lo = (rw_t - rw_hi.astype(F32)).astype(BF16)

    tc = ROWS // bp
    modpat = _adaln(jnp.concatenate([c_prompt, c_sample], axis=0), w_ada[0], row(b_ada[0]), bp)
    r = jnp.arange(ROWS)
    tm_of = (r % tc) * bp + r // tc
    to_tm = (r[:, None] == tm_of[None, :]).astype(BF16)
    from_tm = to_tm.T
    pair = 2 if tiles_p % 2 == 0 else 1

    def conv_tm(cv):
        return jnp.swapaxes(cv, 0, 1).reshape(-1, D_LRU)

    def conv_bm(cv, b):
        return jnp.swapaxes(cv.reshape(CONV_WIDTH - 1, b, D_LRU), 0, 1)

    x_s = jnp.swapaxes(x_sample, 0, 1).reshape(bs * ls, D_MODEL)
    in_args = (row(g_mix[0]), w_in[0].astype(BF16))
    lru_args = (lru_conv_w[0], row(lru_conv_b[0]), wa_bd, row(lru_b_a[0]), wx_bd, row(lru_b_x[0]),
                row(lru_lambda[0]))
    u_p, la_p, lb_p, lg_p, cvp = _in_proj(x_prompt, modpat, 0, to_tm, *in_args,
                                          jnp.zeros(((CONV_WIDTH - 1) * bp, D_LRU), F32), *lru_args, bp, pair)
    u_s, la_s, lb_s, lg_s, cvs = _in_proj(x_s, modpat, 1, None, *in_args, conv_tm(state_conv[0]), *lru_args, bs, 1)

    s5_args = (ar8, ai8, bb, cc, row(s5_d[0]), s5_w_glu[0].astype(BF16), row(s5_b_glu[0]), row(g_out_s5[0]))
    ys5_p, s5p, ylru_p, hp = _mixers(u_p, jnp.zeros((bp, S5_COLS), F32), *s5_args, la_p, lb_p, lg_p,
                                     jnp.zeros((bp, D_LRU), F32), row(g_out_lru[0]), bp, pair)
    ys5_s, s5s, ylru_s, hs = _mixers(u_s, _s5_cols(state_s5_re[0], state_s5_im[0]), *s5_args, la_s, lb_s, lg_s,
                                     state_lru_h[0], row(g_out_lru[0]), bs, 1)

    out_args = (w_out[0].astype(BF16).reshape(2, D_S5, D_MODEL), row(g_ffn[0]), rw_hi, rw_lo,
                router_b[0].reshape(N_EXPERTS, 1))
    h_p, xn_p, pg_p, cnt_p = _out_proj(ys5_p, ylru_p, x_prompt, modpat, 0, from_tm, *out_args, pair)
    h_s, xn_s, pg_s, cnt_s = _out_proj(ys5_s, ylru_s, x_s, modpat, 1, None, *out_args, 1)
    pg = jnp.concatenate([pg_p, pg_s], axis=1)
    cnt = jnp.concatenate([cnt_p, cnt_s], axis=0)

    seg_n, seg_glob, seg_local, seg_tot, block_expert, block_next, block_rows, nb_used = _plan(cnt[:, :, 0])
    xs = _dispatch(seg_n, seg_glob, seg_local, seg_tot, nb_used, pg, xn_p, xn_s,
                   _moe_rows_bound(tiles_p + tiles_s))
    ys = _moe(xs, block_expert, block_next, block_rows, nb_used, moe_w_gu[0], moe_b_gu[0].reshape(N_EXPERTS, 1, 2 * D_FF),
              moe_w_down[0], moe_b_down[0].reshape(N_EXPERTS, 1, D_MODEL))
    y_prompt, y_s = _final(seg_n, seg_glob, seg_local, seg_tot, h_p, h_s, pg.T, modpat, row(g_final), ys,
                           tiles_p, tiles_s)
    y_sample = jnp.swapaxes(y_s.reshape(ls, bs, D_MODEL), 0, 1)
    s5p_re, s5p_im = _s5_uncols(s5p)
    s5s_re, s5s_im = _s5_uncols(s5s)
    return (y_prompt, y_sample,
            s5p_re[None], s5p_im[None], hp[None], conv_bm(cvp, bp)[None],
            s5s_re[None], s5s_im[None], hs[None], conv_bm(cvs, bs)[None])
```

```python
import functools

import jax
import jax.numpy as jnp
from jax import lax
from jax.experimental import pallas as pl
from jax.experimental.pallas import tpu as pltpu

D_MODEL = 1024
D_S5 = 512
D_LRU = 512
S5_GROUPS = 32
S5_GROUP_CH = 16
S5_STATE = 64
S5_COLS = 2 * S5_GROUPS * S5_STATE
S5_BLOCKS = 4
LRU_HEADS = 8
LRU_HEAD_DIM = 64
LRU_C = 8.0
CONV_WIDTH = 4
N_EXPERTS = 32
TOP_K = 4
D_FF = 1024
SWIGLU_LIMIT = 7.0
SWIGLU_ALPHA = 1.702
N_MOD = 6
EPS = 1e-6

ROWS = 512
MOE_TM = 1024
MOE_PART = 256
BIG_PIECE = 128
WAIT_MAX_PIECE = 16384
SEG_ALIGN = 16
SORT_ROWS = 2560
SORT_CHUNK = 256
SUBLANES = 8
LANES = 128
VMEM_LIMIT = 48 * 1024 * 1024

BF16 = jnp.bfloat16
F32 = jnp.float32


def _params(*sem):
    return pltpu.CompilerParams(dimension_semantics=sem, vmem_limit_bytes=VMEM_LIMIT)


def _full(shape):
    return pl.BlockSpec(shape, lambda *_: (0,) * len(shape))


def _rms(x, g):
    return x * lax.rsqrt(jnp.mean(x * x, axis=-1, keepdims=True) + EPS) * g


def _gelu(x):
    return 0.5 * x * (1.0 + lax.erf(x * (2.0 ** -0.5)))


def _expm1(x):
    u = jnp.exp(x)
    d = u - 1.0
    return jnp.where(d == 0.0, x, jnp.where(d == -1.0, -1.0, d * x / jnp.log(u)))


def _row_pieces(n, largest, fn):
    off = 0
    bit = largest
    while bit >= SEG_ALIGN:
        @pl.when((n & bit) != 0)
        def _(off=off, bit=bit):
            fn(off, bit)
        off = off + (n & bit)
        bit //= 2


def _mod_body(bp, c_ref, w_ref, b_ref, o_ref):
    c = c_ref[...]
    s = (c * jax.nn.sigmoid(c)).astype(BF16)
    mod = jnp.dot(s, w_ref[...].astype(BF16), preferred_element_type=F32) + b_ref[...]
    bs = mod.shape[0] - bp
    o_ref[0] = jnp.broadcast_to(mod[:bp][:, None, :], (bp, ROWS // bp, D_MODEL)).reshape(ROWS, D_MODEL)
    o_ref[1] = jnp.broadcast_to(mod[bp:][None], (ROWS // bs, bs, D_MODEL)).reshape(ROWS, D_MODEL)


def _adaln(c, w_ada, b_ada, bp):
    m = c.shape[0]
    return pl.pallas_call(
        functools.partial(_mod_body, bp),
        grid=(N_MOD,),
        in_specs=[pl.BlockSpec((m, D_MODEL), lambda j: (0, 0)),
                  pl.BlockSpec((D_MODEL, D_MODEL), lambda j: (0, j)),
                  pl.BlockSpec((1, D_MODEL), lambda j: (0, j))],
        out_specs=pl.BlockSpec((2, ROWS, D_MODEL), lambda j: (0, 0, j)),
        out_shape=jax.ShapeDtypeStruct((2, ROWS, N_MOD * D_MODEL), F32),
        compiler_params=_params("arbitrary"),
        name="adaln",
    )(c, w_ada, b_ada)


def _mod_spec(k, tiles_p):
    return pl.BlockSpec((1, ROWS, D_MODEL), lambda i, *_: (jnp.where(i < tiles_p, 0, 1), 0, k))


def _tile_specs(bp, tiles_p, tiles_s):
    tc = ROWS // bp
    p_spec = pl.BlockSpec((bp, tc, D_MODEL), lambda i, *_: (0, jnp.minimum(i, tiles_p - 1), 0))
    s_spec = pl.BlockSpec((ROWS, D_MODEL), lambda i, *_: (jnp.clip(i - tiles_p, 0, tiles_s - 1), 0))
    return p_spec, s_spec


def _in_body(subtiles, bt, to_tm, *refs):
    if to_tm:
        x_ref, sc_ref, sh_ref, perm_ref, *refs = refs
    else:
        x_ref, sc_ref, sh_ref, *refs = refs
    (g_ref, w_ref, cv0_ref, cw_ref, cb_ref, wa_ref, ba_ref, wx_ref, bx_ref, lam_ref,
     u_ref, a_ref, b_ref, gg_ref, cvout_ref, xp_ref) = refs
    halo = (CONV_WIDTH - 1) * bt

    @pl.when(pl.program_id(0) == 0)
    def _():
        xp_ref[0:halo, :] = cv0_ref[...]

    lam = lam_ref[...]
    softplus_neg_lam = jnp.maximum(-lam, 0.0) + jnp.log1p(jnp.exp(-jnp.abs(lam)))
    for c in range(subtiles):
        rows = slice(c * ROWS, (c + 1) * ROWS)
        if to_tm:
            tc = x_ref.shape[1] // subtiles
            x = x_ref[:, c * tc:(c + 1) * tc, :].reshape(ROWS, D_MODEL)
        else:
            x = x_ref[rows, :]
        xn = (_rms(x, g_ref[...]) * (1.0 + sc_ref[0]) + sh_ref[0]).astype(BF16)
        if to_tm:
            xn = jnp.dot(perm_ref[...], xn, preferred_element_type=F32).astype(BF16)
        p = jnp.dot(xn, w_ref[...], preferred_element_type=F32)
        u_ref[rows, :] = p[:, :D_S5]
        gg_ref[rows, :] = _gelu(p[:, D_S5 + D_LRU:])

        xp_ref[halo + c * ROWS:halo + (c + 1) * ROWS, :] = p[:, D_S5:D_S5 + D_LRU]
        xc = cb_ref[...] + sum(xp_ref[c * ROWS + k * bt:(c + 1) * ROWS + k * bt, :] * cw_ref[k:k + 1, :]
                               for k in range(CONV_WIDTH))
        xcb = xc.astype(BF16)
        r = jax.nn.sigmoid(jnp.dot(xcb, wa_ref[...], preferred_element_type=F32) + ba_ref[...])
        i = jax.nn.sigmoid(jnp.dot(xcb, wx_ref[...], preferred_element_type=F32) + bx_ref[...])
        log_a = -LRU_C * r * softplus_neg_lam
        a_ref[rows, :] = jnp.exp(log_a)
        b_ref[rows, :] = jnp.sqrt(-_expm1(2.0 * log_a)) * (i * xc)

    tail = xp_ref[subtiles * ROWS:subtiles * ROWS + halo, :]
    xp_ref[0:halo, :] = tail
    cvout_ref[...] = tail


def _in_proj(x, modpat, pattern, to_tm, g_mix, w_in_bf, conv0_tm, conv_w, conv_b, wa_bd, b_a, wx_bd, b_x, lam, bt,
             subtiles):
    tm = to_tm is not None
    t_rows = x.shape[0] * x.shape[1] if tm else x.shape[0]
    step_rows = subtiles * ROWS
    halo = (CONV_WIDTH - 1) * bt
    if tm:
        x_spec = pl.BlockSpec((x.shape[0], step_rows // x.shape[0], D_MODEL), lambda i: (0, i, 0))
    else:
        x_spec = pl.BlockSpec((step_rows, D_MODEL), lambda i: (i, 0))
    mod_spec = lambda k: pl.BlockSpec((1, ROWS, D_MODEL), lambda i: (pattern, 0, k))
    row_spec = pl.BlockSpec((step_rows, D_S5), lambda i: (i, 0))
    vec = _full((1, D_LRU))
    return pl.pallas_call(
        functools.partial(_in_body, subtiles, bt, tm),
        grid=(t_rows // step_rows,),
        in_specs=[x_spec, mod_spec(1), mod_spec(0), *([_full((ROWS, ROWS))] if tm else []), _full((1, D_MODEL)),
                  _full((D_MODEL, D_S5 + 2 * D_LRU)), _full((halo, D_LRU)), _full((CONV_WIDTH, D_LRU)), vec,
                  _full((D_LRU, D_LRU)), vec, _full((D_LRU, D_LRU)), vec, vec],
        out_specs=[row_spec, row_spec, row_spec, row_spec, _full((halo, D_LRU))],
        out_shape=[jax.ShapeDtypeStruct((t_rows, D_S5), F32)] * 4 + [jax.ShapeDtypeStruct((halo, D_LRU), F32)],
        scratch_shapes=[pltpu.VMEM((step_rows + halo, D_LRU), F32)],
        compiler_params=_params("arbitrary"),
        name="in_proj",
    )(x, modpat, modpat, *([to_tm] if tm else []), g_mix, w_in_bf, conv0_tm, conv_w, conv_b, wa_bd, b_a, wx_bd, b_x,
      lam)


def _s5_prep_body(are_ref, aim_ref, ldt_ref, bre_ref, bim_ref, abre_ref, abim_ref, bbre_ref, bbim_ref):
    a_re = are_ref[...]
    a_im = aim_ref[...]
    dt = jnp.exp(ldt_ref[...])
    mag = jnp.exp(dt * a_re)
    ang = dt * a_im
    ab_re = mag * jnp.cos(ang)
    ab_im = mag * jnp.sin(ang)
    den = a_re * a_re + a_im * a_im
    q_re = ((ab_re - 1.0) * a_re + ab_im * a_im) / den
    q_im = (ab_im * a_re - (ab_re - 1.0) * a_im) / den
    abre_ref[...] = ab_re
    abim_ref[...] = ab_im
    b_re = bre_ref[...]
    b_im = bim_ref[...]
    bbre_ref[...] = q_re[:, None, :] * b_re - q_im[:, None, :] * b_im
    bbim_ref[...] = q_re[:, None, :] * b_im + q_im[:, None, :] * b_re


def _s5_prep(a_re, a_im, log_dt, b_re_t, b_im_t):
    gn = jax.ShapeDtypeStruct((S5_GROUPS, S5_STATE), F32)
    gjn = jax.ShapeDtypeStruct((S5_GROUPS, S5_GROUP_CH, S5_STATE), F32)
    return pl.pallas_call(_s5_prep_body, out_shape=[gn, gn, gjn, gjn], name="s5_prep")(
        a_re, a_im, log_dt, b_re_t, b_im_t)


def _s5_body(tc, bt, chunks, u_ref, s0_ref, ar_ref, ai_ref, bb_ref, cc_ref, d_ref, wg_ref, bg_ref, go_ref,
             la_ref, lb_ref, lg_ref, lh0_ref, lgo_ref, y_ref, sout_ref, yl_ref, lhout_ref,
             bu_ref, st_ref, lh_ref, lst_ref):
    half = S5_COLS // S5_BLOCKS // 2

    @pl.when(pl.program_id(0) == 0)
    def _():
        st_ref[...] = s0_ref[...]
        lst_ref[...] = lh0_ref[...]

    for c in range(chunks):
        ub = u_ref[c * ROWS:(c + 1) * ROWS, :].astype(BF16)
        for j in range(S5_BLOCKS):
            bu_ref[c, :, 2 * half * j:2 * half * (j + 1)] = jnp.dot(
                ub[:, LANES * j:LANES * (j + 1)], bb_ref[j], preferred_element_type=F32)

    for sb in range(bt // SUBLANES):
        base = sb * SUBLANES
        hr = [st_ref[base:base + SUBLANES, 2 * half * j:2 * half * j + half] for j in range(S5_BLOCKS)]
        hi = [st_ref[base:base + SUBLANES, 2 * half * j + half:2 * half * (j + 1)] for j in range(S5_BLOCKS)]
        hl = lst_ref[base:base + SUBLANES, :]
        for c in range(chunks):
            for t in range(tc):
                r0 = t * bt + base
                row = c * ROWS + r0
                hl = la_ref[row:row + SUBLANES, :] * hl + lb_ref[row:row + SUBLANES, :]
                lh_ref[row:row + SUBLANES, :] = hl
                for j in range(S5_BLOCKS):
                    ar = ar_ref[:, half * j:half * (j + 1)]
                    ai = ai_ref[:, half * j:half * (j + 1)]
                    b_re = bu_ref[c, r0:r0 + SUBLANES, 2 * half * j:2 * half * j + half]
                    b_im = bu_ref[c, r0:r0 + SUBLANES, 2 * half * j + half:2 * half * (j + 1)]
                    h_re = ar * hr[j] - ai * hi[j] + b_re
                    h_im = ar * hi[j] + ai * hr[j] + b_im
                    bu_ref[c, r0:r0 + SUBLANES, 2 * half * j:2 * half * j + half] = h_re
                    bu_ref[c, r0:r0 + SUBLANES, 2 * half * j + half:2 * half * (j + 1)] = h_im
                    hr[j], hi[j] = h_re, h_im
        for j in range(S5_BLOCKS):
            st_ref[base:base + SUBLANES, 2 * half * j:2 * half * j + half] = hr[j]
            st_ref[base:base + SUBLANES, 2 * half * j + half:2 * half * (j + 1)] = hi[j]
        lst_ref[base:base + SUBLANES, :] = hl
    sout_ref[...] = st_ref[...]
    lhout_ref[...] = lst_ref[...]
    yl_ref[...] = _rms(lh_ref[...] * lg_ref[...], lgo_ref[...]).astype(BF16)

    for c in range(chunks):
        y = jnp.concatenate(
            [jnp.dot(bu_ref[c, :, 2 * half * j:2 * half * (j + 1)].astype(BF16), cc_ref[j],
                     preferred_element_type=F32) for j in range(S5_BLOCKS)], axis=-1)
        y = y + d_ref[...] * u_ref[c * ROWS:(c + 1) * ROWS, :]
        g = _gelu(y)
        z = jnp.dot(g.astype(BF16), wg_ref[...], preferred_element_type=F32) + bg_ref[...]
        out = g * jax.nn.sigmoid(z)
        y_ref[c * ROWS:(c + 1) * ROWS, :] = _rms(out, go_ref[...]).astype(BF16)


def _mixers(u, s0, ar8, ai8, bb, cc, d, w_glu_bf, b_glu, g_out_s5, lru_a, lru_b, lru_gg, lru_h0, g_out_lru, bt,
            chunks):
    tc = ROWS // bt
    t_rows = u.shape[0]
    step_rows = chunks * ROWS
    assert t_rows % step_rows == 0
    rows_f32 = pl.BlockSpec((step_rows, D_S5), lambda i: (i, 0))
    vec = _full((1, D_S5))
    return pl.pallas_call(
        functools.partial(_s5_body, tc, bt, chunks),
        grid=(t_rows // step_rows,),
        in_specs=[rows_f32, _full((bt, S5_COLS)), _full(ar8.shape), _full(ai8.shape), _full(bb.shape),
                  _full(cc.shape), vec, _full((D_S5, D_S5)), vec, vec,
                  rows_f32, rows_f32, rows_f32, _full((bt, D_LRU)), vec],
        out_specs=[rows_f32, _full((bt, S5_COLS)), rows_f32, _full((bt, D_LRU))],
        out_shape=[jax.ShapeDtypeStruct((t_rows, D_S5), BF16), jax.ShapeDtypeStruct((bt, S5_COLS), F32),
                   jax.ShapeDtypeStruct((t_rows, D_LRU), BF16), jax.ShapeDtypeStruct((bt, D_LRU), F32)],
        scratch_shapes=[pltpu.VMEM((chunks, ROWS, S5_COLS), F32), pltpu.VMEM((bt, S5_COLS), F32),
                        pltpu.VMEM((step_rows, D_LRU), F32), pltpu.VMEM((bt, D_LRU), F32)],
        compiler_params=_params("arbitrary"),
        name="s5_lru_mixers",
    )(u, s0, ar8, ai8, bb, cc, d, w_glu_bf, b_glu, g_out_s5, lru_a, lru_b, lru_gg, lru_h0, g_out_lru)


def _route_tile(xn, x_hi, rw_hi, rw_lo, rb):
    x_lo = (xn - x_hi.astype(F32)).astype(BF16)
    nt = (((1,), (1,)), ((), ()))
    logits = (lax.dot_general(rw_hi, x_hi, nt, preferred_element_type=F32)
              + lax.dot_general(rw_hi, x_lo, nt, preferred_element_type=F32)
              + lax.dot_general(rw_lo, x_hi, nt, preferred_element_type=F32)) + rb

    e_iota = lax.broadcasted_iota(jnp.int32, (N_EXPERTS, ROWS), 0).astype(F32)
    work = logits
    sels, vals = [], []
    for _ in range(TOP_K):
        m = jnp.max(work, axis=0, keepdims=True)
        idx = jnp.min(jnp.where(work == m, e_iota, float(N_EXPERTS)), axis=0, keepdims=True)
        sel = e_iota == idx
        work = jnp.where(sel, -jnp.inf, work)
        sels.append(sel)
        vals.append(m)
    exps = [jnp.exp(v - vals[0]) for v in vals]
    denom = exps[0] + exps[1] + exps[2] + exps[3]
    gates = [e / denom for e in exps]

    onehot = sels[0] | sels[1] | sels[2] | sels[3]
    rr = lax.broadcasted_iota(jnp.int32, (ROWS, ROWS), 0)
    cc = lax.broadcasted_iota(jnp.int32, (ROWS, ROWS), 1)
    before = (rr < cc).astype(BF16)
    prefix = jnp.dot(onehot.astype(BF16), before, preferred_element_type=F32)
    cnt = jnp.sum(onehot.astype(F32), axis=1, keepdims=True)
    cnt_pad = jnp.floor((cnt + (SEG_ALIGN - 1)) * (1.0 / SEG_ALIGN)) * SEG_ALIGN
    er = lax.broadcasted_iota(jnp.int32, (N_EXPERTS, N_EXPERTS), 0)
    ec = lax.broadcasted_iota(jnp.int32, (N_EXPERTS, N_EXPERTS), 1)
    seg_start = jnp.dot((ec < er).astype(BF16), jnp.broadcast_to(cnt_pad, (N_EXPERTS, LANES)).astype(BF16),
                        preferred_element_type=F32)[:, 0:1]
    where_to = prefix + seg_start
    poss = [jnp.sum(jnp.where(s, where_to, 0.0), axis=0, keepdims=True) for s in sels]

    s_iota = lax.broadcasted_iota(jnp.int32, (2 * TOP_K, ROWS), 0)
    pg = jnp.zeros((2 * TOP_K, ROWS), F32)
    for k in range(TOP_K):
        pg = jnp.where(s_iota == k, poss[k], pg)
        pg = jnp.where(s_iota == TOP_K + k, gates[k], pg)
    return pg, cnt


def _out_body(subtiles, tm_rows, *refs):
    if tm_rows:
        ys_ref, yl_ref, x_ref, g1_ref, sc_ref, sh_ref, perm_ref, *refs = refs
    else:
        ys_ref, yl_ref, x_ref, g1_ref, sc_ref, sh_ref, *refs = refs
    wo_ref, gf_ref, rwh_ref, rwl_ref, rb_ref, h_ref, xn_ref, pg_ref, cnt_ref = refs
    for c in range(subtiles):
        rows = slice(c * ROWS, (c + 1) * ROWS)
        ys = ys_ref[rows, :]
        yl = yl_ref[rows, :]
        if tm_rows:
            ys = jnp.dot(perm_ref[...], ys, preferred_element_type=F32).astype(BF16)
            yl = jnp.dot(perm_ref[...], yl, preferred_element_type=F32).astype(BF16)
            tc = x_ref.shape[1] // subtiles
            x = x_ref[:, c * tc:(c + 1) * tc, :].reshape(ROWS, D_MODEL)
        else:
            x = x_ref[rows, :]
        mixed = (jnp.dot(ys, wo_ref[0], preferred_element_type=F32)
                 + jnp.dot(yl, wo_ref[1], preferred_element_type=F32))
        h = x + g1_ref[0] * mixed
        if tm_rows:
            h_ref[:, c * tc:(c + 1) * tc, :] = h.reshape(x_ref.shape[0], tc, D_MODEL)
        else:
            h_ref[rows, :] = h
        xn = _rms(h, gf_ref[...]) * (1.0 + sc_ref[0]) + sh_ref[0]
        x_hi = xn.astype(BF16)
        xn_ref[rows, :] = x_hi
        pg, cnt = _route_tile(xn, x_hi, rwh_ref[...], rwl_ref[...], rb_ref[...])
        pg_ref[:, rows] = pg
        cnt_ref[c] = jnp.broadcast_to(cnt, (N_EXPERTS, LANES))


def _out_proj(ys5, ylru, x, modpat, pattern, from_tm, w_out_bf, g_ffn, rw_hi, rw_lo, router_b, subtiles):
    tm_rows = from_tm is not None
    t_rows = ys5.shape[0]
    step_rows = subtiles * ROWS
    half_spec = pl.BlockSpec((step_rows, D_S5), lambda i: (i, 0))
    if tm_rows:
        x_spec = pl.BlockSpec((x.shape[0], step_rows // x.shape[0], D_MODEL), lambda i: (0, i, 0))
    else:
        x_spec = pl.BlockSpec((step_rows, D_MODEL), lambda i: (i, 0))
    mod_spec = lambda k: pl.BlockSpec((1, ROWS, D_MODEL), lambda i: (pattern, 0, k))
    perm = [_full((ROWS, ROWS))] if tm_rows else []
    return pl.pallas_call(
        functools.partial(_out_body, subtiles, tm_rows),
        grid=(t_rows // step_rows,),
        in_specs=[half_spec, half_spec, x_spec, mod_spec(2), mod_spec(4), mod_spec(3), *perm,
                  _full((2, D_S5, D_MODEL)), _full((1, D_MODEL)),
                  _full((N_EXPERTS, D_MODEL)), _full((N_EXPERTS, D_MODEL)), _full((N_EXPERTS, 1))],
        out_specs=[x_spec, pl.BlockSpec((step_rows, D_MODEL), lambda i: (i, 0)),
                   pl.BlockSpec((2 * TOP_K, step_rows), lambda i: (0, i)),
                   pl.BlockSpec((subtiles, N_EXPERTS, LANES), lambda i: (i, 0, 0))],
        out_shape=[jax.ShapeDtypeStruct(x.shape, F32), jax.ShapeDtypeStruct((t_rows, D_MODEL), BF16),
                   jax.ShapeDtypeStruct((2 * TOP_K, t_rows), F32),
                   jax.ShapeDtypeStruct((t_rows // ROWS, N_EXPERTS, LANES), F32)],
        compiler_params=_params("arbitrary"),
        name="out_proj_router",
    )(ys5, ylru, x, modpat, modpat, modpat, *([from_tm] if tm_rows else []), w_out_bf, g_ffn, rw_hi, rw_lo,
      router_b)


def _start_segments(n_ref, hbm_ref, vmem_ref, hbm, vmem_buf, sem, step, to_hbm, unroll):
    def seg(e):
        n = n_ref[step * N_EXPERTS + e]
        h0 = hbm_ref[step * N_EXPERTS + e]
        v0 = vmem_ref[step * N_EXPERTS + e]

        def piece(off, size):
            h = hbm.at[pl.ds(pl.multiple_of(h0 + off, SEG_ALIGN), size)]
            v = vmem_buf.at[pl.ds(pl.multiple_of(v0 + off, SEG_ALIGN), size)]
            (pltpu.make_async_copy(v, h, sem) if to_hbm else pltpu.make_async_copy(h, v, sem)).start()

        if unroll:
            _row_pieces(n, ROWS, piece)
        else:
            def big(k, carry):
                piece(k * BIG_PIECE, BIG_PIECE)
                return carry

            lax.fori_loop(0, n // BIG_PIECE, big, 0)
            _row_pieces(n % BIG_PIECE, BIG_PIECE // 2, lambda off, size: piece(n // BIG_PIECE * BIG_PIECE + off, size))

    if unroll:
        for e in range(N_EXPERTS):
            seg(e)
    else:
        def body(e, carry):
            seg(e)
            return carry

        lax.fori_loop(0, N_EXPERTS, body, 0)


def _wait_rows(total, largest, hbm, sem):
    _row_pieces(total, largest, lambda off, size: pltpu.make_async_copy(
        hbm.at[pl.ds(0, size)], hbm.at[pl.ds(0, size)], sem).wait())


def _dispatch_body(tiles_p, n_ref, glob_ref, local_ref, tot_ref, nb_ref, pg_ref, xp_ref, xsm_ref, xs_hbm, stage,
                   sem):
    j = pl.program_id(0)
    last = pl.num_programs(0) - 1
    slot = j % 2

    def unused_blocks(act):
        def blk(b, carry):
            act(pltpu.make_async_copy(stage.at[slot, pl.ds(0, MOE_TM)],
                                      xs_hbm.at[pl.ds(pl.multiple_of(b * MOE_TM, MOE_TM), MOE_TM)], sem.at[slot]))
            return carry
        lax.fori_loop(nb_ref[0], xs_hbm.shape[0] // MOE_TM, blk, 0)

    def wait_step(step, s):
        _wait_rows(tot_ref[step], WAIT_MAX_PIECE, xs_hbm, sem.at[s])

    @pl.when(j >= 2)
    def _():
        wait_step(jnp.maximum(j - 2, 0), slot)

    @pl.when(j < last)
    def _():
        x = jnp.where(j < tiles_p, xp_ref[...], xsm_ref[...])
        pos = pg_ref[0:TOP_K, :]
        r = lax.broadcasted_iota(jnp.int32, (SORT_CHUNK, ROWS), 0).astype(F32).astype(BF16)
        for c in range(SORT_ROWS // SORT_CHUNK):
            rel = (pos - float(c * SORT_CHUNK)).astype(BF16)
            pick = (r == rel[0:1]) | (r == rel[1:2]) | (r == rel[2:3]) | (r == rel[3:4])
            stage[slot, c * SORT_CHUNK:(c + 1) * SORT_CHUNK, :] = jnp.dot(
                jnp.where(pick, jnp.ones((), BF16), jnp.zeros((), BF16)), x,
                preferred_element_type=F32).astype(BF16)

    @pl.when(j == last)
    def _():
        stage[slot, 0:MOE_TM, :] = jnp.zeros((MOE_TM, D_MODEL), BF16)

    _start_segments(n_ref, glob_ref, local_ref, xs_hbm, stage.at[slot], sem.at[slot], j, True, unroll=True)

    @pl.when(j == last)
    def _():
        unused_blocks(lambda c: c.start())

        @pl.when(j >= 1)
        def _():
            wait_step(jnp.maximum(j - 1, 0), 1 - slot)
        wait_step(j, slot)
        unused_blocks(lambda c: c.wait())


def _dispatch(seg_n, seg_glob, seg_local, seg_tot, nb_used, pg, xn_p, xn_s, n_rows):
    tiles_p = xn_p.shape[0] // ROWS
    tiles_s = xn_s.shape[0] // ROWS
    tiles = tiles_p + tiles_s
    grid_spec = pltpu.PrefetchScalarGridSpec(
        num_scalar_prefetch=5,
        grid=(tiles + 1,),
        in_specs=[pl.BlockSpec((2 * TOP_K, ROWS), lambda j, *_: (0, jnp.minimum(j, tiles - 1))),
                  pl.BlockSpec((ROWS, D_MODEL), lambda j, *_: (jnp.minimum(j, tiles_p - 1), 0)),
                  pl.BlockSpec((ROWS, D_MODEL), lambda j, *_: (jnp.clip(j - tiles_p, 0, tiles_s - 1), 0))],
        out_specs=pl.BlockSpec(memory_space=pl.ANY),
        scratch_shapes=[pltpu.VMEM((2, SORT_ROWS, D_MODEL), BF16), pltpu.SemaphoreType.DMA((2,))],
    )
    return pl.pallas_call(
        functools.partial(_dispatch_body, tiles_p),
        grid_spec=grid_spec,
        out_shape=jax.ShapeDtypeStruct((n_rows, D_MODEL), BF16),
        compiler_params=_params("arbitrary"),
        name="moe_dispatch",
    )(seg_n, seg_glob, seg_local, seg_tot, nb_used, pg, xn_p, xn_s)


def _moe_body(be_ref, nxt_ref, nv_ref, nb_ref, xs_ref, wgu_hbm, bgu_ref, wd_hbm, bd_ref, ys_ref,
              wgu_f32, wd_f32, wgu_bf, wd_bf, sem):
    i = pl.program_id(0)

    def weight_copies(e):
        return (pltpu.make_async_copy(wgu_hbm.at[e], wgu_f32, sem.at[0]),
                pltpu.make_async_copy(wd_hbm.at[e], wd_f32, sem.at[1]))

    @pl.when(i >= nb_ref[0])
    def _():
        ys_ref[...] = jnp.zeros_like(ys_ref)

    @pl.when(i < nb_ref[0])
    def _():
        e = be_ref[i]

        @pl.when(i == 0)
        def _():
            for c in weight_copies(e):
                c.start()

        @pl.when(jnp.logical_or(i == 0, e != be_ref[jnp.maximum(i - 1, 0)]))
        def _():
            for c in weight_copies(e):
                c.wait()
            wgu_bf[...] = wgu_f32[...].astype(BF16)
            wd_bf[...] = wd_f32[...].astype(BF16)

            @pl.when(nxt_ref[i] >= 0)
            def _():
                for c in weight_copies(nxt_ref[i]):
                    c.start()

        def ffn(x):
            hg = jnp.dot(x, wgu_bf[...], preferred_element_type=F32) + bgu_ref[0]
            gate = jnp.minimum(hg[:, :D_FF], SWIGLU_LIMIT)
            up = jnp.clip(hg[:, D_FF:], -SWIGLU_LIMIT, SWIGLU_LIMIT)
            act = (up + 1.0) * (gate * jax.nn.sigmoid(SWIGLU_ALPHA * gate))
            return (jnp.dot(act.astype(BF16), wd_bf[...], preferred_element_type=F32) + bd_ref[0]).astype(BF16)

        for parts in range(1, MOE_TM // MOE_PART + 1):
            lo = (parts - 1) * MOE_PART
            last_path = parts == MOE_TM // MOE_PART

            @pl.when(jnp.logical_and(nv_ref[i] > lo, jnp.logical_or(last_path, nv_ref[i] <= lo + MOE_PART)))
            def _(parts=parts):
                rows = parts * MOE_PART
                for p0 in range(0, rows, 2 * MOE_PART):
                    p1 = min(p0 + 2 * MOE_PART, rows)
                    ys_ref[p0:p1, :] = ffn(xs_ref[p0:p1, :])
                if rows < MOE_TM:
                    ys_ref[rows:, :] = jnp.zeros((MOE_TM - rows, D_MODEL), BF16)


def _moe(xs, block_expert, block_next, block_rows, nb_used, w_gu, b_gu, w_down, b_down):
    n_blocks = xs.shape[0] // MOE_TM
    in_rows = pl.BlockSpec((MOE_TM, D_MODEL), lambda i, be, nx, nv, nb: (jnp.minimum(i, nb[0] - 1), 0))
    grid_spec = pltpu.PrefetchScalarGridSpec(
        num_scalar_prefetch=4,
        grid=(n_blocks,),
        in_specs=[in_rows,
                  pl.BlockSpec(memory_space=pl.ANY),
                  pl.BlockSpec((1, 1, 2 * D_FF), lambda i, be, nx, nv, nb: (be[i], 0, 0)),
                  pl.BlockSpec(memory_space=pl.ANY),
                  pl.BlockSpec((1, 1, D_MODEL), lambda i, be, nx, nv, nb: (be[i], 0, 0))],
        out_specs=pl.BlockSpec((MOE_TM, D_MODEL), lambda i, be, nx, nv, nb: (i, 0)),
        scratch_shapes=[pltpu.VMEM((D_MODEL, 2 * D_FF), F32), pltpu.VMEM((D_FF, D_MODEL), F32),
                        pltpu.VMEM((D_MODEL, 2 * D_FF), BF16), pltpu.VMEM((D_FF, D_MODEL), BF16),
                        pltpu.SemaphoreType.DMA((2,))],
    )
    return pl.pallas_call(
        _moe_body,
        grid_spec=grid_spec,
        out_shape=jax.ShapeDtypeStruct(xs.shape, BF16),
        compiler_params=_params("arbitrary"),
        name="moe_experts",
    )(block_expert, block_next, block_rows, nb_used, xs, w_gu, b_gu, w_down, b_down)


def _fin_body(tiles_p, n_ref, glob_ref, local_ref, tot_ref, hp_ref, hs_ref, pg_ref, g2_ref, gf_ref, ys_hbm,
              op_ref, os_ref, ybuf, sem):
    j = pl.program_id(0)
    tiles = pl.num_programs(0)
    slot = j % 2
    is_prompt = j < tiles_p

    def fetch(step, s):
        _start_segments(n_ref, glob_ref, local_ref, ys_hbm, ybuf.at[s], sem.at[s], step, False, unroll=False)

    @pl.when(j == 0)
    def _():
        ybuf[...] = jnp.zeros_like(ybuf)
        fetch(0, 0)

    _wait_rows(tot_ref[j], WAIT_MAX_PIECE, ys_hbm, sem.at[slot])

    @pl.when(j + 1 < tiles)
    def _():
        fetch(jnp.minimum(j + 1, tiles - 1), 1 - slot)

    pg = pg_ref[...]
    ff = jnp.zeros((ROWS, D_MODEL), F32)
    r = lax.broadcasted_iota(jnp.int32, (ROWS, SORT_CHUNK), 1).astype(F32).astype(BF16)
    gates = pg[:, TOP_K:].astype(BF16)
    for c in range(SORT_ROWS // SORT_CHUNK):
        rel = (pg[:, :TOP_K] - float(c * SORT_CHUNK)).astype(BF16)
        w = jnp.zeros((ROWS, SORT_CHUNK), BF16)
        for k in range(TOP_K):
            w = jnp.where(r == rel[:, k:k + 1], gates[:, k:k + 1], w)
        ff = ff + jnp.dot(w, ybuf[slot, c * SORT_CHUNK:(c + 1) * SORT_CHUNK, :], preferred_element_type=F32)
    h = jnp.where(is_prompt, hp_ref[...].reshape(ROWS, D_MODEL), hs_ref[...])
    y = _rms(h + g2_ref[0] * ff, gf_ref[...])

    @pl.when(is_prompt)
    def _():
        op_ref[...] = y.reshape(op_ref.shape)

    @pl.when(jnp.logical_not(is_prompt))
    def _():
        os_ref[...] = y


def _final(seg_n, seg_glob, seg_local, seg_tot, h_p, h_s, pg_t, modpat, g_final, ys, tiles_p, tiles_s):
    hp_spec, hs_spec = _tile_specs(h_p.shape[0], tiles_p, tiles_s)
    grid_spec = pltpu.PrefetchScalarGridSpec(
        num_scalar_prefetch=4,
        grid=(tiles_p + tiles_s,),
        in_specs=[hp_spec, hs_spec,
                  pl.BlockSpec((ROWS, 2 * TOP_K), lambda j, *_: (j, 0)),
                  _mod_spec(5, tiles_p), _full((1, D_MODEL)),
                  pl.BlockSpec(memory_space=pl.ANY)],
        out_specs=[hp_spec, hs_spec],
        scratch_shapes=[pltpu.VMEM((2, SORT_ROWS, D_MODEL), BF16), pltpu.SemaphoreType.DMA((2,))],
    )
    return pl.pallas_call(
        functools.partial(_fin_body, tiles_p),
        grid_spec=grid_spec,
        out_shape=[jax.ShapeDtypeStruct(h_p.shape, F32), jax.ShapeDtypeStruct(h_s.shape, F32)],
        compiler_params=_params("arbitrary"),
        name="combine_final",
    )(seg_n, seg_glob, seg_local, seg_tot, h_p, h_s, pg_t, modpat, g_final, ys)


def _block_diag(w):
    h, i, j = w.shape
    return jnp.einsum('hij,hk->hikj', w, jnp.eye(h, dtype=w.dtype)).reshape(h * i, h * j)


def _s5_cols(re, im):
    b = re.shape[0]
    stack = jnp.stack([re.reshape(b, S5_BLOCKS, -1), im.reshape(b, S5_BLOCKS, -1)], axis=2)
    return stack.reshape(b, S5_COLS)


def _s5_uncols(cols):
    b = cols.shape[0]
    c = cols.reshape(b, S5_BLOCKS, 2, S5_GROUPS // S5_BLOCKS, S5_STATE)
    return (c[:, :, 0].reshape(b, S5_GROUPS, S5_STATE), c[:, :, 1].reshape(b, S5_GROUPS, S5_STATE))


def _moe_rows_bound(tiles):
    worst = tiles * (TOP_K * ROWS + N_EXPERTS * (SEG_ALIGN - 1)) + N_EXPERTS * (MOE_TM - SEG_ALIGN)
    return (worst + MOE_TM - 1) // MOE_TM * MOE_TM


def _plan(cnt):
    cnt = cnt.astype(jnp.int32)
    tiles = cnt.shape[0]
    cp = (cnt + SEG_ALIGN - 1) // SEG_ALIGN * SEG_ALIGN
    local = jnp.cumsum(cp, axis=1) - cp
    group = jnp.sum(cp, axis=0)
    group_pad = (group + MOE_TM - 1) // MOE_TM * MOE_TM
    pend = jnp.cumsum(group_pad)
    pstart = pend - group_pad
    glob = pstart[None, :] + jnp.cumsum(cp, axis=0) - cp
    gap = group_pad - group
    seg_n = jnp.concatenate([cp, gap[None]], axis=0).reshape(-1)
    seg_local = jnp.concatenate([local, jnp.zeros((1, N_EXPERTS), jnp.int32)], axis=0).reshape(-1)
    seg_glob = jnp.concatenate([glob, (pstart + group)[None]], axis=0).reshape(-1)
    seg_tot = jnp.concatenate([jnp.sum(cp, axis=1), jnp.sum(gap)[None]]).astype(jnp.int32)
    n_blocks = _moe_rows_bound(tiles) // MOE_TM
    block_row0 = jnp.arange(n_blocks, dtype=jnp.int32) * MOE_TM
    block_expert = jnp.minimum(jnp.sum(block_row0[:, None] >= pend[None, :], axis=1), N_EXPERTS - 1).astype(jnp.int32)
    nb_used = (pend[-1] // MOE_TM).astype(jnp.int32).reshape(1)
    experts = jnp.arange(N_EXPERTS, dtype=jnp.int32)
    later_owner = jnp.where((experts[None, :] > experts[:, None]) & (group_pad[None, :] > 0), experts[None, :],
                            N_EXPERTS)
    next_owner = jnp.min(later_owner, axis=1)
    next_owner = jnp.where(next_owner == N_EXPERTS, -1, next_owner).astype(jnp.int32)
    owner = block_expert[:, None] == experts[None, :]
    block_next = jnp.sum(jnp.where(owner, next_owner[None, :], 0), axis=1).astype(jnp.int32)
    group_end = jnp.sum(jnp.where(owner, (pstart + group)[None, :], 0), axis=1)
    block_rows = jnp.clip(group_end - block_row0, 0, MOE_TM).astype(jnp.int32)
    return seg_n, seg_glob, seg_local, seg_tot, block_expert, block_next, block_rows, nb_used


def kernel(x_prompt, x_sample, state_s5_re, state_s5_im, state_lru_h, state_conv, c_prompt, c_sample, w_ada, b_ada, g_mix, w_in, s5_a_re, s5_a_im, s5_log_dt, s5_b_re, s5_b_im, s5_c_re, s5_c_im, s5_d, s5_w_glu, s5_b_glu, lru_conv_w, lru_conv_b, lru_w_a, lru_b_a, lru_w_x, lru_b_x, lru_lambda, g_out_s5, g_out_lru, w_out, g_ffn, router_w, router_b, moe_w_gu, moe_b_gu, moe_w_down, moe_b_down, g_final):
    assert w_ada.shape[0] == 1, "one layer"
    bp, lp, _ = x_prompt.shape
    bs, ls, _ = x_sample.shape
    assert ROWS % bp == 0 and ROWS % bs == 0 and (bp * lp) % ROWS == 0 and (bs * ls) % ROWS == 0
    tiles_p = bp * lp // ROWS
    tiles_s = bs * ls // ROWS
    row = lambda v: v.reshape(1, -1)

    ab_re, ab_im, bb_re, bb_im = _s5_prep(s5_a_re[0], s5_a_im[0], s5_log_dt[0].reshape(S5_GROUPS, 1),
                                          jnp.swapaxes(s5_b_re[0], 1, 2), jnp.swapaxes(s5_b_im[0], 1, 2))
    gpb = S5_GROUPS // S5_BLOCKS
    eye = jnp.eye(gpb, dtype=F32)

    def in_blocks(b):
        b = b.reshape(S5_BLOCKS, gpb, S5_GROUP_CH, S5_STATE)
        return jnp.einsum('bgjn,gh->bgjhn', b, eye).reshape(S5_BLOCKS, gpb * S5_GROUP_CH, gpb * S5_STATE)

    def out_blocks(c):
        c = c.reshape(S5_BLOCKS, gpb, S5_GROUP_CH, S5_STATE)
        return jnp.einsum('bgjn,gh->bgnhj', c, eye).reshape(S5_BLOCKS, gpb * S5_STATE, gpb * S5_GROUP_CH)

    ar8 = jnp.broadcast_to(ab_re.reshape(1, -1), (SUBLANES, S5_GROUPS * S5_STATE))
    ai8 = jnp.broadcast_to(ab_im.reshape(1, -1), (SUBLANES, S5_GROUPS * S5_STATE))
    bb = jnp.concatenate([in_blocks(bb_re), in_blocks(bb_im)], axis=-1).astype(BF16)
    cc = jnp.concatenate([out_blocks(s5_c_re[0]), -out_blocks(s5_c_im[0])], axis=1).astype(BF16)
    wa_bd = _block_diag(lru_w_a[0]).astype(BF16)
    wx_bd = _block_diag(lru_w_x[0]).astype(BF16)
    rw_t = router_w[0].T
    rw_hi = rw_t.astype(BF16)
    rw_lo = (rw_t - rw_hi.astype(F32)).astype(BF16)

    tc = ROWS // bp
    modpat = _adaln(jnp.concatenate([c_prompt, c_sample], axis=0), w_ada[0], row(b_ada[0]), bp)
    r = jnp.arange(ROWS)
    tm_of = (r % tc) * bp + r // tc
    to_tm = (r[:, None] == tm_of[None, :]).astype(BF16)
    from_tm = to_tm.T
    pair = 2 if tiles_p % 2 == 0 else 1

    def conv_tm(cv):
        return jnp.swapaxes(cv, 0, 1).reshape(-1, D_LRU)

    def conv_bm(cv, b):
        return jnp.swapaxes(cv.reshape(CONV_WIDTH - 1, b, D_LRU), 0, 1)

    x_s = jnp.swapaxes(x_sample, 0, 1).reshape(bs * ls, D_MODEL)
    in_args = (row(g_mix[0]), w_in[0].astype(BF16))
    lru_args = (lru_conv_w[0], row(lru_conv_b[0]), wa_bd, row(lru_b_a[0]), wx_bd, row(lru_b_x[0]),
                row(lru_lambda[0]))
    u_p, la_p, lb_p, lg_p, cvp = _in_proj(x_prompt, modpat, 0, to_tm, *in_args,
                                          jnp.zeros(((CONV_WIDTH - 1) * bp, D_LRU), F32), *lru_args, bp, pair)
    u_s, la_s, lb_s, lg_s, cvs = _in_proj(x_s, modpat, 1, None, *in_args, conv_tm(state_conv[0]), *lru_args, bs, 1)

    s5_args = (ar8, ai8, bb, cc, row(s5_d[0]), s5_w_glu[0].astype(BF16), row(s5_b_glu[0]), row(g_out_s5[0]))
    ys5_p, s5p, ylru_p, hp = _mixers(u_p, jnp.zeros((bp, S5_COLS), F32), *s5_args, la_p, lb_p, lg_p,
                                     jnp.zeros((bp, D_LRU), F32), row(g_out_lru[0]), bp, pair)
    ys5_s, s5s, ylru_s, hs = _mixers(u_s, _s5_cols(state_s5_re[0], state_s5_im[0]), *s5_args, la_s, lb_s, lg_s,
                                     state_lru_h[0], row(g_out_lru[0]), bs, 1)

    out_args = (w_out[0].astype(BF16).reshape(2, D_S5, D_MODEL), row(g_ffn[0]), rw_hi, rw_lo,
                router_b[0].reshape(N_EXPERTS, 1))
    h_p, xn_p, pg_p, cnt_p = _out_proj(ys5_p, ylru_p, x_prompt, modpat, 0, from_tm, *out_args, pair)
    h_s, xn_s, pg_s, cnt_s = _out_proj(ys5_s, ylru_s, x_s, modpat, 1, None, *out_args, 1)
    pg = jnp.concatenate([pg_p, pg_s], axis=1)
    cnt = jnp.concatenate([cnt_p, cnt_s], axis=0)

    seg_n, seg_glob, seg_local, seg_tot, block_expert, block_next, block_rows, nb_used = _plan(cnt[:, :, 0])
    xs = _dispatch(seg_n, seg_glob, seg_local, seg_tot, nb_used, pg, xn_p, xn_s,
                   _moe_rows_bound(tiles_p + tiles_s))
    ys = _moe(xs, block_expert, block_next, block_rows, nb_used, moe_w_gu[0], moe_b_gu[0].reshape(N_EXPERTS, 1, 2 * D_FF),
              moe_w_down[0], moe_b_down[0].reshape(N_EXPERTS, 1, D_MODEL))
    y_prompt, y_s = _final(seg_n, seg_glob, seg_local, seg_tot, h_p, h_s, pg.T, modpat, row(g_final), ys,
                           tiles_p, tiles_s)
    y_sample = jnp.swapaxes(y_s.reshape(ls, bs, D_MODEL), 0, 1)
    s5p_re, s5p_im = _s5_uncols(s5p)
    s5s_re, s5s_im = _s5_uncols(s5s)
    return (y_prompt, y_sample,
            s5p_re[None], s5p_im[None], hp[None], conv_bm(cvp, bp)[None],
            s5s_re[None], s5s_im[None], hs[None], conv_bm(cvs, bs)[None])
```

```python
import functools

import jax
import jax.numpy as jnp
from jax import lax
from jax.experimental import pallas as pl
from jax.experimental.pallas import tpu as pltpu

D_MODEL = 1024
D_S5 = 512
D_LRU = 512
S5_GROUPS = 32
S5_GROUP_CH = 16
S5_STATE = 64
S5_COLS = 2 * S5_GROUPS * S5_STATE
S5_BLOCKS = 4
LRU_HEADS = 8
LRU_HEAD_DIM = 64
LRU_C = 8.0
CONV_WIDTH = 4
N_EXPERTS = 32
TOP_K = 4
D_FF = 1024
SWIGLU_LIMIT = 7.0
SWIGLU_ALPHA = 1.702
N_MOD = 6
EPS = 1e-6

ROWS = 512
MOE_TM = 512
MOE_PART = 128
BIG_PIECE = 128
WAIT_MAX_PIECE = 8192
SEG_ALIGN = 16
SORT_ROWS = 2560
SORT_CHUNK = 256
SUBLANES = 8
LANES = 128
VMEM_LIMIT = 48 * 1024 * 1024

BF16 = jnp.bfloat16
F32 = jnp.float32


def _params(*sem):
    return pltpu.CompilerParams(dimension_semantics=sem, vmem_limit_bytes=VMEM_LIMIT)


def _full(shape):
    return pl.BlockSpec(shape, lambda *_: (0,) * len(shape))


def _rms(x, g):
    return x * lax.rsqrt(jnp.mean(x * x, axis=-1, keepdims=True) + EPS) * g


def _gelu(x):
    return 0.5 * x * (1.0 + lax.erf(x * (2.0 ** -0.5)))


def _expm1(x):
    u = jnp.exp(x)
    d = u - 1.0
    return jnp.where(d == 0.0, x, jnp.where(d == -1.0, -1.0, d * x / jnp.log(u)))


def _row_pieces(n, largest, fn):
    off = 0
    bit = largest
    while bit >= SEG_ALIGN:
        @pl.when((n & bit) != 0)
        def _(off=off, bit=bit):
            fn(off, bit)
        off = off + (n & bit)
        bit //= 2


def _mod_body(bp, c_ref, w_ref, b_ref, o_ref):
    c = c_ref[...]
    s = (c * jax.nn.sigmoid(c)).astype(BF16)
    mod = jnp.dot(s, w_ref[...].astype(BF16), preferred_element_type=F32) + b_ref[...]
    bs = mod.shape[0] - bp
    o_ref[0] = jnp.broadcast_to(mod[:bp][:, None, :], (bp, ROWS // bp, D_MODEL)).reshape(ROWS, D_MODEL)
    o_ref[1] = jnp.broadcast_to(mod[bp:][None], (ROWS // bs, bs, D_MODEL)).reshape(ROWS, D_MODEL)


def _adaln(c, w_ada, b_ada, bp):
    m = c.shape[0]
    return pl.pallas_call(
        functools.partial(_mod_body, bp),
        grid=(N_MOD,),
        in_specs=[pl.BlockSpec((m, D_MODEL), lambda j: (0, 0)),
                  pl.BlockSpec((D_MODEL, D_MODEL), lambda j: (0, j)),
                  pl.BlockSpec((1, D_MODEL), lambda j: (0, j))],
        out_specs=pl.BlockSpec((2, ROWS, D_MODEL), lambda j: (0, 0, j)),
        out_shape=jax.ShapeDtypeStruct((2, ROWS, N_MOD * D_MODEL), F32),
        compiler_params=_params("arbitrary"),
        name="adaln",
    )(c, w_ada, b_ada)


def _mod_spec(k, tiles_p):
    return pl.BlockSpec((1, ROWS, D_MODEL), lambda i, *_: (jnp.where(i < tiles_p, 0, 1), 0, k))


def _tile_specs(bp, tiles_p, tiles_s):
    tc = ROWS // bp
    p_spec = pl.BlockSpec((bp, tc, D_MODEL), lambda i, *_: (0, jnp.minimum(i, tiles_p - 1), 0))
    s_spec = pl.BlockSpec((ROWS, D_MODEL), lambda i, *_: (jnp.clip(i - tiles_p, 0, tiles_s - 1), 0))
    return p_spec, s_spec


def _in_body(subtiles, bt, to_tm, *refs):
    if to_tm:
        x_ref, sc_ref, sh_ref, perm_ref, *refs = refs
    else:
        x_ref, sc_ref, sh_ref, *refs = refs
    (g_ref, w_ref, cv0_ref, cw_ref, cb_ref, wa_ref, ba_ref, wx_ref, bx_ref, lam_ref,
     u_ref, a_ref, b_ref, gg_ref, cvout_ref, xp_ref) = refs
    halo = (CONV_WIDTH - 1) * bt

    @pl.when(pl.program_id(0) == 0)
    def _():
        xp_ref[...] = cv0_ref[...]

    lam = lam_ref[...]
    softplus_neg_lam = jnp.maximum(-lam, 0.0) + jnp.log1p(jnp.exp(-jnp.abs(lam)))
    earlier = xp_ref[...]
    for c in range(subtiles):
        rows = slice(c * ROWS, (c + 1) * ROWS)
        if to_tm:
            tc = x_ref.shape[1] // subtiles
            x = x_ref[:, c * tc:(c + 1) * tc, :].reshape(ROWS, D_MODEL)
        else:
            x = x_ref[rows, :]
        xn = (_rms(x, g_ref[...]) * (1.0 + sc_ref[0]) + sh_ref[0]).astype(BF16)
        if to_tm:
            xn = jnp.dot(perm_ref[...], xn, preferred_element_type=F32).astype(BF16)
        p = jnp.dot(xn, w_ref[...], preferred_element_type=F32)
        u_ref[rows, :] = p[:, :D_S5]
        gg_ref[rows, :] = _gelu(p[:, D_S5 + D_LRU:])

        xr = p[:, D_S5:D_S5 + D_LRU]
        xp = jnp.concatenate([earlier, xr], axis=0)
        earlier = xr[ROWS - halo:, :]
        xc = cb_ref[...] + sum(xp[k * bt:k * bt + ROWS, :] * cw_ref[k:k + 1, :] for k in range(CONV_WIDTH))
        xcb = xc.astype(BF16)
        r = jax.nn.sigmoid(jnp.dot(xcb, wa_ref[...], preferred_element_type=F32) + ba_ref[...])
        i = jax.nn.sigmoid(jnp.dot(xcb, wx_ref[...], preferred_element_type=F32) + bx_ref[...])
        log_a = -LRU_C * r * softplus_neg_lam
        a_ref[rows, :] = jnp.exp(log_a)
        b_ref[rows, :] = jnp.sqrt(-_expm1(2.0 * log_a)) * (i * xc)

    xp_ref[...] = earlier
    cvout_ref[...] = earlier


def _in_proj(x, modpat, pattern, to_tm, g_mix, w_in_bf, conv0_tm, conv_w, conv_b, wa_bd, b_a, wx_bd, b_x, lam, bt,
             subtiles):
    tm = to_tm is not None
    t_rows = x.shape[0] * x.shape[1] if tm else x.shape[0]
    step_rows = subtiles * ROWS
    halo = (CONV_WIDTH - 1) * bt
    if tm:
        x_spec = pl.BlockSpec((x.shape[0], step_rows // x.shape[0], D_MODEL), lambda i: (0, i, 0))
    else:
        x_spec = pl.BlockSpec((step_rows, D_MODEL), lambda i: (i, 0))
    mod_spec = lambda k: pl.BlockSpec((1, ROWS, D_MODEL), lambda i: (pattern, 0, k))
    row_spec = pl.BlockSpec((step_rows, D_S5), lambda i: (i, 0))
    vec = _full((1, D_LRU))
    return pl.pallas_call(
        functools.partial(_in_body, subtiles, bt, tm),
        grid=(t_rows // step_rows,),
        in_specs=[x_spec, mod_spec(1), mod_spec(0), *([_full((ROWS, ROWS))] if tm else []), _full((1, D_MODEL)),
                  _full((D_MODEL, D_S5 + 2 * D_LRU)), _full((halo, D_LRU)), _full((CONV_WIDTH, D_LRU)), vec,
                  _full((D_LRU, D_LRU)), vec, _full((D_LRU, D_LRU)), vec, vec],
        out_specs=[row_spec, row_spec, row_spec, row_spec, _full((halo, D_LRU))],
        out_shape=[jax.ShapeDtypeStruct((t_rows, D_S5), F32)] * 4 + [jax.ShapeDtypeStruct((halo, D_LRU), F32)],
        scratch_shapes=[pltpu.VMEM((halo, D_LRU), F32)],
        compiler_params=_params("arbitrary"),
        name="in_proj",
    )(x, modpat, modpat, *([to_tm] if tm else []), g_mix, w_in_bf, conv0_tm, conv_w, conv_b, wa_bd, b_a, wx_bd, b_x,
      lam)


def _s5_prep_body(are_ref, aim_ref, ldt_ref, bre_ref, bim_ref, abre_ref, abim_ref, bbre_ref, bbim_ref):
    a_re = are_ref[...]
    a_im = aim_ref[...]
    dt = jnp.exp(ldt_ref[...])
    mag = jnp.exp(dt * a_re)
    ang = dt * a_im
    ab_re = mag * jnp.cos(ang)
    ab_im = mag * jnp.sin(ang)
    den = a_re * a_re + a_im * a_im
    q_re = ((ab_re - 1.0) * a_re + ab_im * a_im) / den
    q_im = (ab_im * a_re - (ab_re - 1.0) * a_im) / den
    abre_ref[...] = ab_re
    abim_ref[...] = ab_im
    b_re = bre_ref[...]
    b_im = bim_ref[...]
    bbre_ref[...] = q_re[:, None, :] * b_re - q_im[:, None, :] * b_im
    bbim_ref[...] = q_re[:, None, :] * b_im + q_im[:, None, :] * b_re


def _s5_prep(a_re, a_im, log_dt, b_re_t, b_im_t):
    gn = jax.ShapeDtypeStruct((S5_GROUPS, S5_STATE), F32)
    gjn = jax.ShapeDtypeStruct((S5_GROUPS, S5_GROUP_CH, S5_STATE), F32)
    return pl.pallas_call(_s5_prep_body, out_shape=[gn, gn, gjn, gjn], name="s5_prep")(
        a_re, a_im, log_dt, b_re_t, b_im_t)


def _s5_body(tc, bt, chunks, u_ref, s0_ref, ar_ref, ai_ref, bb_ref, cc_ref, d_ref, wg_ref, bg_ref, go_ref,
             la_ref, lb_ref, lg_ref, lh0_ref, lgo_ref, y_ref, sout_ref, yl_ref, lhout_ref,
             bu_ref, st_ref, lh_ref, lst_ref):
    half = S5_COLS // S5_BLOCKS // 2

    @pl.when(pl.program_id(0) == 0)
    def _():
        st_ref[...] = s0_ref[...]
        lst_ref[...] = lh0_ref[...]

    for c in range(chunks):
        ub = u_ref[c * ROWS:(c + 1) * ROWS, :].astype(BF16)
        for j in range(S5_BLOCKS):
            bu_ref[c, :, 2 * half * j:2 * half * (j + 1)] = jnp.dot(
                ub[:, LANES * j:LANES * (j + 1)], bb_ref[j], preferred_element_type=F32)

    for sb in range(bt // SUBLANES):
        base = sb * SUBLANES
        hr = [st_ref[base:base + SUBLANES, 2 * half * j:2 * half * j + half] for j in range(S5_BLOCKS)]
        hi = [st_ref[base:base + SUBLANES, 2 * half * j + half:2 * half * (j + 1)] for j in range(S5_BLOCKS)]
        hl = lst_ref[base:base + SUBLANES, :]
        for c in range(chunks):
            for t in range(tc):
                r0 = t * bt + base
                row = c * ROWS + r0
                hl = la_ref[row:row + SUBLANES, :] * hl + lb_ref[row:row + SUBLANES, :]
                lh_ref[row:row + SUBLANES, :] = hl
                for j in range(S5_BLOCKS):
                    ar = ar_ref[:, half * j:half * (j + 1)]
                    ai = ai_ref[:, half * j:half * (j + 1)]
                    b_re = bu_ref[c, r0:r0 + SUBLANES, 2 * half * j:2 * half * j + half]
                    b_im = bu_ref[c, r0:r0 + SUBLANES, 2 * half * j + half:2 * half * (j + 1)]
                    h_re = ar * hr[j] - ai * hi[j] + b_re
                    h_im = ar * hi[j] + ai * hr[j] + b_im
                    bu_ref[c, r0:r0 + SUBLANES, 2 * half * j:2 * half * j + half] = h_re
                    bu_ref[c, r0:r0 + SUBLANES, 2 * half * j + half:2 * half * (j + 1)] = h_im
                    hr[j], hi[j] = h_re, h_im
        for j in range(S5_BLOCKS):
            st_ref[base:base + SUBLANES, 2 * half * j:2 * half * j + half] = hr[j]
            st_ref[base:base + SUBLANES, 2 * half * j + half:2 * half * (j + 1)] = hi[j]
        lst_ref[base:base + SUBLANES, :] = hl
    sout_ref[...] = st_ref[...]
    lhout_ref[...] = lst_ref[...]
    yl_ref[...] = _rms(lh_ref[...] * lg_ref[...], lgo_ref[...]).astype(BF16)

    for c in range(chunks):
        y = jnp.concatenate(
            [jnp.dot(bu_ref[c, :, 2 * half * j:2 * half * (j + 1)].astype(BF16), cc_ref[j],
                     preferred_element_type=F32) for j in range(S5_BLOCKS)], axis=-1)
        y = y + d_ref[...] * u_ref[c * ROWS:(c + 1) * ROWS, :]
        g = _gelu(y)
        z = jnp.dot(g.astype(BF16), wg_ref[...], preferred_element_type=F32) + bg_ref[...]
        out = g * jax.nn.sigmoid(z)
        y_ref[c * ROWS:(c + 1) * ROWS, :] = _rms(out, go_ref[...]).astype(BF16)


def _mixers(u, s0, ar8, ai8, bb, cc, d, w_glu_bf, b_glu, g_out_s5, lru_a, lru_b, lru_gg, lru_h0, g_out_lru, bt,
            chunks):
    tc = ROWS // bt
    t_rows = u.shape[0]
    step_rows = chunks * ROWS
    assert t_rows % step_rows == 0
    rows_f32 = pl.BlockSpec((step_rows, D_S5), lambda i: (i, 0))
    vec = _full((1, D_S5))
    return pl.pallas_call(
        functools.partial(_s5_body, tc, bt, chunks),
        grid=(t_rows // step_rows,),
        in_specs=[rows_f32, _full((bt, S5_COLS)), _full(ar8.shape), _full(ai8.shape), _full(bb.shape),
                  _full(cc.shape), vec, _full((D_S5, D_S5)), vec, vec,
                  rows_f32, rows_f32, rows_f32, _full((bt, D_LRU)), vec],
        out_specs=[rows_f32, _full((bt, S5_COLS)), rows_f32, _full((bt, D_LRU))],
        out_shape=[jax.ShapeDtypeStruct((t_rows, D_S5), BF16), jax.ShapeDtypeStruct((bt, S5_COLS), F32),
                   jax.ShapeDtypeStruct((t_rows, D_LRU), BF16), jax.ShapeDtypeStruct((bt, D_LRU), F32)],
        scratch_shapes=[pltpu.VMEM((chunks, ROWS, S5_COLS), F32), pltpu.VMEM((bt, S5_COLS), F32),
                        pltpu.VMEM((step_rows, D_LRU), F32), pltpu.VMEM((bt, D_LRU), F32)],
        compiler_params=_params("arbitrary"),
        name="s5_lru_mixers",
    )(u, s0, ar8, ai8, bb, cc, d, w_glu_bf, b_glu, g_out_s5, lru_a, lru_b, lru_gg, lru_h0, g_out_lru)


def _route_tile(xn, x_hi, rw_hi, rw_lo, rb):
    x_lo = (xn - x_hi.astype(F32)).astype(BF16)
    nt = (((1,), (1,)), ((), ()))
    logits = (lax.dot_general(rw_hi, x_hi, nt, preferred_element_type=F32)
              + lax.dot_general(rw_hi, x_lo, nt, preferred_element_type=F32)
              + lax.dot_general(rw_lo, x_hi, nt, preferred_element_type=F32)) + rb

    e_iota = lax.broadcasted_iota(jnp.int32, (N_EXPERTS, ROWS), 0).astype(F32)
    work = logits
    sels, vals = [], []
    for _ in range(TOP_K):
        m = jnp.max(work, axis=0, keepdims=True)
        idx = jnp.min(jnp.where(work == m, e_iota, float(N_EXPERTS)), axis=0, keepdims=True)
        sel = e_iota == idx
        work = jnp.where(sel, -jnp.inf, work)
        sels.append(sel)
        vals.append(m)
    exps = [jnp.exp(v - vals[0]) for v in vals]
    denom = exps[0] + exps[1] + exps[2] + exps[3]
    gates = [e / denom for e in exps]

    onehot = sels[0] | sels[1] | sels[2] | sels[3]
    rr = lax.broadcasted_iota(jnp.int32, (ROWS, ROWS), 0)
    cc = lax.broadcasted_iota(jnp.int32, (ROWS, ROWS), 1)
    before = (rr < cc).astype(BF16)
    prefix = jnp.dot(onehot.astype(BF16), before, preferred_element_type=F32)
    cnt = jnp.sum(onehot.astype(F32), axis=1, keepdims=True)
    cnt_pad = jnp.floor((cnt + (SEG_ALIGN - 1)) * (1.0 / SEG_ALIGN)) * SEG_ALIGN
    er = lax.broadcasted_iota(jnp.int32, (N_EXPERTS, N_EXPERTS), 0)
    ec = lax.broadcasted_iota(jnp.int32, (N_EXPERTS, N_EXPERTS), 1)
    seg_start = jnp.dot((ec < er).astype(BF16), jnp.broadcast_to(cnt_pad, (N_EXPERTS, LANES)).astype(BF16),
                        preferred_element_type=F32)[:, 0:1]
    where_to = prefix + seg_start
    poss = [jnp.sum(jnp.where(s, where_to, 0.0), axis=0, keepdims=True) for s in sels]

    s_iota = lax.broadcasted_iota(jnp.int32, (2 * TOP_K, ROWS), 0)
    pg = jnp.zeros((2 * TOP_K, ROWS), F32)
    for k in range(TOP_K):
        pg = jnp.where(s_iota == k, poss[k], pg)
        pg = jnp.where(s_iota == TOP_K + k, gates[k], pg)
    return pg, cnt


def _out_body(subtiles, tm_rows, *refs):
    if tm_rows:
        ys_ref, yl_ref, x_ref, g1_ref, sc_ref, sh_ref, perm_ref, *refs = refs
    else:
        ys_ref, yl_ref, x_ref, g1_ref, sc_ref, sh_ref, *refs = refs
    wo_ref, gf_ref, rwh_ref, rwl_ref, rb_ref, h_ref, xn_ref, pg_ref, cnt_ref = refs
    for c in range(subtiles):
        rows = slice(c * ROWS, (c + 1) * ROWS)
        ys = ys_ref[rows, :]
        yl = yl_ref[rows, :]
        if tm_rows:
            ys = jnp.dot(perm_ref[...], ys, preferred_element_type=F32).astype(BF16)
            yl = jnp.dot(perm_ref[...], yl, preferred_element_type=F32).astype(BF16)
            tc = x_ref.shape[1] // subtiles
            x = x_ref[:, c * tc:(c + 1) * tc, :].reshape(ROWS, D_MODEL)
        else:
            x = x_ref[rows, :]
        mixed = (jnp.dot(ys, wo_ref[0], preferred_element_type=F32)
                 + jnp.dot(yl, wo_ref[1], preferred_element_type=F32))
        h = x + g1_ref[0] * mixed
        if tm_rows:
            h_ref[:, c * tc:(c + 1) * tc, :] = h.reshape(x_ref.shape[0], tc, D_MODEL)
        else:
            h_ref[rows, :] = h
        xn = _rms(h, gf_ref[...]) * (1.0 + sc_ref[0]) + sh_ref[0]
        x_hi = xn.astype(BF16)
        xn_ref[rows, :] = x_hi
        pg, cnt = _route_tile(xn, x_hi, rwh_ref[...], rwl_ref[...], rb_ref[...])
        pg_ref[:, rows] = pg
        cnt_ref[c] = jnp.broadcast_to(cnt, (N_EXPERTS, LANES))


def _out_proj(ys5, ylru, x, modpat, pattern, from_tm, w_out_bf, g_ffn, rw_hi, rw_lo, router_b, subtiles):
    tm_rows = from_tm is not None
    t_rows = ys5.shape[0]
    step_rows = subtiles * ROWS
    half_spec = pl.BlockSpec((step_rows, D_S5), lambda i: (i, 0))
    if tm_rows:
        x_spec = pl.BlockSpec((x.shape[0], step_rows // x.shape[0], D_MODEL), lambda i: (0, i, 0))
    else:
        x_spec = pl.BlockSpec((step_rows, D_MODEL), lambda i: (i, 0))
    mod_spec = lambda k: pl.BlockSpec((1, ROWS, D_MODEL), lambda i: (pattern, 0, k))
    perm = [_full((ROWS, ROWS))] if tm_rows else []
    return pl.pallas_call(
        functools.partial(_out_body, subtiles, tm_rows),
        grid=(t_rows // step_rows,),
        in_specs=[half_spec, half_spec, x_spec, mod_spec(2), mod_spec(4), mod_spec(3), *perm,
                  _full((2, D_S5, D_MODEL)), _full((1, D_MODEL)),
                  _full((N_EXPERTS, D_MODEL)), _full((N_EXPERTS, D_MODEL)), _full((N_EXPERTS, 1))],
        out_specs=[x_spec, pl.BlockSpec((step_rows, D_MODEL), lambda i: (i, 0)),
                   pl.BlockSpec((2 * TOP_K, step_rows), lambda i: (0, i)),
                   pl.BlockSpec((subtiles, N_EXPERTS, LANES), lambda i: (i, 0, 0))],
        out_shape=[jax.ShapeDtypeStruct(x.shape, F32), jax.ShapeDtypeStruct((t_rows, D_MODEL), BF16),
                   jax.ShapeDtypeStruct((2 * TOP_K, t_rows), F32),
                   jax.ShapeDtypeStruct((t_rows // ROWS, N_EXPERTS, LANES), F32)],
        compiler_params=_params("arbitrary"),
        name="out_proj_router",
    )(ys5, ylru, x, modpat, modpat, modpat, *([from_tm] if tm_rows else []), w_out_bf, g_ffn, rw_hi, rw_lo,
      router_b)


def _start_segments(n_ref, hbm_ref, vmem_ref, hbm, vmem_buf, sem, step, to_hbm, unroll):
    def seg(e):
        n = n_ref[step * N_EXPERTS + e]
        h0 = hbm_ref[step * N_EXPERTS + e]
        v0 = vmem_ref[step * N_EXPERTS + e]

        def piece(off, size):
            h = hbm.at[pl.ds(pl.multiple_of(h0 + off, SEG_ALIGN), size)]
            v = vmem_buf.at[pl.ds(pl.multiple_of(v0 + off, SEG_ALIGN), size)]
            (pltpu.make_async_copy(v, h, sem) if to_hbm else pltpu.make_async_copy(h, v, sem)).start()

        if unroll:
            _row_pieces(n, ROWS, piece)
        else:
            def big(k, carry):
                piece(k * BIG_PIECE, BIG_PIECE)
                return carry

            lax.fori_loop(0, n // BIG_PIECE, big, 0)
            _row_pieces(n % BIG_PIECE, BIG_PIECE // 2, lambda off, size: piece(n // BIG_PIECE * BIG_PIECE + off, size))

    if unroll:
        for e in range(N_EXPERTS):
            seg(e)
    else:
        def body(e, carry):
            seg(e)
            return carry

        lax.fori_loop(0, N_EXPERTS, body, 0)


def _wait_rows(total, largest, hbm, sem):
    _row_pieces(total, largest, lambda off, size: pltpu.make_async_copy(
        hbm.at[pl.ds(0, size)], hbm.at[pl.ds(0, size)], sem).wait())


def _dispatch_body(tiles_p, n_ref, glob_ref, local_ref, tot_ref, nb_ref, pg_ref, xp_ref, xsm_ref, xs_hbm, stage,
                   sem):
    j = pl.program_id(0)
    last = pl.num_programs(0) - 1
    slot = j % 2

    def unused_blocks(act):
        def blk(b, carry):
            act(pltpu.make_async_copy(stage.at[slot, pl.ds(0, MOE_TM)],
                                      xs_hbm.at[pl.ds(pl.multiple_of(b * MOE_TM, MOE_TM), MOE_TM)], sem.at[slot]))
            return carry
        lax.fori_loop(nb_ref[0], xs_hbm.shape[0] // MOE_TM, blk, 0)

    def wait_step(step, s):
        _wait_rows(tot_ref[step], WAIT_MAX_PIECE, xs_hbm, sem.at[s])

    @pl.when(j >= 2)
    def _():
        wait_step(jnp.maximum(j - 2, 0), slot)

    @pl.when(j < last)
    def _():
        x = jnp.where(j < tiles_p, xp_ref[...], xsm_ref[...])
        pos = pg_ref[0:TOP_K, :]
        r = lax.broadcasted_iota(jnp.int32, (SORT_CHUNK, ROWS), 0).astype(F32).astype(BF16)
        for c in range(SORT_ROWS // SORT_CHUNK):
            rel = (pos - float(c * SORT_CHUNK)).astype(BF16)
            pick = (r == rel[0:1]) | (r == rel[1:2]) | (r == rel[2:3]) | (r == rel[3:4])
            stage[slot, c * SORT_CHUNK:(c + 1) * SORT_CHUNK, :] = jnp.dot(
                jnp.where(pick, jnp.ones((), BF16), jnp.zeros((), BF16)), x,
                preferred_element_type=F32).astype(BF16)

    @pl.when(j == last)
    def _():
        stage[slot, 0:MOE_TM, :] = jnp.zeros((MOE_TM, D_MODEL), BF16)

    _start_segments(n_ref, glob_ref, local_ref, xs_hbm, stage.at[slot], sem.at[slot], j, True, unroll=True)

    @pl.when(j == last)
    def _():
        unused_blocks(lambda c: c.start())

        @pl.when(j >= 1)
        def _():
            wait_step(jnp.maximum(j - 1, 0), 1 - slot)
        wait_step(j, slot)
        unused_blocks(lambda c: c.wait())


def _dispatch(seg_n, seg_glob, seg_local, seg_tot, nb_used, pg, xn_p, xn_s, n_rows):
    tiles_p = xn_p.shape[0] // ROWS
    tiles_s = xn_s.shape[0] // ROWS
    tiles = tiles_p + tiles_s
    grid_spec = pltpu.PrefetchScalarGridSpec(
        num_scalar_prefetch=5,
        grid=(tiles + 1,),
        in_specs=[pl.BlockSpec((2 * TOP_K, ROWS), lambda j, *_: (0, jnp.minimum(j, tiles - 1))),
                  pl.BlockSpec((ROWS, D_MODEL), lambda j, *_: (jnp.minimum(j, tiles_p - 1), 0)),
                  pl.BlockSpec((ROWS, D_MODEL), lambda j, *_: (jnp.clip(j - tiles_p, 0, tiles_s - 1), 0))],
        out_specs=pl.BlockSpec(memory_space=pl.ANY),
        scratch_shapes=[pltpu.VMEM((2, SORT_ROWS, D_MODEL), BF16), pltpu.SemaphoreType.DMA((2,))],
    )
    return pl.pallas_call(
        functools.partial(_dispatch_body, tiles_p),
        grid_spec=grid_spec,
        out_shape=jax.ShapeDtypeStruct((n_rows, D_MODEL), BF16),
        compiler_params=_params("arbitrary"),
        name="moe_dispatch",
    )(seg_n, seg_glob, seg_local, seg_tot, nb_used, pg, xn_p, xn_s)


def _moe_body(be_ref, nxt_ref, nv_ref, nb_ref, xs_ref, wgu_hbm, bgu_ref, wd_hbm, bd_ref, ys_ref,
              wgu_f32, wd_f32, wgu_bf, wd_bf, sem):
    i = pl.program_id(0)

    def weight_copies(e):
        return (pltpu.make_async_copy(wgu_hbm.at[e], wgu_f32, sem.at[0]),
                pltpu.make_async_copy(wd_hbm.at[e], wd_f32, sem.at[1]))

    @pl.when(i >= nb_ref[0])
    def _():
        ys_ref[...] = jnp.zeros_like(ys_ref)

    @pl.when(i < nb_ref[0])
    def _():
        e = be_ref[i]

        @pl.when(i == 0)
        def _():
            for c in weight_copies(e):
                c.start()

        @pl.when(jnp.logical_or(i == 0, e != be_ref[jnp.maximum(i - 1, 0)]))
        def _():
            for c in weight_copies(e):
                c.wait()
            wgu_bf[...] = wgu_f32[...].astype(BF16)
            wd_bf[...] = wd_f32[...].astype(BF16)

            @pl.when(nxt_ref[i] >= 0)
            def _():
                for c in weight_copies(nxt_ref[i]):
                    c.start()

        def ffn(x):
            hg = jnp.dot(x, wgu_bf[...], preferred_element_type=F32) + bgu_ref[0]
            gate = jnp.minimum(hg[:, :D_FF], SWIGLU_LIMIT)
            up = jnp.clip(hg[:, D_FF:], -SWIGLU_LIMIT, SWIGLU_LIMIT)
            act = (up + 1.0) * (gate * jax.nn.sigmoid(SWIGLU_ALPHA * gate))
            return (jnp.dot(act.astype(BF16), wd_bf[...], preferred_element_type=F32) + bd_ref[0]).astype(BF16)

        for parts in range(1, MOE_TM // MOE_PART + 1):
            lo = (parts - 1) * MOE_PART
            last_path = parts == MOE_TM // MOE_PART

            @pl.when(jnp.logical_and(nv_ref[i] > lo, jnp.logical_or(last_path, nv_ref[i] <= lo + MOE_PART)))
            def _(parts=parts):
                rows = parts * MOE_PART
                ys_ref[0:rows, :] = ffn(xs_ref[0:rows, :])
                if rows < MOE_TM:
                    ys_ref[rows:, :] = jnp.zeros((MOE_TM - rows, D_MODEL), BF16)


def _moe(xs, block_expert, block_next, block_rows, nb_used, w_gu, b_gu, w_down, b_down):
    n_blocks = xs.shape[0] // MOE_TM
    in_rows = pl.BlockSpec((MOE_TM, D_MODEL), lambda i, be, nx, nv, nb: (jnp.minimum(i, nb[0] - 1), 0))
    grid_spec = pltpu.PrefetchScalarGridSpec(
        num_scalar_prefetch=4,
        grid=(n_blocks,),
        in_specs=[in_rows,
                  pl.BlockSpec(memory_space=pl.ANY),
                  pl.BlockSpec((1, 1, 2 * D_FF), lambda i, be, nx, nv, nb: (be[i], 0, 0)),
                  pl.BlockSpec(memory_space=pl.ANY),
                  pl.BlockSpec((1, 1, D_MODEL), lambda i, be, nx, nv, nb: (be[i], 0, 0))],
        out_specs=pl.BlockSpec((MOE_TM, D_MODEL), lambda i, be, nx, nv, nb: (i, 0)),
        scratch_shapes=[pltpu.VMEM((D_MODEL, 2 * D_FF), F32), pltpu.VMEM((D_FF, D_MODEL), F32),
                        pltpu.VMEM((D_MODEL, 2 * D_FF), BF16), pltpu.VMEM((D_FF, D_MODEL), BF16),
                        pltpu.SemaphoreType.DMA((2,))],
    )
    return pl.pallas_call(
        _moe_body,
        grid_spec=grid_spec,
        out_shape=jax.ShapeDtypeStruct(xs.shape, BF16),
        compiler_params=_params("arbitrary"),
        name="moe_experts",
    )(block_expert, block_next, block_rows, nb_used, xs, w_gu, b_gu, w_down, b_down)


def _fin_body(tiles_p, n_ref, glob_ref, local_ref, tot_ref, hp_ref, hs_ref, pg_ref, g2_ref, gf_ref, ys_hbm,
              op_ref, os_ref, ybuf, sem):
    j = pl.program_id(0)
    tiles = pl.num_programs(0)
    slot = j % 2
    is_prompt = j < tiles_p

    def fetch(step, s):
        _start_segments(n_ref, glob_ref, local_ref, ys_hbm, ybuf.at[s], sem.at[s], step, False, unroll=False)

    @pl.when(j == 0)
    def _():
        ybuf[...] = jnp.zeros_like(ybuf)
        fetch(0, 0)

    _wait_rows(tot_ref[j], WAIT_MAX_PIECE, ys_hbm, sem.at[slot])

    @pl.when(j + 1 < tiles)
    def _():
        fetch(jnp.minimum(j + 1, tiles - 1), 1 - slot)

    pg = pg_ref[...]
    ff = jnp.zeros((ROWS, D_MODEL), F32)
    r = lax.broadcasted_iota(jnp.int32, (ROWS, SORT_CHUNK), 1).astype(F32).astype(BF16)
    gates = pg[:, TOP_K:].astype(BF16)
    for c in range(SORT_ROWS // SORT_CHUNK):
        rel = (pg[:, :TOP_K] - float(c * SORT_CHUNK)).astype(BF16)
        w = jnp.zeros((ROWS, SORT_CHUNK), BF16)
        for k in range(TOP_K):
            w = jnp.where(r == rel[:, k:k + 1], gates[:, k:k + 1], w)
        ff = ff + jnp.dot(w, ybuf[slot, c * SORT_CHUNK:(c + 1) * SORT_CHUNK, :], preferred_element_type=F32)
    h = jnp.where(is_prompt, hp_ref[...].reshape(ROWS, D_MODEL), hs_ref[...])
    y = _rms(h + g2_ref[0] * ff, gf_ref[...])

    @pl.when(is_prompt)
    def _():
        op_ref[...] = y.reshape(op_ref.shape)

    @pl.when(jnp.logical_not(is_prompt))
    def _():
        os_ref[...] = y


def _final(seg_n, seg_glob, seg_local, seg_tot, h_p, h_s, pg_t, modpat, g_final, ys, tiles_p, tiles_s):
    hp_spec, hs_spec = _tile_specs(h_p.shape[0], tiles_p, tiles_s)
    grid_spec = pltpu.PrefetchScalarGridSpec(
        num_scalar_prefetch=4,
        grid=(tiles_p + tiles_s,),
        in_specs=[hp_spec, hs_spec,
                  pl.BlockSpec((ROWS, 2 * TOP_K), lambda j, *_: (j, 0)),
                  _mod_spec(5, tiles_p), _full((1, D_MODEL)),
                  pl.BlockSpec(memory_space=pl.ANY)],
        out_specs=[hp_spec, hs_spec],
        scratch_shapes=[pltpu.VMEM((2, SORT_ROWS, D_MODEL), BF16), pltpu.SemaphoreType.DMA((2,))],
    )
    return pl.pallas_call(
        functools.partial(_fin_body, tiles_p),
        grid_spec=grid_spec,
        out_shape=[jax.ShapeDtypeStruct(h_p.shape, F32), jax.ShapeDtypeStruct(h_s.shape, F32)],
        compiler_params=_params("arbitrary"),
        name="combine_final",
    )(seg_n, seg_glob, seg_local, seg_tot, h_p, h_s, pg_t, modpat, g_final, ys)


def _block_diag(w):
    h, i, j = w.shape
    return jnp.einsum('hij,hk->hikj', w, jnp.eye(h, dtype=w.dtype)).reshape(h * i, h * j)


def _s5_cols(re, im):
    b = re.shape[0]
    stack = jnp.stack([re.reshape(b, S5_BLOCKS, -1), im.reshape(b, S5_BLOCKS, -1)], axis=2)
    return stack.reshape(b, S5_COLS)


def _s5_uncols(cols):
    b = cols.shape[0]
    c = cols.reshape(b, S5_BLOCKS, 2, S5_GROUPS // S5_BLOCKS, S5_STATE)
    return (c[:, :, 0].reshape(b, S5_GROUPS, S5_STATE), c[:, :, 1].reshape(b, S5_GROUPS, S5_STATE))


def _moe_rows_bound(tiles):
    worst = tiles * (TOP_K * ROWS + N_EXPERTS * (SEG_ALIGN - 1)) + N_EXPERTS * (MOE_TM - SEG_ALIGN)
    return (worst + MOE_TM - 1) // MOE_TM * MOE_TM


def _plan(cnt):
    cnt = cnt.astype(jnp.int32)
    tiles = cnt.shape[0]
    cp = (cnt + SEG_ALIGN - 1) // SEG_ALIGN * SEG_ALIGN
    local = jnp.cumsum(cp, axis=1) - cp
    group = jnp.sum(cp, axis=0)
    group_pad = (group + MOE_TM - 1) // MOE_TM * MOE_TM
    pend = jnp.cumsum(group_pad)
    pstart = pend - group_pad
    glob = pstart[None, :] + jnp.cumsum(cp, axis=0) - cp
    gap = group_pad - group
    seg_n = jnp.concatenate([cp, gap[None]], axis=0).reshape(-1)
    seg_local = jnp.concatenate([local, jnp.zeros((1, N_EXPERTS), jnp.int32)], axis=0).reshape(-1)
    seg_glob = jnp.concatenate([glob, (pstart + group)[None]], axis=0).reshape(-1)
    seg_tot = jnp.concatenate([jnp.sum(cp, axis=1), jnp.sum(gap)[None]]).astype(jnp.int32)
    n_blocks = _moe_rows_bound(tiles) // MOE_TM
    block_row0 = jnp.arange(n_blocks, dtype=jnp.int32) * MOE_TM
    block_expert = jnp.minimum(jnp.sum(block_row0[:, None] >= pend[None, :], axis=1), N_EXPERTS - 1).astype(jnp.int32)
    nb_used = (pend[-1] // MOE_TM).astype(jnp.int32).reshape(1)
    experts = jnp.arange(N_EXPERTS, dtype=jnp.int32)
    later_owner = jnp.where((experts[None, :] > experts[:, None]) & (group_pad[None, :] > 0), experts[None, :],
                            N_EXPERTS)
    next_owner = jnp.min(later_owner, axis=1)
    next_owner = jnp.where(next_owner == N_EXPERTS, -1, next_owner).astype(jnp.int32)
    owner = block_expert[:, None] == experts[None, :]
    block_next = jnp.sum(jnp.where(owner, next_owner[None, :], 0), axis=1).astype(jnp.int32)
    group_end = jnp.sum(jnp.where(owner, (pstart + group)[None, :], 0), axis=1)
    block_rows = jnp.clip(group_end - block_row0, 0, MOE_TM).astype(jnp.int32)
    return seg_n, seg_glob, seg_local, seg_tot, block_expert, block_next, block_rows, nb_used


def kernel(x_prompt, x_sample, state_s5_re, state_s5_im, state_lru_h, state_conv, c_prompt, c_sample, w_ada, b_ada, g_mix, w_in, s5_a_re, s5_a_im, s5_log_dt, s5_b_re, s5_b_im, s5_c_re, s5_c_im, s5_d, s5_w_glu, s5_b_glu, lru_conv_w, lru_conv_b, lru_w_a, lru_b_a, lru_w_x, lru_b_x, lru_lambda, g_out_s5, g_out_lru, w_out, g_ffn, router_w, router_b, moe_w_gu, moe_b_gu, moe_w_down, moe_b_down, g_final):
    assert w_ada.shape[0] == 1, "one layer"
    bp, lp, _ = x_prompt.shape
    bs, ls, _ = x_sample.shape
    assert ROWS % bp == 0 and ROWS % bs == 0 and (bp * lp) % ROWS == 0 and (bs * ls) % ROWS == 0
    tiles_p = bp * lp // ROWS
    tiles_s = bs * ls // ROWS
    row = lambda v: v.reshape(1, -1)

    ab_re, ab_im, bb_re, bb_im = _s5_prep(s5_a_re[0], s5_a_im[0], s5_log_dt[0].reshape(S5_GROUPS, 1),
                                          jnp.swapaxes(s5_b_re[0], 1, 2), jnp.swapaxes(s5_b_im[0], 1, 2))
    gpb = S5_GROUPS // S5_BLOCKS
    eye = jnp.eye(gpb, dtype=F32)

    def in_blocks(b):
        b = b.reshape(S5_BLOCKS, gpb, S5_GROUP_CH, S5_STATE)
        return jnp.einsum('bgjn,gh->bgjhn', b, eye).reshape(S5_BLOCKS, gpb * S5_GROUP_CH, gpb * S5_STATE)

    def out_blocks(c):
        c = c.reshape(S5_BLOCKS, gpb, S5_GROUP_CH, S5_STATE)
        return jnp.einsum('bgjn,gh->bgnhj', c, eye).reshape(S5_BLOCKS, gpb * S5_STATE, gpb * S5_GROUP_CH)

    ar8 = jnp.broadcast_to(ab_re.reshape(1, -1), (SUBLANES, S5_GROUPS * S5_STATE))
    ai8 = jnp.broadcast_to(ab_im.reshape(1, -1), (SUBLANES, S5_GROUPS * S5_STATE))
    bb = jnp.concatenate([in_blocks(bb_re), in_blocks(bb_im)], axis=-1).astype(BF16)
    cc = jnp.concatenate([out_blocks(s5_c_re[0]), -out_blocks(s5_c_im[0])], axis=1).astype(BF16)
    wa_bd = _block_diag(lru_w_a[0]).astype(BF16)
    wx_bd = _block_diag(lru_w_x[0]).astype(BF16)
    rw_t = router_w[0].T
    rw_hi = rw_t.astype(BF16)
    rw_lo = (rw_t - rw_hi.astype(F32)).astype(BF16)

    tc = ROWS // bp
    modpat = _adaln(jnp.concatenate([c_prompt, c_sample], axis=0), w_ada[0], row(b_ada[0]), bp)
    r = jnp.arange(ROWS)
    tm_of = (r % tc) * bp + r // tc
    to_tm = (r[:, None] == tm_of[None, :]).astype(BF16)
    from_tm = to_tm.T
    pair = 2 if tiles_p % 2 == 0 else 1

    def conv_tm(cv):
        return jnp.swapaxes(cv, 0, 1).reshape(-1, D_LRU)

    def conv_bm(cv, b):
        return jnp.swapaxes(cv.reshape(CONV_WIDTH - 1, b, D_LRU), 0, 1)

    x_s = jnp.swapaxes(x_sample, 0, 1).reshape(bs * ls, D_MODEL)
    in_args = (row(g_mix[0]), w_in[0].astype(BF16))
    lru_args = (lru_conv_w[0], row(lru_conv_b[0]), wa_bd, row(lru_b_a[0]), wx_bd, row(lru_b_x[0]),
                row(lru_lambda[0]))
    u_p, la_p, lb_p, lg_p, cvp = _in_proj(x_prompt, modpat, 0, to_tm, *in_args,
                                          jnp.zeros(((CONV_WIDTH - 1) * bp, D_LRU), F32), *lru_args, bp, pair)
    u_s, la_s, lb_s, lg_s, cvs = _in_proj(x_s, modpat, 1, None, *in_args, conv_tm(state_conv[0]), *lru_args, bs, 1)

    s5_args = (ar8, ai8, bb, cc, row(s5_d[0]), s5_w_glu[0].astype(BF16), row(s5_b_glu[0]), row(g_out_s5[0]))
    ys5_p, s5p, ylru_p, hp = _mixers(u_p, jnp.zeros((bp, S5_COLS), F32), *s5_args, la_p, lb_p, lg_p,
                                     jnp.zeros((bp, D_LRU), F32), row(g_out_lru[0]), bp, pair)
    ys5_s, s5s, ylru_s, hs = _mixers(u_s, _s5_cols(state_s5_re[0], state_s5_im[0]), *s5_args, la_s, lb_s, lg_s,
                                     state_lru_h[0], row(g_out_lru[0]), bs, 1)

    out_args = (w_out[0].astype(BF16).reshape(2, D_S5, D_MODEL), row(g_ffn[0]), rw_hi, rw_lo,
                router_b[0].reshape(N_EXPERTS, 1))
    h_p, xn_p, pg_p, cnt_p = _out_proj(ys5_p, ylru_p, x_prompt, modpat, 0, from_tm, *out_args, pair)
    h_s, xn_s, pg_s, cnt_s = _out_proj(ys5_s, ylru_s, x_s, modpat, 1, None, *out_args, 1)
    pg = jnp.concatenate([pg_p, pg_s], axis=1)
    cnt = jnp.concatenate([cnt_p, cnt_s], axis=0)

    seg_n, seg_glob, seg_local, seg_tot, block_expert, block_next, block_rows, nb_used = _plan(cnt[:, :, 0])
    xs = _dispatch(seg_n, seg_glob, seg_local, seg_tot, nb_used, pg, xn_p, xn_s,
                   _moe_rows_bound(tiles_p + tiles_s))
    ys = _moe(xs, block_expert, block_next, block_rows, nb_used, moe_w_gu[0], moe_b_gu[0].reshape(N_EXPERTS, 1, 2 * D_FF),
              moe_w_down[0], moe_b_down[0].reshape(N_EXPERTS, 1, D_MODEL))
    y_prompt, y_s = _final(seg_n, seg_glob, seg_local, seg_tot, h_p, h_s, pg.T, modpat, row(g_final), ys,
                           tiles_p, tiles_s)
    y_sample = jnp.swapaxes(y_s.reshape(ls, bs, D_MODEL), 0, 1)
    s5p_re, s5p_im = _s5_uncols(s5p)
    s5s_re, s5s_im = _s5_uncols(s5s)
    return (y_prompt, y_sample,
            s5p_re[None], s5p_im[None], hp[None], conv_bm(cvp, bp)[None],
            s5s_re[None], s5s_im[None], hs[None], conv_bm(cvs, bs)[None])
```

```python
import functools

import jax
import jax.numpy as jnp
from jax import lax
from jax.experimental import pallas as pl
from jax.experimental.pallas import tpu as pltpu

D_MODEL = 1024
D_S5 = 512
D_LRU = 512
S5_GROUPS = 32
S5_GROUP_CH = 16
S5_STATE = 64
S5_COLS = 2 * S5_GROUPS * S5_STATE
S5_BLOCKS = 4
LRU_HEADS = 8
LRU_HEAD_DIM = 64
LRU_C = 8.0
CONV_WIDTH = 4
N_EXPERTS = 32
TOP_K = 4
D_FF = 1024
SWIGLU_LIMIT = 7.0
SWIGLU_ALPHA = 1.702
N_MOD = 6
EPS = 1e-6

ROWS = 512
MOE_TM = 512
MOE_PART = 128
BIG_PIECE = 128
WAIT_MAX_PIECE = 8192
SEG_ALIGN = 16
SORT_ROWS = 2560
SORT_CHUNK = 256
SUBLANES = 8
LANES = 128
VMEM_LIMIT = 48 * 1024 * 1024

BF16 = jnp.bfloat16
F32 = jnp.float32


def _params(*sem):
    return pltpu.CompilerParams(dimension_semantics=sem, vmem_limit_bytes=VMEM_LIMIT)


def _full(shape):
    return pl.BlockSpec(shape, lambda *_: (0,) * len(shape))


def _rms(x, g):
    return x * lax.rsqrt(jnp.mean(x * x, axis=-1, keepdims=True) + EPS) * g


def _gelu(x):
    return 0.5 * x * (1.0 + lax.erf(x * (2.0 ** -0.5)))


def _expm1(x):
    u = jnp.exp(x)
    d = u - 1.0
    return jnp.where(d == 0.0, x, jnp.where(d == -1.0, -1.0, d * x / jnp.log(u)))


def _row_pieces(n, largest, fn):
    off = 0
    bit = largest
    while bit >= SEG_ALIGN:
        @pl.when((n & bit) != 0)
        def _(off=off, bit=bit):
            fn(off, bit)
        off = off + (n & bit)
        bit //= 2


def _mod_body(bp, c_ref, w_ref, b_ref, o_ref):
    c = c_ref[...]
    s = (c * jax.nn.sigmoid(c)).astype(BF16)
    mod = jnp.dot(s, w_ref[...].astype(BF16), preferred_element_type=F32) + b_ref[...]
    bs = mod.shape[0] - bp
    o_ref[0] = jnp.broadcast_to(mod[:bp][:, None, :], (bp, ROWS // bp, D_MODEL)).reshape(ROWS, D_MODEL)
    o_ref[1] = jnp.broadcast_to(mod[bp:][None], (ROWS // bs, bs, D_MODEL)).reshape(ROWS, D_MODEL)


def _adaln(c, w_ada, b_ada, bp):
    m = c.shape[0]
    return pl.pallas_call(
        functools.partial(_mod_body, bp),
        grid=(N_MOD,),
        in_specs=[pl.BlockSpec((m, D_MODEL), lambda j: (0, 0)),
                  pl.BlockSpec((D_MODEL, D_MODEL), lambda j: (0, j)),
                  pl.BlockSpec((1, D_MODEL), lambda j: (0, j))],
        out_specs=pl.BlockSpec((2, ROWS, D_MODEL), lambda j: (0, 0, j)),
        out_shape=jax.ShapeDtypeStruct((2, ROWS, N_MOD * D_MODEL), F32),
        compiler_params=_params("arbitrary"),
        name="adaln",
    )(c, w_ada, b_ada)


def _mod_spec(k, tiles_p):
    return pl.BlockSpec((1, ROWS, D_MODEL), lambda i, *_: (jnp.where(i < tiles_p, 0, 1), 0, k))


def _tile_specs(bp, tiles_p, tiles_s):
    tc = ROWS // bp
    p_spec = pl.BlockSpec((bp, tc, D_MODEL), lambda i, *_: (0, jnp.minimum(i, tiles_p - 1), 0))
    s_spec = pl.BlockSpec((ROWS, D_MODEL), lambda i, *_: (jnp.clip(i - tiles_p, 0, tiles_s - 1), 0))
    return p_spec, s_spec


def _in_body(subtiles, bt, to_tm, *refs):
    if to_tm:
        x_ref, sc_ref, sh_ref, perm_ref, *refs = refs
    else:
        x_ref, sc_ref, sh_ref, *refs = refs
    (g_ref, w_ref, cv0_ref, cw_ref, cb_ref, wa_ref, ba_ref, wx_ref, bx_ref, lam_ref,
     u_ref, a_ref, b_ref, gg_ref, cvout_ref, xp_ref) = refs
    halo = (CONV_WIDTH - 1) * bt

    @pl.when(pl.program_id(0) == 0)
    def _():
        xp_ref[...] = cv0_ref[...]

    lam = lam_ref[...]
    softplus_neg_lam = jnp.maximum(-lam, 0.0) + jnp.log1p(jnp.exp(-jnp.abs(lam)))
    earlier = xp_ref[...]
    for c in range(subtiles):
        rows = slice(c * ROWS, (c + 1) * ROWS)
        if to_tm:
            tc = x_ref.shape[1] // subtiles
            x = x_ref[:, c * tc:(c + 1) * tc, :].reshape(ROWS, D_MODEL)
        else:
            x = x_ref[rows, :]
        xn = (_rms(x, g_ref[...]) * (1.0 + sc_ref[0]) + sh_ref[0]).astype(BF16)
        if to_tm:
            xn = jnp.dot(perm_ref[...], xn, preferred_element_type=F32).astype(BF16)
        p = jnp.dot(xn, w_ref[...], preferred_element_type=F32)
        u_ref[rows, :] = p[:, :D_S5]
        gg_ref[rows, :] = _gelu(p[:, D_S5 + D_LRU:])

        xr = p[:, D_S5:D_S5 + D_LRU]
        xp = jnp.concatenate([earlier, xr], axis=0)
        earlier = xr[ROWS - halo:, :]
        xc = cb_ref[...] + sum(xp[k * bt:k * bt + ROWS, :] * cw_ref[k:k + 1, :] for k in range(CONV_WIDTH))
        xcb = xc.astype(BF16)
        r = jax.nn.sigmoid(jnp.dot(xcb, wa_ref[...], preferred_element_type=F32) + ba_ref[...])
        i = jax.nn.sigmoid(jnp.dot(xcb, wx_ref[...], preferred_element_type=F32) + bx_ref[...])
        log_a = -LRU_C * r * softplus_neg_lam
        a_ref[rows, :] = jnp.exp(log_a)
        b_ref[rows, :] = jnp.sqrt(-_expm1(2.0 * log_a)) * (i * xc)

    xp_ref[...] = earlier
    cvout_ref[...] = earlier


def _in_proj(x, modpat, pattern, to_tm, g_mix, w_in_bf, conv0_tm, conv_w, conv_b, wa_bd, b_a, wx_bd, b_x, lam, bt,
             subtiles):
    tm = to_tm is not None
    t_rows = x.shape[0] * x.shape[1] if tm else x.shape[0]
    step_rows = subtiles * ROWS
    halo = (CONV_WIDTH - 1) * bt
    if tm:
        x_spec = pl.BlockSpec((x.shape[0], step_rows // x.shape[0], D_MODEL), lambda i: (0, i, 0))
    else:
        x_spec = pl.BlockSpec((step_rows, D_MODEL), lambda i: (i, 0))
    mod_spec = lambda k: pl.BlockSpec((1, ROWS, D_MODEL), lambda i: (pattern, 0, k))
    row_spec = pl.BlockSpec((step_rows, D_S5), lambda i: (i, 0))
    vec = _full((1, D_LRU))
    return pl.pallas_call(
        functools.partial(_in_body, subtiles, bt, tm),
        grid=(t_rows // step_rows,),
        in_specs=[x_spec, mod_spec(1), mod_spec(0), *([_full((ROWS, ROWS))] if tm else []), _full((1, D_MODEL)),
                  _full((D_MODEL, D_S5 + 2 * D_LRU)), _full((halo, D_LRU)), _full((CONV_WIDTH, D_LRU)), vec,
                  _full((D_LRU, D_LRU)), vec, _full((D_LRU, D_LRU)), vec, vec],
        out_specs=[row_spec, row_spec, row_spec, row_spec, _full((halo, D_LRU))],
        out_shape=[jax.ShapeDtypeStruct((t_rows, D_S5), F32)] * 4 + [jax.ShapeDtypeStruct((halo, D_LRU), F32)],
        scratch_shapes=[pltpu.VMEM((halo, D_LRU), F32)],
        compiler_params=_params("arbitrary"),
        name="in_proj",
    )(x, modpat, modpat, *([to_tm] if tm else []), g_mix, w_in_bf, conv0_tm, conv_w, conv_b, wa_bd, b_a, wx_bd, b_x,
      lam)


def _s5_prep_body(are_ref, aim_ref, ldt_ref, bre_ref, bim_ref, abre_ref, abim_ref, bbre_ref, bbim_ref):
    a_re = are_ref[...]
    a_im = aim_ref[...]
    dt = jnp.exp(ldt_ref[...])
    mag = jnp.exp(dt * a_re)
    ang = dt * a_im
    ab_re = mag * jnp.cos(ang)
    ab_im = mag * jnp.sin(ang)
    den = a_re * a_re + a_im * a_im
    q_re = ((ab_re - 1.0) * a_re + ab_im * a_im) / den
    q_im = (ab_im * a_re - (ab_re - 1.0) * a_im) / den
    abre_ref[...] = ab_re
    abim_ref[...] = ab_im
    b_re = bre_ref[...]
    b_im = bim_ref[...]
    bbre_ref[...] = q_re[:, None, :] * b_re - q_im[:, None, :] * b_im
    bbim_ref[...] = q_re[:, None, :] * b_im + q_im[:, None, :] * b_re


def _s5_prep(a_re, a_im, log_dt, b_re_t, b_im_t):
    gn = jax.ShapeDtypeStruct((S5_GROUPS, S5_STATE), F32)
    gjn = jax.ShapeDtypeStruct((S5_GROUPS, S5_GROUP_CH, S5_STATE), F32)
    return pl.pallas_call(_s5_prep_body, out_shape=[gn, gn, gjn, gjn], name="s5_prep")(
        a_re, a_im, log_dt, b_re_t, b_im_t)


def _s5_body(tc, bt, chunks, u_ref, s0_ref, ar_ref, ai_ref, bb_ref, cc_ref, d_ref, wg_ref, bg_ref, go_ref,
             la_ref, lb_ref, lg_ref, lh0_ref, lgo_ref, y_ref, sout_ref, yl_ref, lhout_ref,
             bu_ref, st_ref, lh_ref, lst_ref):
    half = S5_COLS // S5_BLOCKS // 2
    re = [slice(half * j, half * (j + 1)) for j in range(S5_BLOCKS)]
    im = [slice(S5_COLS // 2 + half * j, S5_COLS // 2 + half * (j + 1)) for j in range(S5_BLOCKS)]

    @pl.when(pl.program_id(0) == 0)
    def _():
        st_ref[...] = s0_ref[...]
        lst_ref[...] = lh0_ref[...]

    for c in range(chunks):
        ub = u_ref[c * ROWS:(c + 1) * ROWS, :].astype(BF16)
        for j in range(S5_BLOCKS):
            bu = jnp.dot(ub[:, LANES * j:LANES * (j + 1)], bb_ref[j], preferred_element_type=F32)
            bu_ref[c, :, re[j]] = bu[:, :half]
            bu_ref[c, :, im[j]] = bu[:, half:]

    for sb in range(bt // SUBLANES):
        base = sb * SUBLANES
        hr = [st_ref[base:base + SUBLANES, re[j]] for j in range(S5_BLOCKS)]
        hi = [st_ref[base:base + SUBLANES, im[j]] for j in range(S5_BLOCKS)]
        hl = lst_ref[base:base + SUBLANES, :]
        for c in range(chunks):
            for t in range(tc):
                r0 = t * bt + base
                row = c * ROWS + r0
                hl = la_ref[row:row + SUBLANES, :] * hl + lb_ref[row:row + SUBLANES, :]
                lh_ref[row:row + SUBLANES, :] = hl
                for j in range(S5_BLOCKS):
                    ar = ar_ref[:, re[j]]
                    ai = ai_ref[:, re[j]]
                    h_re = ar * hr[j] - ai * hi[j] + bu_ref[c, r0:r0 + SUBLANES, re[j]]
                    h_im = ar * hi[j] + ai * hr[j] + bu_ref[c, r0:r0 + SUBLANES, im[j]]
                    bu_ref[c, r0:r0 + SUBLANES, re[j]] = h_re
                    bu_ref[c, r0:r0 + SUBLANES, im[j]] = h_im
                    hr[j], hi[j] = h_re, h_im
        for j in range(S5_BLOCKS):
            st_ref[base:base + SUBLANES, re[j]] = hr[j]
            st_ref[base:base + SUBLANES, im[j]] = hi[j]
        lst_ref[base:base + SUBLANES, :] = hl
    sout_ref[...] = st_ref[...]
    lhout_ref[...] = lst_ref[...]
    yl_ref[...] = _rms(lh_ref[...] * lg_ref[...], lgo_ref[...]).astype(BF16)

    for c in range(chunks):
        y = jnp.concatenate(
            [jnp.dot(jnp.concatenate([bu_ref[c, :, re[j]], bu_ref[c, :, im[j]]], axis=-1).astype(BF16), cc_ref[j],
                     preferred_element_type=F32) for j in range(S5_BLOCKS)], axis=-1)
        y = y + d_ref[...] * u_ref[c * ROWS:(c + 1) * ROWS, :]
        g = _gelu(y)
        z = jnp.dot(g.astype(BF16), wg_ref[...], preferred_element_type=F32) + bg_ref[...]
        out = g * jax.nn.sigmoid(z)
        y_ref[c * ROWS:(c + 1) * ROWS, :] = _rms(out, go_ref[...]).astype(BF16)


def _mixers(u, s0, ar8, ai8, bb, cc, d, w_glu_bf, b_glu, g_out_s5, lru_a, lru_b, lru_gg, lru_h0, g_out_lru, bt,
            chunks):
    tc = ROWS // bt
    t_rows = u.shape[0]
    step_rows = chunks * ROWS
    assert t_rows % step_rows == 0
    rows_f32 = pl.BlockSpec((step_rows, D_S5), lambda i: (i, 0))
    vec = _full((1, D_S5))
    return pl.pallas_call(
        functools.partial(_s5_body, tc, bt, chunks),
        grid=(t_rows // step_rows,),
        in_specs=[rows_f32, _full((bt, S5_COLS)), _full(ar8.shape), _full(ai8.shape), _full(bb.shape),
                  _full(cc.shape), vec, _full((D_S5, D_S5)), vec, vec,
                  rows_f32, rows_f32, rows_f32, _full((bt, D_LRU)), vec],
        out_specs=[rows_f32, _full((bt, S5_COLS)), rows_f32, _full((bt, D_LRU))],
        out_shape=[jax.ShapeDtypeStruct((t_rows, D_S5), BF16), jax.ShapeDtypeStruct((bt, S5_COLS), F32),
                   jax.ShapeDtypeStruct((t_rows, D_LRU), BF16), jax.ShapeDtypeStruct((bt, D_LRU), F32)],
        scratch_shapes=[pltpu.VMEM((chunks, ROWS, S5_COLS), F32), pltpu.VMEM((bt, S5_COLS), F32),
                        pltpu.VMEM((step_rows, D_LRU), F32), pltpu.VMEM((bt, D_LRU), F32)],
        compiler_params=_params("arbitrary"),
        name="s5_lru_mixers",
    )(u, s0, ar8, ai8, bb, cc, d, w_glu_bf, b_glu, g_out_s5, lru_a, lru_b, lru_gg, lru_h0, g_out_lru)


def _route_tile(xn, x_hi, rw_hi, rw_lo, rb):
    x_lo = (xn - x_hi.astype(F32)).astype(BF16)
    nt = (((1,), (1,)), ((), ()))
    logits = (lax.dot_general(rw_hi, x_hi, nt, preferred_element_type=F32)
              + lax.dot_general(rw_hi, x_lo, nt, preferred_element_type=F32)
              + lax.dot_general(rw_lo, x_hi, nt, preferred_element_type=F32)) + rb

    e_iota = lax.broadcasted_iota(jnp.int32, (N_EXPERTS, ROWS), 0).astype(F32)
    work = logits
    sels, vals = [], []
    for _ in range(TOP_K):
        m = jnp.max(work, axis=0, keepdims=True)
        idx = jnp.min(jnp.where(work == m, e_iota, float(N_EXPERTS)), axis=0, keepdims=True)
        sel = e_iota == idx
        work = jnp.where(sel, -jnp.inf, work)
        sels.append(sel)
        vals.append(m)
    exps = [jnp.exp(v - vals[0]) for v in vals]
    denom = exps[0] + exps[1] + exps[2] + exps[3]
    gates = [e / denom for e in exps]

    onehot = sels[0] | sels[1] | sels[2] | sels[3]
    rr = lax.broadcasted_iota(jnp.int32, (ROWS, ROWS), 0)
    cc = lax.broadcasted_iota(jnp.int32, (ROWS, ROWS), 1)
    before = (rr < cc).astype(BF16)
    prefix = jnp.dot(onehot.astype(BF16), before, preferred_element_type=F32)
    cnt = jnp.sum(onehot.astype(F32), axis=1, keepdims=True)
    cnt_pad = jnp.floor((cnt + (SEG_ALIGN - 1)) * (1.0 / SEG_ALIGN)) * SEG_ALIGN
    er = lax.broadcasted_iota(jnp.int32, (N_EXPERTS, N_EXPERTS), 0)
    ec = lax.broadcasted_iota(jnp.int32, (N_EXPERTS, N_EXPERTS), 1)
    seg_start = jnp.dot((ec < er).astype(BF16), jnp.broadcast_to(cnt_pad, (N_EXPERTS, LANES)).astype(BF16),
                        preferred_element_type=F32)[:, 0:1]
    where_to = prefix + seg_start
    poss = [jnp.sum(jnp.where(s, where_to, 0.0), axis=0, keepdims=True) for s in sels]

    s_iota = lax.broadcasted_iota(jnp.int32, (2 * TOP_K, ROWS), 0)
    pg = jnp.zeros((2 * TOP_K, ROWS), F32)
    for k in range(TOP_K):
        pg = jnp.where(s_iota == k, poss[k], pg)
        pg = jnp.where(s_iota == TOP_K + k, gates[k], pg)
    return pg, cnt


def _out_body(subtiles, tm_rows, *refs):
    if tm_rows:
        ys_ref, yl_ref, x_ref, g1_ref, sc_ref, sh_ref, perm_ref, *refs = refs
    else:
        ys_ref, yl_ref, x_ref, g1_ref, sc_ref, sh_ref, *refs = refs
    wo_ref, gf_ref, rwh_ref, rwl_ref, rb_ref, h_ref, xn_ref, pg_ref, cnt_ref = refs
    for c in range(subtiles):
        rows = slice(c * ROWS, (c + 1) * ROWS)
        ys = ys_ref[rows, :]
        yl = yl_ref[rows, :]
        if tm_rows:
            ys = jnp.dot(perm_ref[...], ys, preferred_element_type=F32).astype(BF16)
            yl = jnp.dot(perm_ref[...], yl, preferred_element_type=F32).astype(BF16)
            tc = x_ref.shape[1] // subtiles
            x = x_ref[:, c * tc:(c + 1) * tc, :].reshape(ROWS, D_MODEL)
        else:
            x = x_ref[rows, :]
        mixed = (jnp.dot(ys, wo_ref[0], preferred_element_type=F32)
                 + jnp.dot(yl, wo_ref[1], preferred_element_type=F32))
        h = x + g1_ref[0] * mixed
        if tm_rows:
            h_ref[:, c * tc:(c + 1) * tc, :] = h.reshape(x_ref.shape[0], tc, D_MODEL)
        else:
            h_ref[rows, :] = h
        xn = _rms(h, gf_ref[...]) * (1.0 + sc_ref[0]) + sh_ref[0]
        x_hi = xn.astype(BF16)
        xn_ref[rows, :] = x_hi
        pg, cnt = _route_tile(xn, x_hi, rwh_ref[...], rwl_ref[...], rb_ref[...])
        pg_ref[:, rows] = pg
        cnt_ref[c] = jnp.broadcast_to(cnt, (N_EXPERTS, LANES))


def _out_proj(ys5, ylru, x, modpat, pattern, from_tm, w_out_bf, g_ffn, rw_hi, rw_lo, router_b, subtiles):
    tm_rows = from_tm is not None
    t_rows = ys5.shape[0]
    step_rows = subtiles * ROWS
    half_spec = pl.BlockSpec((step_rows, D_S5), lambda i: (i, 0))
    if tm_rows:
        x_spec = pl.BlockSpec((x.shape[0], step_rows // x.shape[0], D_MODEL), lambda i: (0, i, 0))
    else:
        x_spec = pl.BlockSpec((step_rows, D_MODEL), lambda i: (i, 0))
    mod_spec = lambda k: pl.BlockSpec((1, ROWS, D_MODEL), lambda i: (pattern, 0, k))
    perm = [_full((ROWS, ROWS))] if tm_rows else []
    return pl.pallas_call(
        functools.partial(_out_body, subtiles, tm_rows),
        grid=(t_rows // step_rows,),
        in_specs=[half_spec, half_spec, x_spec, mod_spec(2), mod_spec(4), mod_spec(3), *perm,
                  _full((2, D_S5, D_MODEL)), _full((1, D_MODEL)),
                  _full((N_EXPERTS, D_MODEL)), _full((N_EXPERTS, D_MODEL)), _full((N_EXPERTS, 1))],
        out_specs=[x_spec, pl.BlockSpec((step_rows, D_MODEL), lambda i: (i, 0)),
                   pl.BlockSpec((2 * TOP_K, step_rows), lambda i: (0, i)),
                   pl.BlockSpec((subtiles, N_EXPERTS, LANES), lambda i: (i, 0, 0))],
        out_shape=[jax.ShapeDtypeStruct(x.shape, F32), jax.ShapeDtypeStruct((t_rows, D_MODEL), BF16),
                   jax.ShapeDtypeStruct((2 * TOP_K, t_rows), F32),
                   jax.ShapeDtypeStruct((t_rows // ROWS, N_EXPERTS, LANES), F32)],
        compiler_params=_params("arbitrary"),
        name="out_proj_router",
    )(ys5, ylru, x, modpat, modpat, modpat, *([from_tm] if tm_rows else []), w_out_bf, g_ffn, rw_hi, rw_lo,
      router_b)


def _start_segments(n_ref, hbm_ref, vmem_ref, hbm, vmem_buf, sem, step, to_hbm, unroll):
    def seg(e):
        n = n_ref[step * N_EXPERTS + e]
        h0 = hbm_ref[step * N_EXPERTS + e]
        v0 = vmem_ref[step * N_EXPERTS + e]

        def piece(off, size):
            h = hbm.at[pl.ds(pl.multiple_of(h0 + off, SEG_ALIGN), size)]
            v = vmem_buf.at[pl.ds(pl.multiple_of(v0 + off, SEG_ALIGN), size)]
            (pltpu.make_async_copy(v, h, sem) if to_hbm else pltpu.make_async_copy(h, v, sem)).start()

        if unroll:
            _row_pieces(n, ROWS, piece)
        else:
            def big(k, carry):
                piece(k * BIG_PIECE, BIG_PIECE)
                return carry

            lax.fori_loop(0, n // BIG_PIECE, big, 0)
            _row_pieces(n % BIG_PIECE, BIG_PIECE // 2, lambda off, size: piece(n // BIG_PIECE * BIG_PIECE + off, size))

    if unroll:
        for e in range(N_EXPERTS):
            seg(e)
    else:
        def body(e, carry):
            seg(e)
            return carry

        lax.fori_loop(0, N_EXPERTS, body, 0)


def _wait_rows(total, largest, hbm, sem):
    _row_pieces(total, largest, lambda off, size: pltpu.make_async_copy(
        hbm.at[pl.ds(0, size)], hbm.at[pl.ds(0, size)], sem).wait())


def _dispatch_body(tiles_p, n_ref, glob_ref, local_ref, tot_ref, nb_ref, pg_ref, xp_ref, xsm_ref, xs_hbm, stage,
                   sem):
    j = pl.program_id(0)
    last = pl.num_programs(0) - 1
    slot = j % 2

    def unused_blocks(act):
        def blk(b, carry):
            act(pltpu.make_async_copy(stage.at[slot, pl.ds(0, MOE_TM)],
                                      xs_hbm.at[pl.ds(pl.multiple_of(b * MOE_TM, MOE_TM), MOE_TM)], sem.at[slot]))
            return carry
        lax.fori_loop(nb_ref[0], xs_hbm.shape[0] // MOE_TM, blk, 0)

    def wait_step(step, s):
        _wait_rows(tot_ref[step], WAIT_MAX_PIECE, xs_hbm, sem.at[s])

    @pl.when(j >= 2)
    def _():
        wait_step(jnp.maximum(j - 2, 0), slot)

    @pl.when(j < last)
    def _():
        x = jnp.where(j < tiles_p, xp_ref[...], xsm_ref[...])
        pos = pg_ref[0:TOP_K, :]
        r = lax.broadcasted_iota(jnp.int32, (SORT_CHUNK, ROWS), 0).astype(F32).astype(BF16)
        for c in range(SORT_ROWS // SORT_CHUNK):
            rel = (pos - float(c * SORT_CHUNK)).astype(BF16)
            pick = (r == rel[0:1]) | (r == rel[1:2]) | (r == rel[2:3]) | (r == rel[3:4])
            stage[slot, c * SORT_CHUNK:(c + 1) * SORT_CHUNK, :] = jnp.dot(
                jnp.where(pick, jnp.ones((), BF16), jnp.zeros((), BF16)), x,
                preferred_element_type=F32).astype(BF16)

    @pl.when(j == last)
    def _():
        stage[slot, 0:MOE_TM, :] = jnp.zeros((MOE_TM, D_MODEL), BF16)

    _start_segments(n_ref, glob_ref, local_ref, xs_hbm, stage.at[slot], sem.at[slot], j, True, unroll=True)

    @pl.when(j == last)
    def _():
        unused_blocks(lambda c: c.start())

        @pl.when(j >= 1)
        def _():
            wait_step(jnp.maximum(j - 1, 0), 1 - slot)
        wait_step(j, slot)
        unused_blocks(lambda c: c.wait())


def _dispatch(seg_n, seg_glob, seg_local, seg_tot, nb_used, pg, xn_p, xn_s, n_rows):
    tiles_p = xn_p.shape[0] // ROWS
    tiles_s = xn_s.shape[0] // ROWS
    tiles = tiles_p + tiles_s
    grid_spec = pltpu.PrefetchScalarGridSpec(
        num_scalar_prefetch=5,
        grid=(tiles + 1,),
        in_specs=[pl.BlockSpec((2 * TOP_K, ROWS), lambda j, *_: (0, jnp.minimum(j, tiles - 1))),
                  pl.BlockSpec((ROWS, D_MODEL), lambda j, *_: (jnp.minimum(j, tiles_p - 1), 0)),
                  pl.BlockSpec((ROWS, D_MODEL), lambda j, *_: (jnp.clip(j - tiles_p, 0, tiles_s - 1), 0))],
        out_specs=pl.BlockSpec(memory_space=pl.ANY),
        scratch_shapes=[pltpu.VMEM((2, SORT_ROWS, D_MODEL), BF16), pltpu.SemaphoreType.DMA((2,))],
    )
    return pl.pallas_call(
        functools.partial(_dispatch_body, tiles_p),
        grid_spec=grid_spec,
        out_shape=jax.ShapeDtypeStruct((n_rows, D_MODEL), BF16),
        compiler_params=_params("arbitrary"),
        name="moe_dispatch",
    )(seg_n, seg_glob, seg_local, seg_tot, nb_used, pg, xn_p, xn_s)


def _moe_body(be_ref, nxt_ref, nv_ref, nb_ref, xs_ref, wgu_hbm, bgu_ref, wd_hbm, bd_ref, ys_ref,
              wgu_f32, wd_f32, wgu_bf, wd_bf, sem):
    i = pl.program_id(0)

    def weight_copies(e):
        return (pltpu.make_async_copy(wgu_hbm.at[e], wgu_f32, sem.at[0]),
                pltpu.make_async_copy(wd_hbm.at[e], wd_f32, sem.at[1]))

    @pl.when(i >= nb_ref[0])
    def _():
        ys_ref[...] = jnp.zeros_like(ys_ref)

    @pl.when(i < nb_ref[0])
    def _():
        e = be_ref[i]

        @pl.when(i == 0)
        def _():
            for c in weight_copies(e):
                c.start()

        @pl.when(jnp.logical_or(i == 0, e != be_ref[jnp.maximum(i - 1, 0)]))
        def _():
            for c in weight_copies(e):
                c.wait()
            wgu_bf[...] = wgu_f32[...].astype(BF16)
            wd_bf[...] = wd_f32[...].astype(BF16)

            @pl.when(nxt_ref[i] >= 0)
            def _():
                for c in weight_copies(nxt_ref[i]):
                    c.start()

        def ffn(x):
            hg = jnp.dot(x, wgu_bf[...], preferred_element_type=F32) + bgu_ref[0]
            gate = jnp.minimum(hg[:, :D_FF], SWIGLU_LIMIT)
            up = jnp.clip(hg[:, D_FF:], -SWIGLU_LIMIT, SWIGLU_LIMIT)
            act = (up + 1.0) * (gate * jax.nn.sigmoid(SWIGLU_ALPHA * gate))
            return (jnp.dot(act.astype(BF16), wd_bf[...], preferred_element_type=F32) + bd_ref[0]).astype(BF16)

        for parts in range(1, MOE_TM // MOE_PART + 1):
            lo = (parts - 1) * MOE_PART
            last_path = parts == MOE_TM // MOE_PART

            @pl.when(jnp.logical_and(nv_ref[i] > lo, jnp.logical_or(last_path, nv_ref[i] <= lo + MOE_PART)))
            def _(parts=parts):
                rows = parts * MOE_PART
                ys_ref[0:rows, :] = ffn(xs_ref[0:rows, :])
                if rows < MOE_TM:
                    ys_ref[rows:, :] = jnp.zeros((MOE_TM - rows, D_MODEL), BF16)


def _moe(xs, block_expert, block_next, block_rows, nb_used, w_gu, b_gu, w_down, b_down):
    n_blocks = xs.shape[0] // MOE_TM
    in_rows = pl.BlockSpec((MOE_TM, D_MODEL), lambda i, be, nx, nv, nb: (jnp.minimum(i, nb[0] - 1), 0))
    grid_spec = pltpu.PrefetchScalarGridSpec(
        num_scalar_prefetch=4,
        grid=(n_blocks,),
        in_specs=[in_rows,
                  pl.BlockSpec(memory_space=pl.ANY),
                  pl.BlockSpec((1, 1, 2 * D_FF), lambda i, be, nx, nv, nb: (be[i], 0, 0)),
                  pl.BlockSpec(memory_space=pl.ANY),
                  pl.BlockSpec((1, 1, D_MODEL), lambda i, be, nx, nv, nb: (be[i], 0, 0))],
        out_specs=pl.BlockSpec((MOE_TM, D_MODEL), lambda i, be, nx, nv, nb: (i, 0)),
        scratch_shapes=[pltpu.VMEM((D_MODEL, 2 * D_FF), F32), pltpu.VMEM((D_FF, D_MODEL), F32),
                        pltpu.VMEM((D_MODEL, 2 * D_FF), BF16), pltpu.VMEM((D_FF, D_MODEL), BF16),
                        pltpu.SemaphoreType.DMA((2,))],
    )
    return pl.pallas_call(
        _moe_body,
        grid_spec=grid_spec,
        out_shape=jax.ShapeDtypeStruct(xs.shape, BF16),
        compiler_params=_params("arbitrary"),
        name="moe_experts",
    )(block_expert, block_next, block_rows, nb_used, xs, w_gu, b_gu, w_down, b_down)


def _fin_body(tiles_p, n_ref, glob_ref, local_ref, tot_ref, hp_ref, hs_ref, pg_ref, g2_ref, gf_ref, ys_hbm,
              op_ref, os_ref, ybuf, sem):
    j = pl.program_id(0)
    tiles = pl.num_programs(0)
    slot = j % 2
    is_prompt = j < tiles_p

    def fetch(step, s):
        _start_segments(n_ref, glob_ref, local_ref, ys_hbm, ybuf.at[s], sem.at[s], step, False, unroll=False)

    @pl.when(j == 0)
    def _():
        ybuf[...] = jnp.zeros_like(ybuf)
        fetch(0, 0)

    _wait_rows(tot_ref[j], WAIT_MAX_PIECE, ys_hbm, sem.at[slot])

    @pl.when(j + 1 < tiles)
    def _():
        fetch(jnp.minimum(j + 1, tiles - 1), 1 - slot)

    pg = pg_ref[...]
    ff = jnp.zeros((ROWS, D_MODEL), F32)
    r = lax.broadcasted_iota(jnp.int32, (ROWS, SORT_CHUNK), 1).astype(F32).astype(BF16)
    gates = pg[:, TOP_K:].astype(BF16)
    for c in range(SORT_ROWS // SORT_CHUNK):
        rel = (pg[:, :TOP_K] - float(c * SORT_CHUNK)).astype(BF16)
        w = jnp.zeros((ROWS, SORT_CHUNK), BF16)
        for k in range(TOP_K):
            w = jnp.where(r == rel[:, k:k + 1], gates[:, k:k + 1], w)
        ff = ff + jnp.dot(w, ybuf[slot, c * SORT_CHUNK:(c + 1) * SORT_CHUNK, :], preferred_element_type=F32)
    h = jnp.where(is_prompt, hp_ref[...].reshape(ROWS, D_MODEL), hs_ref[...])
    y = _rms(h + g2_ref[0] * ff, gf_ref[...])

    @pl.when(is_prompt)
    def _():
        op_ref[...] = y.reshape(op_ref.shape)

    @pl.when(jnp.logical_not(is_prompt))
    def _():
        os_ref[...] = y


def _final(seg_n, seg_glob, seg_local, seg_tot, h_p, h_s, pg_t, modpat, g_final, ys, tiles_p, tiles_s):
    hp_spec, hs_spec = _tile_specs(h_p.shape[0], tiles_p, tiles_s)
    grid_spec = pltpu.PrefetchScalarGridSpec(
        num_scalar_prefetch=4,
        grid=(tiles_p + tiles_s,),
        in_specs=[hp_spec, hs_spec,
                  pl.BlockSpec((ROWS, 2 * TOP_K), lambda j, *_: (j, 0)),
                  _mod_spec(5, tiles_p), _full((1, D_MODEL)),
                  pl.BlockSpec(memory_space=pl.ANY)],
        out_specs=[hp_spec, hs_spec],
        scratch_shapes=[pltpu.VMEM((2, SORT_ROWS, D_MODEL), BF16), pltpu.SemaphoreType.DMA((2,))],
    )
    return pl.pallas_call(
        functools.partial(_fin_body, tiles_p),
        grid_spec=grid_spec,
        out_shape=[jax.ShapeDtypeStruct(h_p.shape, F32), jax.ShapeDtypeStruct(h_s.shape, F32)],
        compiler_params=_params("arbitrary"),
        name="combine_final",
    )(seg_n, seg_glob, seg_local, seg_tot, h_p, h_s, pg_t, modpat, g_final, ys)


def _block_diag(w):
    h, i, j = w.shape
    return jnp.einsum('hij,hk->hikj', w, jnp.eye(h, dtype=w.dtype)).reshape(h * i, h * j)


def _s5_cols(re, im):
    b = re.shape[0]
    return jnp.concatenate([re.reshape(b, -1), im.reshape(b, -1)], axis=1)


def _s5_uncols(cols):
    b = cols.shape[0]
    return (cols[:, :S5_COLS // 2].reshape(b, S5_GROUPS, S5_STATE),
            cols[:, S5_COLS // 2:].reshape(b, S5_GROUPS, S5_STATE))


def _moe_rows_bound(tiles):
    worst = tiles * (TOP_K * ROWS + N_EXPERTS * (SEG_ALIGN - 1)) + N_EXPERTS * (MOE_TM - SEG_ALIGN)
    return (worst + MOE_TM - 1) // MOE_TM * MOE_TM


def _plan(cnt):
    cnt = cnt.astype(jnp.int32)
    tiles = cnt.shape[0]
    cp = (cnt + SEG_ALIGN - 1) // SEG_ALIGN * SEG_ALIGN
    local = jnp.cumsum(cp, axis=1) - cp
    group = jnp.sum(cp, axis=0)
    group_pad = (group + MOE_TM - 1) // MOE_TM * MOE_TM
    pend = jnp.cumsum(group_pad)
    pstart = pend - group_pad
    glob = pstart[None, :] + jnp.cumsum(cp, axis=0) - cp
    gap = group_pad - group
    seg_n = jnp.concatenate([cp, gap[None]], axis=0).reshape(-1)
    seg_local = jnp.concatenate([local, jnp.zeros((1, N_EXPERTS), jnp.int32)], axis=0).reshape(-1)
    seg_glob = jnp.concatenate([glob, (pstart + group)[None]], axis=0).reshape(-1)
    seg_tot = jnp.concatenate([jnp.sum(cp, axis=1), jnp.sum(gap)[None]]).astype(jnp.int32)
    n_blocks = _moe_rows_bound(tiles) // MOE_TM
    block_row0 = jnp.arange(n_blocks, dtype=jnp.int32) * MOE_TM
    block_expert = jnp.minimum(jnp.sum(block_row0[:, None] >= pend[None, :], axis=1), N_EXPERTS - 1).astype(jnp.int32)
    nb_used = (pend[-1] // MOE_TM).astype(jnp.int32).reshape(1)
    experts = jnp.arange(N_EXPERTS, dtype=jnp.int32)
    later_owner = jnp.where((experts[None, :] > experts[:, None]) & (group_pad[None, :] > 0), experts[None, :],
                            N_EXPERTS)
    next_owner = jnp.min(later_owner, axis=1)
    next_owner = jnp.where(next_owner == N_EXPERTS, -1, next_owner).astype(jnp.int32)
    owner = block_expert[:, None] == experts[None, :]
    block_next = jnp.sum(jnp.where(owner, next_owner[None, :], 0), axis=1).astype(jnp.int32)
    group_end = jnp.sum(jnp.where(owner, (pstart + group)[None, :], 0), axis=1)
    block_rows = jnp.clip(group_end - block_row0, 0, MOE_TM).astype(jnp.int32)
    return seg_n, seg_glob, seg_local, seg_tot, block_expert, block_next, block_rows, nb_used


def kernel(x_prompt, x_sample, state_s5_re, state_s5_im, state_lru_h, state_conv, c_prompt, c_sample, w_ada, b_ada, g_mix, w_in, s5_a_re, s5_a_im, s5_log_dt, s5_b_re, s5_b_im, s5_c_re, s5_c_im, s5_d, s5_w_glu, s5_b_glu, lru_conv_w, lru_conv_b, lru_w_a, lru_b_a, lru_w_x, lru_b_x, lru_lambda, g_out_s5, g_out_lru, w_out, g_ffn, router_w, router_b, moe_w_gu, moe_b_gu, moe_w_down, moe_b_down, g_final):
    assert w_ada.shape[0] == 1, "one layer"
    bp, lp, _ = x_prompt.shape
    bs, ls, _ = x_sample.shape
    assert ROWS % bp == 0 and ROWS % bs == 0 and (bp * lp) % ROWS == 0 and (bs * ls) % ROWS == 0
    tiles_p = bp * lp // ROWS
    tiles_s = bs * ls // ROWS
    row = lambda v: v.reshape(1, -1)

    ab_re, ab_im, bb_re, bb_im = _s5_prep(s5_a_re[0], s5_a_im[0], s5_log_dt[0].reshape(S5_GROUPS, 1),
                                          jnp.swapaxes(s5_b_re[0], 1, 2), jnp.swapaxes(s5_b_im[0], 1, 2))
    gpb = S5_GROUPS // S5_BLOCKS
    eye = jnp.eye(gpb, dtype=F32)

    def in_blocks(b):
        b = b.reshape(S5_BLOCKS, gpb, S5_GROUP_CH, S5_STATE)
        return jnp.einsum('bgjn,gh->bgjhn', b, eye).reshape(S5_BLOCKS, gpb * S5_GROUP_CH, gpb * S5_STATE)

    def out_blocks(c):
        c = c.reshape(S5_BLOCKS, gpb, S5_GROUP_CH, S5_STATE)
        return jnp.einsum('bgjn,gh->bgnhj', c, eye).reshape(S5_BLOCKS, gpb * S5_STATE, gpb * S5_GROUP_CH)

    ar8 = jnp.broadcast_to(ab_re.reshape(1, -1), (SUBLANES, S5_GROUPS * S5_STATE))
    ai8 = jnp.broadcast_to(ab_im.reshape(1, -1), (SUBLANES, S5_GROUPS * S5_STATE))
    bb = jnp.concatenate([in_blocks(bb_re), in_blocks(bb_im)], axis=-1).astype(BF16)
    cc = jnp.concatenate([out_blocks(s5_c_re[0]), -out_blocks(s5_c_im[0])], axis=1).astype(BF16)
    wa_bd = _block_diag(lru_w_a[0]).astype(BF16)
    wx_bd = _block_diag(lru_w_x[0]).astype(BF16)
    rw_t = router_w[0].T
    rw_hi = rw_t.astype(BF16)
    rw_lo = (rw_t - rw_hi.astype(F32)).astype(BF16)

    tc = ROWS // bp
    modpat = _adaln(jnp.concatenate([c_prompt, c_sample], axis=0), w_ada[0], row(b_ada[0]), bp)
    r = jnp.arange(ROWS)
    tm_of = (r % tc) * bp + r // tc
    to_tm = (r[:, None] == tm_of[None, :]).astype(BF16)
    from_tm = to_tm.T
    pair = 2 if tiles_p % 2 == 0 else 1

    def conv_tm(cv):
        return jnp.swapaxes(cv, 0, 1).reshape(-1, D_LRU)

    def conv_bm(cv, b):
        return jnp.swapaxes(cv.reshape(CONV_WIDTH - 1, b, D_LRU), 0, 1)

    x_s = jnp.swapaxes(x_sample, 0, 1).reshape(bs * ls, D_MODEL)
    in_args = (row(g_mix[0]), w_in[0].astype(BF16))
    lru_args = (lru_conv_w[0], row(lru_conv_b[0]), wa_bd, row(lru_b_a[0]), wx_bd, row(lru_b_x[0]),
                row(lru_lambda[0]))
    u_p, la_p, lb_p, lg_p, cvp = _in_proj(x_prompt, modpat, 0, to_tm, *in_args,
                                          jnp.zeros(((CONV_WIDTH - 1) * bp, D_LRU), F32), *lru_args, bp, pair)
    u_s, la_s, lb_s, lg_s, cvs = _in_proj(x_s, modpat, 1, None, *in_args, conv_tm(state_conv[0]), *lru_args, bs, 1)

    s5_args = (ar8, ai8, bb, cc, row(s5_d[0]), s5_w_glu[0].astype(BF16), row(s5_b_glu[0]), row(g_out_s5[0]))
    ys5_p, s5p, ylru_p, hp = _mixers(u_p, jnp.zeros((bp, S5_COLS), F32), *s5_args, la_p, lb_p, lg_p,
                                     jnp.zeros((bp, D_LRU), F32), row(g_out_lru[0]), bp, pair)
    ys5_s, s5s, ylru_s, hs = _mixers(u_s, _s5_cols(state_s5_re[0], state_s5_im[0]), *s5_args, la_s, lb_s, lg_s,
                                     state_lru_h[0], row(g_out_lru[0]), bs, 1)

    out_args = (w_out[0].astype(BF16).reshape(2, D_S5, D_MODEL), row(g_ffn[0]), rw_hi, rw_lo,
                router_b[0].reshape(N_EXPERTS, 1))
    h_p, xn_p, pg_p, cnt_p = _out_proj(ys5_p, ylru_p, x_prompt, modpat, 0, from_tm, *out_args, pair)
    h_s, xn_s, pg_s, cnt_s = _out_proj(ys5_s, ylru_s, x_s, modpat, 1, None, *out_args, 1)
    pg = jnp.concatenate([pg_p, pg_s], axis=1)
    cnt = jnp.concatenate([cnt_p, cnt_s], axis=0)

    seg_n, seg_glob, seg_local, seg_tot, block_expert, block_next, block_rows, nb_used = _plan(cnt[:, :, 0])
    xs = _dispatch(seg_n, seg_glob, seg_local, seg_tot, nb_used, pg, xn_p, xn_s,
                   _moe_rows_bound(tiles_p + tiles_s))
    ys = _moe(xs, block_expert, block_next, block_rows, nb_used, moe_w_gu[0], moe_b_gu[0].reshape(N_EXPERTS, 1, 2 * D_FF),
              moe_w_down[0], moe_b_down[0].reshape(N_EXPERTS, 1, D_MODEL))
    y_prompt, y_s = _final(seg_n, seg_glob, seg_local, seg_tot, h_p, h_s, pg.T, modpat, row(g_final), ys,
                           tiles_p, tiles_s)
    y_sample = jnp.swapaxes(y_s.reshape(ls, bs, D_MODEL), 0, 1)
    s5p_re, s5p_im = _s5_uncols(s5p)
    s5s_re, s5s_im = _s5_uncols(s5s)
    return (y_prompt, y_sample,
            s5p_re[None], s5p_im[None], hp[None], conv_bm(cvp, bp)[None],
            s5s_re[None], s5s_im[None], hs[None], conv_bm(cvs, bs)[None])
```

```python
import functools

import jax
import jax.numpy as jnp
from jax import lax
from jax.experimental import pallas as pl
from jax.experimental.pallas import tpu as pltpu

D_MODEL = 1024
D_S5 = 512
D_LRU = 512
S5_GROUPS = 32
S5_GROUP_CH = 16
S5_STATE = 64
S5_COLS = 2 * S5_GROUPS * S5_STATE
S5_BLOCKS = 4
LRU_HEADS = 8
LRU_HEAD_DIM = 64
LRU_C = 8.0
CONV_WIDTH = 4
N_EXPERTS = 32
TOP_K = 4
D_FF = 1024
SWIGLU_LIMIT = 7.0
SWIGLU_ALPHA = 1.702
N_MOD = 6
EPS = 1e-6

ROWS = 512
MOE_TM = 512
MOE_PART = 128
BIG_PIECE = 128
WAIT_MAX_PIECE = 8192
SEG_ALIGN = 8
SORT_ROWS = 2304
PACKED = D_MODEL // 2
SORT_CHUNK = 256
SUBLANES = 8
LANES = 128
VMEM_LIMIT = 48 * 1024 * 1024

BF16 = jnp.bfloat16
F32 = jnp.float32
U32 = jnp.uint32


def _params(*sem):
    return pltpu.CompilerParams(dimension_semantics=sem, vmem_limit_bytes=VMEM_LIMIT)


def _full(shape):
    return pl.BlockSpec(shape, lambda *_: (0,) * len(shape))


def _rms(x, g):
    return x * lax.rsqrt(jnp.mean(x * x, axis=-1, keepdims=True) + EPS) * g


def _gelu(x):
    return 0.5 * x * (1.0 + lax.erf(x * (2.0 ** -0.5)))


def _expm1(x):
    u = jnp.exp(x)
    d = u - 1.0
    return jnp.where(d == 0.0, x, jnp.where(d == -1.0, -1.0, d * x / jnp.log(u)))


def _pack_rows(x):
    lo = lax.shift_right_logical(lax.bitcast_convert_type(x[:, :PACKED], U32), jnp.uint32(16))
    hi = lax.bitcast_convert_type(x[:, PACKED:], U32) & jnp.uint32(0xFFFF0000)
    return lo | hi


def _unpack_rows(p):
    lo = lax.bitcast_convert_type(lax.shift_left(p, jnp.uint32(16)), F32)
    hi = lax.bitcast_convert_type(p & jnp.uint32(0xFFFF0000), F32)
    return jnp.concatenate([lo, hi], axis=-1).astype(BF16)


def _row_pieces(n, largest, fn):
    off = 0
    bit = largest
    while bit >= SEG_ALIGN:
        @pl.when((n & bit) != 0)
        def _(off=off, bit=bit):
            fn(off, bit)
        off = off + (n & bit)
        bit //= 2


def _mod_body(bp, c_ref, w_ref, b_ref, o_ref):
    c = c_ref[...]
    s = (c * jax.nn.sigmoid(c)).astype(BF16)
    mod = jnp.dot(s, w_ref[...].astype(BF16), preferred_element_type=F32) + b_ref[...]
    bs = mod.shape[0] - bp
    o_ref[0] = jnp.broadcast_to(mod[:bp][:, None, :], (bp, ROWS // bp, D_MODEL)).reshape(ROWS, D_MODEL)
    o_ref[1] = jnp.broadcast_to(mod[bp:][None], (ROWS // bs, bs, D_MODEL)).reshape(ROWS, D_MODEL)


def _adaln(c, w_ada, b_ada, bp):
    m = c.shape[0]
    return pl.pallas_call(
        functools.partial(_mod_body, bp),
        grid=(N_MOD,),
        in_specs=[pl.BlockSpec((m, D_MODEL), lambda j: (0, 0)),
                  pl.BlockSpec((D_MODEL, D_MODEL), lambda j: (0, j)),
                  pl.BlockSpec((1, D_MODEL), lambda j: (0, j))],
        out_specs=pl.BlockSpec((2, ROWS, D_MODEL), lambda j: (0, 0, j)),
        out_shape=jax.ShapeDtypeStruct((2, ROWS, N_MOD * D_MODEL), F32),
        compiler_params=_params("arbitrary"),
        name="adaln",
    )(c, w_ada, b_ada)


def _mod_spec(k, tiles_p):
    return pl.BlockSpec((1, ROWS, D_MODEL), lambda i, *_: (jnp.where(i < tiles_p, 0, 1), 0, k))


def _tile_specs(bp, tiles_p, tiles_s):
    tc = ROWS // bp
    p_spec = pl.BlockSpec((bp, tc, D_MODEL), lambda i, *_: (0, jnp.minimum(i, tiles_p - 1), 0))
    s_spec = pl.BlockSpec((ROWS, D_MODEL), lambda i, *_: (jnp.clip(i - tiles_p, 0, tiles_s - 1), 0))
    return p_spec, s_spec


def _in_body(subtiles, bt, to_tm, *refs):
    if to_tm:
        x_ref, sc_ref, sh_ref, perm_ref, *refs = refs
    else:
        x_ref, sc_ref, sh_ref, *refs = refs
    (g_ref, w_ref, cv0_ref, cw_ref, cb_ref, wa_ref, ba_ref, wx_ref, bx_ref, lam_ref,
     u_ref, a_ref, b_ref, gg_ref, cvout_ref, xp_ref) = refs
    halo = (CONV_WIDTH - 1) * bt

    @pl.when(pl.program_id(0) == 0)
    def _():
        xp_ref[...] = cv0_ref[...]

    lam = lam_ref[...]
    softplus_neg_lam = jnp.maximum(-lam, 0.0) + jnp.log1p(jnp.exp(-jnp.abs(lam)))
    earlier = xp_ref[...]
    for c in range(subtiles):
        rows = slice(c * ROWS, (c + 1) * ROWS)
        if to_tm:
            tc = x_ref.shape[1] // subtiles
            x = x_ref[:, c * tc:(c + 1) * tc, :].reshape(ROWS, D_MODEL)
        else:
            x = x_ref[rows, :]
        xn = (_rms(x, g_ref[...]) * (1.0 + sc_ref[0]) + sh_ref[0]).astype(BF16)
        if to_tm:
            xn = jnp.dot(perm_ref[...], xn, preferred_element_type=F32).astype(BF16)
        p = jnp.dot(xn, w_ref[...], preferred_element_type=F32)
        u_ref[rows, :] = p[:, :D_S5]
        gg_ref[rows, :] = _gelu(p[:, D_S5 + D_LRU:])

        xr = p[:, D_S5:D_S5 + D_LRU]
        xp = jnp.concatenate([earlier, xr], axis=0)
        earlier = xr[ROWS - halo:, :]
        xc = cb_ref[...] + sum(xp[k * bt:k * bt + ROWS, :] * cw_ref[k:k + 1, :] for k in range(CONV_WIDTH))
        xcb = xc.astype(BF16)
        r = jax.nn.sigmoid(jnp.dot(xcb, wa_ref[...], preferred_element_type=F32) + ba_ref[...])
        i = jax.nn.sigmoid(jnp.dot(xcb, wx_ref[...], preferred_element_type=F32) + bx_ref[...])
        log_a = -LRU_C * r * softplus_neg_lam
        a_ref[rows, :] = jnp.exp(log_a)
        b_ref[rows, :] = jnp.sqrt(-_expm1(2.0 * log_a)) * (i * xc)

    xp_ref[...] = earlier
    cvout_ref[...] = earlier


def _in_proj(x, modpat, pattern, to_tm, g_mix, w_in_bf, conv0_tm, conv_w, conv_b, wa_bd, b_a, wx_bd, b_x, lam, bt,
             subtiles):
    tm = to_tm is not None
    t_rows = x.shape[0] * x.shape[1] if tm else x.shape[0]
    step_rows = subtiles * ROWS
    halo = (CONV_WIDTH - 1) * bt
    if tm:
        x_spec = pl.BlockSpec((x.shape[0], step_rows // x.shape[0], D_MODEL), lambda i: (0, i, 0))
    else:
        x_spec = pl.BlockSpec((step_rows, D_MODEL), lambda i: (i, 0))
    mod_spec = lambda k: pl.BlockSpec((1, ROWS, D_MODEL), lambda i: (pattern, 0, k))
    row_spec = pl.BlockSpec((step_rows, D_S5), lambda i: (i, 0))
    vec = _full((1, D_LRU))
    return pl.pallas_call(
        functools.partial(_in_body, subtiles, bt, tm),
        grid=(t_rows // step_rows,),
        in_specs=[x_spec, mod_spec(1), mod_spec(0), *([_full((ROWS, ROWS))] if tm else []), _full((1, D_MODEL)),
                  _full((D_MODEL, D_S5 + 2 * D_LRU)), _full((halo, D_LRU)), _full((CONV_WIDTH, D_LRU)), vec,
                  _full((D_LRU, D_LRU)), vec, _full((D_LRU, D_LRU)), vec, vec],
        out_specs=[row_spec, row_spec, row_spec, row_spec, _full((halo, D_LRU))],
        out_shape=[jax.ShapeDtypeStruct((t_rows, D_S5), F32)] * 4 + [jax.ShapeDtypeStruct((halo, D_LRU), F32)],
        scratch_shapes=[pltpu.VMEM((halo, D_LRU), F32)],
        compiler_params=_params("arbitrary"),
        name="in_proj",
    )(x, modpat, modpat, *([to_tm] if tm else []), g_mix, w_in_bf, conv0_tm, conv_w, conv_b, wa_bd, b_a, wx_bd, b_x,
      lam)


def _s5_prep_body(are_ref, aim_ref, ldt_ref, bre_ref, bim_ref, abre_ref, abim_ref, bbre_ref, bbim_ref):
    a_re = are_ref[...]
    a_im = aim_ref[...]
    dt = jnp.exp(ldt_ref[...])
    mag = jnp.exp(dt * a_re)
    ang = dt * a_im
    ab_re = mag * jnp.cos(ang)
    ab_im = mag * jnp.sin(ang)
    den = a_re * a_re + a_im * a_im
    q_re = ((ab_re - 1.0) * a_re + ab_im * a_im) / den
    q_im = (ab_im * a_re - (ab_re - 1.0) * a_im) / den
    abre_ref[...] = ab_re
    abim_ref[...] = ab_im
    b_re = bre_ref[...]
    b_im = bim_ref[...]
    bbre_ref[...] = q_re[:, None, :] * b_re - q_im[:, None, :] * b_im
    bbim_ref[...] = q_re[:, None, :] * b_im + q_im[:, None, :] * b_re


def _s5_prep(a_re, a_im, log_dt, b_re_t, b_im_t):
    gn = jax.ShapeDtypeStruct((S5_GROUPS, S5_STATE), F32)
    gjn = jax.ShapeDtypeStruct((S5_GROUPS, S5_GROUP_CH, S5_STATE), F32)
    return pl.pallas_call(_s5_prep_body, out_shape=[gn, gn, gjn, gjn], name="s5_prep")(
        a_re, a_im, log_dt, b_re_t, b_im_t)


def _s5_body(tc, bt, chunks, u_ref, s0_ref, ar_ref, ai_ref, bb_ref, cc_ref, d_ref, wg_ref, bg_ref, go_ref,
             la_ref, lb_ref, lg_ref, lh0_ref, lgo_ref, y_ref, sout_ref, yl_ref, lhout_ref,
             bu_ref, st_ref, lh_ref, lst_ref):
    half = S5_COLS // S5_BLOCKS // 2
    re = [slice(half * j, half * (j + 1)) for j in range(S5_BLOCKS)]
    im = [slice(S5_COLS // 2 + half * j, S5_COLS // 2 + half * (j + 1)) for j in range(S5_BLOCKS)]

    @pl.when(pl.program_id(0) == 0)
    def _():
        st_ref[...] = s0_ref[...]
        lst_ref[...] = lh0_ref[...]

    for c in range(chunks):
        ub = u_ref[c * ROWS:(c + 1) * ROWS, :].astype(BF16)
        for j in range(S5_BLOCKS):
            bu = jnp.dot(ub[:, LANES * j:LANES * (j + 1)], bb_ref[j], preferred_element_type=F32)
            bu_ref[c, :, re[j]] = bu[:, :half]
            bu_ref[c, :, im[j]] = bu[:, half:]

    for sb in range(bt // SUBLANES):
        base = sb * SUBLANES
        hr = [st_ref[base:base + SUBLANES, re[j]] for j in range(S5_BLOCKS)]
        hi = [st_ref[base:base + SUBLANES, im[j]] for j in range(S5_BLOCKS)]
        hl = lst_ref[base:base + SUBLANES, :]
        for c in range(chunks):
            for t in range(tc):
                r0 = t * bt + base
                row = c * ROWS + r0
                hl = la_ref[row:row + SUBLANES, :] * hl + lb_ref[row:row + SUBLANES, :]
                lh_ref[row:row + SUBLANES, :] = hl
                for j in range(S5_BLOCKS):
                    ar = ar_ref[:, re[j]]
                    ai = ai_ref[:, re[j]]
                    h_re = ar * hr[j] - ai * hi[j] + bu_ref[c, r0:r0 + SUBLANES, re[j]]
                    h_im = ar * hi[j] + ai * hr[j] + bu_ref[c, r0:r0 + SUBLANES, im[j]]
                    bu_ref[c, r0:r0 + SUBLANES, re[j]] = h_re
                    bu_ref[c, r0:r0 + SUBLANES, im[j]] = h_im
                    hr[j], hi[j] = h_re, h_im
        for j in range(S5_BLOCKS):
            st_ref[base:base + SUBLANES, re[j]] = hr[j]
            st_ref[base:base + SUBLANES, im[j]] = hi[j]
        lst_ref[base:base + SUBLANES, :] = hl
    sout_ref[...] = st_ref[...]
    lhout_ref[...] = lst_ref[...]
    yl_ref[...] = _rms(lh_ref[...] * lg_ref[...], lgo_ref[...]).astype(BF16)

    for c in range(chunks):
        y = jnp.concatenate(
            [jnp.dot(jnp.concatenate([bu_ref[c, :, re[j]], bu_ref[c, :, im[j]]], axis=-1).astype(BF16), cc_ref[j],
                     preferred_element_type=F32) for j in range(S5_BLOCKS)], axis=-1)
        y = y + d_ref[...] * u_ref[c * ROWS:(c + 1) * ROWS, :]
        g = _gelu(y)
        z = jnp.dot(g.astype(BF16), wg_ref[...], preferred_element_type=F32) + bg_ref[...]
        out = g * jax.nn.sigmoid(z)
        y_ref[c * ROWS:(c + 1) * ROWS, :] = _rms(out, go_ref[...]).astype(BF16)


def _mixers(u, s0, ar8, ai8, bb, cc, d, w_glu_bf, b_glu, g_out_s5, lru_a, lru_b, lru_gg, lru_h0, g_out_lru, bt,
            chunks):
    tc = ROWS // bt
    t_rows = u.shape[0]
    step_rows = chunks * ROWS
    assert t_rows % step_rows == 0
    rows_f32 = pl.BlockSpec((step_rows, D_S5), lambda i: (i, 0))
    vec = _full((1, D_S5))
    return pl.pallas_call(
        functools.partial(_s5_body, tc, bt, chunks),
        grid=(t_rows // step_rows,),
        in_specs=[rows_f32, _full((bt, S5_COLS)), _full(ar8.shape), _full(ai8.shape), _full(bb.shape),
                  _full(cc.shape), vec, _full((D_S5, D_S5)), vec, vec,
                  rows_f32, rows_f32, rows_f32, _full((bt, D_LRU)), vec],
        out_specs=[rows_f32, _full((bt, S5_COLS)), rows_f32, _full((bt, D_LRU))],
        out_shape=[jax.ShapeDtypeStruct((t_rows, D_S5), BF16), jax.ShapeDtypeStruct((bt, S5_COLS), F32),
                   jax.ShapeDtypeStruct((t_rows, D_LRU), BF16), jax.ShapeDtypeStruct((bt, D_LRU), F32)],
        scratch_shapes=[pltpu.VMEM((chunks, ROWS, S5_COLS), F32), pltpu.VMEM((bt, S5_COLS), F32),
                        pltpu.VMEM((step_rows, D_LRU), F32), pltpu.VMEM((bt, D_LRU), F32)],
        compiler_params=_params("arbitrary"),
        name="s5_lru_mixers",
    )(u, s0, ar8, ai8, bb, cc, d, w_glu_bf, b_glu, g_out_s5, lru_a, lru_b, lru_gg, lru_h0, g_out_lru)


def _route_tile(xn, x_hi, rw_hi, rw_lo, rb):
    x_lo = (xn - x_hi.astype(F32)).astype(BF16)
    nt = (((1,), (1,)), ((), ()))
    logits = (lax.dot_general(rw_hi, x_hi, nt, preferred_element_type=F32)
              + lax.dot_general(rw_hi, x_lo, nt, preferred_element_type=F32)
              + lax.dot_general(rw_lo, x_hi, nt, preferred_element_type=F32)) + rb

    e_iota = lax.broadcasted_iota(jnp.int32, (N_EXPERTS, ROWS), 0).astype(F32)
    work = logits
    sels, vals = [], []
    for _ in range(TOP_K):
        m = jnp.max(work, axis=0, keepdims=True)
        idx = jnp.min(jnp.where(work == m, e_iota, float(N_EXPERTS)), axis=0, keepdims=True)
        sel = e_iota == idx
        work = jnp.where(sel, -jnp.inf, work)
        sels.append(sel)
        vals.append(m)
    exps = [jnp.exp(v - vals[0]) for v in vals]
    denom = exps[0] + exps[1] + exps[2] + exps[3]
    gates = [e / denom for e in exps]

    onehot = sels[0] | sels[1] | sels[2] | sels[3]
    rr = lax.broadcasted_iota(jnp.int32, (ROWS, ROWS), 0)
    cc = lax.broadcasted_iota(jnp.int32, (ROWS, ROWS), 1)
    before = (rr < cc).astype(BF16)
    prefix = jnp.dot(onehot.astype(BF16), before, preferred_element_type=F32)
    cnt = jnp.sum(onehot.astype(F32), axis=1, keepdims=True)
    cnt_pad = jnp.floor((cnt + (SEG_ALIGN - 1)) * (1.0 / SEG_ALIGN)) * SEG_ALIGN
    er = lax.broadcasted_iota(jnp.int32, (N_EXPERTS, N_EXPERTS), 0)
    ec = lax.broadcasted_iota(jnp.int32, (N_EXPERTS, N_EXPERTS), 1)
    seg_start = jnp.dot((ec < er).astype(BF16), jnp.broadcast_to(cnt_pad, (N_EXPERTS, LANES)).astype(BF16),
                        preferred_element_type=F32)[:, 0:1]
    where_to = prefix + seg_start
    poss = [jnp.sum(jnp.where(s, where_to, 0.0), axis=0, keepdims=True) for s in sels]

    s_iota = lax.broadcasted_iota(jnp.int32, (2 * TOP_K, ROWS), 0)
    pg = jnp.zeros((2 * TOP_K, ROWS), F32)
    for k in range(TOP_K):
        pg = jnp.where(s_iota == k, poss[k], pg)
        pg = jnp.where(s_iota == TOP_K + k, gates[k], pg)
    return pg, cnt


def _out_body(subtiles, tm_rows, *refs):
    if tm_rows:
        ys_ref, yl_ref, x_ref, g1_ref, sc_ref, sh_ref, perm_ref, *refs = refs
    else:
        ys_ref, yl_ref, x_ref, g1_ref, sc_ref, sh_ref, *refs = refs
    wo_ref, gf_ref, rwh_ref, rwl_ref, rb_ref, h_ref, xn_ref, pg_ref, cnt_ref = refs
    for c in range(subtiles):
        rows = slice(c * ROWS, (c + 1) * ROWS)
        ys = ys_ref[rows, :]
        yl = yl_ref[rows, :]
        if tm_rows:
            ys = jnp.dot(perm_ref[...], ys, preferred_element_type=F32).astype(BF16)
            yl = jnp.dot(perm_ref[...], yl, preferred_element_type=F32).astype(BF16)
            tc = x_ref.shape[1] // subtiles
            x = x_ref[:, c * tc:(c + 1) * tc, :].reshape(ROWS, D_MODEL)
        else:
            x = x_ref[rows, :]
        mixed = (jnp.dot(ys, wo_ref[0], preferred_element_type=F32)
                 + jnp.dot(yl, wo_ref[1], preferred_element_type=F32))
        h = x + g1_ref[0] * mixed
        if tm_rows:
            h_ref[:, c * tc:(c + 1) * tc, :] = h.reshape(x_ref.shape[0], tc, D_MODEL)
        else:
            h_ref[rows, :] = h
        xn = _rms(h, gf_ref[...]) * (1.0 + sc_ref[0]) + sh_ref[0]
        x_hi = xn.astype(BF16)
        xn_ref[rows, :] = x_hi
        pg, cnt = _route_tile(xn, x_hi, rwh_ref[...], rwl_ref[...], rb_ref[...])
        pg_ref[:, rows] = pg
        cnt_ref[c] = jnp.broadcast_to(cnt, (N_EXPERTS, LANES))


def _out_proj(ys5, ylru, x, modpat, pattern, from_tm, w_out_bf, g_ffn, rw_hi, rw_lo, router_b, subtiles):
    tm_rows = from_tm is not None
    t_rows = ys5.shape[0]
    step_rows = subtiles * ROWS
    half_spec = pl.BlockSpec((step_rows, D_S5), lambda i: (i, 0))
    if tm_rows:
        x_spec = pl.BlockSpec((x.shape[0], step_rows // x.shape[0], D_MODEL), lambda i: (0, i, 0))
    else:
        x_spec = pl.BlockSpec((step_rows, D_MODEL), lambda i: (i, 0))
    mod_spec = lambda k: pl.BlockSpec((1, ROWS, D_MODEL), lambda i: (pattern, 0, k))
    perm = [_full((ROWS, ROWS))] if tm_rows else []
    return pl.pallas_call(
        functools.partial(_out_body, subtiles, tm_rows),
        grid=(t_rows // step_rows,),
        in_specs=[half_spec, half_spec, x_spec, mod_spec(2), mod_spec(4), mod_spec(3), *perm,
                  _full((2, D_S5, D_MODEL)), _full((1, D_MODEL)),
                  _full((N_EXPERTS, D_MODEL)), _full((N_EXPERTS, D_MODEL)), _full((N_EXPERTS, 1))],
        out_specs=[x_spec, pl.BlockSpec((step_rows, D_MODEL), lambda i: (i, 0)),
                   pl.BlockSpec((2 * TOP_K, step_rows), lambda i: (0, i)),
                   pl.BlockSpec((subtiles, N_EXPERTS, LANES), lambda i: (i, 0, 0))],
        out_shape=[jax.ShapeDtypeStruct(x.shape, F32), jax.ShapeDtypeStruct((t_rows, D_MODEL), BF16),
                   jax.ShapeDtypeStruct((2 * TOP_K, t_rows), F32),
                   jax.ShapeDtypeStruct((t_rows // ROWS, N_EXPERTS, LANES), F32)],
        compiler_params=_params("arbitrary"),
        name="out_proj_router",
    )(ys5, ylru, x, modpat, modpat, modpat, *([from_tm] if tm_rows else []), w_out_bf, g_ffn, rw_hi, rw_lo,
      router_b)


def _start_segments(n_ref, hbm_ref, vmem_ref, hbm, vmem_buf, sem, step, to_hbm, unroll):
    def seg(e):
        n = n_ref[step * N_EXPERTS + e]
        h0 = hbm_ref[step * N_EXPERTS + e]
        v0 = vmem_ref[step * N_EXPERTS + e]

        def piece(off, size):
            h = hbm.at[pl.ds(pl.multiple_of(h0 + off, SEG_ALIGN), size)]
            v = vmem_buf.at[pl.ds(pl.multiple_of(v0 + off, SEG_ALIGN), size)]
            (pltpu.make_async_copy(v, h, sem) if to_hbm else pltpu.make_async_copy(h, v, sem)).start()

        if unroll:
            _row_pieces(n, ROWS, piece)
        else:
            def big(k, carry):
                piece(k * BIG_PIECE, BIG_PIECE)
                return carry

            lax.fori_loop(0, n // BIG_PIECE, big, 0)
            _row_pieces(n % BIG_PIECE, BIG_PIECE // 2, lambda off, size: piece(n // BIG_PIECE * BIG_PIECE + off, size))

    if unroll:
        for e in range(N_EXPERTS):
            seg(e)
    else:
        def body(e, carry):
            seg(e)
            return carry

        lax.fori_loop(0, N_EXPERTS, body, 0)


def _wait_rows(total, largest, hbm, sem):
    _row_pieces(total, largest, lambda off, size: pltpu.make_async_copy(
        hbm.at[pl.ds(0, size)], hbm.at[pl.ds(0, size)], sem).wait())


def _dispatch_body(tiles_p, n_ref, glob_ref, local_ref, tot_ref, nb_ref, pg_ref, xp_ref, xsm_ref, xs_hbm, stage,
                   sem):
    j = pl.program_id(0)
    last = pl.num_programs(0) - 1
    slot = j % 2

    def unused_blocks(act):
        def blk(b, carry):
            act(pltpu.make_async_copy(stage.at[slot, pl.ds(0, MOE_TM)],
                                      xs_hbm.at[pl.ds(pl.multiple_of(b * MOE_TM, MOE_TM), MOE_TM)], sem.at[slot]))
            return carry
        lax.fori_loop(nb_ref[0], xs_hbm.shape[0] // MOE_TM, blk, 0)

    def wait_step(step, s):
        _wait_rows(tot_ref[step], WAIT_MAX_PIECE, xs_hbm, sem.at[s])

    @pl.when(j >= 2)
    def _():
        wait_step(jnp.maximum(j - 2, 0), slot)

    @pl.when(j < last)
    def _():
        x = jnp.where(j < tiles_p, xp_ref[...], xsm_ref[...])
        pos = pg_ref[0:TOP_K, :]
        r = lax.broadcasted_iota(jnp.int32, (SORT_CHUNK, ROWS), 0).astype(F32).astype(BF16)
        for c in range(SORT_ROWS // SORT_CHUNK):
            rel = (pos - float(c * SORT_CHUNK)).astype(BF16)
            pick = (r == rel[0:1]) | (r == rel[1:2]) | (r == rel[2:3]) | (r == rel[3:4])
            stage[slot, c * SORT_CHUNK:(c + 1) * SORT_CHUNK, :] = _pack_rows(jnp.dot(
                jnp.where(pick, jnp.ones((), BF16), jnp.zeros((), BF16)), x, preferred_element_type=F32))

    @pl.when(j == last)
    def _():
        stage[slot, 0:MOE_TM, :] = jnp.zeros((MOE_TM, PACKED), U32)

    _start_segments(n_ref, glob_ref, local_ref, xs_hbm, stage.at[slot], sem.at[slot], j, True, unroll=True)

    @pl.when(j == last)
    def _():
        unused_blocks(lambda c: c.start())

        @pl.when(j >= 1)
        def _():
            wait_step(jnp.maximum(j - 1, 0), 1 - slot)
        wait_step(j, slot)
        unused_blocks(lambda c: c.wait())


def _dispatch(seg_n, seg_glob, seg_local, seg_tot, nb_used, pg, xn_p, xn_s, n_rows):
    tiles_p = xn_p.shape[0] // ROWS
    tiles_s = xn_s.shape[0] // ROWS
    tiles = tiles_p + tiles_s
    grid_spec = pltpu.PrefetchScalarGridSpec(
        num_scalar_prefetch=5,
        grid=(tiles + 1,),
        in_specs=[pl.BlockSpec((2 * TOP_K, ROWS), lambda j, *_: (0, jnp.minimum(j, tiles - 1))),
                  pl.BlockSpec((ROWS, D_MODEL), lambda j, *_: (jnp.minimum(j, tiles_p - 1), 0)),
                  pl.BlockSpec((ROWS, D_MODEL), lambda j, *_: (jnp.clip(j - tiles_p, 0, tiles_s - 1), 0))],
        out_specs=pl.BlockSpec(memory_space=pl.ANY),
        scratch_shapes=[pltpu.VMEM((2, SORT_ROWS, PACKED), U32), pltpu.SemaphoreType.DMA((2,))],
    )
    return pl.pallas_call(
        functools.partial(_dispatch_body, tiles_p),
        grid_spec=grid_spec,
        out_shape=jax.ShapeDtypeStruct((n_rows, PACKED), U32),
        compiler_params=_params("arbitrary"),
        name="moe_dispatch",
    )(seg_n, seg_glob, seg_local, seg_tot, nb_used, pg, xn_p, xn_s)


def _moe_body(be_ref, nxt_ref, nv_ref, nb_ref, xs_ref, wgu_hbm, bgu_ref, wd_hbm, bd_ref, ys_ref,
              wgu_f32, wd_f32, wgu_bf, wd_bf, sem):
    i = pl.program_id(0)

    def weight_copies(e):
        return (pltpu.make_async_copy(wgu_hbm.at[e], wgu_f32, sem.at[0]),
                pltpu.make_async_copy(wd_hbm.at[e], wd_f32, sem.at[1]))

    @pl.when(i >= nb_ref[0])
    def _():
        ys_ref[...] = jnp.zeros_like(ys_ref)

    @pl.when(i < nb_ref[0])
    def _():
        e = be_ref[i]

        @pl.when(i == 0)
        def _():
            for c in weight_copies(e):
                c.start()

        @pl.when(jnp.logical_or(i == 0, e != be_ref[jnp.maximum(i - 1, 0)]))
        def _():
            for c in weight_copies(e):
                c.wait()
            wgu_bf[...] = wgu_f32[...].astype(BF16)
            wd_bf[...] = wd_f32[...].astype(BF16)

            @pl.when(nxt_ref[i] >= 0)
            def _():
                for c in weight_copies(nxt_ref[i]):
                    c.start()

        def ffn(packed):
            hg = jnp.dot(_unpack_rows(packed), wgu_bf[...], preferred_element_type=F32) + bgu_ref[0]
            gate = jnp.minimum(hg[:, :D_FF], SWIGLU_LIMIT)
            up = jnp.clip(hg[:, D_FF:], -SWIGLU_LIMIT, SWIGLU_LIMIT)
            act = (up + 1.0) * (gate * jax.nn.sigmoid(SWIGLU_ALPHA * gate))
            y = jnp.dot(act.astype(BF16), wd_bf[...], preferred_element_type=F32) + bd_ref[0]
            return _pack_rows(y.astype(BF16).astype(F32))

        for parts in range(1, MOE_TM // MOE_PART + 1):
            lo = (parts - 1) * MOE_PART
            last_path = parts == MOE_TM // MOE_PART

            @pl.when(jnp.logical_and(nv_ref[i] > lo, jnp.logical_or(last_path, nv_ref[i] <= lo + MOE_PART)))
            def _(parts=parts):
                rows = parts * MOE_PART
                ys_ref[0:rows, :] = ffn(xs_ref[0:rows, :])
                if rows < MOE_TM:
                    ys_ref[rows:, :] = jnp.zeros((MOE_TM - rows, PACKED), U32)


def _moe(xs, block_expert, block_next, block_rows, nb_used, w_gu, b_gu, w_down, b_down):
    n_blocks = xs.shape[0] // MOE_TM
    in_rows = pl.BlockSpec((MOE_TM, PACKED), lambda i, be, nx, nv, nb: (jnp.minimum(i, nb[0] - 1), 0))
    grid_spec = pltpu.PrefetchScalarGridSpec(
        num_scalar_prefetch=4,
        grid=(n_blocks,),
        in_specs=[in_rows,
                  pl.BlockSpec(memory_space=pl.ANY),
                  pl.BlockSpec((1, 1, 2 * D_FF), lambda i, be, nx, nv, nb: (be[i], 0, 0)),
                  pl.BlockSpec(memory_space=pl.ANY),
                  pl.BlockSpec((1, 1, D_MODEL), lambda i, be, nx, nv, nb: (be[i], 0, 0))],
        out_specs=pl.BlockSpec((MOE_TM, PACKED), lambda i, be, nx, nv, nb: (i, 0)),
        scratch_shapes=[pltpu.VMEM((D_MODEL, 2 * D_FF), F32), pltpu.VMEM((D_FF, D_MODEL), F32),
                        pltpu.VMEM((D_MODEL, 2 * D_FF), BF16), pltpu.VMEM((D_FF, D_MODEL), BF16),
                        pltpu.SemaphoreType.DMA((2,))],
    )
    return pl.pallas_call(
        _moe_body,
        grid_spec=grid_spec,
        out_shape=jax.ShapeDtypeStruct(xs.shape, U32),
        compiler_params=_params("arbitrary"),
        name="moe_experts",
    )(block_expert, block_next, block_rows, nb_used, xs, w_gu, b_gu, w_down, b_down)


def _fin_body(tiles_p, n_ref, glob_ref, local_ref, tot_ref, hp_ref, hs_ref, pg_ref, g2_ref, gf_ref, ys_hbm,
              op_ref, os_ref, ybuf, sem):
    j = pl.program_id(0)
    tiles = pl.num_programs(0)
    slot = j % 2
    is_prompt = j < tiles_p

    def fetch(step, s):
        _start_segments(n_ref, glob_ref, local_ref, ys_hbm, ybuf.at[s], sem.at[s], step, False, unroll=False)

    @pl.when(j == 0)
    def _():
        ybuf[...] = jnp.zeros_like(ybuf)
        fetch(0, 0)

    _wait_rows(tot_ref[j], WAIT_MAX_PIECE, ys_hbm, sem.at[slot])

    @pl.when(j + 1 < tiles)
    def _():
        fetch(jnp.minimum(j + 1, tiles - 1), 1 - slot)

    pg = pg_ref[...]
    ff = jnp.zeros((ROWS, D_MODEL), F32)
    r = lax.broadcasted_iota(jnp.int32, (ROWS, SORT_CHUNK), 1).astype(F32).astype(BF16)
    gates = pg[:, TOP_K:].astype(BF16)
    for c in range(SORT_ROWS // SORT_CHUNK):
        rel = (pg[:, :TOP_K] - float(c * SORT_CHUNK)).astype(BF16)
        w = jnp.zeros((ROWS, SORT_CHUNK), BF16)
        for k in range(TOP_K):
            w = jnp.where(r == rel[:, k:k + 1], gates[:, k:k + 1], w)
        ff = ff + jnp.dot(w, _unpack_rows(ybuf[slot, c * SORT_CHUNK:(c + 1) * SORT_CHUNK, :]),
                          preferred_element_type=F32)
    h = jnp.where(is_prompt, hp_ref[...].reshape(ROWS, D_MODEL), hs_ref[...])
    y = _rms(h + g2_ref[0] * ff, gf_ref[...])

    @pl.when(is_prompt)
    def _():
        op_ref[...] = y.reshape(op_ref.shape)

    @pl.when(jnp.logical_not(is_prompt))
    def _():
        os_ref[...] = y


def _final(seg_n, seg_glob, seg_local, seg_tot, h_p, h_s, pg_t, modpat, g_final, ys, tiles_p, tiles_s):
    hp_spec, hs_spec = _tile_specs(h_p.shape[0], tiles_p, tiles_s)
    grid_spec = pltpu.PrefetchScalarGridSpec(
        num_scalar_prefetch=4,
        grid=(tiles_p + tiles_s,),
        in_specs=[hp_spec, hs_spec,
                  pl.BlockSpec((ROWS, 2 * TOP_K), lambda j, *_: (j, 0)),
                  _mod_spec(5, tiles_p), _full((1, D_MODEL)),
                  pl.BlockSpec(memory_space=pl.ANY)],
        out_specs=[hp_spec, hs_spec],
        scratch_shapes=[pltpu.VMEM((2, SORT_ROWS, PACKED), U32), pltpu.SemaphoreType.DMA((2,))],
    )
    return pl.pallas_call(
        functools.partial(_fin_body, tiles_p),
        grid_spec=grid_spec,
        out_shape=[jax.ShapeDtypeStruct(h_p.shape, F32), jax.ShapeDtypeStruct(h_s.shape, F32)],
        compiler_params=_params("arbitrary"),
        name="combine_final",
    )(seg_n, seg_glob, seg_local, seg_tot, h_p, h_s, pg_t, modpat, g_final, ys)


def _block_diag(w):
    h, i, j = w.shape
    return jnp.einsum('hij,hk->hikj', w, jnp.eye(h, dtype=w.dtype)).reshape(h * i, h * j)


def _s5_cols(re, im):
    b = re.shape[0]
    return jnp.concatenate([re.reshape(b, -1), im.reshape(b, -1)], axis=1)


def _s5_uncols(cols):
    b = cols.shape[0]
    return (cols[:, :S5_COLS // 2].reshape(b, S5_GROUPS, S5_STATE),
            cols[:, S5_COLS // 2:].reshape(b, S5_GROUPS, S5_STATE))


def _moe_rows_bound(tiles):
    worst = tiles * (TOP_K * ROWS + N_EXPERTS * (SEG_ALIGN - 1)) + N_EXPERTS * (MOE_TM - SEG_ALIGN)
    return (worst + MOE_TM - 1) // MOE_TM * MOE_TM


def _plan(cnt):
    cnt = cnt.astype(jnp.int32)
    tiles = cnt.shape[0]
    cp = (cnt + SEG_ALIGN - 1) // SEG_ALIGN * SEG_ALIGN
    local = jnp.cumsum(cp, axis=1) - cp
    group = jnp.sum(cp, axis=0)
    group_pad = (group + MOE_TM - 1) // MOE_TM * MOE_TM
    pend = jnp.cumsum(group_pad)
    pstart = pend - group_pad
    glob = pstart[None, :] + jnp.cumsum(cp, axis=0) - cp
    gap = group_pad - group
    seg_n = jnp.concatenate([cp, gap[None]], axis=0).reshape(-1)
    seg_local = jnp.concatenate([local, jnp.zeros((1, N_EXPERTS), jnp.int32)], axis=0).reshape(-1)
    seg_glob = jnp.concatenate([glob, (pstart + group)[None]], axis=0).reshape(-1)
    seg_tot = jnp.concatenate([jnp.sum(cp, axis=1), jnp.sum(gap)[None]]).astype(jnp.int32)
    n_blocks = _moe_rows_bound(tiles) // MOE_TM
    block_row0 = jnp.arange(n_blocks, dtype=jnp.int32) * MOE_TM
    block_expert = jnp.minimum(jnp.sum(block_row0[:, None] >= pend[None, :], axis=1), N_EXPERTS - 1).astype(jnp.int32)
    nb_used = (pend[-1] // MOE_TM).astype(jnp.int32).reshape(1)
    experts = jnp.arange(N_EXPERTS, dtype=jnp.int32)
    later_owner = jnp.where((experts[None, :] > experts[:, None]) & (group_pad[None, :] > 0), experts[None, :],
                            N_EXPERTS)
    next_owner = jnp.min(later_owner, axis=1)
    next_owner = jnp.where(next_owner == N_EXPERTS, -1, next_owner).astype(jnp.int32)
    owner = block_expert[:, None] == experts[None, :]
    block_next = jnp.sum(jnp.where(owner, next_owner[None, :], 0), axis=1).astype(jnp.int32)
    group_end = jnp.sum(jnp.where(owner, (pstart + group)[None, :], 0), axis=1)
    block_rows = jnp.clip(group_end - block_row0, 0, MOE_TM).astype(jnp.int32)
    return seg_n, seg_glob, seg_local, seg_tot, block_expert, block_next, block_rows, nb_used


def kernel(x_prompt, x_sample, state_s5_re, state_s5_im, state_lru_h, state_conv, c_prompt, c_sample, w_ada, b_ada, g_mix, w_in, s5_a_re, s5_a_im, s5_log_dt, s5_b_re, s5_b_im, s5_c_re, s5_c_im, s5_d, s5_w_glu, s5_b_glu, lru_conv_w, lru_conv_b, lru_w_a, lru_b_a, lru_w_x, lru_b_x, lru_lambda, g_out_s5, g_out_lru, w_out, g_ffn, router_w, router_b, moe_w_gu, moe_b_gu, moe_w_down, moe_b_down, g_final):
    assert w_ada.shape[0] == 1, "one layer"
    bp, lp, _ = x_prompt.shape
    bs, ls, _ = x_sample.shape
    assert ROWS % bp == 0 and ROWS % bs == 0 and (bp * lp) % ROWS == 0 and (bs * ls) % ROWS == 0
    tiles_p = bp * lp // ROWS
    tiles_s = bs * ls // ROWS
    row = lambda v: v.reshape(1, -1)

    ab_re, ab_im, bb_re, bb_im = _s5_prep(s5_a_re[0], s5_a_im[0], s5_log_dt[0].reshape(S5_GROUPS, 1),
                                          jnp.swapaxes(s5_b_re[0], 1, 2), jnp.swapaxes(s5_b_im[0], 1, 2))
    gpb = S5_GROUPS // S5_BLOCKS
    eye = jnp.eye(gpb, dtype=F32)

    def in_blocks(b):
        b = b.reshape(S5_BLOCKS, gpb, S5_GROUP_CH, S5_STATE)
        return jnp.einsum('bgjn,gh->bgjhn', b, eye).reshape(S5_BLOCKS, gpb * S5_GROUP_CH, gpb * S5_STATE)

    def out_blocks(c):
        c = c.reshape(S5_BLOCKS, gpb, S5_GROUP_CH, S5_STATE)
        return jnp.einsum('bgjn,gh->bgnhj', c, eye).reshape(S5_BLOCKS, gpb * S5_STATE, gpb * S5_GROUP_CH)

    ar8 = jnp.broadcast_to(ab_re.reshape(1, -1), (SUBLANES, S5_GROUPS * S5_STATE))
    ai8 = jnp.broadcast_to(ab_im.reshape(1, -1), (SUBLANES, S5_GROUPS * S5_STATE))
    bb = jnp.concatenate([in_blocks(bb_re), in_blocks(bb_im)], axis=-1).astype(BF16)
    cc = jnp.concatenate([out_blocks(s5_c_re[0]), -out_blocks(s5_c_im[0])], axis=1).astype(BF16)
    wa_bd = _block_diag(lru_w_a[0]).astype(BF16)
    wx_bd = _block_diag(lru_w_x[0]).astype(BF16)
    rw_t = router_w[0].T
    rw_hi = rw_t.astype(BF16)
    rw_lo = (rw_t - rw_hi.astype(F32)).astype(BF16)

    tc = ROWS // bp
    modpat = _adaln(jnp.concatenate([c_prompt, c_sample], axis=0), w_ada[0], row(b_ada[0]), bp)
    r = jnp.arange(ROWS)
    tm_of = (r % tc) * bp + r // tc
    to_tm = (r[:, None] == tm_of[None, :]).astype(BF16)
    from_tm = to_tm.T
    pair = 2 if tiles_p % 2 == 0 else 1

    def conv_tm(cv):
        return jnp.swapaxes(cv, 0, 1).reshape(-1, D_LRU)

    def conv_bm(cv, b):
        return jnp.swapaxes(cv.reshape(CONV_WIDTH - 1, b, D_LRU), 0, 1)

    x_s = jnp.swapaxes(x_sample, 0, 1).reshape(bs * ls, D_MODEL)
    in_args = (row(g_mix[0]), w_in[0].astype(BF16))
    lru_args = (lru_conv_w[0], row(lru_conv_b[0]), wa_bd, row(lru_b_a[0]), wx_bd, row(lru_b_x[0]),
                row(lru_lambda[0]))
    u_p, la_p, lb_p, lg_p, cvp = _in_proj(x_prompt, modpat, 0, to_tm, *in_args,
                                          jnp.zeros(((CONV_WIDTH - 1) * bp, D_LRU), F32), *lru_args, bp, pair)
    u_s, la_s, lb_s, lg_s, cvs = _in_proj(x_s, modpat, 1, None, *in_args, conv_tm(state_conv[0]), *lru_args, bs, 1)

    s5_args = (ar8, ai8, bb, cc, row(s5_d[0]), s5_w_glu[0].astype(BF16), row(s5_b_glu[0]), row(g_out_s5[0]))
    ys5_p, s5p, ylru_p, hp = _mixers(u_p, jnp.zeros((bp, S5_COLS), F32), *s5_args, la_p, lb_p, lg_p,
                                     jnp.zeros((bp, D_LRU), F32), row(g_out_lru[0]), bp, pair)
    ys5_s, s5s, ylru_s, hs = _mixers(u_s, _s5_cols(state_s5_re[0], state_s5_im[0]), *s5_args, la_s, lb_s, lg_s,
                                     state_lru_h[0], row(g_out_lru[0]), bs, 1)

    out_args = (w_out[0].astype(BF16).reshape(2, D_S5, D_MODEL), row(g_ffn[0]), rw_hi, rw_lo,
                router_b[0].reshape(N_EXPERTS, 1))
    h_p, xn_p, pg_p, cnt_p = _out_proj(ys5_p, ylru_p, x_prompt, modpat, 0, from_tm, *out_args, pair)
    h_s, xn_s, pg_s, cnt_s = _out_proj(ys5_s, ylru_s, x_s, modpat, 1, None, *out_args, 1)
    pg = jnp.concatenate([pg_p, pg_s], axis=1)
    cnt = jnp.concatenate([cnt_p, cnt_s], axis=0)

    seg_n, seg_glob, seg_local, seg_tot, block_expert, block_next, block_rows, nb_used = _plan(cnt[:, :, 0])
    xs = _dispatch(seg_n, seg_glob, seg_local, seg_tot, nb_used, pg, xn_p, xn_s,
                   _moe_rows_bound(tiles_p + tiles_s))
    ys = _moe(xs, block_expert, block_next, block_rows, nb_used, moe_w_gu[0], moe_b_gu[0].reshape(N_EXPERTS, 1, 2 * D_FF),
              moe_w_down[0], moe_b_down[0].reshape(N_EXPERTS, 1, D_MODEL))
    y_prompt, y_s = _final(seg_n, seg_glob, seg_local, seg_tot, h_p, h_s, pg.T, modpat, row(g_final), ys,
                           tiles_p, tiles_s)
    y_sample = jnp.swapaxes(y_s.reshape(ls, bs, D_MODEL), 0, 1)
    s5p_re, s5p_im = _s5_uncols(s5p)
    s5s_re, s5s_im = _s5_uncols(s5s)
    return (y_prompt, y_sample,
            s5p_re[None], s5p_im[None], hp[None], conv_bm(cvp, bp)[None],
            s5s_re[None], s5s_im[None], hs[None], conv_bm(cvs, bs)[None])
```

```python
import functools

import jax
import jax.numpy as jnp
from jax import lax
from jax.experimental import pallas as pl
from jax.experimental.pallas import tpu as pltpu

D_MODEL = 1024
D_S5 = 512
D_LRU = 512
S5_GROUPS = 32
S5_GROUP_CH = 16
S5_STATE = 64
S5_COLS = 2 * S5_GROUPS * S5_STATE
S5_BLOCKS = 4
LRU_C = 8.0
CONV_WIDTH = 4
N_EXPERTS = 32
TOP_K = 4
D_FF = 1024
SWIGLU_LIMIT = 7.0
SWIGLU_ALPHA = 1.702
N_MOD = 6
EPS = 1e-6

SUBLANES = 8
LANES = 128
VMEM_LIMIT = 48 * 1024 * 1024

ROWS = 512
MOE_TM = 512
MOE_PART = 128
PACKED = D_MODEL // 2
SEG_ALIGN = SUBLANES
SORT_CHUNK = 256
SORT_ROWS = -(-(TOP_K * ROWS + N_EXPERTS * (SEG_ALIGN - 1)) // SORT_CHUNK) * SORT_CHUNK
WAIT_MAX_PIECE = N_EXPERTS * MOE_TM // 2

BF16 = jnp.bfloat16
F32 = jnp.float32
U32 = jnp.uint32


def _params(*sem):
    return pltpu.CompilerParams(dimension_semantics=sem, vmem_limit_bytes=VMEM_LIMIT)


def _full(shape):
    return pl.BlockSpec(shape, lambda *_: (0,) * len(shape))


def _rms(x, g):
    return x * lax.rsqrt(jnp.mean(x * x, axis=-1, keepdims=True) + EPS) * g


def _gelu(x):
    return 0.5 * x * (1.0 + lax.erf(x * (2.0 ** -0.5)))


def _expm1(x):
    u = jnp.exp(x)
    d = u - 1.0
    return jnp.where(d == 0.0, x, jnp.where(d == -1.0, -1.0, d * x / jnp.log(u)))


def _pack_rows(x):
    lo = lax.shift_right_logical(lax.bitcast_convert_type(x[:, :PACKED], U32), jnp.uint32(16))
    hi = lax.bitcast_convert_type(x[:, PACKED:], U32) & jnp.uint32(0xFFFF0000)
    return lo | hi


def _unpack_rows(p):
    lo = lax.bitcast_convert_type(lax.shift_left(p, jnp.uint32(16)), F32)
    hi = lax.bitcast_convert_type(p & jnp.uint32(0xFFFF0000), F32)
    return jnp.concatenate([lo, hi], axis=-1).astype(BF16)


def _row_pieces(n, largest, fn):
    off = 0
    bit = largest
    while bit >= SEG_ALIGN:
        @pl.when((n & bit) != 0)
        def _(off=off, bit=bit):
            fn(off, bit)
        off = off + (n & bit)
        bit //= 2


def _mod_body(bp, c_ref, w_ref, b_ref, o_ref):
    c = c_ref[...]
    s = (c * jax.nn.sigmoid(c)).astype(BF16)
    mod = jnp.dot(s, w_ref[...].astype(BF16), preferred_element_type=F32) + b_ref[...]
    bs = mod.shape[0] - bp
    o_ref[0] = jnp.broadcast_to(mod[:bp][:, None, :], (bp, ROWS // bp, D_MODEL)).reshape(ROWS, D_MODEL)
    o_ref[1] = jnp.broadcast_to(mod[bp:][None], (ROWS // bs, bs, D_MODEL)).reshape(ROWS, D_MODEL)


def _adaln(c, w_ada, b_ada, bp):
    m = c.shape[0]
    return pl.pallas_call(
        functools.partial(_mod_body, bp),
        grid=(N_MOD,),
        in_specs=[pl.BlockSpec((m, D_MODEL), lambda j: (0, 0)),
                  pl.BlockSpec((D_MODEL, D_MODEL), lambda j: (0, j)),
                  pl.BlockSpec((1, D_MODEL), lambda j: (0, j))],
        out_specs=pl.BlockSpec((2, ROWS, D_MODEL), lambda j: (0, 0, j)),
        out_shape=jax.ShapeDtypeStruct((2, ROWS, N_MOD * D_MODEL), F32),
        compiler_params=_params("arbitrary"),
        name="adaln",
    )(c, w_ada, b_ada)


def _mod_spec(k, tiles_p):
    return pl.BlockSpec((1, ROWS, D_MODEL), lambda i, *_: (jnp.where(i < tiles_p, 0, 1), 0, k))


def _tile_specs(bp, tiles_p, tiles_s):
    tc = ROWS // bp
    p_spec = pl.BlockSpec((bp, tc, D_MODEL), lambda i, *_: (0, jnp.minimum(i, tiles_p - 1), 0))
    s_spec = pl.BlockSpec((ROWS, D_MODEL), lambda i, *_: (jnp.clip(i - tiles_p, 0, tiles_s - 1), 0))
    return p_spec, s_spec


def _in_body(subtiles, bt, to_tm, *refs):
    if to_tm:
        x_ref, sc_ref, sh_ref, perm_ref, *refs = refs
    else:
        x_ref, sc_ref, sh_ref, *refs = refs
    (g_ref, w_ref, cv0_ref, cw_ref, cb_ref, wa_ref, ba_ref, wx_ref, bx_ref, lam_ref,
     u_ref, a_ref, b_ref, gg_ref, cvout_ref, xp_ref) = refs
    halo = (CONV_WIDTH - 1) * bt

    @pl.when(pl.program_id(0) == 0)
    def _():
        xp_ref[...] = cv0_ref[...]

    lam = lam_ref[...]
    softplus_neg_lam = jnp.maximum(-lam, 0.0) + jnp.log1p(jnp.exp(-jnp.abs(lam)))
    earlier = xp_ref[...]
    for c in range(subtiles):
        rows = slice(c * ROWS, (c + 1) * ROWS)
        if to_tm:
            tc = x_ref.shape[1] // subtiles
            x = x_ref[:, c * tc:(c + 1) * tc, :].reshape(ROWS, D_MODEL)
        else:
            x = x_ref[rows, :]
        xn = (_rms(x, g_ref[...]) * (1.0 + sc_ref[0]) + sh_ref[0]).astype(BF16)
        if to_tm:
            xn = jnp.dot(perm_ref[...], xn, preferred_element_type=F32).astype(BF16)
        p = jnp.dot(xn, w_ref[...], preferred_element_type=F32)
        u_ref[rows, :] = p[:, :D_S5]
        gg_ref[rows, :] = _gelu(p[:, D_S5 + D_LRU:])

        xr = p[:, D_S5:D_S5 + D_LRU]
        xp = jnp.concatenate([earlier, xr], axis=0)
        earlier = xr[ROWS - halo:, :]
        xc = cb_ref[...] + sum(xp[k * bt:k * bt + ROWS, :] * cw_ref[k:k + 1, :] for k in range(CONV_WIDTH))
        xcb = xc.astype(BF16)
        r = jax.nn.sigmoid(jnp.dot(xcb, wa_ref[...], preferred_element_type=F32) + ba_ref[...])
        i = jax.nn.sigmoid(jnp.dot(xcb, wx_ref[...], preferred_element_type=F32) + bx_ref[...])
        log_a = -LRU_C * r * softplus_neg_lam
        a_ref[rows, :] = jnp.exp(log_a)
        b_ref[rows, :] = jnp.sqrt(-_expm1(2.0 * log_a)) * (i * xc)

    xp_ref[...] = earlier
    cvout_ref[...] = earlier


def _in_proj(x, modpat, pattern, to_tm, g_mix, w_in_bf, conv0_tm, conv_w, conv_b, wa_bd, b_a, wx_bd, b_x, lam, bt,
             subtiles):
    tm = to_tm is not None
    t_rows = x.shape[0] * x.shape[1] if tm else x.shape[0]
    step_rows = subtiles * ROWS
    halo = (CONV_WIDTH - 1) * bt
    if tm:
        x_spec = pl.BlockSpec((x.shape[0], step_rows // x.shape[0], D_MODEL), lambda i: (0, i, 0))
    else:
        x_spec = pl.BlockSpec((step_rows, D_MODEL), lambda i: (i, 0))
    mod_spec = lambda k: pl.BlockSpec((1, ROWS, D_MODEL), lambda i: (pattern, 0, k))
    row_spec = pl.BlockSpec((step_rows, D_S5), lambda i: (i, 0))
    vec = _full((1, D_LRU))
    return pl.pallas_call(
        functools.partial(_in_body, subtiles, bt, tm),
        grid=(t_rows // step_rows,),
        in_specs=[x_spec, mod_spec(1), mod_spec(0), *([_full((ROWS, ROWS))] if tm else []), _full((1, D_MODEL)),
                  _full((D_MODEL, D_S5 + 2 * D_LRU)), _full((halo, D_LRU)), _full((CONV_WIDTH, D_LRU)), vec,
                  _full((D_LRU, D_LRU)), vec, _full((D_LRU, D_LRU)), vec, vec],
        out_specs=[row_spec, row_spec, row_spec, row_spec, _full((halo, D_LRU))],
        out_shape=[jax.ShapeDtypeStruct((t_rows, D_S5), F32)] * 4 + [jax.ShapeDtypeStruct((halo, D_LRU), F32)],
        scratch_shapes=[pltpu.VMEM((halo, D_LRU), F32)],
        compiler_params=_params("arbitrary"),
        name="in_proj",
    )(x, modpat, modpat, *([to_tm] if tm else []), g_mix, w_in_bf, conv0_tm, conv_w, conv_b, wa_bd, b_a, wx_bd, b_x,
      lam)


def _s5_prep_body(are_ref, aim_ref, ldt_ref, bre_ref, bim_ref, abre_ref, abim_ref, bbre_ref, bbim_ref):
    a_re = are_ref[...]
    a_im = aim_ref[...]
    dt = jnp.exp(ldt_ref[...])
    mag = jnp.exp(dt * a_re)
    ang = dt * a_im
    ab_re = mag * jnp.cos(ang)
    ab_im = mag * jnp.sin(ang)
    den = a_re * a_re + a_im * a_im
    q_re = ((ab_re - 1.0) * a_re + ab_im * a_im) / den
    q_im = (ab_im * a_re - (ab_re - 1.0) * a_im) / den
    abre_ref[...] = ab_re
    abim_ref[...] = ab_im
    b_re = bre_ref[...]
    b_im = bim_ref[...]
    bbre_ref[...] = q_re[:, None, :] * b_re - q_im[:, None, :] * b_im
    bbim_ref[...] = q_re[:, None, :] * b_im + q_im[:, None, :] * b_re


def _s5_prep(a_re, a_im, log_dt, b_re_t, b_im_t):
    gn = jax.ShapeDtypeStruct((S5_GROUPS, S5_STATE), F32)
    gjn = jax.ShapeDtypeStruct((S5_GROUPS, S5_GROUP_CH, S5_STATE), F32)
    return pl.pallas_call(_s5_prep_body, out_shape=[gn, gn, gjn, gjn], name="s5_prep")(
        a_re, a_im, log_dt, b_re_t, b_im_t)


def _s5_body(tc, bt, chunks, u_ref, s0_ref, ar_ref, ai_ref, bb_ref, cc_ref, d_ref, wg_ref, bg_ref, go_ref,
             la_ref, lb_ref, lg_ref, lh0_ref, lgo_ref, y_ref, sout_ref, yl_ref, lhout_ref,
             bu_ref, st_ref, lh_ref, lst_ref):
    half = S5_COLS // S5_BLOCKS // 2
    re = [slice(half * j, half * (j + 1)) for j in range(S5_BLOCKS)]
    im = [slice(S5_COLS // 2 + half * j, S5_COLS // 2 + half * (j + 1)) for j in range(S5_BLOCKS)]

    @pl.when(pl.program_id(0) == 0)
    def _():
        st_ref[...] = s0_ref[...]
        lst_ref[...] = lh0_ref[...]

    for c in range(chunks):
        ub = u_ref[c * ROWS:(c + 1) * ROWS, :].astype(BF16)
        for j in range(S5_BLOCKS):
            bu = jnp.dot(ub[:, LANES * j:LANES * (j + 1)], bb_ref[j], preferred_element_type=F32)
            bu_ref[c, :, re[j]] = bu[:, :half]
            bu_ref[c, :, im[j]] = bu[:, half:]

    for sb in range(bt // SUBLANES):
        base = sb * SUBLANES
        hr = [st_ref[base:base + SUBLANES, re[j]] for j in range(S5_BLOCKS)]
        hi = [st_ref[base:base + SUBLANES, im[j]] for j in range(S5_BLOCKS)]
        hl = lst_ref[base:base + SUBLANES, :]
        for c in range(chunks):
            for t in range(tc):
                r0 = t * bt + base
                row = c * ROWS + r0
                hl = la_ref[row:row + SUBLANES, :] * hl + lb_ref[row:row + SUBLANES, :]
                lh_ref[row:row + SUBLANES, :] = hl
                for j in range(S5_BLOCKS):
                    ar = ar_ref[:, re[j]]
                    ai = ai_ref[:, re[j]]
                    h_re = ar * hr[j] - ai * hi[j] + bu_ref[c, r0:r0 + SUBLANES, re[j]]
                    h_im = ar * hi[j] + ai * hr[j] + bu_ref[c, r0:r0 + SUBLANES, im[j]]
                    bu_ref[c, r0:r0 + SUBLANES, re[j]] = h_re
                    bu_ref[c, r0:r0 + SUBLANES, im[j]] = h_im
                    hr[j], hi[j] = h_re, h_im
        for j in range(S5_BLOCKS):
            st_ref[base:base + SUBLANES, re[j]] = hr[j]
            st_ref[base:base + SUBLANES, im[j]] = hi[j]
        lst_ref[base:base + SUBLANES, :] = hl
    sout_ref[...] = st_ref[...]
    lhout_ref[...] = lst_ref[...]
    yl_ref[...] = _rms(lh_ref[...] * lg_ref[...], lgo_ref[...]).astype(BF16)

    for c in range(chunks):
        y = jnp.concatenate(
            [jnp.dot(jnp.concatenate([bu_ref[c, :, re[j]], bu_ref[c, :, im[j]]], axis=-1).astype(BF16), cc_ref[j],
                     preferred_element_type=F32) for j in range(S5_BLOCKS)], axis=-1)
        y = y + d_ref[...] * u_ref[c * ROWS:(c + 1) * ROWS, :]
        g = _gelu(y)
        z = jnp.dot(g.astype(BF16), wg_ref[...], preferred_element_type=F32) + bg_ref[...]
        out = g * jax.nn.sigmoid(z)
        y_ref[c * ROWS:(c + 1) * ROWS, :] = _rms(out, go_ref[...]).astype(BF16)


def _mixers(u, s0, ar8, ai8, bb, cc, d, w_glu_bf, b_glu, g_out_s5, lru_a, lru_b, lru_gg, lru_h0, g_out_lru, bt,
            chunks):
    tc = ROWS // bt
    t_rows = u.shape[0]
    step_rows = chunks * ROWS
    assert t_rows % step_rows == 0
    rows_f32 = pl.BlockSpec((step_rows, D_S5), lambda i: (i, 0))
    vec = _full((1, D_S5))
    return pl.pallas_call(
        functools.partial(_s5_body, tc, bt, chunks),
        grid=(t_rows // step_rows,),
        in_specs=[rows_f32, _full((bt, S5_COLS)), _full(ar8.shape), _full(ai8.shape), _full(bb.shape),
                  _full(cc.shape), vec, _full((D_S5, D_S5)), vec, vec,
                  rows_f32, rows_f32, rows_f32, _full((bt, D_LRU)), vec],
        out_specs=[rows_f32, _full((bt, S5_COLS)), rows_f32, _full((bt, D_LRU))],
        out_shape=[jax.ShapeDtypeStruct((t_rows, D_S5), BF16), jax.ShapeDtypeStruct((bt, S5_COLS), F32),
                   jax.ShapeDtypeStruct((t_rows, D_LRU), BF16), jax.ShapeDtypeStruct((bt, D_LRU), F32)],
        scratch_shapes=[pltpu.VMEM((chunks, ROWS, S5_COLS), F32), pltpu.VMEM((bt, S5_COLS), F32),
                        pltpu.VMEM((step_rows, D_LRU), F32), pltpu.VMEM((bt, D_LRU), F32)],
        compiler_params=_params("arbitrary"),
        name="s5_lru_mixers",
    )(u, s0, ar8, ai8, bb, cc, d, w_glu_bf, b_glu, g_out_s5, lru_a, lru_b, lru_gg, lru_h0, g_out_lru)


def _route_tile(xn, x_hi, rw_hi, rw_lo, rb):
    x_lo = (xn - x_hi.astype(F32)).astype(BF16)
    nt = (((1,), (1,)), ((), ()))
    logits = (lax.dot_general(rw_hi, x_hi, nt, preferred_element_type=F32)
              + lax.dot_general(rw_hi, x_lo, nt, preferred_element_type=F32)
              + lax.dot_general(rw_lo, x_hi, nt, preferred_element_type=F32)) + rb

    e_iota = lax.broadcasted_iota(jnp.int32, (N_EXPERTS, ROWS), 0).astype(F32)
    work = logits
    sels, vals = [], []
    for _ in range(TOP_K):
        m = jnp.max(work, axis=0, keepdims=True)
        idx = jnp.min(jnp.where(work == m, e_iota, float(N_EXPERTS)), axis=0, keepdims=True)
        sel = e_iota == idx
        work = jnp.where(sel, -jnp.inf, work)
        sels.append(sel)
        vals.append(m)
    exps = [jnp.exp(v - vals[0]) for v in vals]
    denom = exps[0] + exps[1] + exps[2] + exps[3]
    gates = [e / denom for e in exps]

    onehot = sels[0] | sels[1] | sels[2] | sels[3]
    rr = lax.broadcasted_iota(jnp.int32, (ROWS, ROWS), 0)
    cc = lax.broadcasted_iota(jnp.int32, (ROWS, ROWS), 1)
    before = (rr < cc).astype(BF16)
    prefix = jnp.dot(onehot.astype(BF16), before, preferred_element_type=F32)
    cnt = jnp.sum(onehot.astype(F32), axis=1, keepdims=True)
    cnt_pad = jnp.floor((cnt + (SEG_ALIGN - 1)) * (1.0 / SEG_ALIGN)) * SEG_ALIGN
    er = lax.broadcasted_iota(jnp.int32, (N_EXPERTS, N_EXPERTS), 0)
    ec = lax.broadcasted_iota(jnp.int32, (N_EXPERTS, N_EXPERTS), 1)
    seg_start = jnp.dot((ec < er).astype(BF16), jnp.broadcast_to(cnt_pad, (N_EXPERTS, LANES)).astype(BF16),
                        preferred_element_type=F32)[:, 0:1]
    where_to = prefix + seg_start
    poss = [jnp.sum(jnp.where(s, where_to, 0.0), axis=0, keepdims=True) for s in sels]

    s_iota = lax.broadcasted_iota(jnp.int32, (2 * TOP_K, ROWS), 0)
    pg = jnp.zeros((2 * TOP_K, ROWS), F32)
    for k in range(TOP_K):
        pg = jnp.where(s_iota == k, poss[k], pg)
        pg = jnp.where(s_iota == TOP_K + k, gates[k], pg)
    return pg, cnt


def _out_body(subtiles, tm_rows, *refs):
    if tm_rows:
        ys_ref, yl_ref, x_ref, g1_ref, sc_ref, sh_ref, perm_ref, *refs = refs
    else:
        ys_ref, yl_ref, x_ref, g1_ref, sc_ref, sh_ref, *refs = refs
    wo_ref, gf_ref, rwh_ref, rwl_ref, rb_ref, h_ref, xn_ref, pg_ref, cnt_ref = refs
    for c in range(subtiles):
        rows = slice(c * ROWS, (c + 1) * ROWS)
        ys = ys_ref[rows, :]
        yl = yl_ref[rows, :]
        if tm_rows:
            ys = jnp.dot(perm_ref[...], ys, preferred_element_type=F32).astype(BF16)
            yl = jnp.dot(perm_ref[...], yl, preferred_element_type=F32).astype(BF16)
            tc = x_ref.shape[1] // subtiles
            x = x_ref[:, c * tc:(c + 1) * tc, :].reshape(ROWS, D_MODEL)
        else:
            x = x_ref[rows, :]
        mixed = (jnp.dot(ys, wo_ref[0], preferred_element_type=F32)
                 + jnp.dot(yl, wo_ref[1], preferred_element_type=F32))
        h = x + g1_ref[0] * mixed
        if tm_rows:
            h_ref[:, c * tc:(c + 1) * tc, :] = h.reshape(x_ref.shape[0], tc, D_MODEL)
        else:
            h_ref[rows, :] = h
        xn = _rms(h, gf_ref[...]) * (1.0 + sc_ref[0]) + sh_ref[0]
        x_hi = xn.astype(BF16)
        xn_ref[rows, :] = x_hi
        pg, cnt = _route_tile(xn, x_hi, rwh_ref[...], rwl_ref[...], rb_ref[...])
        pg_ref[:, rows] = pg
        cnt_ref[c] = jnp.broadcast_to(cnt, (N_EXPERTS, LANES))


def _out_proj(ys5, ylru, x, modpat, pattern, from_tm, w_out_bf, g_ffn, rw_hi, rw_lo, router_b, subtiles):
    tm_rows = from_tm is not None
    t_rows = ys5.shape[0]
    step_rows = subtiles * ROWS
    half_spec = pl.BlockSpec((step_rows, D_S5), lambda i: (i, 0))
    if tm_rows:
        x_spec = pl.BlockSpec((x.shape[0], step_rows // x.shape[0], D_MODEL), lambda i: (0, i, 0))
    else:
        x_spec = pl.BlockSpec((step_rows, D_MODEL), lambda i: (i, 0))
    mod_spec = lambda k: pl.BlockSpec((1, ROWS, D_MODEL), lambda i: (pattern, 0, k))
    perm = [_full((ROWS, ROWS))] if tm_rows else []
    return pl.pallas_call(
        functools.partial(_out_body, subtiles, tm_rows),
        grid=(t_rows // step_rows,),
        in_specs=[half_spec, half_spec, x_spec, mod_spec(2), mod_spec(4), mod_spec(3), *perm,
                  _full((2, D_S5, D_MODEL)), _full((1, D_MODEL)),
                  _full((N_EXPERTS, D_MODEL)), _full((N_EXPERTS, D_MODEL)), _full((N_EXPERTS, 1))],
        out_specs=[x_spec, pl.BlockSpec((step_rows, D_MODEL), lambda i: (i, 0)),
                   pl.BlockSpec((2 * TOP_K, step_rows), lambda i: (0, i)),
                   pl.BlockSpec((subtiles, N_EXPERTS, LANES), lambda i: (i, 0, 0))],
        out_shape=[jax.ShapeDtypeStruct(x.shape, F32), jax.ShapeDtypeStruct((t_rows, D_MODEL), BF16),
                   jax.ShapeDtypeStruct((2 * TOP_K, t_rows), F32),
                   jax.ShapeDtypeStruct((t_rows // ROWS, N_EXPERTS, LANES), F32)],
        compiler_params=_params("arbitrary"),
        name="out_proj_router",
    )(ys5, ylru, x, modpat, modpat, modpat, *([from_tm] if tm_rows else []), w_out_bf, g_ffn, rw_hi, rw_lo,
      router_b)


def _start_segments(n_ref, hbm_ref, vmem_ref, hbm, vmem_buf, sem, step, to_hbm):
    for e in range(N_EXPERTS):
        n = n_ref[step * N_EXPERTS + e]
        h0 = hbm_ref[step * N_EXPERTS + e]
        v0 = vmem_ref[step * N_EXPERTS + e]

        def piece(off, size, h0=h0, v0=v0):
            h = hbm.at[pl.ds(pl.multiple_of(h0 + off, SEG_ALIGN), size)]
            v = vmem_buf.at[pl.ds(pl.multiple_of(v0 + off, SEG_ALIGN), size)]
            (pltpu.make_async_copy(v, h, sem) if to_hbm else pltpu.make_async_copy(h, v, sem)).start()

        _row_pieces(n, ROWS, piece)


def _wait_rows(total, largest, hbm, sem):
    _row_pieces(total, largest, lambda off, size: pltpu.make_async_copy(
        hbm.at[pl.ds(0, size)], hbm.at[pl.ds(0, size)], sem).wait())


def _dispatch_body(tiles_p, n_ref, glob_ref, local_ref, tot_ref, nb_ref, pg_ref, xp_ref, xsm_ref, xs_hbm, stage,
                   sem):
    j = pl.program_id(0)
    last = pl.num_programs(0) - 1
    slot = j % 2

    def unused_blocks(act):
        def blk(b, carry):
            act(pltpu.make_async_copy(stage.at[slot, pl.ds(0, MOE_TM)],
                                      xs_hbm.at[pl.ds(pl.multiple_of(b * MOE_TM, MOE_TM), MOE_TM)], sem.at[slot]))
            return carry
        lax.fori_loop(nb_ref[0], xs_hbm.shape[0] // MOE_TM, blk, 0)

    def wait_step(step, s):
        _wait_rows(tot_ref[step], WAIT_MAX_PIECE, xs_hbm, sem.at[s])

    @pl.when(j >= 2)
    def _():
        wait_step(jnp.maximum(j - 2, 0), slot)

    @pl.when(j < last)
    def _():
        x = jnp.where(j < tiles_p, xp_ref[...], xsm_ref[...])
        pos = pg_ref[0:TOP_K, :]
        r = lax.broadcasted_iota(jnp.int32, (SORT_CHUNK, ROWS), 0).astype(F32).astype(BF16)
        for c in range(SORT_ROWS // SORT_CHUNK):
            rel = (pos - float(c * SORT_CHUNK)).astype(BF16)
            pick = (r == rel[0:1]) | (r == rel[1:2]) | (r == rel[2:3]) | (r == rel[3:4])
            stage[slot, c * SORT_CHUNK:(c + 1) * SORT_CHUNK, :] = _pack_rows(jnp.dot(
                jnp.where(pick, jnp.ones((), BF16), jnp.zeros((), BF16)), x, preferred_element_type=F32))

    @pl.when(j == last)
    def _():
        stage[slot, 0:MOE_TM, :] = jnp.zeros((MOE_TM, PACKED), U32)

    _start_segments(n_ref, glob_ref, local_ref, xs_hbm, stage.at[slot], sem.at[slot], j, True)

    @pl.when(j == last)
    def _():
        unused_blocks(lambda c: c.start())

        @pl.when(j >= 1)
        def _():
            wait_step(jnp.maximum(j - 1, 0), 1 - slot)
        wait_step(j, slot)
        unused_blocks(lambda c: c.wait())


def _dispatch(seg_n, seg_glob, seg_local, seg_tot, nb_used, pg, xn_p, xn_s, n_rows):
    tiles_p = xn_p.shape[0] // ROWS
    tiles_s = xn_s.shape[0] // ROWS
    tiles = tiles_p + tiles_s
    grid_spec = pltpu.PrefetchScalarGridSpec(
        num_scalar_prefetch=5,
        grid=(tiles + 1,),
        in_specs=[pl.BlockSpec((2 * TOP_K, ROWS), lambda j, *_: (0, jnp.minimum(j, tiles - 1))),
                  pl.BlockSpec((ROWS, D_MODEL), lambda j, *_: (jnp.minimum(j, tiles_p - 1), 0)),
                  pl.BlockSpec((ROWS, D_MODEL), lambda j, *_: (jnp.clip(j - tiles_p, 0, tiles_s - 1), 0))],
        out_specs=pl.BlockSpec(memory_space=pl.ANY),
        scratch_shapes=[pltpu.VMEM((2, SORT_ROWS, PACKED), U32), pltpu.SemaphoreType.DMA((2,))],
    )
    return pl.pallas_call(
        functools.partial(_dispatch_body, tiles_p),
        grid_spec=grid_spec,
        out_shape=jax.ShapeDtypeStruct((n_rows, PACKED), U32),
        compiler_params=_params("arbitrary"),
        name="moe_dispatch",
    )(seg_n, seg_glob, seg_local, seg_tot, nb_used, pg, xn_p, xn_s)


def _moe_body(be_ref, nxt_ref, nv_ref, nb_ref, xs_ref, wgu_hbm, bgu_ref, wd_hbm, bd_ref, ys_ref,
              wgu_f32, wd_f32, wgu_bf, wd_bf, sem):
    i = pl.program_id(0)

    def weight_copies(e):
        return (pltpu.make_async_copy(wgu_hbm.at[e], wgu_f32, sem.at[0]),
                pltpu.make_async_copy(wd_hbm.at[e], wd_f32, sem.at[1]))

    @pl.when(i >= nb_ref[0])
    def _():
        ys_ref[...] = jnp.zeros_like(ys_ref)

    @pl.when(i < nb_ref[0])
    def _():
        e = be_ref[i]

        @pl.when(i == 0)
        def _():
            for c in weight_copies(e):
                c.start()

        @pl.when(jnp.logical_or(i == 0, e != be_ref[jnp.maximum(i - 1, 0)]))
        def _():
            for c in weight_copies(e):
                c.wait()
            wgu_bf[...] = wgu_f32[...].astype(BF16)
            wd_bf[...] = wd_f32[...].astype(BF16)

            @pl.when(nxt_ref[i] >= 0)
            def _():
                for c in weight_copies(nxt_ref[i]):
                    c.start()

        def ffn(packed):
            hg = jnp.dot(_unpack_rows(packed), wgu_bf[...], preferred_element_type=F32) + bgu_ref[0]
            gate = jnp.minimum(hg[:, :D_FF], SWIGLU_LIMIT)
            up = jnp.clip(hg[:, D_FF:], -SWIGLU_LIMIT, SWIGLU_LIMIT)
            act = (up + 1.0) * (gate * jax.nn.sigmoid(SWIGLU_ALPHA * gate))
            y = jnp.dot(act.astype(BF16), wd_bf[...], preferred_element_type=F32) + bd_ref[0]
            return _pack_rows(y.astype(BF16).astype(F32))

        for parts in range(1, MOE_TM // MOE_PART + 1):
            lo = (parts - 1) * MOE_PART
            last_path = parts == MOE_TM // MOE_PART

            @pl.when(jnp.logical_and(nv_ref[i] > lo, jnp.logical_or(last_path, nv_ref[i] <= lo + MOE_PART)))
            def _(parts=parts):
                rows = parts * MOE_PART
                ys_ref[0:rows, :] = ffn(xs_ref[0:rows, :])
                if rows < MOE_TM:
                    ys_ref[rows:, :] = jnp.zeros((MOE_TM - rows, PACKED), U32)


def _moe(xs, block_expert, block_next, block_rows, nb_used, w_gu, b_gu, w_down, b_down):
    n_blocks = xs.shape[0] // MOE_TM
    in_rows = pl.BlockSpec((MOE_TM, PACKED), lambda i, be, nx, nv, nb: (jnp.minimum(i, nb[0] - 1), 0))
    grid_spec = pltpu.PrefetchScalarGridSpec(
        num_scalar_prefetch=4,
        grid=(n_blocks,),
        in_specs=[in_rows,
                  pl.BlockSpec(memory_space=pl.ANY),
                  pl.BlockSpec((1, 1, 2 * D_FF), lambda i, be, nx, nv, nb: (be[i], 0, 0)),
                  pl.BlockSpec(memory_space=pl.ANY),
                  pl.BlockSpec((1, 1, D_MODEL), lambda i, be, nx, nv, nb: (be[i], 0, 0))],
        out_specs=pl.BlockSpec((MOE_TM, PACKED), lambda i, be, nx, nv, nb: (i, 0)),
        scratch_shapes=[pltpu.VMEM((D_MODEL, 2 * D_FF), F32), pltpu.VMEM((D_FF, D_MODEL), F32),
                        pltpu.VMEM((D_MODEL, 2 * D_FF), BF16), pltpu.VMEM((D_FF, D_MODEL), BF16),
                        pltpu.SemaphoreType.DMA((2,))],
    )
    return pl.pallas_call(
        _moe_body,
        grid_spec=grid_spec,
        out_shape=jax.ShapeDtypeStruct(xs.shape, U32),
        compiler_params=_params("arbitrary"),
        name="moe_experts",
    )(block_expert, block_next, block_rows, nb_used, xs, w_gu, b_gu, w_down, b_down)


def _fin_body(tiles_p, n_ref, glob_ref, local_ref, tot_ref, hp_ref, hs_ref, pg_ref, g2_ref, gf_ref, ys_hbm,
              op_ref, os_ref, ybuf, sem):
    j = pl.program_id(0)
    tiles = pl.num_programs(0)
    slot = j % 2
    is_prompt = j < tiles_p

    def fetch(step, s):
        _start_segments(n_ref, glob_ref, local_ref, ys_hbm, ybuf.at[s], sem.at[s], step, False)

    @pl.when(j == 0)
    def _():
        ybuf[...] = jnp.zeros_like(ybuf)
        fetch(0, 0)

    _wait_rows(tot_ref[j], WAIT_MAX_PIECE, ys_hbm, sem.at[slot])

    @pl.when(j + 1 < tiles)
    def _():
        fetch(jnp.minimum(j + 1, tiles - 1), 1 - slot)

    pg = pg_ref[...]
    ff = jnp.zeros((ROWS, D_MODEL), F32)
    r = lax.broadcasted_iota(jnp.int32, (ROWS, SORT_CHUNK), 1).astype(F32).astype(BF16)
    gates = pg[:, TOP_K:].astype(BF16)
    for c in range(SORT_ROWS // SORT_CHUNK):
        rel = (pg[:, :TOP_K] - float(c * SORT_CHUNK)).astype(BF16)
        w = jnp.zeros((ROWS, SORT_CHUNK), BF16)
        for k in range(TOP_K):
            w = jnp.where(r == rel[:, k:k + 1], gates[:, k:k + 1], w)
        ff = ff + jnp.dot(w, _unpack_rows(ybuf[slot, c * SORT_CHUNK:(c + 1) * SORT_CHUNK, :]),
                          preferred_element_type=F32)
    h = jnp.where(is_prompt, hp_ref[...].reshape(ROWS, D_MODEL), hs_ref[...])
    y = _rms(h + g2_ref[0] * ff, gf_ref[...])

    @pl.when(is_prompt)
    def _():
        op_ref[...] = y.reshape(op_ref.shape)

    @pl.when(jnp.logical_not(is_prompt))
    def _():
        os_ref[...] = y


def _final(seg_n, seg_glob, seg_local, seg_tot, h_p, h_s, pg_t, modpat, g_final, ys, tiles_p, tiles_s):
    hp_spec, hs_spec = _tile_specs(h_p.shape[0], tiles_p, tiles_s)
    grid_spec = pltpu.PrefetchScalarGridSpec(
        num_scalar_prefetch=4,
        grid=(tiles_p + tiles_s,),
        in_specs=[hp_spec, hs_spec,
                  pl.BlockSpec((ROWS, 2 * TOP_K), lambda j, *_: (j, 0)),
                  _mod_spec(5, tiles_p), _full((1, D_MODEL)),
                  pl.BlockSpec(memory_space=pl.ANY)],
        out_specs=[hp_spec, hs_spec],
        scratch_shapes=[pltpu.VMEM((2, SORT_ROWS, PACKED), U32), pltpu.SemaphoreType.DMA((2,))],
    )
    return pl.pallas_call(
        functools.partial(_fin_body, tiles_p),
        grid_spec=grid_spec,
        out_shape=[jax.ShapeDtypeStruct(h_p.shape, F32), jax.ShapeDtypeStruct(h_s.shape, F32)],
        compiler_params=_params("arbitrary"),
        name="combine_final",
    )(seg_n, seg_glob, seg_local, seg_tot, h_p, h_s, pg_t, modpat, g_final, ys)


def _block_diag(w):
    h, i, j = w.shape
    return jnp.einsum('hij,hk->hikj', w, jnp.eye(h, dtype=w.dtype)).reshape(h * i, h * j)


def _s5_cols(re, im):
    b = re.shape[0]
    return jnp.concatenate([re.reshape(b, -1), im.reshape(b, -1)], axis=1)


def _s5_uncols(cols):
    b = cols.shape[0]
    return (cols[:, :S5_COLS // 2].reshape(b, S5_GROUPS, S5_STATE),
            cols[:, S5_COLS // 2:].reshape(b, S5_GROUPS, S5_STATE))


def _moe_rows_bound(tiles):
    worst = tiles * (TOP_K * ROWS + N_EXPERTS * (SEG_ALIGN - 1)) + N_EXPERTS * (MOE_TM - SEG_ALIGN)
    return (worst + MOE_TM - 1) // MOE_TM * MOE_TM


def _plan(cnt):
    cnt = cnt.astype(jnp.int32)
    tiles = cnt.shape[0]
    cp = (cnt + SEG_ALIGN - 1) // SEG_ALIGN * SEG_ALIGN
    local = jnp.cumsum(cp, axis=1) - cp
    group = jnp.sum(cp, axis=0)
    group_pad = (group + MOE_TM - 1) // MOE_TM * MOE_TM
    pend = jnp.cumsum(group_pad)
    pstart = pend - group_pad
    glob = pstart[None, :] + jnp.cumsum(cp, axis=0) - cp
    gap = group_pad - group
    seg_n = jnp.concatenate([cp, gap[None]], axis=0).reshape(-1)
    seg_local = jnp.concatenate([local, jnp.zeros((1, N_EXPERTS), jnp.int32)], axis=0).reshape(-1)
    seg_glob = jnp.concatenate([glob, (pstart + group)[None]], axis=0).reshape(-1)
    seg_tot = jnp.concatenate([jnp.sum(cp, axis=1), jnp.sum(gap)[None]]).astype(jnp.int32)
    n_blocks = _moe_rows_bound(tiles) // MOE_TM
    block_row0 = jnp.arange(n_blocks, dtype=jnp.int32) * MOE_TM
    block_expert = jnp.minimum(jnp.sum(block_row0[:, None] >= pend[None, :], axis=1), N_EXPERTS - 1).astype(jnp.int32)
    nb_used = (pend[-1] // MOE_TM).astype(jnp.int32).reshape(1)
    experts = jnp.arange(N_EXPERTS, dtype=jnp.int32)
    later_owner = jnp.where((experts[None, :] > experts[:, None]) & (group_pad[None, :] > 0), experts[None, :],
                            N_EXPERTS)
    next_owner = jnp.min(later_owner, axis=1)
    next_owner = jnp.where(next_owner == N_EXPERTS, -1, next_owner).astype(jnp.int32)
    owner = block_expert[:, None] == experts[None, :]
    block_next = jnp.sum(jnp.where(owner, next_owner[None, :], 0), axis=1).astype(jnp.int32)
    group_end = jnp.sum(jnp.where(owner, (pstart + group)[None, :], 0), axis=1)
    block_rows = jnp.clip(group_end - block_row0, 0, MOE_TM).astype(jnp.int32)
    return seg_n, seg_glob, seg_local, seg_tot, block_expert, block_next, block_rows, nb_used


def kernel(x_prompt, x_sample, state_s5_re, state_s5_im, state_lru_h, state_conv, c_prompt, c_sample, w_ada, b_ada, g_mix, w_in, s5_a_re, s5_a_im, s5_log_dt, s5_b_re, s5_b_im, s5_c_re, s5_c_im, s5_d, s5_w_glu, s5_b_glu, lru_conv_w, lru_conv_b, lru_w_a, lru_b_a, lru_w_x, lru_b_x, lru_lambda, g_out_s5, g_out_lru, w_out, g_ffn, router_w, router_b, moe_w_gu, moe_b_gu, moe_w_down, moe_b_down, g_final):
    assert w_ada.shape[0] == 1, "one layer"
    bp, lp, _ = x_prompt.shape
    bs, ls, _ = x_sample.shape
    assert ROWS % bp == 0 and ROWS % bs == 0 and (bp * lp) % ROWS == 0 and (bs * ls) % ROWS == 0
    tiles_p = bp * lp // ROWS
    tiles_s = bs * ls // ROWS
    row = lambda v: v.reshape(1, -1)

    ab_re, ab_im, bb_re, bb_im = _s5_prep(s5_a_re[0], s5_a_im[0], s5_log_dt[0].reshape(S5_GROUPS, 1),
                                          jnp.swapaxes(s5_b_re[0], 1, 2), jnp.swapaxes(s5_b_im[0], 1, 2))
    gpb = S5_GROUPS // S5_BLOCKS
    eye = jnp.eye(gpb, dtype=F32)

    def in_blocks(b):
        b = b.reshape(S5_BLOCKS, gpb, S5_GROUP_CH, S5_STATE)
        return jnp.einsum('bgjn,gh->bgjhn', b, eye).reshape(S5_BLOCKS, gpb * S5_GROUP_CH, gpb * S5_STATE)

    def out_blocks(c):
        c = c.reshape(S5_BLOCKS, gpb, S5_GROUP_CH, S5_STATE)
        return jnp.einsum('bgjn,gh->bgnhj', c, eye).reshape(S5_BLOCKS, gpb * S5_STATE, gpb * S5_GROUP_CH)

    ar8 = jnp.broadcast_to(ab_re.reshape(1, -1), (SUBLANES, S5_GROUPS * S5_STATE))
    ai8 = jnp.broadcast_to(ab_im.reshape(1, -1), (SUBLANES, S5_GROUPS * S5_STATE))
    bb = jnp.concatenate([in_blocks(bb_re), in_blocks(bb_im)], axis=-1).astype(BF16)
    cc = jnp.concatenate([out_blocks(s5_c_re[0]), -out_blocks(s5_c_im[0])], axis=1).astype(BF16)
    wa_bd = _block_diag(lru_w_a[0]).astype(BF16)
    wx_bd = _block_diag(lru_w_x[0]).astype(BF16)
    rw_t = router_w[0].T
    rw_hi = rw_t.astype(BF16)
    rw_lo = (rw_t - rw_hi.astype(F32)).astype(BF16)

    tc = ROWS // bp
    modpat = _adaln(jnp.concatenate([c_prompt, c_sample], axis=0), w_ada[0], row(b_ada[0]), bp)
    r = jnp.arange(ROWS)
    tm_of = (r % tc) * bp + r // tc
    to_tm = (r[:, None] == tm_of[None, :]).astype(BF16)
    from_tm = to_tm.T
    pair = 2 if tiles_p % 2 == 0 else 1

    def conv_tm(cv):
        return jnp.swapaxes(cv, 0, 1).reshape(-1, D_LRU)

    def conv_bm(cv, b):
        return jnp.swapaxes(cv.reshape(CONV_WIDTH - 1, b, D_LRU), 0, 1)

    x_s = jnp.swapaxes(x_sample, 0, 1).reshape(bs * ls, D_MODEL)
    in_args = (row(g_mix[0]), w_in[0].astype(BF16))
    lru_args = (lru_conv_w[0], row(lru_conv_b[0]), wa_bd, row(lru_b_a[0]), wx_bd, row(lru_b_x[0]),
                row(lru_lambda[0]))
    u_p, la_p, lb_p, lg_p, cvp = _in_proj(x_prompt, modpat, 0, to_tm, *in_args,
                                          jnp.zeros(((CONV_WIDTH - 1) * bp, D_LRU), F32), *lru_args, bp, pair)
    u_s, la_s, lb_s, lg_s, cvs = _in_proj(x_s, modpat, 1, None, *in_args, conv_tm(state_conv[0]), *lru_args, bs, 1)

    s5_args = (ar8, ai8, bb, cc, row(s5_d[0]), s5_w_glu[0].astype(BF16), row(s5_b_glu[0]), row(g_out_s5[0]))
    ys5_p, s5p, ylru_p, hp = _mixers(u_p, jnp.zeros((bp, S5_COLS), F32), *s5_args, la_p, lb_p, lg_p,
                                     jnp.zeros((bp, D_LRU), F32), row(g_out_lru[0]), bp, pair)
    ys5_s, s5s, ylru_s, hs = _mixers(u_s, _s5_cols(state_s5_re[0], state_s5_im[0]), *s5_args, la_s, lb_s, lg_s,
                                     state_lru_h[0], row(g_out_lru[0]), bs, 1)

    out_args = (w_out[0].astype(BF16).reshape(2, D_S5, D_MODEL), row(g_ffn[0]), rw_hi, rw_lo,
                router_b[0].reshape(N_EXPERTS, 1))
    h_p, xn_p, pg_p, cnt_p = _out_proj(ys5_p, ylru_p, x_prompt, modpat, 0, from_tm, *out_args, pair)
    h_s, xn_s, pg_s, cnt_s = _out_proj(ys5_s, ylru_s, x_s, modpat, 1, None, *out_args, 1)
    pg = jnp.concatenate([pg_p, pg_s], axis=1)
    cnt = jnp.concatenate([cnt_p, cnt_s], axis=0)

    seg_n, seg_glob, seg_local, seg_tot, block_expert, block_next, block_rows, nb_used = _plan(cnt[:, :, 0])
    xs = _dispatch(seg_n, seg_glob, seg_local, seg_tot, nb_used, pg, xn_p, xn_s,
                   _moe_rows_bound(tiles_p + tiles_s))
    ys = _moe(xs, block_expert, block_next, block_rows, nb_used, moe_w_gu[0], moe_b_gu[0].reshape(N_EXPERTS, 1, 2 * D_FF),
              moe_w_down[0], moe_b_down[0].reshape(N_EXPERTS, 1, D_MODEL))
    y_prompt, y_s = _final(seg_n, seg_glob, seg_local, seg_tot, h_p, h_s, pg.T, modpat, row(g_final), ys,
                           tiles_p, tiles_s)
    y_sample = jnp.swapaxes(y_s.reshape(ls, bs, D_MODEL), 0, 1)
    s5p_re, s5p_im = _s5_uncols(s5p)
    s5s_re, s5s_im = _s5_uncols(s5s)
    return (y_prompt, y_sample,
            s5p_re[None], s5p_im[None], hp[None], conv_bm(cvp, bp)[None],
            s5s_re[None], s5s_im[None], hs[None], conv_bm(cvs, bs)[None])
```

```python
import functools

import jax
import jax.numpy as jnp
from jax import lax
from jax.experimental import pallas as pl
from jax.experimental.pallas import tpu as pltpu

D_MODEL = 1024
D_S5 = 512
D_LRU = 512
S5_GROUPS = 32
S5_GROUP_CH = 16
S5_STATE = 64
S5_COLS = 2 * S5_GROUPS * S5_STATE
S5_BLOCKS = 4
LRU_C = 8.0
CONV_WIDTH = 4
N_EXPERTS = 32
TOP_K = 4
D_FF = 1024
SWIGLU_LIMIT = 7.0
SWIGLU_ALPHA = 1.702
N_MOD = 6
EPS = 1e-6

SUBLANES = 8
LANES = 128
VMEM_LIMIT = 48 * 1024 * 1024

ROWS = 512
MOE_TM = 512
MOE_PART = 128
PACKED = D_MODEL // 2
SEG_ALIGN = SUBLANES
SORT_CHUNK = 256
SORT_ROWS = -(-(TOP_K * ROWS + N_EXPERTS * (SEG_ALIGN - 1)) // SORT_CHUNK) * SORT_CHUNK
WAIT_MAX_PIECE = N_EXPERTS * MOE_TM // 2

BF16 = jnp.bfloat16
F32 = jnp.float32
U32 = jnp.uint32


def _params(*sem):
    return pltpu.CompilerParams(dimension_semantics=sem, vmem_limit_bytes=VMEM_LIMIT)


def _full(shape):
    return pl.BlockSpec(shape, lambda *_: (0,) * len(shape))


def _rms(x, g):
    return x * lax.rsqrt(jnp.mean(x * x, axis=-1, keepdims=True) + EPS) * g


def _gelu(x):
    return 0.5 * x * (1.0 + lax.erf(x * (2.0 ** -0.5)))


def _expm1(x):
    u = jnp.exp(x)
    d = u - 1.0
    return jnp.where(d == 0.0, x, jnp.where(d == -1.0, -1.0, d * x / jnp.log(u)))


def _pack_rows(x):
    lo = lax.shift_right_logical(lax.bitcast_convert_type(x[:, :PACKED], U32), jnp.uint32(16))
    hi = lax.bitcast_convert_type(x[:, PACKED:], U32) & jnp.uint32(0xFFFF0000)
    return lo | hi


def _unpack_rows(p):
    lo = lax.bitcast_convert_type(lax.shift_left(p, jnp.uint32(16)), F32)
    hi = lax.bitcast_convert_type(p & jnp.uint32(0xFFFF0000), F32)
    return jnp.concatenate([lo, hi], axis=-1).astype(BF16)


def _row_pieces(n, largest, fn):
    off = 0
    bit = largest
    while bit >= SEG_ALIGN:
        @pl.when((n & bit) != 0)
        def _(off=off, bit=bit):
            fn(off, bit)
        off = off + (n & bit)
        bit //= 2


def _mod_body(bp, c_ref, w_ref, b_ref, o_ref):
    c = c_ref[...]
    s = (c * jax.nn.sigmoid(c)).astype(BF16)
    mod = jnp.dot(s, w_ref[...].astype(BF16), preferred_element_type=F32) + b_ref[...]
    bs = mod.shape[0] - bp
    o_ref[0] = jnp.broadcast_to(mod[:bp][:, None, :], (bp, ROWS // bp, D_MODEL)).reshape(ROWS, D_MODEL)
    o_ref[1] = jnp.broadcast_to(mod[bp:][None], (ROWS // bs, bs, D_MODEL)).reshape(ROWS, D_MODEL)


def _adaln(c, w_ada, b_ada, bp):
    m = c.shape[0]
    return pl.pallas_call(
        functools.partial(_mod_body, bp),
        grid=(N_MOD,),
        in_specs=[pl.BlockSpec((m, D_MODEL), lambda j: (0, 0)),
                  pl.BlockSpec((D_MODEL, D_MODEL), lambda j: (0, j)),
                  pl.BlockSpec((1, D_MODEL), lambda j: (0, j))],
        out_specs=pl.BlockSpec((2, ROWS, D_MODEL), lambda j: (0, 0, j)),
        out_shape=jax.ShapeDtypeStruct((2, ROWS, N_MOD * D_MODEL), F32),
        compiler_params=_params("arbitrary"),
        name="adaln",
    )(c, w_ada, b_ada)


def _mod_spec(k, tiles_p):
    return pl.BlockSpec((1, ROWS, D_MODEL), lambda i, *_: (jnp.where(i < tiles_p, 0, 1), 0, k))


def _tile_specs(bp, tiles_p, tiles_s):
    tc = ROWS // bp
    p_spec = pl.BlockSpec((bp, tc, D_MODEL), lambda i, *_: (0, jnp.minimum(i, tiles_p - 1), 0))
    s_spec = pl.BlockSpec((ROWS, D_MODEL), lambda i, *_: (jnp.clip(i - tiles_p, 0, tiles_s - 1), 0))
    return p_spec, s_spec


def _in_body(subtiles, bt, to_tm, *refs):
    if to_tm:
        x_ref, sc_ref, sh_ref, perm_ref, *refs = refs
    else:
        x_ref, sc_ref, sh_ref, *refs = refs
    (g_ref, w_ref, cv0_ref, cw_ref, cb_ref, wa_ref, ba_ref, wx_ref, bx_ref, lam_ref,
     u_ref, a_ref, b_ref, gg_ref, cvout_ref, xp_ref) = refs
    halo = (CONV_WIDTH - 1) * bt

    @pl.when(pl.program_id(0) == 0)
    def _():
        xp_ref[...] = cv0_ref[...]

    lam = lam_ref[...]
    softplus_neg_lam = jnp.maximum(-lam, 0.0) + jnp.log1p(jnp.exp(-jnp.abs(lam)))
    earlier = xp_ref[...]
    for c in range(subtiles):
        rows = slice(c * ROWS, (c + 1) * ROWS)
        if to_tm:
            tc = x_ref.shape[1] // subtiles
            x = x_ref[:, c * tc:(c + 1) * tc, :].reshape(ROWS, D_MODEL)
        else:
            x = x_ref[rows, :]
        xn = (_rms(x, g_ref[...]) * (1.0 + sc_ref[0]) + sh_ref[0]).astype(BF16)
        if to_tm:
            xn = jnp.dot(perm_ref[...], xn, preferred_element_type=F32).astype(BF16)
        p = jnp.dot(xn, w_ref[...], preferred_element_type=F32)
        u_ref[rows, :] = p[:, :D_S5]
        gg_ref[rows, :] = _gelu(p[:, D_S5 + D_LRU:])

        xr = p[:, D_S5:D_S5 + D_LRU]
        xp = jnp.concatenate([earlier, xr], axis=0)
        earlier = xr[ROWS - halo:, :]
        xc = cb_ref[...] + sum(xp[k * bt:k * bt + ROWS, :] * cw_ref[k:k + 1, :] for k in range(CONV_WIDTH))
        xcb = xc.astype(BF16)
        r = jax.nn.sigmoid(jnp.dot(xcb, wa_ref[...], preferred_element_type=F32) + ba_ref[...])
        i = jax.nn.sigmoid(jnp.dot(xcb, wx_ref[...], preferred_element_type=F32) + bx_ref[...])
        log_a = -LRU_C * r * softplus_neg_lam
        a_ref[rows, :] = jnp.exp(log_a)
        b_ref[rows, :] = jnp.sqrt(-_expm1(2.0 * log_a)) * (i * xc)

    xp_ref[...] = earlier
    cvout_ref[...] = earlier


def _in_proj(x, modpat, pattern, to_tm, g_mix, w_in_bf, conv0_tm, conv_w, conv_b, wa_bd, b_a, wx_bd, b_x, lam, bt,
             subtiles):
    tm = to_tm is not None
    t_rows = x.shape[0] * x.shape[1] if tm else x.shape[0]
    step_rows = subtiles * ROWS
    halo = (CONV_WIDTH - 1) * bt
    if tm:
        x_spec = pl.BlockSpec((x.shape[0], step_rows // x.shape[0], D_MODEL), lambda i: (0, i, 0))
    else:
        x_spec = pl.BlockSpec((step_rows, D_MODEL), lambda i: (i, 0))
    mod_spec = lambda k: pl.BlockSpec((1, ROWS, D_MODEL), lambda i: (pattern, 0, k))
    row_spec = pl.BlockSpec((step_rows, D_S5), lambda i: (i, 0))
    vec = _full((1, D_LRU))
    return pl.pallas_call(
        functools.partial(_in_body, subtiles, bt, tm),
        grid=(t_rows // step_rows,),
        in_specs=[x_spec, mod_spec(1), mod_spec(0), *([_full((ROWS, ROWS))] if tm else []), _full((1, D_MODEL)),
                  _full((D_MODEL, D_S5 + 2 * D_LRU)), _full((halo, D_LRU)), _full((CONV_WIDTH, D_LRU)), vec,
                  _full((D_LRU, D_LRU)), vec, _full((D_LRU, D_LRU)), vec, vec],
        out_specs=[row_spec, row_spec, row_spec, row_spec, _full((halo, D_LRU))],
        out_shape=[jax.ShapeDtypeStruct((t_rows, D_S5), F32)] * 4 + [jax.ShapeDtypeStruct((halo, D_LRU), F32)],
        scratch_shapes=[pltpu.VMEM((halo, D_LRU), F32)],
        compiler_params=_params("arbitrary"),
        name="in_proj",
    )(x, modpat, modpat, *([to_tm] if tm else []), g_mix, w_in_bf, conv0_tm, conv_w, conv_b, wa_bd, b_a, wx_bd, b_x,
      lam)


def _s5_prep_body(are_ref, aim_ref, ldt_ref, bre_ref, bim_ref, abre_ref, abim_ref, bbre_ref, bbim_ref):
    a_re = are_ref[...]
    a_im = aim_ref[...]
    dt = jnp.exp(ldt_ref[...])
    mag = jnp.exp(dt * a_re)
    ang = dt * a_im
    ab_re = mag * jnp.cos(ang)
    ab_im = mag * jnp.sin(ang)
    den = a_re * a_re + a_im * a_im
    q_re = ((ab_re - 1.0) * a_re + ab_im * a_im) / den
    q_im = (ab_im * a_re - (ab_re - 1.0) * a_im) / den
    abre_ref[...] = ab_re
    abim_ref[...] = ab_im
    b_re = bre_ref[...]
    b_im = bim_ref[...]
    bbre_ref[...] = q_re[:, None, :] * b_re - q_im[:, None, :] * b_im
    bbim_ref[...] = q_re[:, None, :] * b_im + q_im[:, None, :] * b_re


def _s5_prep(a_re, a_im, log_dt, b_re_t, b_im_t):
    gn = jax.ShapeDtypeStruct((S5_GROUPS, S5_STATE), F32)
    gjn = jax.ShapeDtypeStruct((S5_GROUPS, S5_GROUP_CH, S5_STATE), F32)
    return pl.pallas_call(_s5_prep_body, out_shape=[gn, gn, gjn, gjn], name="s5_prep")(
        a_re, a_im, log_dt, b_re_t, b_im_t)


def _s5_body(tc, bt, chunks, u_ref, s0_ref, ar_ref, ai_ref, bb_ref, cc_ref, d_ref, wg_ref, bg_ref, go_ref,
             la_ref, lb_ref, lg_ref, lh0_ref, lgo_ref, y_ref, sout_ref, yl_ref, lhout_ref,
             bu_ref, st_ref, lh_ref, lst_ref):
    half = S5_COLS // S5_BLOCKS // 2
    re = [slice(half * j, half * (j + 1)) for j in range(S5_BLOCKS)]
    im = [slice(S5_COLS // 2 + half * j, S5_COLS // 2 + half * (j + 1)) for j in range(S5_BLOCKS)]

    @pl.when(pl.program_id(0) == 0)
    def _():
        st_ref[...] = s0_ref[...]
        lst_ref[...] = lh0_ref[...]

    for c in range(chunks):
        ub = u_ref[c * ROWS:(c + 1) * ROWS, :].astype(BF16)
        for j in range(S5_BLOCKS):
            bu = jnp.dot(ub[:, LANES * j:LANES * (j + 1)], bb_ref[j], preferred_element_type=F32)
            bu_ref[c, :, re[j]] = bu[:, :half]
            bu_ref[c, :, im[j]] = bu[:, half:]

    for sb in range(bt // SUBLANES):
        base = sb * SUBLANES
        hr = [st_ref[base:base + SUBLANES, re[j]] for j in range(S5_BLOCKS)]
        hi = [st_ref[base:base + SUBLANES, im[j]] for j in range(S5_BLOCKS)]
        hl = lst_ref[base:base + SUBLANES, :]
        for c in range(chunks):
            for t in range(tc):
                r0 = t * bt + base
                row = c * ROWS + r0
                hl = la_ref[row:row + SUBLANES, :] * hl + lb_ref[row:row + SUBLANES, :]
                lh_ref[row:row + SUBLANES, :] = hl
                for j in range(S5_BLOCKS):
                    ar = ar_ref[:, re[j]]
                    ai = ai_ref[:, re[j]]
                    h_re = ar * hr[j] - ai * hi[j] + bu_ref[c, r0:r0 + SUBLANES, re[j]]
                    h_im = ar * hi[j] + ai * hr[j] + bu_ref[c, r0:r0 + SUBLANES, im[j]]
                    bu_ref[c, r0:r0 + SUBLANES, re[j]] = h_re
                    bu_ref[c, r0:r0 + SUBLANES, im[j]] = h_im
                    hr[j], hi[j] = h_re, h_im
        for j in range(S5_BLOCKS):
            st_ref[base:base + SUBLANES, re[j]] = hr[j]
            st_ref[base:base + SUBLANES, im[j]] = hi[j]
        lst_ref[base:base + SUBLANES, :] = hl
    sout_ref[...] = st_ref[...]
    lhout_ref[...] = lst_ref[...]
    yl_ref[...] = _rms(lh_ref[...] * lg_ref[...], lgo_ref[...]).astype(BF16)

    for c in range(chunks):
        y = jnp.concatenate(
            [jnp.dot(jnp.concatenate([bu_ref[c, :, re[j]], bu_ref[c, :, im[j]]], axis=-1).astype(BF16), cc_ref[j],
                     preferred_element_type=F32) for j in range(S5_BLOCKS)], axis=-1)
        y = y + d_ref[...] * u_ref[c * ROWS:(c + 1) * ROWS, :]
        g = _gelu(y)
        z = jnp.dot(g.astype(BF16), wg_ref[...], preferred_element_type=F32) + bg_ref[...]
        out = g * jax.nn.sigmoid(z)
        y_ref[c * ROWS:(c + 1) * ROWS, :] = _rms(out, go_ref[...]).astype(BF16)


def _mixers(u, s0, ar8, ai8, bb, cc, d, w_glu_bf, b_glu, g_out_s5, lru_a, lru_b, lru_gg, lru_h0, g_out_lru, bt,
            chunks):
    tc = ROWS // bt
    t_rows = u.shape[0]
    step_rows = chunks * ROWS
    assert t_rows % step_rows == 0
    rows_f32 = pl.BlockSpec((step_rows, D_S5), lambda i: (i, 0))
    vec = _full((1, D_S5))
    return pl.pallas_call(
        functools.partial(_s5_body, tc, bt, chunks),
        grid=(t_rows // step_rows,),
        in_specs=[rows_f32, _full((bt, S5_COLS)), _full(ar8.shape), _full(ai8.shape), _full(bb.shape),
                  _full(cc.shape), vec, _full((D_S5, D_S5)), vec, vec,
                  rows_f32, rows_f32, rows_f32, _full((bt, D_LRU)), vec],
        out_specs=[rows_f32, _full((bt, S5_COLS)), rows_f32, _full((bt, D_LRU))],
        out_shape=[jax.ShapeDtypeStruct((t_rows, D_S5), BF16), jax.ShapeDtypeStruct((bt, S5_COLS), F32),
                   jax.ShapeDtypeStruct((t_rows, D_LRU), BF16), jax.ShapeDtypeStruct((bt, D_LRU), F32)],
        scratch_shapes=[pltpu.VMEM((chunks, ROWS, S5_COLS), F32), pltpu.VMEM((bt, S5_COLS), F32),
                        pltpu.VMEM((step_rows, D_LRU), F32), pltpu.VMEM((bt, D_LRU), F32)],
        compiler_params=_params("arbitrary"),
        name="s5_lru_mixers",
    )(u, s0, ar8, ai8, bb, cc, d, w_glu_bf, b_glu, g_out_s5, lru_a, lru_b, lru_gg, lru_h0, g_out_lru)


def _route_tile(xn, x_hi, rw_hi, rw_lo, rb):
    x_lo = (xn - x_hi.astype(F32)).astype(BF16)
    nt = (((1,), (1,)), ((), ()))
    logits = (lax.dot_general(rw_hi, x_hi, nt, preferred_element_type=F32)
              + lax.dot_general(rw_hi, x_lo, nt, preferred_element_type=F32)
              + lax.dot_general(rw_lo, x_hi, nt, preferred_element_type=F32)) + rb

    e_iota = lax.broadcasted_iota(jnp.int32, (N_EXPERTS, ROWS), 0).astype(F32)
    work = logits
    sels, vals = [], []
    for _ in range(TOP_K):
        m = jnp.max(work, axis=0, keepdims=True)
        idx = jnp.min(jnp.where(work == m, e_iota, float(N_EXPERTS)), axis=0, keepdims=True)
        sel = e_iota == idx
        work = jnp.where(sel, -jnp.inf, work)
        sels.append(sel)
        vals.append(m)
    exps = [jnp.exp(v - vals[0]) for v in vals]
    denom = exps[0] + exps[1] + exps[2] + exps[3]
    gates = [e / denom for e in exps]

    onehot = sels[0] | sels[1] | sels[2] | sels[3]
    rr = lax.broadcasted_iota(jnp.int32, (ROWS, ROWS), 0)
    cc = lax.broadcasted_iota(jnp.int32, (ROWS, ROWS), 1)
    before = (rr < cc).astype(BF16)
    prefix = jnp.dot(onehot.astype(BF16), before, preferred_element_type=F32)
    cnt = jnp.sum(onehot.astype(F32), axis=1, keepdims=True)
    cnt_pad = jnp.floor((cnt + (SEG_ALIGN - 1)) * (1.0 / SEG_ALIGN)) * SEG_ALIGN
    er = lax.broadcasted_iota(jnp.int32, (N_EXPERTS, N_EXPERTS), 0)
    ec = lax.broadcasted_iota(jnp.int32, (N_EXPERTS, N_EXPERTS), 1)
    seg_start = jnp.dot((ec < er).astype(BF16), jnp.broadcast_to(cnt_pad, (N_EXPERTS, LANES)).astype(BF16),
                        preferred_element_type=F32)[:, 0:1]
    where_to = prefix + seg_start
    poss = [jnp.sum(jnp.where(s, where_to, 0.0), axis=0, keepdims=True) for s in sels]

    s_iota = lax.broadcasted_iota(jnp.int32, (2 * TOP_K, ROWS), 0)
    pg = jnp.zeros((2 * TOP_K, ROWS), F32)
    for k in range(TOP_K):
        pg = jnp.where(s_iota == k, poss[k], pg)
        pg = jnp.where(s_iota == TOP_K + k, gates[k], pg)
    return pg, cnt


def _out_body(subtiles, tm_rows, *refs):
    if tm_rows:
        ys_ref, yl_ref, x_ref, g1_ref, sc_ref, sh_ref, perm_ref, *refs = refs
    else:
        ys_ref, yl_ref, x_ref, g1_ref, sc_ref, sh_ref, *refs = refs
    wo_ref, gf_ref, rwh_ref, rwl_ref, rb_ref, h_ref, xn_ref, pg_ref, cnt_ref = refs
    for c in range(subtiles):
        rows = slice(c * ROWS, (c + 1) * ROWS)
        ys = ys_ref[rows, :]
        yl = yl_ref[rows, :]
        if tm_rows:
            ys = jnp.dot(perm_ref[...], ys, preferred_element_type=F32).astype(BF16)
            yl = jnp.dot(perm_ref[...], yl, preferred_element_type=F32).astype(BF16)
            tc = x_ref.shape[1] // subtiles
            x = x_ref[:, c * tc:(c + 1) * tc, :].reshape(ROWS, D_MODEL)
        else:
            x = x_ref[rows, :]
        mixed = (jnp.dot(ys, wo_ref[0], preferred_element_type=F32)
                 + jnp.dot(yl, wo_ref[1], preferred_element_type=F32))
        h = x + g1_ref[0] * mixed
        if tm_rows:
            h_ref[:, c * tc:(c + 1) * tc, :] = h.reshape(x_ref.shape[0], tc, D_MODEL)
        else:
            h_ref[rows, :] = h
        xn = _rms(h, gf_ref[...]) * (1.0 + sc_ref[0]) + sh_ref[0]
        x_hi = xn.astype(BF16)
        xn_ref[rows, :] = x_hi
        pg, cnt = _route_tile(xn, x_hi, rwh_ref[...], rwl_ref[...], rb_ref[...])
        pg_ref[:, rows] = pg
        cnt_ref[c] = jnp.broadcast_to(cnt, (N_EXPERTS, LANES))


def _out_proj(ys5, ylru, x, modpat, pattern, from_tm, w_out_bf, g_ffn, rw_hi, rw_lo, router_b, subtiles):
    tm_rows = from_tm is not None
    t_rows = ys5.shape[0]
    step_rows = subtiles * ROWS
    half_spec = pl.BlockSpec((step_rows, D_S5), lambda i: (i, 0))
    if tm_rows:
        x_spec = pl.BlockSpec((x.shape[0], step_rows // x.shape[0], D_MODEL), lambda i: (0, i, 0))
    else:
        x_spec = pl.BlockSpec((step_rows, D_MODEL), lambda i: (i, 0))
    mod_spec = lambda k: pl.BlockSpec((1, ROWS, D_MODEL), lambda i: (pattern, 0, k))
    perm = [_full((ROWS, ROWS))] if tm_rows else []
    return pl.pallas_call(
        functools.partial(_out_body, subtiles, tm_rows),
        grid=(t_rows // step_rows,),
        in_specs=[half_spec, half_spec, x_spec, mod_spec(2), mod_spec(4), mod_spec(3), *perm,
                  _full((2, D_S5, D_MODEL)), _full((1, D_MODEL)),
                  _full((N_EXPERTS, D_MODEL)), _full((N_EXPERTS, D_MODEL)), _full((N_EXPERTS, 1))],
        out_specs=[x_spec, pl.BlockSpec((step_rows, D_MODEL), lambda i: (i, 0)),
                   pl.BlockSpec((2 * TOP_K, step_rows), lambda i: (0, i)),
                   pl.BlockSpec((subtiles, N_EXPERTS, LANES), lambda i: (i, 0, 0))],
        out_shape=[jax.ShapeDtypeStruct(x.shape, F32), jax.ShapeDtypeStruct((t_rows, D_MODEL), BF16),
                   jax.ShapeDtypeStruct((2 * TOP_K, t_rows), F32),
                   jax.ShapeDtypeStruct((t_rows // ROWS, N_EXPERTS, LANES), F32)],
        compiler_params=_params("arbitrary"),
        name="out_proj_router",
    )(ys5, ylru, x, modpat, modpat, modpat, *([from_tm] if tm_rows else []), w_out_bf, g_ffn, rw_hi, rw_lo,
      router_b)


def _start_segments(n_ref, hbm_ref, vmem_ref, hbm, vmem_buf, sem, step, to_hbm):
    for e in range(N_EXPERTS):
        n = n_ref[step * N_EXPERTS + e]
        h0 = hbm_ref[step * N_EXPERTS + e]
        v0 = vmem_ref[step * N_EXPERTS + e]

        def piece(off, size, h0=h0, v0=v0, e=e):
            h = hbm.at[pl.ds(pl.multiple_of(h0 + off, SEG_ALIGN), size)]
            v = vmem_buf.at[pl.ds(pl.multiple_of(v0 + off, SEG_ALIGN), size)]
            copy = pltpu.make_async_copy(v, h, sem) if to_hbm else pltpu.make_async_copy(h, v, sem)
            copy.start(priority=e % 2)

        _row_pieces(n, ROWS, piece)


def _wait_rows(total, largest, hbm, sem):
    _row_pieces(total, largest, lambda off, size: pltpu.make_async_copy(
        hbm.at[pl.ds(0, size)], hbm.at[pl.ds(0, size)], sem).wait())


def _dispatch_body(tiles_p, n_ref, glob_ref, local_ref, tot_ref, nb_ref, pg_ref, xp_ref, xsm_ref, xs_hbm, stage,
                   sem):
    j = pl.program_id(0)
    last = pl.num_programs(0) - 1
    slot = j % 2

    def unused_blocks(act):
        def blk(b, carry):
            act(pltpu.make_async_copy(stage.at[slot, pl.ds(0, MOE_TM)],
                                      xs_hbm.at[pl.ds(pl.multiple_of(b * MOE_TM, MOE_TM), MOE_TM)], sem.at[slot]))
            return carry
        lax.fori_loop(nb_ref[0], xs_hbm.shape[0] // MOE_TM, blk, 0)

    def wait_step(step, s):
        _wait_rows(tot_ref[step], WAIT_MAX_PIECE, xs_hbm, sem.at[s])

    @pl.when(j >= 2)
    def _():
        wait_step(jnp.maximum(j - 2, 0), slot)

    @pl.when(j < last)
    def _():
        x = jnp.where(j < tiles_p, xp_ref[...], xsm_ref[...])
        pos = pg_ref[0:TOP_K, :]
        r = lax.broadcasted_iota(jnp.int32, (SORT_CHUNK, ROWS), 0).astype(F32).astype(BF16)
        for c in range(SORT_ROWS // SORT_CHUNK):
            rel = (pos - float(c * SORT_CHUNK)).astype(BF16)
            pick = (r == rel[0:1]) | (r == rel[1:2]) | (r == rel[2:3]) | (r == rel[3:4])
            stage[slot, c * SORT_CHUNK:(c + 1) * SORT_CHUNK, :] = _pack_rows(jnp.dot(
                jnp.where(pick, jnp.ones((), BF16), jnp.zeros((), BF16)), x, preferred_element_type=F32))

    @pl.when(j == last)
    def _():
        stage[slot, 0:MOE_TM, :] = jnp.zeros((MOE_TM, PACKED), U32)

    _start_segments(n_ref, glob_ref, local_ref, xs_hbm, stage.at[slot], sem.at[slot], j, True)

    @pl.when(j == last)
    def _():
        unused_blocks(lambda c: c.start())

        @pl.when(j >= 1)
        def _():
            wait_step(jnp.maximum(j - 1, 0), 1 - slot)
        wait_step(j, slot)
        unused_blocks(lambda c: c.wait())


def _dispatch(seg_n, seg_glob, seg_local, seg_tot, nb_used, pg, xn_p, xn_s, n_rows):
    tiles_p = xn_p.shape[0] // ROWS
    tiles_s = xn_s.shape[0] // ROWS
    tiles = tiles_p + tiles_s
    grid_spec = pltpu.PrefetchScalarGridSpec(
        num_scalar_prefetch=5,
        grid=(tiles + 1,),
        in_specs=[pl.BlockSpec((2 * TOP_K, ROWS), lambda j, *_: (0, jnp.minimum(j, tiles - 1))),
                  pl.BlockSpec((ROWS, D_MODEL), lambda j, *_: (jnp.minimum(j, tiles_p - 1), 0)),
                  pl.BlockSpec((ROWS, D_MODEL), lambda j, *_: (jnp.clip(j - tiles_p, 0, tiles_s - 1), 0))],
        out_specs=pl.BlockSpec(memory_space=pl.ANY),
        scratch_shapes=[pltpu.VMEM((2, SORT_ROWS, PACKED), U32), pltpu.SemaphoreType.DMA((2,))],
    )
    return pl.pallas_call(
        functools.partial(_dispatch_body, tiles_p),
        grid_spec=grid_spec,
        out_shape=jax.ShapeDtypeStruct((n_rows, PACKED), U32),
        compiler_params=_params("arbitrary"),
        name="moe_dispatch",
    )(seg_n, seg_glob, seg_local, seg_tot, nb_used, pg, xn_p, xn_s)


def _moe_body(be_ref, nxt_ref, nv_ref, nb_ref, xs_ref, wgu_hbm, bgu_ref, wd_hbm, bd_ref, ys_ref,
              wgu_f32, wd_f32, wgu_bf, wd_bf, sem):
    i = pl.program_id(0)

    def weight_copies(e):
        return (pltpu.make_async_copy(wgu_hbm.at[e], wgu_f32, sem.at[0]),
                pltpu.make_async_copy(wd_hbm.at[e], wd_f32, sem.at[1]))

    @pl.when(i >= nb_ref[0])
    def _():
        ys_ref[...] = jnp.zeros_like(ys_ref)

    @pl.when(i < nb_ref[0])
    def _():
        e = be_ref[i]

        @pl.when(i == 0)
        def _():
            for c in weight_copies(e):
                c.start()

        @pl.when(jnp.logical_or(i == 0, e != be_ref[jnp.maximum(i - 1, 0)]))
        def _():
            for c in weight_copies(e):
                c.wait()
            wgu_bf[...] = wgu_f32[...].astype(BF16)
            wd_bf[...] = wd_f32[...].astype(BF16)

            @pl.when(nxt_ref[i] >= 0)
            def _():
                for c in weight_copies(nxt_ref[i]):
                    c.start()

        def ffn(packed):
            hg = jnp.dot(_unpack_rows(packed), wgu_bf[...], preferred_element_type=F32) + bgu_ref[0]
            gate = jnp.minimum(hg[:, :D_FF], SWIGLU_LIMIT)
            up = jnp.clip(hg[:, D_FF:], -SWIGLU_LIMIT, SWIGLU_LIMIT)
            act = (up + 1.0) * (gate * jax.nn.sigmoid(SWIGLU_ALPHA * gate))
            y = jnp.dot(act.astype(BF16), wd_bf[...], preferred_element_type=F32) + bd_ref[0]
            return _pack_rows(y.astype(BF16).astype(F32))

        for parts in range(1, MOE_TM // MOE_PART + 1):
            lo = (parts - 1) * MOE_PART
            last_path = parts == MOE_TM // MOE_PART

            @pl.when(jnp.logical_and(nv_ref[i] > lo, jnp.logical_or(last_path, nv_ref[i] <= lo + MOE_PART)))
            def _(parts=parts):
                rows = parts * MOE_PART
                ys_ref[0:rows, :] = ffn(xs_ref[0:rows, :])
                if rows < MOE_TM:
                    ys_ref[rows:, :] = jnp.zeros((MOE_TM - rows, PACKED), U32)


def _moe(xs, block_expert, block_next, block_rows, nb_used, w_gu, b_gu, w_down, b_down):
    n_blocks = xs.shape[0] // MOE_TM
    in_rows = pl.BlockSpec((MOE_TM, PACKED), lambda i, be, nx, nv, nb: (jnp.minimum(i, nb[0] - 1), 0))
    grid_spec = pltpu.PrefetchScalarGridSpec(
        num_scalar_prefetch=4,
        grid=(n_blocks,),
        in_specs=[in_rows,
                  pl.BlockSpec(memory_space=pl.ANY),
                  pl.BlockSpec((1, 1, 2 * D_FF), lambda i, be, nx, nv, nb: (be[i], 0, 0)),
                  pl.BlockSpec(memory_space=pl.ANY),
                  pl.BlockSpec((1, 1, D_MODEL), lambda i, be, nx, nv, nb: (be[i], 0, 0))],
        out_specs=pl.BlockSpec((MOE_TM, PACKED), lambda i, be, nx, nv, nb: (i, 0)),
        scratch_shapes=[pltpu.VMEM((D_MODEL, 2 * D_FF), F32), pltpu.VMEM((D_FF, D_MODEL), F32),
                        pltpu.VMEM((D_MODEL, 2 * D_FF), BF16), pltpu.VMEM((D_FF, D_MODEL), BF16),
                        pltpu.SemaphoreType.DMA((2,))],
    )
    return pl.pallas_call(
        _moe_body,
        grid_spec=grid_spec,
        out_shape=jax.ShapeDtypeStruct(xs.shape, U32),
        compiler_params=_params("arbitrary"),
        name="moe_experts",
    )(block_expert, block_next, block_rows, nb_used, xs, w_gu, b_gu, w_down, b_down)


def _fin_body(tiles_p, n_ref, glob_ref, local_ref, tot_ref, hp_ref, hs_ref, pg_ref, g2_ref, gf_ref, ys_hbm,
              op_ref, os_ref, ybuf, sem):
    j = pl.program_id(0)
    tiles = pl.num_programs(0)
    slot = j % 2
    is_prompt = j < tiles_p

    def fetch(step, s):
        _start_segments(n_ref, glob_ref, local_ref, ys_hbm, ybuf.at[s], sem.at[s], step, False)

    @pl.when(j == 0)
    def _():
        ybuf[...] = jnp.zeros_like(ybuf)
        fetch(0, 0)

    _wait_rows(tot_ref[j], WAIT_MAX_PIECE, ys_hbm, sem.at[slot])

    @pl.when(j + 1 < tiles)
    def _():
        fetch(jnp.minimum(j + 1, tiles - 1), 1 - slot)

    pg = pg_ref[...]
    ff = jnp.zeros((ROWS, D_MODEL), F32)
    r = lax.broadcasted_iota(jnp.int32, (ROWS, SORT_CHUNK), 1).astype(F32).astype(BF16)
    gates = pg[:, TOP_K:].astype(BF16)
    for c in range(SORT_ROWS // SORT_CHUNK):
        rel = (pg[:, :TOP_K] - float(c * SORT_CHUNK)).astype(BF16)
        w = jnp.zeros((ROWS, SORT_CHUNK), BF16)
        for k in range(TOP_K):
            w = jnp.where(r == rel[:, k:k + 1], gates[:, k:k + 1], w)
        ff = ff + jnp.dot(w, _unpack_rows(ybuf[slot, c * SORT_CHUNK:(c + 1) * SORT_CHUNK, :]),
                          preferred_element_type=F32)
    h = jnp.where(is_prompt, hp_ref[...].reshape(ROWS, D_MODEL), hs_ref[...])
    y = _rms(h + g2_ref[0] * ff, gf_ref[...])

    @pl.when(is_prompt)
    def _():
        op_ref[...] = y.reshape(op_ref.shape)

    @pl.when(jnp.logical_not(is_prompt))
    def _():
        os_ref[...] = y


def _final(seg_n, seg_glob, seg_local, seg_tot, h_p, h_s, pg_t, modpat, g_final, ys, tiles_p, tiles_s):
    hp_spec, hs_spec = _tile_specs(h_p.shape[0], tiles_p, tiles_s)
    grid_spec = pltpu.PrefetchScalarGridSpec(
        num_scalar_prefetch=4,
        grid=(tiles_p + tiles_s,),
        in_specs=[hp_spec, hs_spec,
                  pl.BlockSpec((ROWS, 2 * TOP_K), lambda j, *_: (j, 0)),
                  _mod_spec(5, tiles_p), _full((1, D_MODEL)),
                  pl.BlockSpec(memory_space=pl.ANY)],
        out_specs=[hp_spec, hs_spec],
        scratch_shapes=[pltpu.VMEM((2, SORT_ROWS, PACKED), U32), pltpu.SemaphoreType.DMA((2,))],
    )
    return pl.pallas_call(
        functools.partial(_fin_body, tiles_p),
        grid_spec=grid_spec,
        out_shape=[jax.ShapeDtypeStruct(h_p.shape, F32), jax.ShapeDtypeStruct(h_s.shape, F32)],
        compiler_params=_params("arbitrary"),
        name="combine_final",
    )(seg_n, seg_glob, seg_local, seg_tot, h_p, h_s, pg_t, modpat, g_final, ys)


def _block_diag(w):
    h, i, j = w.shape
    return jnp.einsum('hij,hk->hikj', w, jnp.eye(h, dtype=w.dtype)).reshape(h * i, h * j)


def _s5_cols(re, im):
    b = re.shape[0]
    return jnp.concatenate([re.reshape(b, -1), im.reshape(b, -1)], axis=1)


def _s5_uncols(cols):
    b = cols.shape[0]
    return (cols[:, :S5_COLS // 2].reshape(b, S5_GROUPS, S5_STATE),
            cols[:, S5_COLS // 2:].reshape(b, S5_GROUPS, S5_STATE))


def _moe_rows_bound(tiles):
    worst = tiles * (TOP_K * ROWS + N_EXPERTS * (SEG_ALIGN - 1)) + N_EXPERTS * (MOE_TM - SEG_ALIGN)
    return (worst + MOE_TM - 1) // MOE_TM * MOE_TM


def _plan(cnt):
    cnt = cnt.astype(jnp.int32)
    tiles = cnt.shape[0]
    cp = (cnt + SEG_ALIGN - 1) // SEG_ALIGN * SEG_ALIGN
    local = jnp.cumsum(cp, axis=1) - cp
    group = jnp.sum(cp, axis=0)
    group_pad = (group + MOE_TM - 1) // MOE_TM * MOE_TM
    pend = jnp.cumsum(group_pad)
    pstart = pend - group_pad
    glob = pstart[None, :] + jnp.cumsum(cp, axis=0) - cp
    gap = group_pad - group
    seg_n = jnp.concatenate([cp, gap[None]], axis=0).reshape(-1)
    seg_local = jnp.concatenate([local, jnp.zeros((1, N_EXPERTS), jnp.int32)], axis=0).reshape(-1)
    seg_glob = jnp.concatenate([glob, (pstart + group)[None]], axis=0).reshape(-1)
    seg_tot = jnp.concatenate([jnp.sum(cp, axis=1), jnp.sum(gap)[None]]).astype(jnp.int32)
    n_blocks = _moe_rows_bound(tiles) // MOE_TM
    block_row0 = jnp.arange(n_blocks, dtype=jnp.int32) * MOE_TM
    block_expert = jnp.minimum(jnp.sum(block_row0[:, None] >= pend[None, :], axis=1), N_EXPERTS - 1).astype(jnp.int32)
    nb_used = (pend[-1] // MOE_TM).astype(jnp.int32).reshape(1)
    experts = jnp.arange(N_EXPERTS, dtype=jnp.int32)
    later_owner = jnp.where((experts[None, :] > experts[:, None]) & (group_pad[None, :] > 0), experts[None, :],
                            N_EXPERTS)
    next_owner = jnp.min(later_owner, axis=1)
    next_owner = jnp.where(next_owner == N_EXPERTS, -1, next_owner).astype(jnp.int32)
    owner = block_expert[:, None] == experts[None, :]
    block_next = jnp.sum(jnp.where(owner, next_owner[None, :], 0), axis=1).astype(jnp.int32)
    group_end = jnp.sum(jnp.where(owner, (pstart + group)[None, :], 0), axis=1)
    block_rows = jnp.clip(group_end - block_row0, 0, MOE_TM).astype(jnp.int32)
    return seg_n, seg_glob, seg_local, seg_tot, block_expert, block_next, block_rows, nb_used


def kernel(x_prompt, x_sample, state_s5_re, state_s5_im, state_lru_h, state_conv, c_prompt, c_sample, w_ada, b_ada, g_mix, w_in, s5_a_re, s5_a_im, s5_log_dt, s5_b_re, s5_b_im, s5_c_re, s5_c_im, s5_d, s5_w_glu, s5_b_glu, lru_conv_w, lru_conv_b, lru_w_a, lru_b_a, lru_w_x, lru_b_x, lru_lambda, g_out_s5, g_out_lru, w_out, g_ffn, router_w, router_b, moe_w_gu, moe_b_gu, moe_w_down, moe_b_down, g_final):
    assert w_ada.shape[0] == 1, "one layer"
    bp, lp, _ = x_prompt.shape
    bs, ls, _ = x_sample.shape
    assert ROWS % bp == 0 and ROWS % bs == 0 and (bp * lp) % ROWS == 0 and (bs * ls) % ROWS == 0
    tiles_p = bp * lp // ROWS
    tiles_s = bs * ls // ROWS
    row = lambda v: v.reshape(1, -1)

    ab_re, ab_im, bb_re, bb_im = _s5_prep(s5_a_re[0], s5_a_im[0], s5_log_dt[0].reshape(S5_GROUPS, 1),
                                          jnp.swapaxes(s5_b_re[0], 1, 2), jnp.swapaxes(s5_b_im[0], 1, 2))
    gpb = S5_GROUPS // S5_BLOCKS
    eye = jnp.eye(gpb, dtype=F32)

    def in_blocks(b):
        b = b.reshape(S5_BLOCKS, gpb, S5_GROUP_CH, S5_STATE)
        return jnp.einsum('bgjn,gh->bgjhn', b, eye).reshape(S5_BLOCKS, gpb * S5_GROUP_CH, gpb * S5_STATE)

    def out_blocks(c):
        c = c.reshape(S5_BLOCKS, gpb, S5_GROUP_CH, S5_STATE)
        return jnp.einsum('bgjn,gh->bgnhj', c, eye).reshape(S5_BLOCKS, gpb * S5_STATE, gpb * S5_GROUP_CH)

    ar8 = jnp.broadcast_to(ab_re.reshape(1, -1), (SUBLANES, S5_GROUPS * S5_STATE))
    ai8 = jnp.broadcast_to(ab_im.reshape(1, -1), (SUBLANES, S5_GROUPS * S5_STATE))
    bb = jnp.concatenate([in_blocks(bb_re), in_blocks(bb_im)], axis=-1).astype(BF16)
    cc = jnp.concatenate([out_blocks(s5_c_re[0]), -out_blocks(s5_c_im[0])], axis=1).astype(BF16)
    wa_bd = _block_diag(lru_w_a[0]).astype(BF16)
    wx_bd = _block_diag(lru_w_x[0]).astype(BF16)
    rw_t = router_w[0].T
    rw_hi = rw_t.astype(BF16)
    rw_lo = (rw_t - rw_hi.astype(F32)).astype(BF16)

    tc = ROWS // bp
    modpat = _adaln(jnp.concatenate([c_prompt, c_sample], axis=0), w_ada[0], row(b_ada[0]), bp)
    r = jnp.arange(ROWS)
    tm_of = (r % tc) * bp + r // tc
    to_tm = (r[:, None] == tm_of[None, :]).astype(BF16)
    from_tm = to_tm.T
    pair = 2 if tiles_p % 2 == 0 else 1

    def conv_tm(cv):
        return jnp.swapaxes(cv, 0, 1).reshape(-1, D_LRU)

    def conv_bm(cv, b):
        return jnp.swapaxes(cv.reshape(CONV_WIDTH - 1, b, D_LRU), 0, 1)

    x_s = jnp.swapaxes(x_sample, 0, 1).reshape(bs * ls, D_MODEL)
    in_args = (row(g_mix[0]), w_in[0].astype(BF16))
    lru_args = (lru_conv_w[0], row(lru_conv_b[0]), wa_bd, row(lru_b_a[0]), wx_bd, row(lru_b_x[0]),
                row(lru_lambda[0]))
    u_p, la_p, lb_p, lg_p, cvp = _in_proj(x_prompt, modpat, 0, to_tm, *in_args,
                                          jnp.zeros(((CONV_WIDTH - 1) * bp, D_LRU), F32), *lru_args, bp, pair)
    u_s, la_s, lb_s, lg_s, cvs = _in_proj(x_s, modpat, 1, None, *in_args, conv_tm(state_conv[0]), *lru_args, bs, 1)

    s5_args = (ar8, ai8, bb, cc, row(s5_d[0]), s5_w_glu[0].astype(BF16), row(s5_b_glu[0]), row(g_out_s5[0]))
    ys5_p, s5p, ylru_p, hp = _mixers(u_p, jnp.zeros((bp, S5_COLS), F32), *s5_args, la_p, lb_p, lg_p,
                                     jnp.zeros((bp, D_LRU), F32), row(g_out_lru[0]), bp, pair)
    ys5_s, s5s, ylru_s, hs = _mixers(u_s, _s5_cols(state_s5_re[0], state_s5_im[0]), *s5_args, la_s, lb_s, lg_s,
                                     state_lru_h[0], row(g_out_lru[0]), bs, 1)

    out_args = (w_out[0].astype(BF16).reshape(2, D_S5, D_MODEL), row(g_ffn[0]), rw_hi, rw_lo,
                router_b[0].reshape(N_EXPERTS, 1))
    h_p, xn_p, pg_p, cnt_p = _out_proj(ys5_p, ylru_p, x_prompt, modpat, 0, from_tm, *out_args, pair)
    h_s, xn_s, pg_s, cnt_s = _out_proj(ys5_s, ylru_s, x_s, modpat, 1, None, *out_args, 1)
    pg = jnp.concatenate([pg_p, pg_s], axis=1)
    cnt = jnp.concatenate([cnt_p, cnt_s], axis=0)

    seg_n, seg_glob, seg_local, seg_tot, block_expert, block_next, block_rows, nb_used = _plan(cnt[:, :, 0])
    xs = _dispatch(seg_n, seg_glob, seg_local, seg_tot, nb_used, pg, xn_p, xn_s,
                   _moe_rows_bound(tiles_p + tiles_s))
    ys = _moe(xs, block_expert, block_next, block_rows, nb_used, moe_w_gu[0], moe_b_gu[0].reshape(N_EXPERTS, 1, 2 * D_FF),
              moe_w_down[0], moe_b_down[0].reshape(N_EXPERTS, 1, D_MODEL))
    y_prompt, y_s = _final(seg_n, seg_glob, seg_local, seg_tot, h_p, h_s, pg.T, modpat, row(g_final), ys,
                           tiles_p, tiles_s)
    y_sample = jnp.swapaxes(y_s.reshape(ls, bs, D_MODEL), 0, 1)
    s5p_re, s5p_im = _s5_uncols(s5p)
    s5s_re, s5s_im = _s5_uncols(s5s)
    return (y_prompt, y_sample,
            s5p_re[None], s5p_im[None], hp[None], conv_bm(cvp, bp)[None],
            s5s_re[None], s5s_im[None], hs[None], conv_bm(cvs, bs)[None])
```

```python
import functools

import jax
import jax.numpy as jnp
from jax import lax
from jax.experimental import pallas as pl
from jax.experimental.pallas import tpu as pltpu

D_MODEL = 1024
D_S5 = 512
D_LRU = 512
S5_GROUPS = 32
S5_GROUP_CH = 16
S5_STATE = 64
S5_COLS = 2 * S5_GROUPS * S5_STATE
S5_BLOCKS = 4
LRU_C = 8.0
CONV_WIDTH = 4
N_EXPERTS = 32
TOP_K = 4
D_FF = 1024
SWIGLU_LIMIT = 7.0
SWIGLU_ALPHA = 1.702
N_MOD = 6
EPS = 1e-6

SUBLANES = 8
LANES = 128
VMEM_LIMIT = 48 * 1024 * 1024

ROWS = 512
MOE_TM = 512
MOE_PART = 128
PACKED = D_MODEL // 2
SEG_ALIGN = SUBLANES
SORT_CHUNK = 256
SORT_ROWS = -(-(TOP_K * ROWS + N_EXPERTS * (SEG_ALIGN - 1)) // SORT_CHUNK) * SORT_CHUNK
WAIT_MAX_PIECE = N_EXPERTS * MOE_TM // 2

BF16 = jnp.bfloat16
F32 = jnp.float32
U32 = jnp.uint32


def _params(*sem):
    return pltpu.CompilerParams(dimension_semantics=sem, vmem_limit_bytes=VMEM_LIMIT)


def _full(shape):
    return pl.BlockSpec(shape, lambda *_: (0,) * len(shape))


def _rms(x, g):
    return x * lax.rsqrt(jnp.mean(x * x, axis=-1, keepdims=True) + EPS) * g


def _gelu(x):
    return 0.5 * x * (1.0 + lax.erf(x * (2.0 ** -0.5)))


def _expm1(x):
    u = jnp.exp(x)
    d = u - 1.0
    return jnp.where(d == 0.0, x, jnp.where(d == -1.0, -1.0, d * x / jnp.log(u)))


def _pack_rows(x):
    lo = lax.shift_right_logical(lax.bitcast_convert_type(x[:, :PACKED], U32), jnp.uint32(16))
    hi = lax.bitcast_convert_type(x[:, PACKED:], U32) & jnp.uint32(0xFFFF0000)
    return lo | hi


def _unpack_rows(p):
    lo = lax.bitcast_convert_type(lax.shift_left(p, jnp.uint32(16)), F32)
    hi = lax.bitcast_convert_type(p & jnp.uint32(0xFFFF0000), F32)
    return jnp.concatenate([lo, hi], axis=-1).astype(BF16)


def _row_pieces(n, largest, fn):
    off = 0
    bit = largest
    while bit >= SEG_ALIGN:
        @pl.when((n & bit) != 0)
        def _(off=off, bit=bit):
            fn(off, bit)
        off = off + (n & bit)
        bit //= 2


def _mod_body(bp, c_ref, w_ref, b_ref, o_ref):
    c = c_ref[...]
    s = (c * jax.nn.sigmoid(c)).astype(BF16)
    mod = jnp.dot(s, w_ref[...].astype(BF16), preferred_element_type=F32) + b_ref[...]
    bs = mod.shape[0] - bp
    o_ref[0] = jnp.broadcast_to(mod[:bp][:, None, :], (bp, ROWS // bp, D_MODEL)).reshape(ROWS, D_MODEL)
    o_ref[1] = jnp.broadcast_to(mod[bp:][None], (ROWS // bs, bs, D_MODEL)).reshape(ROWS, D_MODEL)
    o_ref[2] = jnp.broadcast_to(mod[:bp][None], (ROWS // bp, bp, D_MODEL)).reshape(ROWS, D_MODEL)


def _adaln(c, w_ada, b_ada, bp):
    m = c.shape[0]
    return pl.pallas_call(
        functools.partial(_mod_body, bp),
        grid=(N_MOD,),
        in_specs=[pl.BlockSpec((m, D_MODEL), lambda j: (0, 0)),
                  pl.BlockSpec((D_MODEL, D_MODEL), lambda j: (0, j)),
                  pl.BlockSpec((1, D_MODEL), lambda j: (0, j))],
        out_specs=pl.BlockSpec((3, ROWS, D_MODEL), lambda j: (0, 0, j)),
        out_shape=jax.ShapeDtypeStruct((3, ROWS, N_MOD * D_MODEL), F32),
        compiler_params=_params("arbitrary"),
        name="adaln",
    )(c, w_ada, b_ada)


def _mod_spec(k, tiles_p):
    return pl.BlockSpec((1, ROWS, D_MODEL), lambda i, *_: (jnp.where(i < tiles_p, 0, 1), 0, k))


def _tile_specs(bp, tiles_p, tiles_s):
    tc = ROWS // bp
    p_spec = pl.BlockSpec((bp, tc, D_MODEL), lambda i, *_: (0, jnp.minimum(i, tiles_p - 1), 0))
    s_spec = pl.BlockSpec((ROWS, D_MODEL), lambda i, *_: (jnp.clip(i - tiles_p, 0, tiles_s - 1), 0))
    return p_spec, s_spec


def _in_body(subtiles, bt, to_tm, *refs):
    x_ref, sc_ref, sh_ref, *refs = refs
    (g_ref, w_ref, cv0_ref, cw_ref, cb_ref, wa_ref, ba_ref, wx_ref, bx_ref, lam_ref,
     u_ref, a_ref, b_ref, gg_ref, cvout_ref, xp_ref, *dma) = refs
    halo = (CONV_WIDTH - 1) * bt
    step = pl.program_id(0)

    if to_tm:
        xbuf, sem = dma
        tcs = xbuf.shape[1]

        def fetch(s, slot, act):
            for b in range(bt):
                act(pltpu.make_async_copy(x_ref.at[b, pl.ds(s * tcs, tcs), :], xbuf.at[slot, :, b, :], sem.at[slot]))

        @pl.when(step == 0)
        def _():
            fetch(0, 0, lambda c: c.start())

        fetch(step, step % 2, lambda c: c.wait())

        @pl.when(step + 1 < pl.num_programs(0))
        def _():
            fetch(step + 1, 1 - step % 2, lambda c: c.start())

    @pl.when(step == 0)
    def _():
        xp_ref[...] = cv0_ref[...]

    lam = lam_ref[...]
    softplus_neg_lam = jnp.maximum(-lam, 0.0) + jnp.log1p(jnp.exp(-jnp.abs(lam)))
    earlier = xp_ref[...]
    for c in range(subtiles):
        rows = slice(c * ROWS, (c + 1) * ROWS)
        if to_tm:
            tc = ROWS // bt
            x = xbuf[step % 2, c * tc:(c + 1) * tc].reshape(ROWS, D_MODEL)
        else:
            x = x_ref[rows, :]
        xn = (_rms(x, g_ref[...]) * (1.0 + sc_ref[0]) + sh_ref[0]).astype(BF16)
        p = jnp.dot(xn, w_ref[...], preferred_element_type=F32)
        u_ref[rows, :] = p[:, :D_S5]
        gg_ref[rows, :] = _gelu(p[:, D_S5 + D_LRU:])

        xr = p[:, D_S5:D_S5 + D_LRU]
        xp = jnp.concatenate([earlier, xr], axis=0)
        earlier = xr[ROWS - halo:, :]
        xc = cb_ref[...] + sum(xp[k * bt:k * bt + ROWS, :] * cw_ref[k:k + 1, :] for k in range(CONV_WIDTH))
        xcb = xc.astype(BF16)
        r = jax.nn.sigmoid(jnp.dot(xcb, wa_ref[...], preferred_element_type=F32) + ba_ref[...])
        i = jax.nn.sigmoid(jnp.dot(xcb, wx_ref[...], preferred_element_type=F32) + bx_ref[...])
        log_a = -LRU_C * r * softplus_neg_lam
        a_ref[rows, :] = jnp.exp(log_a)
        b_ref[rows, :] = jnp.sqrt(-_expm1(2.0 * log_a)) * (i * xc)

    xp_ref[...] = earlier
    cvout_ref[...] = earlier


def _in_proj(x, modpat, pattern, to_tm, g_mix, w_in_bf, conv0_tm, conv_w, conv_b, wa_bd, b_a, wx_bd, b_x, lam, bt,
             subtiles):
    tm = to_tm is not None
    t_rows = x.shape[0] * x.shape[1] if tm else x.shape[0]
    step_rows = subtiles * ROWS
    halo = (CONV_WIDTH - 1) * bt
    if tm:
        x_spec = pl.BlockSpec(memory_space=pl.ANY)
        dma_scratch = [pltpu.VMEM((2, step_rows // bt, bt, D_MODEL), F32), pltpu.SemaphoreType.DMA((2,))]
    else:
        x_spec = pl.BlockSpec((step_rows, D_MODEL), lambda i: (i, 0))
        dma_scratch = []
    mod_spec = lambda k: pl.BlockSpec((1, ROWS, D_MODEL), lambda i: (pattern, 0, k))
    row_spec = pl.BlockSpec((step_rows, D_S5), lambda i: (i, 0))
    vec = _full((1, D_LRU))
    return pl.pallas_call(
        functools.partial(_in_body, subtiles, bt, tm),
        grid=(t_rows // step_rows,),
        in_specs=[x_spec, mod_spec(1), mod_spec(0), _full((1, D_MODEL)),
                  _full((D_MODEL, D_S5 + 2 * D_LRU)), _full((halo, D_LRU)), _full((CONV_WIDTH, D_LRU)), vec,
                  _full((D_LRU, D_LRU)), vec, _full((D_LRU, D_LRU)), vec, vec],
        out_specs=[row_spec, row_spec, row_spec, row_spec, _full((halo, D_LRU))],
        out_shape=[jax.ShapeDtypeStruct((t_rows, D_S5), F32)] * 4 + [jax.ShapeDtypeStruct((halo, D_LRU), F32)],
        scratch_shapes=[pltpu.VMEM((halo, D_LRU), F32), *dma_scratch],
        compiler_params=_params("arbitrary"),
        name="in_proj",
    )(x, modpat, modpat, g_mix, w_in_bf, conv0_tm, conv_w, conv_b, wa_bd, b_a, wx_bd, b_x, lam)


def _s5_prep_body(are_ref, aim_ref, ldt_ref, bre_ref, bim_ref, abre_ref, abim_ref, bbre_ref, bbim_ref):
    a_re = are_ref[...]
    a_im = aim_ref[...]
    dt = jnp.exp(ldt_ref[...])
    mag = jnp.exp(dt * a_re)
    ang = dt * a_im
    ab_re = mag * jnp.cos(ang)
    ab_im = mag * jnp.sin(ang)
    den = a_re * a_re + a_im * a_im
    q_re = ((ab_re - 1.0) * a_re + ab_im * a_im) / den
    q_im = (ab_im * a_re - (ab_re - 1.0) * a_im) / den
    abre_ref[...] = ab_re
    abim_ref[...] = ab_im
    b_re = bre_ref[...]
    b_im = bim_ref[...]
    bbre_ref[...] = q_re[:, None, :] * b_re - q_im[:, None, :] * b_im
    bbim_ref[...] = q_re[:, None, :] * b_im + q_im[:, None, :] * b_re


def _s5_prep(a_re, a_im, log_dt, b_re_t, b_im_t):
    gn = jax.ShapeDtypeStruct((S5_GROUPS, S5_STATE), F32)
    gjn = jax.ShapeDtypeStruct((S5_GROUPS, S5_GROUP_CH, S5_STATE), F32)
    return pl.pallas_call(_s5_prep_body, out_shape=[gn, gn, gjn, gjn], name="s5_prep")(
        a_re, a_im, log_dt, b_re_t, b_im_t)


def _s5_body(tc, bt, chunks, u_ref, s0_ref, ar_ref, ai_ref, bb_ref, cc_ref, d_ref, wg_ref, bg_ref, go_ref,
             la_ref, lb_ref, lg_ref, lh0_ref, lgo_ref, y_ref, sout_ref, yl_ref, lhout_ref,
             bu_ref, st_ref, lh_ref, lst_ref):
    half = S5_COLS // S5_BLOCKS // 2
    re = [slice(half * j, half * (j + 1)) for j in range(S5_BLOCKS)]
    im = [slice(S5_COLS // 2 + half * j, S5_COLS // 2 + half * (j + 1)) for j in range(S5_BLOCKS)]

    @pl.when(pl.program_id(0) == 0)
    def _():
        st_ref[...] = s0_ref[...]
        lst_ref[...] = lh0_ref[...]

    for c in range(chunks):
        ub = u_ref[c * ROWS:(c + 1) * ROWS, :].astype(BF16)
        for j in range(S5_BLOCKS):
            bu = jnp.dot(ub[:, LANES * j:LANES * (j + 1)], bb_ref[j], preferred_element_type=F32)
            bu_ref[c, :, re[j]] = bu[:, :half]
            bu_ref[c, :, im[j]] = bu[:, half:]

    for sb in range(bt // SUBLANES):
        base = sb * SUBLANES
        hr = [st_ref[base:base + SUBLANES, re[j]] for j in range(S5_BLOCKS)]
        hi = [st_ref[base:base + SUBLANES, im[j]] for j in range(S5_BLOCKS)]
        hl = lst_ref[base:base + SUBLANES, :]
        for c in range(chunks):
            for t in range(tc):
                r0 = t * bt + base
                row = c * ROWS + r0
                hl = la_ref[row:row + SUBLANES, :] * hl + lb_ref[row:row + SUBLANES, :]
                lh_ref[row:row + SUBLANES, :] = hl
                for j in range(S5_BLOCKS):
                    ar = ar_ref[:, re[j]]
                    ai = ai_ref[:, re[j]]
                    h_re = ar * hr[j] - ai * hi[j] + bu_ref[c, r0:r0 + SUBLANES, re[j]]
                    h_im = ar * hi[j] + ai * hr[j] + bu_ref[c, r0:r0 + SUBLANES, im[j]]
                    bu_ref[c, r0:r0 + SUBLANES, re[j]] = h_re
                    bu_ref[c, r0:r0 + SUBLANES, im[j]] = h_im
                    hr[j], hi[j] = h_re, h_im
        for j in range(S5_BLOCKS):
            st_ref[base:base + SUBLANES, re[j]] = hr[j]
            st_ref[base:base + SUBLANES, im[j]] = hi[j]
        lst_ref[base:base + SUBLANES, :] = hl
    sout_ref[...] = st_ref[...]
    lhout_ref[...] = lst_ref[...]
    yl_ref[...] = _rms(lh_ref[...] * lg_ref[...], lgo_ref[...]).astype(BF16)

    for c in range(chunks):
        y = jnp.concatenate(
            [jnp.dot(jnp.concatenate([bu_ref[c, :, re[j]], bu_ref[c, :, im[j]]], axis=-1).astype(BF16), cc_ref[j],
                     preferred_element_type=F32) for j in range(S5_BLOCKS)], axis=-1)
        y = y + d_ref[...] * u_ref[c * ROWS:(c + 1) * ROWS, :]
        g = _gelu(y)
        z = jnp.dot(g.astype(BF16), wg_ref[...], preferred_element_type=F32) + bg_ref[...]
        out = g * jax.nn.sigmoid(z)
        y_ref[c * ROWS:(c + 1) * ROWS, :] = _rms(out, go_ref[...]).astype(BF16)


def _mixers(u, s0, ar8, ai8, bb, cc, d, w_glu_bf, b_glu, g_out_s5, lru_a, lru_b, lru_gg, lru_h0, g_out_lru, bt,
            chunks):
    tc = ROWS // bt
    t_rows = u.shape[0]
    step_rows = chunks * ROWS
    assert t_rows % step_rows == 0
    rows_f32 = pl.BlockSpec((step_rows, D_S5), lambda i: (i, 0))
    vec = _full((1, D_S5))
    return pl.pallas_call(
        functools.partial(_s5_body, tc, bt, chunks),
        grid=(t_rows // step_rows,),
        in_specs=[rows_f32, _full((bt, S5_COLS)), _full(ar8.shape), _full(ai8.shape), _full(bb.shape),
                  _full(cc.shape), vec, _full((D_S5, D_S5)), vec, vec,
                  rows_f32, rows_f32, rows_f32, _full((bt, D_LRU)), vec],
        out_specs=[rows_f32, _full((bt, S5_COLS)), rows_f32, _full((bt, D_LRU))],
        out_shape=[jax.ShapeDtypeStruct((t_rows, D_S5), BF16), jax.ShapeDtypeStruct((bt, S5_COLS), F32),
                   jax.ShapeDtypeStruct((t_rows, D_LRU), BF16), jax.ShapeDtypeStruct((bt, D_LRU), F32)],
        scratch_shapes=[pltpu.VMEM((chunks, ROWS, S5_COLS), F32), pltpu.VMEM((bt, S5_COLS), F32),
                        pltpu.VMEM((step_rows, D_LRU), F32), pltpu.VMEM((bt, D_LRU), F32)],
        compiler_params=_params("arbitrary"),
        name="s5_lru_mixers",
    )(u, s0, ar8, ai8, bb, cc, d, w_glu_bf, b_glu, g_out_s5, lru_a, lru_b, lru_gg, lru_h0, g_out_lru)


def _route_tile(xn, x_hi, rw_hi, rw_lo, rb):
    x_lo = (xn - x_hi.astype(F32)).astype(BF16)
    nt = (((1,), (1,)), ((), ()))
    logits = (lax.dot_general(rw_hi, x_hi, nt, preferred_element_type=F32)
              + lax.dot_general(rw_hi, x_lo, nt, preferred_element_type=F32)
              + lax.dot_general(rw_lo, x_hi, nt, preferred_element_type=F32)) + rb

    e_iota = lax.broadcasted_iota(jnp.int32, (N_EXPERTS, ROWS), 0).astype(F32)
    work = logits
    sels, vals = [], []
    for _ in range(TOP_K):
        m = jnp.max(work, axis=0, keepdims=True)
        idx = jnp.min(jnp.where(work == m, e_iota, float(N_EXPERTS)), axis=0, keepdims=True)
        sel = e_iota == idx
        work = jnp.where(sel, -jnp.inf, work)
        sels.append(sel)
        vals.append(m)
    exps = [jnp.exp(v - vals[0]) for v in vals]
    denom = exps[0] + exps[1] + exps[2] + exps[3]
    gates = [e / denom for e in exps]

    onehot = sels[0] | sels[1] | sels[2] | sels[3]
    rr = lax.broadcasted_iota(jnp.int32, (ROWS, ROWS), 0)
    cc = lax.broadcasted_iota(jnp.int32, (ROWS, ROWS), 1)
    before = (rr < cc).astype(BF16)
    prefix = jnp.dot(onehot.astype(BF16), before, preferred_element_type=F32)
    cnt = jnp.sum(onehot.astype(F32), axis=1, keepdims=True)
    cnt_pad = jnp.floor((cnt + (SEG_ALIGN - 1)) * (1.0 / SEG_ALIGN)) * SEG_ALIGN
    er = lax.broadcasted_iota(jnp.int32, (N_EXPERTS, N_EXPERTS), 0)
    ec = lax.broadcasted_iota(jnp.int32, (N_EXPERTS, N_EXPERTS), 1)
    seg_start = jnp.dot((ec < er).astype(BF16), jnp.broadcast_to(cnt_pad, (N_EXPERTS, LANES)).astype(BF16),
                        preferred_element_type=F32)[:, 0:1]
    where_to = prefix + seg_start
    poss = [jnp.sum(jnp.where(s, where_to, 0.0), axis=0, keepdims=True) for s in sels]

    s_iota = lax.broadcasted_iota(jnp.int32, (2 * TOP_K, ROWS), 0)
    pg = jnp.zeros((2 * TOP_K, ROWS), F32)
    for k in range(TOP_K):
        pg = jnp.where(s_iota == k, poss[k], pg)
        pg = jnp.where(s_iota == TOP_K + k, gates[k], pg)
    return pg, cnt


def _out_body(subtiles, tm_rows, *refs):
    if tm_rows:
        ys_ref, yl_ref, x_ref, g1_ref, sc_ref, sh_ref, perm_ref, *refs = refs
    else:
        ys_ref, yl_ref, x_ref, g1_ref, sc_ref, sh_ref, *refs = refs
    wo_ref, gf_ref, rwh_ref, rwl_ref, rb_ref, h_ref, xn_ref, pg_ref, cnt_ref = refs
    for c in range(subtiles):
        rows = slice(c * ROWS, (c + 1) * ROWS)
        ys = ys_ref[rows, :]
        yl = yl_ref[rows, :]
        if tm_rows:
            ys = jnp.dot(perm_ref[...], ys, preferred_element_type=F32).astype(BF16)
            yl = jnp.dot(perm_ref[...], yl, preferred_element_type=F32).astype(BF16)
            tc = x_ref.shape[1] // subtiles
            x = x_ref[:, c * tc:(c + 1) * tc, :].reshape(ROWS, D_MODEL)
        else:
            x = x_ref[rows, :]
        mixed = (jnp.dot(ys, wo_ref[0], preferred_element_type=F32)
                 + jnp.dot(yl, wo_ref[1], preferred_element_type=F32))
        h = x + g1_ref[0] * mixed
        if tm_rows:
            h_ref[:, c * tc:(c + 1) * tc, :] = h.reshape(x_ref.shape[0], tc, D_MODEL)
        else:
            h_ref[rows, :] = h
        xn = _rms(h, gf_ref[...]) * (1.0 + sc_ref[0]) + sh_ref[0]
        x_hi = xn.astype(BF16)
        xn_ref[rows, :] = x_hi
        pg, cnt = _route_tile(xn, x_hi, rwh_ref[...], rwl_ref[...], rb_ref[...])
        pg_ref[:, rows] = pg
        cnt_ref[c] = jnp.broadcast_to(cnt, (N_EXPERTS, LANES))


def _out_proj(ys5, ylru, x, modpat, pattern, from_tm, w_out_bf, g_ffn, rw_hi, rw_lo, router_b, subtiles):
    tm_rows = from_tm is not None
    t_rows = ys5.shape[0]
    step_rows = subtiles * ROWS
    half_spec = pl.BlockSpec((step_rows, D_S5), lambda i: (i, 0))
    if tm_rows:
        x_spec = pl.BlockSpec((x.shape[0], step_rows // x.shape[0], D_MODEL), lambda i: (0, i, 0))
    else:
        x_spec = pl.BlockSpec((step_rows, D_MODEL), lambda i: (i, 0))
    mod_spec = lambda k: pl.BlockSpec((1, ROWS, D_MODEL), lambda i: (pattern, 0, k))
    perm = [_full((ROWS, ROWS))] if tm_rows else []
    return pl.pallas_call(
        functools.partial(_out_body, subtiles, tm_rows),
        grid=(t_rows // step_rows,),
        in_specs=[half_spec, half_spec, x_spec, mod_spec(2), mod_spec(4), mod_spec(3), *perm,
                  _full((2, D_S5, D_MODEL)), _full((1, D_MODEL)),
                  _full((N_EXPERTS, D_MODEL)), _full((N_EXPERTS, D_MODEL)), _full((N_EXPERTS, 1))],
        out_specs=[x_spec, pl.BlockSpec((step_rows, D_MODEL), lambda i: (i, 0)),
                   pl.BlockSpec((2 * TOP_K, step_rows), lambda i: (0, i)),
                   pl.BlockSpec((subtiles, N_EXPERTS, LANES), lambda i: (i, 0, 0))],
        out_shape=[jax.ShapeDtypeStruct(x.shape, F32), jax.ShapeDtypeStruct((t_rows, D_MODEL), BF16),
                   jax.ShapeDtypeStruct((2 * TOP_K, t_rows), F32),
                   jax.ShapeDtypeStruct((t_rows // ROWS, N_EXPERTS, LANES), F32)],
        compiler_params=_params("arbitrary"),
        name="out_proj_router",
    )(ys5, ylru, x, modpat, modpat, modpat, *([from_tm] if tm_rows else []), w_out_bf, g_ffn, rw_hi, rw_lo,
      router_b)


def _start_segments(n_ref, hbm_ref, vmem_ref, hbm, vmem_buf, sem, step, to_hbm):
    for e in range(N_EXPERTS):
        n = n_ref[step * N_EXPERTS + e]
        h0 = hbm_ref[step * N_EXPERTS + e]
        v0 = vmem_ref[step * N_EXPERTS + e]

        def piece(off, size, h0=h0, v0=v0):
            h = hbm.at[pl.ds(pl.multiple_of(h0 + off, SEG_ALIGN), size)]
            v = vmem_buf.at[pl.ds(pl.multiple_of(v0 + off, SEG_ALIGN), size)]
            (pltpu.make_async_copy(v, h, sem) if to_hbm else pltpu.make_async_copy(h, v, sem)).start()

        _row_pieces(n, ROWS, piece)


def _wait_rows(total, largest, hbm, sem):
    _row_pieces(total, largest, lambda off, size: pltpu.make_async_copy(
        hbm.at[pl.ds(0, size)], hbm.at[pl.ds(0, size)], sem).wait())


def _dispatch_body(tiles_p, n_ref, glob_ref, local_ref, tot_ref, nb_ref, pg_ref, xp_ref, xsm_ref, xs_hbm, stage,
                   sem):
    j = pl.program_id(0)
    last = pl.num_programs(0) - 1
    slot = j % 2

    def unused_blocks(act):
        def blk(b, carry):
            act(pltpu.make_async_copy(stage.at[slot, pl.ds(0, MOE_TM)],
                                      xs_hbm.at[pl.ds(pl.multiple_of(b * MOE_TM, MOE_TM), MOE_TM)], sem.at[slot]))
            return carry
        lax.fori_loop(nb_ref[0], xs_hbm.shape[0] // MOE_TM, blk, 0)

    def wait_step(step, s):
        _wait_rows(tot_ref[step], WAIT_MAX_PIECE, xs_hbm, sem.at[s])

    @pl.when(j >= 2)
    def _():
        wait_step(jnp.maximum(j - 2, 0), slot)

    @pl.when(j < last)
    def _():
        x = jnp.where(j < tiles_p, xp_ref[...], xsm_ref[...])
        pos = pg_ref[0:TOP_K, :]
        r = lax.broadcasted_iota(jnp.int32, (SORT_CHUNK, ROWS), 0).astype(F32).astype(BF16)
        for c in range(SORT_ROWS // SORT_CHUNK):
            rel = (pos - float(c * SORT_CHUNK)).astype(BF16)
            pick = (r == rel[0:1]) | (r == rel[1:2]) | (r == rel[2:3]) | (r == rel[3:4])
            stage[slot, c * SORT_CHUNK:(c + 1) * SORT_CHUNK, :] = _pack_rows(jnp.dot(
                jnp.where(pick, jnp.ones((), BF16), jnp.zeros((), BF16)), x, preferred_element_type=F32))

    @pl.when(j == last)
    def _():
        stage[slot, 0:MOE_TM, :] = jnp.zeros((MOE_TM, PACKED), U32)

    _start_segments(n_ref, glob_ref, local_ref, xs_hbm, stage.at[slot], sem.at[slot], j, True)

    @pl.when(j == last)
    def _():
        unused_blocks(lambda c: c.start())

        @pl.when(j >= 1)
        def _():
            wait_step(jnp.maximum(j - 1, 0), 1 - slot)
        wait_step(j, slot)
        unused_blocks(lambda c: c.wait())


def _dispatch(seg_n, seg_glob, seg_local, seg_tot, nb_used, pg, xn_p, xn_s, n_rows):
    tiles_p = xn_p.shape[0] // ROWS
    tiles_s = xn_s.shape[0] // ROWS
    tiles = tiles_p + tiles_s
    grid_spec = pltpu.PrefetchScalarGridSpec(
        num_scalar_prefetch=5,
        grid=(tiles + 1,),
        in_specs=[pl.BlockSpec((2 * TOP_K, ROWS), lambda j, *_: (0, jnp.minimum(j, tiles - 1))),
                  pl.BlockSpec((ROWS, D_MODEL), lambda j, *_: (jnp.minimum(j, tiles_p - 1), 0)),
                  pl.BlockSpec((ROWS, D_MODEL), lambda j, *_: (jnp.clip(j - tiles_p, 0, tiles_s - 1), 0))],
        out_specs=pl.BlockSpec(memory_space=pl.ANY),
        scratch_shapes=[pltpu.VMEM((2, SORT_ROWS, PACKED), U32), pltpu.SemaphoreType.DMA((2,))],
    )
    return pl.pallas_call(
        functools.partial(_dispatch_body, tiles_p),
        grid_spec=grid_spec,
        out_shape=jax.ShapeDtypeStruct((n_rows, PACKED), U32),
        compiler_params=_params("arbitrary"),
        name="moe_dispatch",
    )(seg_n, seg_glob, seg_local, seg_tot, nb_used, pg, xn_p, xn_s)


def _moe_body(be_ref, nxt_ref, nv_ref, nb_ref, xs_ref, wgu_hbm, bgu_ref, wd_hbm, bd_ref, ys_ref,
              wgu_f32, wd_f32, wgu_bf, wd_bf, sem):
    i = pl.program_id(0)

    def weight_copies(e):
        return (pltpu.make_async_copy(wgu_hbm.at[e], wgu_f32, sem.at[0]),
                pltpu.make_async_copy(wd_hbm.at[e], wd_f32, sem.at[1]))

    @pl.when(i >= nb_ref[0])
    def _():
        ys_ref[...] = jnp.zeros_like(ys_ref)

    @pl.when(i < nb_ref[0])
    def _():
        e = be_ref[i]

        @pl.when(i == 0)
        def _():
            for c in weight_copies(e):
                c.start()

        @pl.when(jnp.logical_or(i == 0, e != be_ref[jnp.maximum(i - 1, 0)]))
        def _():
            for c in weight_copies(e):
                c.wait()
            wgu_bf[...] = wgu_f32[...].astype(BF16)
            wd_bf[...] = wd_f32[...].astype(BF16)

            @pl.when(nxt_ref[i] >= 0)
            def _():
                for c in weight_copies(nxt_ref[i]):
                    c.start()

        def ffn(packed):
            hg = jnp.dot(_unpack_rows(packed), wgu_bf[...], preferred_element_type=F32) + bgu_ref[0]
            gate = jnp.minimum(hg[:, :D_FF], SWIGLU_LIMIT)
            up = jnp.clip(hg[:, D_FF:], -SWIGLU_LIMIT, SWIGLU_LIMIT)
            act = (up + 1.0) * (gate * jax.nn.sigmoid(SWIGLU_ALPHA * gate))
            y = jnp.dot(act.astype(BF16), wd_bf[...], preferred_element_type=F32) + bd_ref[0]
            return _pack_rows(y.astype(BF16).astype(F32))

        for parts in range(1, MOE_TM // MOE_PART + 1):
            lo = (parts - 1) * MOE_PART
            last_path = parts == MOE_TM // MOE_PART

            @pl.when(jnp.logical_and(nv_ref[i] > lo, jnp.logical_or(last_path, nv_ref[i] <= lo + MOE_PART)))
            def _(parts=parts):
                rows = parts * MOE_PART
                ys_ref[0:rows, :] = ffn(xs_ref[0:rows, :])
                if rows < MOE_TM:
                    ys_ref[rows:, :] = jnp.zeros((MOE_TM - rows, PACKED), U32)


def _moe(xs, block_expert, block_next, block_rows, nb_used, w_gu, b_gu, w_down, b_down):
    n_blocks = xs.shape[0] // MOE_TM
    in_rows = pl.BlockSpec((MOE_TM, PACKED), lambda i, be, nx, nv, nb: (jnp.minimum(i, nb[0] - 1), 0))
    grid_spec = pltpu.PrefetchScalarGridSpec(
        num_scalar_prefetch=4,
        grid=(n_blocks,),
        in_specs=[in_rows,
                  pl.BlockSpec(memory_space=pl.ANY),
                  pl.BlockSpec((1, 1, 2 * D_FF), lambda i, be, nx, nv, nb: (be[i], 0, 0)),
                  pl.BlockSpec(memory_space=pl.ANY),
                  pl.BlockSpec((1, 1, D_MODEL), lambda i, be, nx, nv, nb: (be[i], 0, 0))],
        out_specs=pl.BlockSpec((MOE_TM, PACKED), lambda i, be, nx, nv, nb: (i, 0)),
        scratch_shapes=[pltpu.VMEM((D_MODEL, 2 * D_FF), F32), pltpu.VMEM((D_FF, D_MODEL), F32),
                        pltpu.VMEM((D_MODEL, 2 * D_FF), BF16), pltpu.VMEM((D_FF, D_MODEL), BF16),
                        pltpu.SemaphoreType.DMA((2,))],
    )
    return pl.pallas_call(
        _moe_body,
        grid_spec=grid_spec,
        out_shape=jax.ShapeDtypeStruct(xs.shape, U32),
        compiler_params=_params("arbitrary"),
        name="moe_experts",
    )(block_expert, block_next, block_rows, nb_used, xs, w_gu, b_gu, w_down, b_down)


def _fin_body(tiles_p, n_ref, glob_ref, local_ref, tot_ref, hp_ref, hs_ref, pg_ref, g2_ref, gf_ref, ys_hbm,
              op_ref, os_ref, ybuf, sem):
    j = pl.program_id(0)
    tiles = pl.num_programs(0)
    slot = j % 2
    is_prompt = j < tiles_p

    def fetch(step, s):
        _start_segments(n_ref, glob_ref, local_ref, ys_hbm, ybuf.at[s], sem.at[s], step, False)

    @pl.when(j == 0)
    def _():
        ybuf[...] = jnp.zeros_like(ybuf)
        fetch(0, 0)

    _wait_rows(tot_ref[j], WAIT_MAX_PIECE, ys_hbm, sem.at[slot])

    @pl.when(j + 1 < tiles)
    def _():
        fetch(jnp.minimum(j + 1, tiles - 1), 1 - slot)

    pg = pg_ref[...]
    ff = jnp.zeros((ROWS, D_MODEL), F32)
    r = lax.broadcasted_iota(jnp.int32, (ROWS, SORT_CHUNK), 1).astype(F32).astype(BF16)
    gates = pg[:, TOP_K:].astype(BF16)
    for c in range(SORT_ROWS // SORT_CHUNK):
        rel = (pg[:, :TOP_K] - float(c * SORT_CHUNK)).astype(BF16)
        w = jnp.zeros((ROWS, SORT_CHUNK), BF16)
        for k in range(TOP_K):
            w = jnp.where(r == rel[:, k:k + 1], gates[:, k:k + 1], w)
        ff = ff + jnp.dot(w, _unpack_rows(ybuf[slot, c * SORT_CHUNK:(c + 1) * SORT_CHUNK, :]),
                          preferred_element_type=F32)
    h = jnp.where(is_prompt, hp_ref[...].reshape(ROWS, D_MODEL), hs_ref[...])
    y = _rms(h + g2_ref[0] * ff, gf_ref[...])

    @pl.when(is_prompt)
    def _():
        op_ref[...] = y.reshape(op_ref.shape)

    @pl.when(jnp.logical_not(is_prompt))
    def _():
        os_ref[...] = y


def _final(seg_n, seg_glob, seg_local, seg_tot, h_p, h_s, pg_t, modpat, g_final, ys, tiles_p, tiles_s):
    hp_spec, hs_spec = _tile_specs(h_p.shape[0], tiles_p, tiles_s)
    grid_spec = pltpu.PrefetchScalarGridSpec(
        num_scalar_prefetch=4,
        grid=(tiles_p + tiles_s,),
        in_specs=[hp_spec, hs_spec,
                  pl.BlockSpec((ROWS, 2 * TOP_K), lambda j, *_: (j, 0)),
                  _mod_spec(5, tiles_p), _full((1, D_MODEL)),
                  pl.BlockSpec(memory_space=pl.ANY)],
        out_specs=[hp_spec, hs_spec],
        scratch_shapes=[pltpu.VMEM((2, SORT_ROWS, PACKED), U32), pltpu.SemaphoreType.DMA((2,))],
    )
    return pl.pallas_call(
        functools.partial(_fin_body, tiles_p),
        grid_spec=grid_spec,
        out_shape=[jax.ShapeDtypeStruct(h_p.shape, F32), jax.ShapeDtypeStruct(h_s.shape, F32)],
        compiler_params=_params("arbitrary"),
        name="combine_final",
    )(seg_n, seg_glob, seg_local, seg_tot, h_p, h_s, pg_t, modpat, g_final, ys)


def _block_diag(w):
    h, i, j = w.shape
    return jnp.einsum('hij,hk->hikj', w, jnp.eye(h, dtype=w.dtype)).reshape(h * i, h * j)


def _s5_cols(re, im):
    b = re.shape[0]
    return jnp.concatenate([re.reshape(b, -1), im.reshape(b, -1)], axis=1)


def _s5_uncols(cols):
    b = cols.shape[0]
    return (cols[:, :S5_COLS // 2].reshape(b, S5_GROUPS, S5_STATE),
            cols[:, S5_COLS // 2:].reshape(b, S5_GROUPS, S5_STATE))


def _moe_rows_bound(tiles):
    worst = tiles * (TOP_K * ROWS + N_EXPERTS * (SEG_ALIGN - 1)) + N_EXPERTS * (MOE_TM - SEG_ALIGN)
    return (worst + MOE_TM - 1) // MOE_TM * MOE_TM


def _plan(cnt):
    cnt = cnt.astype(jnp.int32)
    tiles = cnt.shape[0]
    cp = (cnt + SEG_ALIGN - 1) // SEG_ALIGN * SEG_ALIGN
    local = jnp.cumsum(cp, axis=1) - cp
    group = jnp.sum(cp, axis=0)
    group_pad = (group + MOE_TM - 1) // MOE_TM * MOE_TM
    pend = jnp.cumsum(group_pad)
    pstart = pend - group_pad
    glob = pstart[None, :] + jnp.cumsum(cp, axis=0) - cp
    gap = group_pad - group
    seg_n = jnp.concatenate([cp, gap[None]], axis=0).reshape(-1)
    seg_local = jnp.concatenate([local, jnp.zeros((1, N_EXPERTS), jnp.int32)], axis=0).reshape(-1)
    seg_glob = jnp.concatenate([glob, (pstart + group)[None]], axis=0).reshape(-1)
    seg_tot = jnp.concatenate([jnp.sum(cp, axis=1), jnp.sum(gap)[None]]).astype(jnp.int32)
    n_blocks = _moe_rows_bound(tiles) // MOE_TM
    block_row0 = jnp.arange(n_blocks, dtype=jnp.int32) * MOE_TM
    block_expert = jnp.minimum(jnp.sum(block_row0[:, None] >= pend[None, :], axis=1), N_EXPERTS - 1).astype(jnp.int32)
    nb_used = (pend[-1] // MOE_TM).astype(jnp.int32).reshape(1)
    experts = jnp.arange(N_EXPERTS, dtype=jnp.int32)
    later_owner = jnp.where((experts[None, :] > experts[:, None]) & (group_pad[None, :] > 0), experts[None, :],
                            N_EXPERTS)
    next_owner = jnp.min(later_owner, axis=1)
    next_owner = jnp.where(next_owner == N_EXPERTS, -1, next_owner).astype(jnp.int32)
    owner = block_expert[:, None] == experts[None, :]
    block_next = jnp.sum(jnp.where(owner, next_owner[None, :], 0), axis=1).astype(jnp.int32)
    group_end = jnp.sum(jnp.where(owner, (pstart + group)[None, :], 0), axis=1)
    block_rows = jnp.clip(group_end - block_row0, 0, MOE_TM).astype(jnp.int32)
    return seg_n, seg_glob, seg_local, seg_tot, block_expert, block_next, block_rows, nb_used


def kernel(x_prompt, x_sample, state_s5_re, state_s5_im, state_lru_h, state_conv, c_prompt, c_sample, w_ada, b_ada, g_mix, w_in, s5_a_re, s5_a_im, s5_log_dt, s5_b_re, s5_b_im, s5_c_re, s5_c_im, s5_d, s5_w_glu, s5_b_glu, lru_conv_w, lru_conv_b, lru_w_a, lru_b_a, lru_w_x, lru_b_x, lru_lambda, g_out_s5, g_out_lru, w_out, g_ffn, router_w, router_b, moe_w_gu, moe_b_gu, moe_w_down, moe_b_down, g_final):
    assert w_ada.shape[0] == 1, "one layer"
    bp, lp, _ = x_prompt.shape
    bs, ls, _ = x_sample.shape
    assert ROWS % bp == 0 and ROWS % bs == 0 and (bp * lp) % ROWS == 0 and (bs * ls) % ROWS == 0
    tiles_p = bp * lp // ROWS
    tiles_s = bs * ls // ROWS
    row = lambda v: v.reshape(1, -1)

    ab_re, ab_im, bb_re, bb_im = _s5_prep(s5_a_re[0], s5_a_im[0], s5_log_dt[0].reshape(S5_GROUPS, 1),
                                          jnp.swapaxes(s5_b_re[0], 1, 2), jnp.swapaxes(s5_b_im[0], 1, 2))
    gpb = S5_GROUPS // S5_BLOCKS
    eye = jnp.eye(gpb, dtype=F32)

    def in_blocks(b):
        b = b.reshape(S5_BLOCKS, gpb, S5_GROUP_CH, S5_STATE)
        return jnp.einsum('bgjn,gh->bgjhn', b, eye).reshape(S5_BLOCKS, gpb * S5_GROUP_CH, gpb * S5_STATE)

    def out_blocks(c):
        c = c.reshape(S5_BLOCKS, gpb, S5_GROUP_CH, S5_STATE)
        return jnp.einsum('bgjn,gh->bgnhj', c, eye).reshape(S5_BLOCKS, gpb * S5_STATE, gpb * S5_GROUP_CH)

    ar8 = jnp.broadcast_to(ab_re.reshape(1, -1), (SUBLANES, S5_GROUPS * S5_STATE))
    ai8 = jnp.broadcast_to(ab_im.reshape(1, -1), (SUBLANES, S5_GROUPS * S5_STATE))
    bb = jnp.concatenate([in_blocks(bb_re), in_blocks(bb_im)], axis=-1).astype(BF16)
    cc = jnp.concatenate([out_blocks(s5_c_re[0]), -out_blocks(s5_c_im[0])], axis=1).astype(BF16)
    wa_bd = _block_diag(lru_w_a[0]).astype(BF16)
    wx_bd = _block_diag(lru_w_x[0]).astype(BF16)
    rw_t = router_w[0].T
    rw_hi = rw_t.astype(BF16)
    rw_lo = (rw_t - rw_hi.astype(F32)).astype(BF16)

    tc = ROWS // bp
    modpat = _adaln(jnp.concatenate([c_prompt, c_sample], axis=0), w_ada[0], row(b_ada[0]), bp)
    r = jnp.arange(ROWS)
    tm_of = (r % tc) * bp + r // tc
    to_tm = (r[:, None] == tm_of[None, :]).astype(BF16)
    from_tm = to_tm.T
    pair = 2 if tiles_p % 2 == 0 else 1

    def conv_tm(cv):
        return jnp.swapaxes(cv, 0, 1).reshape(-1, D_LRU)

    def conv_bm(cv, b):
        return jnp.swapaxes(cv.reshape(CONV_WIDTH - 1, b, D_LRU), 0, 1)

    x_s = jnp.swapaxes(x_sample, 0, 1).reshape(bs * ls, D_MODEL)
    in_args = (row(g_mix[0]), w_in[0].astype(BF16))
    lru_args = (lru_conv_w[0], row(lru_conv_b[0]), wa_bd, row(lru_b_a[0]), wx_bd, row(lru_b_x[0]),
                row(lru_lambda[0]))
    u_p, la_p, lb_p, lg_p, cvp = _in_proj(x_prompt, modpat, 2, True, *in_args,
                                          jnp.zeros(((CONV_WIDTH - 1) * bp, D_LRU), F32), *lru_args, bp, pair)
    u_s, la_s, lb_s, lg_s, cvs = _in_proj(x_s, modpat, 1, None, *in_args, conv_tm(state_conv[0]), *lru_args, bs, 1)

    s5_args = (ar8, ai8, bb, cc, row(s5_d[0]), s5_w_glu[0].astype(BF16), row(s5_b_glu[0]), row(g_out_s5[0]))
    ys5_p, s5p, ylru_p, hp = _mixers(u_p, jnp.zeros((bp, S5_COLS), F32), *s5_args, la_p, lb_p, lg_p,
                                     jnp.zeros((bp, D_LRU), F32), row(g_out_lru[0]), bp, pair)
    ys5_s, s5s, ylru_s, hs = _mixers(u_s, _s5_cols(state_s5_re[0], state_s5_im[0]), *s5_args, la_s, lb_s, lg_s,
                                     state_lru_h[0], row(g_out_lru[0]), bs, 1)

    out_args = (w_out[0].astype(BF16).reshape(2, D_S5, D_MODEL), row(g_ffn[0]), rw_hi, rw_lo,
                router_b[0].reshape(N_EXPERTS, 1))
    h_p, xn_p, pg_p, cnt_p = _out_proj(ys5_p, ylru_p, x_prompt, modpat, 0, from_tm, *out_args, pair)
    h_s, xn_s, pg_s, cnt_s = _out_proj(ys5_s, ylru_s, x_s, modpat, 1, None, *out_args, 1)
    pg = jnp.concatenate([pg_p, pg_s], axis=1)
    cnt = jnp.concatenate([cnt_p, cnt_s], axis=0)

    seg_n, seg_glob, seg_local, seg_tot, block_expert, block_next, block_rows, nb_used = _plan(cnt[:, :, 0])
    xs = _dispatch(seg_n, seg_glob, seg_local, seg_tot, nb_used, pg, xn_p, xn_s,
                   _moe_rows_bound(tiles_p + tiles_s))
    ys = _moe(xs, block_expert, block_next, block_rows, nb_used, moe_w_gu[0], moe_b_gu[0].reshape(N_EXPERTS, 1, 2 * D_FF),
              moe_w_down[0], moe_b_down[0].reshape(N_EXPERTS, 1, D_MODEL))
    y_prompt, y_s = _final(seg_n, seg_glob, seg_local, seg_tot, h_p, h_s, pg.T, modpat, row(g_final), ys,
                           tiles_p, tiles_s)
    y_sample = jnp.swapaxes(y_s.reshape(ls, bs, D_MODEL), 0, 1)
    s5p_re, s5p_im = _s5_uncols(s5p)
    s5s_re, s5s_im = _s5_uncols(s5s)
    return (y_prompt, y_sample,
            s5p_re[None], s5p_im[None], hp[None], conv_bm(cvp, bp)[None],
            s5s_re[None], s5s_im[None], hs[None], conv_bm(cvs, bs)[None])
```

```python
import functools

import jax
import jax.numpy as jnp
from jax import lax
from jax.experimental import pallas as pl
from jax.experimental.pallas import tpu as pltpu

D_MODEL = 1024
D_S5 = 512
D_LRU = 512
S5_GROUPS = 32
S5_GROUP_CH = 16
S5_STATE = 64
S5_COLS = 2 * S5_GROUPS * S5_STATE
S5_BLOCKS = 4
LRU_C = 8.0
CONV_WIDTH = 4
N_EXPERTS = 32
TOP_K = 4
D_FF = 1024
SWIGLU_LIMIT = 7.0
SWIGLU_ALPHA = 1.702
N_MOD = 6
EPS = 1e-6

SUBLANES = 8
LANES = 128
VMEM_LIMIT = 48 * 1024 * 1024

ROWS = 512
MOE_TM = 512
MOE_PART = 128
PACKED = D_MODEL // 2
SEG_ALIGN = SUBLANES
SORT_CHUNK = 256
SORT_ROWS = -(-(TOP_K * ROWS + N_EXPERTS * (SEG_ALIGN - 1)) // SORT_CHUNK) * SORT_CHUNK
WAIT_MAX_PIECE = N_EXPERTS * MOE_TM // 2

BF16 = jnp.bfloat16
F32 = jnp.float32
U32 = jnp.uint32


def _params(*sem):
    return pltpu.CompilerParams(dimension_semantics=sem, vmem_limit_bytes=VMEM_LIMIT)


def _full(shape):
    return pl.BlockSpec(shape, lambda *_: (0,) * len(shape))


def _rms(x, g):
    return x * lax.rsqrt(jnp.mean(x * x, axis=-1, keepdims=True) + EPS) * g


def _gelu(x):
    return 0.5 * x * (1.0 + lax.erf(x * (2.0 ** -0.5)))


def _expm1(x):
    u = jnp.exp(x)
    d = u - 1.0
    return jnp.where(d == 0.0, x, jnp.where(d == -1.0, -1.0, d * x / jnp.log(u)))


def _pack_rows(x):
    lo = lax.shift_right_logical(lax.bitcast_convert_type(x[:, :PACKED], U32), jnp.uint32(16))
    hi = lax.bitcast_convert_type(x[:, PACKED:], U32) & jnp.uint32(0xFFFF0000)
    return lo | hi


def _unpack_rows(p):
    lo = lax.bitcast_convert_type(lax.shift_left(p, jnp.uint32(16)), F32)
    hi = lax.bitcast_convert_type(p & jnp.uint32(0xFFFF0000), F32)
    return jnp.concatenate([lo, hi], axis=-1).astype(BF16)


def _row_pieces(n, largest, fn):
    off = 0
    bit = largest
    while bit >= SEG_ALIGN:
        @pl.when((n & bit) != 0)
        def _(off=off, bit=bit):
            fn(off, bit)
        off = off + (n & bit)
        bit //= 2


def _mod_body(bp, c_ref, w_ref, b_ref, o_ref):
    c = c_ref[...]
    s = (c * jax.nn.sigmoid(c)).astype(BF16)
    mod = jnp.dot(s, w_ref[...].astype(BF16), preferred_element_type=F32) + b_ref[...]
    bs = mod.shape[0] - bp
    o_ref[0] = jnp.broadcast_to(mod[:bp][None], (ROWS // bp, bp, D_MODEL)).reshape(ROWS, D_MODEL)
    o_ref[1] = jnp.broadcast_to(mod[bp:][None], (ROWS // bs, bs, D_MODEL)).reshape(ROWS, D_MODEL)


def _adaln(c, w_ada, b_ada, bp):
    m = c.shape[0]
    return pl.pallas_call(
        functools.partial(_mod_body, bp),
        grid=(N_MOD,),
        in_specs=[pl.BlockSpec((m, D_MODEL), lambda j: (0, 0)),
                  pl.BlockSpec((D_MODEL, D_MODEL), lambda j: (0, j)),
                  pl.BlockSpec((1, D_MODEL), lambda j: (0, j))],
        out_specs=pl.BlockSpec((2, ROWS, D_MODEL), lambda j: (0, 0, j)),
        out_shape=jax.ShapeDtypeStruct((2, ROWS, N_MOD * D_MODEL), F32),
        compiler_params=_params("arbitrary"),
        name="adaln",
    )(c, w_ada, b_ada)


def _mod_spec(k, tiles_p):
    return pl.BlockSpec((1, ROWS, D_MODEL), lambda i, *_: (jnp.where(i < tiles_p, 0, 1), 0, k))


def _tile_specs(tiles_p, tiles_s):
    p_spec = pl.BlockSpec((ROWS, D_MODEL), lambda i, *_: (jnp.minimum(i, tiles_p - 1), 0))
    s_spec = pl.BlockSpec((ROWS, D_MODEL), lambda i, *_: (jnp.clip(i - tiles_p, 0, tiles_s - 1), 0))
    return p_spec, s_spec


def _time_major_ring(x_hbm, xbuf, sem):
    step = pl.program_id(0)
    tcs, bt = xbuf.shape[1], xbuf.shape[2]

    def copies(s, slot, act):
        for b in range(bt):
            act(pltpu.make_async_copy(x_hbm.at[b, pl.ds(s * tcs, tcs), :], xbuf.at[slot, :, b, :], sem.at[slot]))

    @pl.when(step == 0)
    def _():
        copies(0, 0, lambda c: c.start())

    copies(step, step % 2, lambda c: c.wait())

    @pl.when(step + 1 < pl.num_programs(0))
    def _():
        copies(step + 1, 1 - step % 2, lambda c: c.start())


def _in_body(subtiles, bt, to_tm, *refs):
    x_ref, sc_ref, sh_ref, *refs = refs
    (g_ref, w_ref, cv0_ref, cw_ref, cb_ref, wa_ref, ba_ref, wx_ref, bx_ref, lam_ref,
     u_ref, a_ref, b_ref, gg_ref, cvout_ref, xp_ref, *dma) = refs
    halo = (CONV_WIDTH - 1) * bt
    step = pl.program_id(0)

    if to_tm:
        xbuf, sem = dma
        _time_major_ring(x_ref, xbuf, sem)

    @pl.when(step == 0)
    def _():
        xp_ref[...] = cv0_ref[...]

    lam = lam_ref[...]
    softplus_neg_lam = jnp.maximum(-lam, 0.0) + jnp.log1p(jnp.exp(-jnp.abs(lam)))
    earlier = xp_ref[...]
    for c in range(subtiles):
        rows = slice(c * ROWS, (c + 1) * ROWS)
        if to_tm:
            tc = ROWS // bt
            x = xbuf[step % 2, c * tc:(c + 1) * tc].reshape(ROWS, D_MODEL)
        else:
            x = x_ref[rows, :]
        xn = (_rms(x, g_ref[...]) * (1.0 + sc_ref[0]) + sh_ref[0]).astype(BF16)
        p = jnp.dot(xn, w_ref[...], preferred_element_type=F32)
        u_ref[rows, :] = p[:, :D_S5]
        gg_ref[rows, :] = _gelu(p[:, D_S5 + D_LRU:])

        xr = p[:, D_S5:D_S5 + D_LRU]
        xp = jnp.concatenate([earlier, xr], axis=0)
        earlier = xr[ROWS - halo:, :]
        xc = cb_ref[...] + sum(xp[k * bt:k * bt + ROWS, :] * cw_ref[k:k + 1, :] for k in range(CONV_WIDTH))
        xcb = xc.astype(BF16)
        r = jax.nn.sigmoid(jnp.dot(xcb, wa_ref[...], preferred_element_type=F32) + ba_ref[...])
        i = jax.nn.sigmoid(jnp.dot(xcb, wx_ref[...], preferred_element_type=F32) + bx_ref[...])
        log_a = -LRU_C * r * softplus_neg_lam
        a_ref[rows, :] = jnp.exp(log_a)
        b_ref[rows, :] = jnp.sqrt(-_expm1(2.0 * log_a)) * (i * xc)

    xp_ref[...] = earlier
    cvout_ref[...] = earlier


def _in_proj(x, modpat, pattern, to_tm, g_mix, w_in_bf, conv0_tm, conv_w, conv_b, wa_bd, b_a, wx_bd, b_x, lam, bt,
             subtiles):
    tm = to_tm is not None
    t_rows = x.shape[0] * x.shape[1] if tm else x.shape[0]
    step_rows = subtiles * ROWS
    halo = (CONV_WIDTH - 1) * bt
    if tm:
        x_spec = pl.BlockSpec(memory_space=pl.ANY)
        dma_scratch = [pltpu.VMEM((2, step_rows // bt, bt, D_MODEL), F32), pltpu.SemaphoreType.DMA((2,))]
    else:
        x_spec = pl.BlockSpec((step_rows, D_MODEL), lambda i: (i, 0))
        dma_scratch = []
    mod_spec = lambda k: pl.BlockSpec((1, ROWS, D_MODEL), lambda i: (pattern, 0, k))
    row_spec = pl.BlockSpec((step_rows, D_S5), lambda i: (i, 0))
    vec = _full((1, D_LRU))
    return pl.pallas_call(
        functools.partial(_in_body, subtiles, bt, tm),
        grid=(t_rows // step_rows,),
        in_specs=[x_spec, mod_spec(1), mod_spec(0), _full((1, D_MODEL)),
                  _full((D_MODEL, D_S5 + 2 * D_LRU)), _full((halo, D_LRU)), _full((CONV_WIDTH, D_LRU)), vec,
                  _full((D_LRU, D_LRU)), vec, _full((D_LRU, D_LRU)), vec, vec],
        out_specs=[row_spec, row_spec, row_spec, row_spec, _full((halo, D_LRU))],
        out_shape=[jax.ShapeDtypeStruct((t_rows, D_S5), F32)] * 4 + [jax.ShapeDtypeStruct((halo, D_LRU), F32)],
        scratch_shapes=[pltpu.VMEM((halo, D_LRU), F32), *dma_scratch],
        compiler_params=_params("arbitrary"),
        name="in_proj",
    )(x, modpat, modpat, g_mix, w_in_bf, conv0_tm, conv_w, conv_b, wa_bd, b_a, wx_bd, b_x, lam)


def _s5_prep_body(are_ref, aim_ref, ldt_ref, bre_ref, bim_ref, abre_ref, abim_ref, bbre_ref, bbim_ref):
    a_re = are_ref[...]
    a_im = aim_ref[...]
    dt = jnp.exp(ldt_ref[...])
    mag = jnp.exp(dt * a_re)
    ang = dt * a_im
    ab_re = mag * jnp.cos(ang)
    ab_im = mag * jnp.sin(ang)
    den = a_re * a_re + a_im * a_im
    q_re = ((ab_re - 1.0) * a_re + ab_im * a_im) / den
    q_im = (ab_im * a_re - (ab_re - 1.0) * a_im) / den
    abre_ref[...] = ab_re
    abim_ref[...] = ab_im
    b_re = bre_ref[...]
    b_im = bim_ref[...]
    bbre_ref[...] = q_re[:, None, :] * b_re - q_im[:, None, :] * b_im
    bbim_ref[...] = q_re[:, None, :] * b_im + q_im[:, None, :] * b_re


def _s5_prep(a_re, a_im, log_dt, b_re_t, b_im_t):
    gn = jax.ShapeDtypeStruct((S5_GROUPS, S5_STATE), F32)
    gjn = jax.ShapeDtypeStruct((S5_GROUPS, S5_GROUP_CH, S5_STATE), F32)
    return pl.pallas_call(_s5_prep_body, out_shape=[gn, gn, gjn, gjn], name="s5_prep")(
        a_re, a_im, log_dt, b_re_t, b_im_t)


def _s5_body(tc, bt, chunks, u_ref, s0_ref, ar_ref, ai_ref, bb_ref, cc_ref, d_ref, wg_ref, bg_ref, go_ref,
             la_ref, lb_ref, lg_ref, lh0_ref, lgo_ref, y_ref, sout_ref, yl_ref, lhout_ref,
             bu_ref, st_ref, lh_ref, lst_ref):
    half = S5_COLS // S5_BLOCKS // 2
    re = [slice(half * j, half * (j + 1)) for j in range(S5_BLOCKS)]
    im = [slice(S5_COLS // 2 + half * j, S5_COLS // 2 + half * (j + 1)) for j in range(S5_BLOCKS)]

    @pl.when(pl.program_id(0) == 0)
    def _():
        st_ref[...] = s0_ref[...]
        lst_ref[...] = lh0_ref[...]

    for c in range(chunks):
        ub = u_ref[c * ROWS:(c + 1) * ROWS, :].astype(BF16)
        for j in range(S5_BLOCKS):
            bu = jnp.dot(ub[:, LANES * j:LANES * (j + 1)], bb_ref[j], preferred_element_type=F32)
            bu_ref[c, :, re[j]] = bu[:, :half]
            bu_ref[c, :, im[j]] = bu[:, half:]

    for sb in range(bt // SUBLANES):
        base = sb * SUBLANES
        hr = [st_ref[base:base + SUBLANES, re[j]] for j in range(S5_BLOCKS)]
        hi = [st_ref[base:base + SUBLANES, im[j]] for j in range(S5_BLOCKS)]
        hl = lst_ref[base:base + SUBLANES, :]
        for c in range(chunks):
            for t in range(tc):
                r0 = t * bt + base
                row = c * ROWS + r0
                hl = la_ref[row:row + SUBLANES, :] * hl + lb_ref[row:row + SUBLANES, :]
                lh_ref[row:row + SUBLANES, :] = hl
                for j in range(S5_BLOCKS):
                    ar = ar_ref[:, re[j]]
                    ai = ai_ref[:, re[j]]
                    h_re = ar * hr[j] - ai * hi[j] + bu_ref[c, r0:r0 + SUBLANES, re[j]]
                    h_im = ar * hi[j] + ai * hr[j] + bu_ref[c, r0:r0 + SUBLANES, im[j]]
                    bu_ref[c, r0:r0 + SUBLANES, re[j]] = h_re
                    bu_ref[c, r0:r0 + SUBLANES, im[j]] = h_im
                    hr[j], hi[j] = h_re, h_im
        for j in range(S5_BLOCKS):
            st_ref[base:base + SUBLANES, re[j]] = hr[j]
            st_ref[base:base + SUBLANES, im[j]] = hi[j]
        lst_ref[base:base + SUBLANES, :] = hl
    sout_ref[...] = st_ref[...]
    lhout_ref[...] = lst_ref[...]
    yl_ref[...] = _rms(lh_ref[...] * lg_ref[...], lgo_ref[...]).astype(BF16)

    for c in range(chunks):
        y = jnp.concatenate(
            [jnp.dot(jnp.concatenate([bu_ref[c, :, re[j]], bu_ref[c, :, im[j]]], axis=-1).astype(BF16), cc_ref[j],
                     preferred_element_type=F32) for j in range(S5_BLOCKS)], axis=-1)
        y = y + d_ref[...] * u_ref[c * ROWS:(c + 1) * ROWS, :]
        g = _gelu(y)
        z = jnp.dot(g.astype(BF16), wg_ref[...], preferred_element_type=F32) + bg_ref[...]
        out = g * jax.nn.sigmoid(z)
        y_ref[c * ROWS:(c + 1) * ROWS, :] = _rms(out, go_ref[...]).astype(BF16)


def _mixers(u, s0, ar8, ai8, bb, cc, d, w_glu_bf, b_glu, g_out_s5, lru_a, lru_b, lru_gg, lru_h0, g_out_lru, bt,
            chunks):
    tc = ROWS // bt
    t_rows = u.shape[0]
    step_rows = chunks * ROWS
    assert t_rows % step_rows == 0
    rows_f32 = pl.BlockSpec((step_rows, D_S5), lambda i: (i, 0))
    vec = _full((1, D_S5))
    return pl.pallas_call(
        functools.partial(_s5_body, tc, bt, chunks),
        grid=(t_rows // step_rows,),
        in_specs=[rows_f32, _full((bt, S5_COLS)), _full(ar8.shape), _full(ai8.shape), _full(bb.shape),
                  _full(cc.shape), vec, _full((D_S5, D_S5)), vec, vec,
                  rows_f32, rows_f32, rows_f32, _full((bt, D_LRU)), vec],
        out_specs=[rows_f32, _full((bt, S5_COLS)), rows_f32, _full((bt, D_LRU))],
        out_shape=[jax.ShapeDtypeStruct((t_rows, D_S5), BF16), jax.ShapeDtypeStruct((bt, S5_COLS), F32),
                   jax.ShapeDtypeStruct((t_rows, D_LRU), BF16), jax.ShapeDtypeStruct((bt, D_LRU), F32)],
        scratch_shapes=[pltpu.VMEM((chunks, ROWS, S5_COLS), F32), pltpu.VMEM((bt, S5_COLS), F32),
                        pltpu.VMEM((step_rows, D_LRU), F32), pltpu.VMEM((bt, D_LRU), F32)],
        compiler_params=_params("arbitrary"),
        name="s5_lru_mixers",
    )(u, s0, ar8, ai8, bb, cc, d, w_glu_bf, b_glu, g_out_s5, lru_a, lru_b, lru_gg, lru_h0, g_out_lru)


def _route_tile(xn, x_hi, rw_hi, rw_lo, rb):
    x_lo = (xn - x_hi.astype(F32)).astype(BF16)
    nt = (((1,), (1,)), ((), ()))
    logits = (lax.dot_general(rw_hi, x_hi, nt, preferred_element_type=F32)
              + lax.dot_general(rw_hi, x_lo, nt, preferred_element_type=F32)
              + lax.dot_general(rw_lo, x_hi, nt, preferred_element_type=F32)) + rb

    e_iota = lax.broadcasted_iota(jnp.int32, (N_EXPERTS, ROWS), 0).astype(F32)
    work = logits
    sels, vals = [], []
    for _ in range(TOP_K):
        m = jnp.max(work, axis=0, keepdims=True)
        idx = jnp.min(jnp.where(work == m, e_iota, float(N_EXPERTS)), axis=0, keepdims=True)
        sel = e_iota == idx
        work = jnp.where(sel, -jnp.inf, work)
        sels.append(sel)
        vals.append(m)
    exps = [jnp.exp(v - vals[0]) for v in vals]
    denom = exps[0] + exps[1] + exps[2] + exps[3]
    gates = [e / denom for e in exps]

    onehot = sels[0] | sels[1] | sels[2] | sels[3]
    rr = lax.broadcasted_iota(jnp.int32, (ROWS, ROWS), 0)
    cc = lax.broadcasted_iota(jnp.int32, (ROWS, ROWS), 1)
    before = (rr < cc).astype(BF16)
    prefix = jnp.dot(onehot.astype(BF16), before, preferred_element_type=F32)
    cnt = jnp.sum(onehot.astype(F32), axis=1, keepdims=True)
    cnt_pad = jnp.floor((cnt + (SEG_ALIGN - 1)) * (1.0 / SEG_ALIGN)) * SEG_ALIGN
    er = lax.broadcasted_iota(jnp.int32, (N_EXPERTS, N_EXPERTS), 0)
    ec = lax.broadcasted_iota(jnp.int32, (N_EXPERTS, N_EXPERTS), 1)
    seg_start = jnp.dot((ec < er).astype(BF16), jnp.broadcast_to(cnt_pad, (N_EXPERTS, LANES)).astype(BF16),
                        preferred_element_type=F32)[:, 0:1]
    where_to = prefix + seg_start
    poss = [jnp.sum(jnp.where(s, where_to, 0.0), axis=0, keepdims=True) for s in sels]

    s_iota = lax.broadcasted_iota(jnp.int32, (2 * TOP_K, ROWS), 0)
    pg = jnp.zeros((2 * TOP_K, ROWS), F32)
    for k in range(TOP_K):
        pg = jnp.where(s_iota == k, poss[k], pg)
        pg = jnp.where(s_iota == TOP_K + k, gates[k], pg)
    return pg, cnt


def _out_body(subtiles, tm_rows, *refs):
    ys_ref, yl_ref, x_ref, g1_ref, sc_ref, sh_ref, wo_ref, gf_ref, rwh_ref, rwl_ref, rb_ref, *refs = refs
    h_ref, xn_ref, pg_ref, cnt_ref, *dma = refs
    if tm_rows:
        xbuf, sem = dma
        _time_major_ring(x_ref, xbuf, sem)
    for c in range(subtiles):
        rows = slice(c * ROWS, (c + 1) * ROWS)
        if tm_rows:
            tc = ROWS // xbuf.shape[2]
            x = xbuf[pl.program_id(0) % 2, c * tc:(c + 1) * tc].reshape(ROWS, D_MODEL)
        else:
            x = x_ref[rows, :]
        mixed = (jnp.dot(ys_ref[rows, :], wo_ref[0], preferred_element_type=F32)
                 + jnp.dot(yl_ref[rows, :], wo_ref[1], preferred_element_type=F32))
        h = x + g1_ref[0] * mixed
        h_ref[rows, :] = h
        xn = _rms(h, gf_ref[...]) * (1.0 + sc_ref[0]) + sh_ref[0]
        x_hi = xn.astype(BF16)
        xn_ref[rows, :] = x_hi
        pg, cnt = _route_tile(xn, x_hi, rwh_ref[...], rwl_ref[...], rb_ref[...])
        pg_ref[:, rows] = pg
        cnt_ref[c] = jnp.broadcast_to(cnt, (N_EXPERTS, LANES))


def _out_proj(ys5, ylru, x, modpat, pattern, tm_rows, w_out_bf, g_ffn, rw_hi, rw_lo, router_b, subtiles):
    t_rows = ys5.shape[0]
    step_rows = subtiles * ROWS
    half_spec = pl.BlockSpec((step_rows, D_S5), lambda i: (i, 0))
    row_spec = pl.BlockSpec((step_rows, D_MODEL), lambda i: (i, 0))
    if tm_rows:
        bt = x.shape[0]
        x_spec = pl.BlockSpec(memory_space=pl.ANY)
        dma_scratch = [pltpu.VMEM((2, step_rows // bt, bt, D_MODEL), F32), pltpu.SemaphoreType.DMA((2,))]
    else:
        x_spec = row_spec
        dma_scratch = []
    mod_spec = lambda k: pl.BlockSpec((1, ROWS, D_MODEL), lambda i: (pattern, 0, k))
    return pl.pallas_call(
        functools.partial(_out_body, subtiles, tm_rows),
        grid=(t_rows // step_rows,),
        in_specs=[half_spec, half_spec, x_spec, mod_spec(2), mod_spec(4), mod_spec(3),
                  _full((2, D_S5, D_MODEL)), _full((1, D_MODEL)),
                  _full((N_EXPERTS, D_MODEL)), _full((N_EXPERTS, D_MODEL)), _full((N_EXPERTS, 1))],
        out_specs=[row_spec, row_spec,
                   pl.BlockSpec((2 * TOP_K, step_rows), lambda i: (0, i)),
                   pl.BlockSpec((subtiles, N_EXPERTS, LANES), lambda i: (i, 0, 0))],
        out_shape=[jax.ShapeDtypeStruct((t_rows, D_MODEL), F32), jax.ShapeDtypeStruct((t_rows, D_MODEL), BF16),
                   jax.ShapeDtypeStruct((2 * TOP_K, t_rows), F32),
                   jax.ShapeDtypeStruct((t_rows // ROWS, N_EXPERTS, LANES), F32)],
        scratch_shapes=dma_scratch,
        compiler_params=_params("arbitrary"),
        name="out_proj_router",
    )(ys5, ylru, x, modpat, modpat, modpat, w_out_bf, g_ffn, rw_hi, rw_lo, router_b)


def _start_segments(n_ref, hbm_ref, vmem_ref, hbm, vmem_buf, sem, step, to_hbm):
    for e in range(N_EXPERTS):
        n = n_ref[step * N_EXPERTS + e]
        h0 = hbm_ref[step * N_EXPERTS + e]
        v0 = vmem_ref[step * N_EXPERTS + e]

        def piece(off, size, h0=h0, v0=v0):
            h = hbm.at[pl.ds(pl.multiple_of(h0 + off, SEG_ALIGN), size)]
            v = vmem_buf.at[pl.ds(pl.multiple_of(v0 + off, SEG_ALIGN), size)]
            (pltpu.make_async_copy(v, h, sem) if to_hbm else pltpu.make_async_copy(h, v, sem)).start()

        _row_pieces(n, ROWS, piece)


def _wait_rows(total, largest, hbm, sem):
    _row_pieces(total, largest, lambda off, size: pltpu.make_async_copy(
        hbm.at[pl.ds(0, size)], hbm.at[pl.ds(0, size)], sem).wait())


def _dispatch_body(tiles_p, n_ref, glob_ref, local_ref, tot_ref, nb_ref, pg_ref, xp_ref, xsm_ref, xs_hbm, stage,
                   sem):
    j = pl.program_id(0)
    last = pl.num_programs(0) - 1
    slot = j % 2

    def unused_blocks(act):
        def blk(b, carry):
            act(pltpu.make_async_copy(stage.at[slot, pl.ds(0, MOE_TM)],
                                      xs_hbm.at[pl.ds(pl.multiple_of(b * MOE_TM, MOE_TM), MOE_TM)], sem.at[slot]))
            return carry
        lax.fori_loop(nb_ref[0], xs_hbm.shape[0] // MOE_TM, blk, 0)

    def wait_step(step, s):
        _wait_rows(tot_ref[step], WAIT_MAX_PIECE, xs_hbm, sem.at[s])

    @pl.when(j >= 2)
    def _():
        wait_step(jnp.maximum(j - 2, 0), slot)

    @pl.when(j < last)
    def _():
        x = jnp.where(j < tiles_p, xp_ref[...], xsm_ref[...])
        pos = pg_ref[0:TOP_K, :]
        r = lax.broadcasted_iota(jnp.int32, (SORT_CHUNK, ROWS), 0).astype(F32).astype(BF16)
        for c in range(SORT_ROWS // SORT_CHUNK):
            rel = (pos - float(c * SORT_CHUNK)).astype(BF16)
            pick = (r == rel[0:1]) | (r == rel[1:2]) | (r == rel[2:3]) | (r == rel[3:4])
            stage[slot, c * SORT_CHUNK:(c + 1) * SORT_CHUNK, :] = _pack_rows(jnp.dot(
                jnp.where(pick, jnp.ones((), BF16), jnp.zeros((), BF16)), x, preferred_element_type=F32))

    @pl.when(j == last)
    def _():
        stage[slot, 0:MOE_TM, :] = jnp.zeros((MOE_TM, PACKED), U32)

    _start_segments(n_ref, glob_ref, local_ref, xs_hbm, stage.at[slot], sem.at[slot], j, True)

    @pl.when(j == last)
    def _():
        unused_blocks(lambda c: c.start())

        @pl.when(j >= 1)
        def _():
            wait_step(jnp.maximum(j - 1, 0), 1 - slot)
        wait_step(j, slot)
        unused_blocks(lambda c: c.wait())


def _dispatch(seg_n, seg_glob, seg_local, seg_tot, nb_used, pg, xn_p, xn_s, n_rows):
    tiles_p = xn_p.shape[0] // ROWS
    tiles_s = xn_s.shape[0] // ROWS
    tiles = tiles_p + tiles_s
    grid_spec = pltpu.PrefetchScalarGridSpec(
        num_scalar_prefetch=5,
        grid=(tiles + 1,),
        in_specs=[pl.BlockSpec((2 * TOP_K, ROWS), lambda j, *_: (0, jnp.minimum(j, tiles - 1))),
                  pl.BlockSpec((ROWS, D_MODEL), lambda j, *_: (jnp.minimum(j, tiles_p - 1), 0)),
                  pl.BlockSpec((ROWS, D_MODEL), lambda j, *_: (jnp.clip(j - tiles_p, 0, tiles_s - 1), 0))],
        out_specs=pl.BlockSpec(memory_space=pl.ANY),
        scratch_shapes=[pltpu.VMEM((2, SORT_ROWS, PACKED), U32), pltpu.SemaphoreType.DMA((2,))],
    )
    return pl.pallas_call(
        functools.partial(_dispatch_body, tiles_p),
        grid_spec=grid_spec,
        out_shape=jax.ShapeDtypeStruct((n_rows, PACKED), U32),
        compiler_params=_params("arbitrary"),
        name="moe_dispatch",
    )(seg_n, seg_glob, seg_local, seg_tot, nb_used, pg, xn_p, xn_s)


def _moe_body(be_ref, nxt_ref, nv_ref, nb_ref, xs_ref, wgu_hbm, bgu_ref, wd_hbm, bd_ref, ys_ref,
              wgu_f32, wd_f32, wgu_bf, wd_bf, sem):
    i = pl.program_id(0)

    def weight_copies(e):
        return (pltpu.make_async_copy(wgu_hbm.at[e], wgu_f32, sem.at[0]),
                pltpu.make_async_copy(wd_hbm.at[e], wd_f32, sem.at[1]))

    @pl.when(i >= nb_ref[0])
    def _():
        ys_ref[...] = jnp.zeros_like(ys_ref)

    @pl.when(i < nb_ref[0])
    def _():
        e = be_ref[i]

        @pl.when(i == 0)
        def _():
            for c in weight_copies(e):
                c.start()

        @pl.when(jnp.logical_or(i == 0, e != be_ref[jnp.maximum(i - 1, 0)]))
        def _():
            for c in weight_copies(e):
                c.wait()
            wgu_bf[...] = wgu_f32[...].astype(BF16)
            wd_bf[...] = wd_f32[...].astype(BF16)

            @pl.when(nxt_ref[i] >= 0)
            def _():
                for c in weight_copies(nxt_ref[i]):
                    c.start()

        def ffn(packed):
            hg = jnp.dot(_unpack_rows(packed), wgu_bf[...], preferred_element_type=F32) + bgu_ref[0]
            gate = jnp.minimum(hg[:, :D_FF], SWIGLU_LIMIT)
            up = jnp.clip(hg[:, D_FF:], -SWIGLU_LIMIT, SWIGLU_LIMIT)
            act = (up + 1.0) * (gate * jax.nn.sigmoid(SWIGLU_ALPHA * gate))
            y = jnp.dot(act.astype(BF16), wd_bf[...], preferred_element_type=F32) + bd_ref[0]
            return _pack_rows(y.astype(BF16).astype(F32))

        for parts in range(1, MOE_TM // MOE_PART + 1):
            lo = (parts - 1) * MOE_PART
            last_path = parts == MOE_TM // MOE_PART

            @pl.when(jnp.logical_and(nv_ref[i] > lo, jnp.logical_or(last_path, nv_ref[i] <= lo + MOE_PART)))
            def _(parts=parts):
                rows = parts * MOE_PART
                ys_ref[0:rows, :] = ffn(xs_ref[0:rows, :])
                if rows < MOE_TM:
                    ys_ref[rows:, :] = jnp.zeros((MOE_TM - rows, PACKED), U32)


def _moe(xs, block_expert, block_next, block_rows, nb_used, w_gu, b_gu, w_down, b_down):
    n_blocks = xs.shape[0] // MOE_TM
    in_rows = pl.BlockSpec((MOE_TM, PACKED), lambda i, be, nx, nv, nb: (jnp.minimum(i, nb[0] - 1), 0))
    grid_spec = pltpu.PrefetchScalarGridSpec(
        num_scalar_prefetch=4,
        grid=(n_blocks,),
        in_specs=[in_rows,
                  pl.BlockSpec(memory_space=pl.ANY),
                  pl.BlockSpec((1, 1, 2 * D_FF), lambda i, be, nx, nv, nb: (be[i], 0, 0)),
                  pl.BlockSpec(memory_space=pl.ANY),
                  pl.BlockSpec((1, 1, D_MODEL), lambda i, be, nx, nv, nb: (be[i], 0, 0))],
        out_specs=pl.BlockSpec((MOE_TM, PACKED), lambda i, be, nx, nv, nb: (i, 0)),
        scratch_shapes=[pltpu.VMEM((D_MODEL, 2 * D_FF), F32), pltpu.VMEM((D_FF, D_MODEL), F32),
                        pltpu.VMEM((D_MODEL, 2 * D_FF), BF16), pltpu.VMEM((D_FF, D_MODEL), BF16),
                        pltpu.SemaphoreType.DMA((2,))],
    )
    return pl.pallas_call(
        _moe_body,
        grid_spec=grid_spec,
        out_shape=jax.ShapeDtypeStruct(xs.shape, U32),
        compiler_params=_params("arbitrary"),
        name="moe_experts",
    )(block_expert, block_next, block_rows, nb_used, xs, w_gu, b_gu, w_down, b_down)


def _fin_body(tiles_p, n_ref, glob_ref, local_ref, tot_ref, hp_ref, hs_ref, pg_ref, g2_ref, gf_ref, ys_hbm,
              op_hbm, os_ref, ybuf, sem, yst, osem):
    j = pl.program_id(0)
    tiles = pl.num_programs(0)
    slot = j % 2
    is_prompt = j < tiles_p

    def fetch(step, s):
        _start_segments(n_ref, glob_ref, local_ref, ys_hbm, ybuf.at[s], sem.at[s], step, False)

    @pl.when(j == 0)
    def _():
        ybuf[...] = jnp.zeros_like(ybuf)
        fetch(0, 0)

    _wait_rows(tot_ref[j], WAIT_MAX_PIECE, ys_hbm, sem.at[slot])

    @pl.when(j + 1 < tiles)
    def _():
        fetch(jnp.minimum(j + 1, tiles - 1), 1 - slot)

    pg = pg_ref[...]
    ff = jnp.zeros((ROWS, D_MODEL), F32)
    r = lax.broadcasted_iota(jnp.int32, (ROWS, SORT_CHUNK), 1).astype(F32).astype(BF16)
    gates = pg[:, TOP_K:].astype(BF16)
    for c in range(SORT_ROWS // SORT_CHUNK):
        rel = (pg[:, :TOP_K] - float(c * SORT_CHUNK)).astype(BF16)
        w = jnp.zeros((ROWS, SORT_CHUNK), BF16)
        for k in range(TOP_K):
            w = jnp.where(r == rel[:, k:k + 1], gates[:, k:k + 1], w)
        ff = ff + jnp.dot(w, _unpack_rows(ybuf[slot, c * SORT_CHUNK:(c + 1) * SORT_CHUNK, :]),
                          preferred_element_type=F32)
    h = jnp.where(is_prompt, hp_ref[...], hs_ref[...])
    y = _rms(h + g2_ref[0] * ff, gf_ref[...])

    tc, bp = yst.shape[1], yst.shape[2]

    def out_copies(tile, s, act):
        for b in range(bp):
            act(pltpu.make_async_copy(yst.at[s, :, b, :], op_hbm.at[b, pl.ds(tile * tc, tc), :], osem.at[s]))

    @pl.when(is_prompt)
    def _():
        @pl.when(j >= 2)
        def _():
            out_copies(0, slot, lambda c: c.wait())
        yst[slot] = y.reshape(tc, bp, D_MODEL)
        out_copies(j, slot, lambda c: c.start())

    @pl.when(jnp.logical_not(is_prompt))
    def _():
        os_ref[...] = y

    @pl.when(j == tiles - 1)
    def _():
        for tile in range(max(tiles_p - 2, 0), tiles_p):
            out_copies(0, tile % 2, lambda c: c.wait())


def _final(seg_n, seg_glob, seg_local, seg_tot, h_p, h_s, pg_t, modpat, g_final, ys, bp, tiles_p, tiles_s):
    hp_spec, hs_spec = _tile_specs(tiles_p, tiles_s)
    grid_spec = pltpu.PrefetchScalarGridSpec(
        num_scalar_prefetch=4,
        grid=(tiles_p + tiles_s,),
        in_specs=[hp_spec, hs_spec,
                  pl.BlockSpec((ROWS, 2 * TOP_K), lambda j, *_: (j, 0)),
                  _mod_spec(5, tiles_p), _full((1, D_MODEL)),
                  pl.BlockSpec(memory_space=pl.ANY)],
        out_specs=[pl.BlockSpec(memory_space=pl.ANY), hs_spec],
        scratch_shapes=[pltpu.VMEM((2, SORT_ROWS, PACKED), U32), pltpu.SemaphoreType.DMA((2,)),
                        pltpu.VMEM((2, ROWS // bp, bp, D_MODEL), F32), pltpu.SemaphoreType.DMA((2,))],
    )
    return pl.pallas_call(
        functools.partial(_fin_body, tiles_p),
        grid_spec=grid_spec,
        out_shape=[jax.ShapeDtypeStruct((bp, h_p.shape[0] // bp, D_MODEL), F32), jax.ShapeDtypeStruct(h_s.shape, F32)],
        compiler_params=_params("arbitrary"),
        name="combine_final",
    )(seg_n, seg_glob, seg_local, seg_tot, h_p, h_s, pg_t, modpat, g_final, ys)


def _block_diag(w):
    h, i, j = w.shape
    return jnp.einsum('hij,hk->hikj', w, jnp.eye(h, dtype=w.dtype)).reshape(h * i, h * j)


def _s5_cols(re, im):
    b = re.shape[0]
    return jnp.concatenate([re.reshape(b, -1), im.reshape(b, -1)], axis=1)


def _s5_uncols(cols):
    b = cols.shape[0]
    return (cols[:, :S5_COLS // 2].reshape(b, S5_GROUPS, S5_STATE),
            cols[:, S5_COLS // 2:].reshape(b, S5_GROUPS, S5_STATE))


def _moe_rows_bound(tiles):
    worst = tiles * (TOP_K * ROWS + N_EXPERTS * (SEG_ALIGN - 1)) + N_EXPERTS * (MOE_TM - SEG_ALIGN)
    return (worst + MOE_TM - 1) // MOE_TM * MOE_TM


def _plan(cnt):
    cnt = cnt.astype(jnp.int32)
    tiles = cnt.shape[0]
    cp = (cnt + SEG_ALIGN - 1) // SEG_ALIGN * SEG_ALIGN
    local = jnp.cumsum(cp, axis=1) - cp
    group = jnp.sum(cp, axis=0)
    group_pad = (group + MOE_TM - 1) // MOE_TM * MOE_TM
    pend = jnp.cumsum(group_pad)
    pstart = pend - group_pad
    glob = pstart[None, :] + jnp.cumsum(cp, axis=0) - cp
    gap = group_pad - group
    seg_n = jnp.concatenate([cp, gap[None]], axis=0).reshape(-1)
    seg_local = jnp.concatenate([local, jnp.zeros((1, N_EXPERTS), jnp.int32)], axis=0).reshape(-1)
    seg_glob = jnp.concatenate([glob, (pstart + group)[None]], axis=0).reshape(-1)
    seg_tot = jnp.concatenate([jnp.sum(cp, axis=1), jnp.sum(gap)[None]]).astype(jnp.int32)
    n_blocks = _moe_rows_bound(tiles) // MOE_TM
    block_row0 = jnp.arange(n_blocks, dtype=jnp.int32) * MOE_TM
    block_expert = jnp.minimum(jnp.sum(block_row0[:, None] >= pend[None, :], axis=1), N_EXPERTS - 1).astype(jnp.int32)
    nb_used = (pend[-1] // MOE_TM).astype(jnp.int32).reshape(1)
    experts = jnp.arange(N_EXPERTS, dtype=jnp.int32)
    later_owner = jnp.where((experts[None, :] > experts[:, None]) & (group_pad[None, :] > 0), experts[None, :],
                            N_EXPERTS)
    next_owner = jnp.min(later_owner, axis=1)
    next_owner = jnp.where(next_owner == N_EXPERTS, -1, next_owner).astype(jnp.int32)
    owner = block_expert[:, None] == experts[None, :]
    block_next = jnp.sum(jnp.where(owner, next_owner[None, :], 0), axis=1).astype(jnp.int32)
    group_end = jnp.sum(jnp.where(owner, (pstart + group)[None, :], 0), axis=1)
    block_rows = jnp.clip(group_end - block_row0, 0, MOE_TM).astype(jnp.int32)
    return seg_n, seg_glob, seg_local, seg_tot, block_expert, block_next, block_rows, nb_used


def kernel(x_prompt, x_sample, state_s5_re, state_s5_im, state_lru_h, state_conv, c_prompt, c_sample, w_ada, b_ada, g_mix, w_in, s5_a_re, s5_a_im, s5_log_dt, s5_b_re, s5_b_im, s5_c_re, s5_c_im, s5_d, s5_w_glu, s5_b_glu, lru_conv_w, lru_conv_b, lru_w_a, lru_b_a, lru_w_x, lru_b_x, lru_lambda, g_out_s5, g_out_lru, w_out, g_ffn, router_w, router_b, moe_w_gu, moe_b_gu, moe_w_down, moe_b_down, g_final):
    assert w_ada.shape[0] == 1, "one layer"
    bp, lp, _ = x_prompt.shape
    bs, ls, _ = x_sample.shape
    assert ROWS % bp == 0 and ROWS % bs == 0 and (bp * lp) % ROWS == 0 and (bs * ls) % ROWS == 0
    tiles_p = bp * lp // ROWS
    tiles_s = bs * ls // ROWS
    row = lambda v: v.reshape(1, -1)

    ab_re, ab_im, bb_re, bb_im = _s5_prep(s5_a_re[0], s5_a_im[0], s5_log_dt[0].reshape(S5_GROUPS, 1),
                                          jnp.swapaxes(s5_b_re[0], 1, 2), jnp.swapaxes(s5_b_im[0], 1, 2))
    gpb = S5_GROUPS // S5_BLOCKS
    eye = jnp.eye(gpb, dtype=F32)

    def in_blocks(b):
        b = b.reshape(S5_BLOCKS, gpb, S5_GROUP_CH, S5_STATE)
        return jnp.einsum('bgjn,gh->bgjhn', b, eye).reshape(S5_BLOCKS, gpb * S5_GROUP_CH, gpb * S5_STATE)

    def out_blocks(c):
        c = c.reshape(S5_BLOCKS, gpb, S5_GROUP_CH, S5_STATE)
        return jnp.einsum('bgjn,gh->bgnhj', c, eye).reshape(S5_BLOCKS, gpb * S5_STATE, gpb * S5_GROUP_CH)

    ar8 = jnp.broadcast_to(ab_re.reshape(1, -1), (SUBLANES, S5_GROUPS * S5_STATE))
    ai8 = jnp.broadcast_to(ab_im.reshape(1, -1), (SUBLANES, S5_GROUPS * S5_STATE))
    bb = jnp.concatenate([in_blocks(bb_re), in_blocks(bb_im)], axis=-1).astype(BF16)
    cc = jnp.concatenate([out_blocks(s5_c_re[0]), -out_blocks(s5_c_im[0])], axis=1).astype(BF16)
    wa_bd = _block_diag(lru_w_a[0]).astype(BF16)
    wx_bd = _block_diag(lru_w_x[0]).astype(BF16)
    rw_t = router_w[0].T
    rw_hi = rw_t.astype(BF16)
    rw_lo = (rw_t - rw_hi.astype(F32)).astype(BF16)

    modpat = _adaln(jnp.concatenate([c_prompt, c_sample], axis=0), w_ada[0], row(b_ada[0]), bp)
    pair = 2 if tiles_p % 2 == 0 else 1

    def conv_tm(cv):
        return jnp.swapaxes(cv, 0, 1).reshape(-1, D_LRU)

    def conv_bm(cv, b):
        return jnp.swapaxes(cv.reshape(CONV_WIDTH - 1, b, D_LRU), 0, 1)

    x_s = jnp.swapaxes(x_sample, 0, 1).reshape(bs * ls, D_MODEL)
    in_args = (row(g_mix[0]), w_in[0].astype(BF16))
    lru_args = (lru_conv_w[0], row(lru_conv_b[0]), wa_bd, row(lru_b_a[0]), wx_bd, row(lru_b_x[0]),
                row(lru_lambda[0]))
    u_p, la_p, lb_p, lg_p, cvp = _in_proj(x_prompt, modpat, 0, True, *in_args,
                                          jnp.zeros(((CONV_WIDTH - 1) * bp, D_LRU), F32), *lru_args, bp, pair)
    u_s, la_s, lb_s, lg_s, cvs = _in_proj(x_s, modpat, 1, None, *in_args, conv_tm(state_conv[0]), *lru_args, bs, 1)

    s5_args = (ar8, ai8, bb, cc, row(s5_d[0]), s5_w_glu[0].astype(BF16), row(s5_b_glu[0]), row(g_out_s5[0]))
    ys5_p, s5p, ylru_p, hp = _mixers(u_p, jnp.zeros((bp, S5_COLS), F32), *s5_args, la_p, lb_p, lg_p,
                                     jnp.zeros((bp, D_LRU), F32), row(g_out_lru[0]), bp, pair)
    ys5_s, s5s, ylru_s, hs = _mixers(u_s, _s5_cols(state_s5_re[0], state_s5_im[0]), *s5_args, la_s, lb_s, lg_s,
                                     state_lru_h[0], row(g_out_lru[0]), bs, 1)

    out_args = (w_out[0].astype(BF16).reshape(2, D_S5, D_MODEL), row(g_ffn[0]), rw_hi, rw_lo,
                router_b[0].reshape(N_EXPERTS, 1))
    h_p, xn_p, pg_p, cnt_p = _out_proj(ys5_p, ylru_p, x_prompt, modpat, 0, True, *out_args, pair)
    h_s, xn_s, pg_s, cnt_s = _out_proj(ys5_s, ylru_s, x_s, modpat, 1, False, *out_args, 1)
    pg = jnp.concatenate([pg_p, pg_s], axis=1)
    cnt = jnp.concatenate([cnt_p, cnt_s], axis=0)

    seg_n, seg_glob, seg_local, seg_tot, block_expert, block_next, block_rows, nb_used = _plan(cnt[:, :, 0])
    xs = _dispatch(seg_n, seg_glob, seg_local, seg_tot, nb_used, pg, xn_p, xn_s,
                   _moe_rows_bound(tiles_p + tiles_s))
    ys = _moe(xs, block_expert, block_next, block_rows, nb_used, moe_w_gu[0], moe_b_gu[0].reshape(N_EXPERTS, 1, 2 * D_FF),
              moe_w_down[0], moe_b_down[0].reshape(N_EXPERTS, 1, D_MODEL))
    y_prompt, y_s = _final(seg_n, seg_glob, seg_local, seg_tot, h_p, h_s, pg.T, modpat, row(g_final), ys,
                           bp, tiles_p, tiles_s)
    y_sample = jnp.swapaxes(y_s.reshape(ls, bs, D_MODEL), 0, 1)
    s5p_re, s5p_im = _s5_uncols(s5p)
    s5s_re, s5s_im = _s5_uncols(s5s)
    return (y_prompt, y_sample,
            s5p_re[None], s5p_im[None], hp[None], conv_bm(cvp, bp)[None],
            s5s_re[None], s5s_im[None], hs[None], conv_bm(cvs, bs)[None])
```

```python
import functools

import jax
import jax.numpy as jnp
from jax import lax
from jax.experimental import pallas as pl
from jax.experimental.pallas import tpu as pltpu

D_MODEL = 1024
D_S5 = 512
D_LRU = 512
S5_GROUPS = 32
S5_GROUP_CH = 16
S5_STATE = 64
S5_COLS = 2 * S5_GROUPS * S5_STATE
S5_BLOCKS = 4
LRU_C = 8.0
CONV_WIDTH = 4
N_EXPERTS = 32
TOP_K = 4
D_FF = 1024
SWIGLU_LIMIT = 7.0
SWIGLU_ALPHA = 1.702
N_MOD = 6
EPS = 1e-6

SUBLANES = 8
LANES = 128
VMEM_LIMIT = 48 * 1024 * 1024

ROWS = 512
MOE_TM = 512
MOE_PART = 128
PACKED = D_MODEL // 2
SEG_ALIGN = SUBLANES
SORT_CHUNK = 256
SORT_ROWS = -(-(TOP_K * ROWS + N_EXPERTS * (SEG_ALIGN - 1)) // SORT_CHUNK) * SORT_CHUNK
WAIT_MAX_PIECE = N_EXPERTS * MOE_TM // 2

BF16 = jnp.bfloat16
F32 = jnp.float32
U32 = jnp.uint32


def _params(*sem):
    return pltpu.CompilerParams(dimension_semantics=sem, vmem_limit_bytes=VMEM_LIMIT)


def _full(shape):
    return pl.BlockSpec(shape, lambda *_: (0,) * len(shape))


def _rms(x, g):
    return x * lax.rsqrt(jnp.mean(x * x, axis=-1, keepdims=True) + EPS) * g


def _gelu(x):
    return 0.5 * x * (1.0 + lax.erf(x * (2.0 ** -0.5)))


def _expm1(x):
    u = jnp.exp(x)
    d = u - 1.0
    return jnp.where(d == 0.0, x, jnp.where(d == -1.0, -1.0, d * x / jnp.log(u)))


def _pack_rows(x):
    lo = lax.shift_right_logical(lax.bitcast_convert_type(x[:, :PACKED], U32), jnp.uint32(16))
    hi = lax.bitcast_convert_type(x[:, PACKED:], U32) & jnp.uint32(0xFFFF0000)
    return lo | hi


def _unpack_rows(p):
    lo = lax.bitcast_convert_type(lax.shift_left(p, jnp.uint32(16)), F32)
    hi = lax.bitcast_convert_type(p & jnp.uint32(0xFFFF0000), F32)
    return jnp.concatenate([lo, hi], axis=-1).astype(BF16)


def _row_pieces(n, largest, fn):
    off = 0
    bit = largest
    while bit >= SEG_ALIGN:
        @pl.when((n & bit) != 0)
        def _(off=off, bit=bit):
            fn(off, bit)
        off = off + (n & bit)
        bit //= 2


def _mod_body(bp, c_ref, w_ref, b_ref, o_ref):
    c = c_ref[...]
    s = (c * jax.nn.sigmoid(c)).astype(BF16)
    mod = jnp.dot(s, w_ref[...].astype(BF16), preferred_element_type=F32) + b_ref[...]
    bs = mod.shape[0] - bp
    o_ref[0] = jnp.broadcast_to(mod[:bp][None], (ROWS // bp, bp, D_MODEL)).reshape(ROWS, D_MODEL)
    o_ref[1] = jnp.broadcast_to(mod[bp:][None], (ROWS // bs, bs, D_MODEL)).reshape(ROWS, D_MODEL)


def _adaln(c, w_ada, b_ada, bp):
    m = c.shape[0]
    return pl.pallas_call(
        functools.partial(_mod_body, bp),
        grid=(N_MOD,),
        in_specs=[pl.BlockSpec((m, D_MODEL), lambda j: (0, 0)),
                  pl.BlockSpec((D_MODEL, D_MODEL), lambda j: (0, j)),
                  pl.BlockSpec((1, D_MODEL), lambda j: (0, j))],
        out_specs=pl.BlockSpec((2, ROWS, D_MODEL), lambda j: (0, 0, j)),
        out_shape=jax.ShapeDtypeStruct((2, ROWS, N_MOD * D_MODEL), F32),
        compiler_params=_params("arbitrary"),
        name="adaln",
    )(c, w_ada, b_ada)


def _mod_spec(k, tiles_p):
    return pl.BlockSpec((1, ROWS, D_MODEL), lambda i, *_: (jnp.where(i < tiles_p, 0, 1), 0, k))


def _tile_specs(tiles_p, tiles_s):
    p_spec = pl.BlockSpec((ROWS, D_MODEL), lambda i, *_: (jnp.minimum(i, tiles_p - 1), 0))
    s_spec = pl.BlockSpec((ROWS, D_MODEL), lambda i, *_: (jnp.clip(i - tiles_p, 0, tiles_s - 1), 0))
    return p_spec, s_spec


def _time_major_ring(x_hbm, xbuf, sem):
    step = pl.program_id(0)
    tcs, bt = xbuf.shape[1], xbuf.shape[2]

    def copies(s, slot, act):
        for b in range(bt):
            act(pltpu.make_async_copy(x_hbm.at[b, pl.ds(s * tcs, tcs), :], xbuf.at[slot, :, b, :], sem.at[slot]))

    @pl.when(step == 0)
    def _():
        copies(0, 0, lambda c: c.start())

    copies(step, step % 2, lambda c: c.wait())

    @pl.when(step + 1 < pl.num_programs(0))
    def _():
        copies(step + 1, 1 - step % 2, lambda c: c.start())


def _in_body(subtiles, bt, to_tm, *refs):
    x_ref, sc_ref, sh_ref, *refs = refs
    (g_ref, w_ref, cv0_ref, cw_ref, cb_ref, wa_ref, ba_ref, wx_ref, bx_ref, lam_ref,
     u_ref, a_ref, b_ref, gg_ref, cvout_ref, xp_ref, *dma) = refs
    halo = (CONV_WIDTH - 1) * bt
    step = pl.program_id(0)

    if to_tm:
        xbuf, sem = dma
        _time_major_ring(x_ref, xbuf, sem)

    @pl.when(step == 0)
    def _():
        xp_ref[...] = cv0_ref[...]

    lam = lam_ref[...]
    softplus_neg_lam = jnp.maximum(-lam, 0.0) + jnp.log1p(jnp.exp(-jnp.abs(lam)))
    earlier = xp_ref[...]
    for c in range(subtiles):
        rows = slice(c * ROWS, (c + 1) * ROWS)
        if to_tm:
            tc = ROWS // bt
            x = xbuf[step % 2, c * tc:(c + 1) * tc].reshape(ROWS, D_MODEL)
        else:
            x = x_ref[rows, :]
        xn = (_rms(x, g_ref[...]) * (1.0 + sc_ref[0]) + sh_ref[0]).astype(BF16)
        p = jnp.dot(xn, w_ref[...], preferred_element_type=F32)
        u_ref[rows, :] = p[:, :D_S5]
        gg_ref[rows, :] = _gelu(p[:, D_S5 + D_LRU:])

        xr = p[:, D_S5:D_S5 + D_LRU]
        xp = jnp.concatenate([earlier, xr], axis=0)
        earlier = xr[ROWS - halo:, :]
        xc = cb_ref[...] + sum(xp[k * bt:k * bt + ROWS, :] * cw_ref[k:k + 1, :] for k in range(CONV_WIDTH))
        xcb = xc.astype(BF16)
        r = jax.nn.sigmoid(jnp.dot(xcb, wa_ref[...], preferred_element_type=F32) + ba_ref[...])
        i = jax.nn.sigmoid(jnp.dot(xcb, wx_ref[...], preferred_element_type=F32) + bx_ref[...])
        log_a = -LRU_C * r * softplus_neg_lam
        a_ref[rows, :] = jnp.exp(log_a)
        b_ref[rows, :] = jnp.sqrt(-_expm1(2.0 * log_a)) * (i * xc)

    xp_ref[...] = earlier
    cvout_ref[...] = earlier


def _in_proj(x, modpat, pattern, to_tm, g_mix, w_in_bf, conv0_tm, conv_w, conv_b, wa_bd, b_a, wx_bd, b_x, lam, bt,
             subtiles):
    tm = to_tm is not None
    t_rows = x.shape[0] * x.shape[1] if tm else x.shape[0]
    step_rows = subtiles * ROWS
    halo = (CONV_WIDTH - 1) * bt
    if tm:
        x_spec = pl.BlockSpec(memory_space=pl.ANY)
        dma_scratch = [pltpu.VMEM((2, step_rows // bt, bt, D_MODEL), F32), pltpu.SemaphoreType.DMA((2,))]
    else:
        x_spec = pl.BlockSpec((step_rows, D_MODEL), lambda i: (i, 0))
        dma_scratch = []
    mod_spec = lambda k: pl.BlockSpec((1, ROWS, D_MODEL), lambda i: (pattern, 0, k))
    row_spec = pl.BlockSpec((step_rows, D_S5), lambda i: (i, 0))
    vec = _full((1, D_LRU))
    return pl.pallas_call(
        functools.partial(_in_body, subtiles, bt, tm),
        grid=(t_rows // step_rows,),
        in_specs=[x_spec, mod_spec(1), mod_spec(0), _full((1, D_MODEL)),
                  _full((D_MODEL, D_S5 + 2 * D_LRU)), _full((halo, D_LRU)), _full((CONV_WIDTH, D_LRU)), vec,
                  _full((D_LRU, D_LRU)), vec, _full((D_LRU, D_LRU)), vec, vec],
        out_specs=[row_spec, row_spec, row_spec, row_spec, _full((halo, D_LRU))],
        out_shape=[jax.ShapeDtypeStruct((t_rows, D_S5), F32)] * 4 + [jax.ShapeDtypeStruct((halo, D_LRU), F32)],
        scratch_shapes=[pltpu.VMEM((halo, D_LRU), F32), *dma_scratch],
        compiler_params=_params("arbitrary"),
        name="in_proj",
    )(x, modpat, modpat, g_mix, w_in_bf, conv0_tm, conv_w, conv_b, wa_bd, b_a, wx_bd, b_x, lam)


def _s5_prep_body(are_ref, aim_ref, ldt_ref, bre_ref, bim_ref, abre_ref, abim_ref, bbre_ref, bbim_ref):
    a_re = are_ref[...]
    a_im = aim_ref[...]
    dt = jnp.exp(ldt_ref[...])
    mag = jnp.exp(dt * a_re)
    ang = dt * a_im
    ab_re = mag * jnp.cos(ang)
    ab_im = mag * jnp.sin(ang)
    den = a_re * a_re + a_im * a_im
    q_re = ((ab_re - 1.0) * a_re + ab_im * a_im) / den
    q_im = (ab_im * a_re - (ab_re - 1.0) * a_im) / den
    abre_ref[...] = ab_re
    abim_ref[...] = ab_im
    b_re = bre_ref[...]
    b_im = bim_ref[...]
    bbre_ref[...] = q_re[:, None, :] * b_re - q_im[:, None, :] * b_im
    bbim_ref[...] = q_re[:, None, :] * b_im + q_im[:, None, :] * b_re


def _s5_prep(a_re, a_im, log_dt, b_re_t, b_im_t):
    gn = jax.ShapeDtypeStruct((S5_GROUPS, S5_STATE), F32)
    gjn = jax.ShapeDtypeStruct((S5_GROUPS, S5_GROUP_CH, S5_STATE), F32)
    return pl.pallas_call(_s5_prep_body, out_shape=[gn, gn, gjn, gjn], name="s5_prep")(
        a_re, a_im, log_dt, b_re_t, b_im_t)


def _s5_body(tc, bt, chunks, u_ref, s0_ref, ar_ref, ai_ref, bb_ref, cc_ref, d_ref, wg_ref, bg_ref, go_ref,
             la_ref, lb_ref, lg_ref, lh0_ref, lgo_ref, y_ref, sout_ref, yl_ref, lhout_ref,
             bu_ref, st_ref, lh_ref, lst_ref):
    half = S5_COLS // S5_BLOCKS // 2
    re = [slice(half * j, half * (j + 1)) for j in range(S5_BLOCKS)]
    im = [slice(S5_COLS // 2 + half * j, S5_COLS // 2 + half * (j + 1)) for j in range(S5_BLOCKS)]

    @pl.when(pl.program_id(0) == 0)
    def _():
        st_ref[...] = s0_ref[...]
        lst_ref[...] = lh0_ref[...]

    for c in range(chunks):
        ub = u_ref[c * ROWS:(c + 1) * ROWS, :].astype(BF16)
        for j in range(S5_BLOCKS):
            bu = jnp.dot(ub[:, LANES * j:LANES * (j + 1)], bb_ref[j], preferred_element_type=F32)
            bu_ref[c, :, re[j]] = bu[:, :half]
            bu_ref[c, :, im[j]] = bu[:, half:]

    for sb in range(bt // SUBLANES):
        base = sb * SUBLANES
        hr = [st_ref[base:base + SUBLANES, re[j]] for j in range(S5_BLOCKS)]
        hi = [st_ref[base:base + SUBLANES, im[j]] for j in range(S5_BLOCKS)]
        hl = lst_ref[base:base + SUBLANES, :]
        for c in range(chunks):
            for t in range(tc):
                r0 = t * bt + base
                row = c * ROWS + r0
                hl = la_ref[row:row + SUBLANES, :] * hl + lb_ref[row:row + SUBLANES, :]
                lh_ref[row:row + SUBLANES, :] = hl
                for j in range(S5_BLOCKS):
                    ar = ar_ref[:, re[j]]
                    ai = ai_ref[:, re[j]]
                    h_re = ar * hr[j] - ai * hi[j] + bu_ref[c, r0:r0 + SUBLANES, re[j]]
                    h_im = ar * hi[j] + ai * hr[j] + bu_ref[c, r0:r0 + SUBLANES, im[j]]
                    bu_ref[c, r0:r0 + SUBLANES, re[j]] = h_re
                    bu_ref[c, r0:r0 + SUBLANES, im[j]] = h_im
                    hr[j], hi[j] = h_re, h_im
        for j in range(S5_BLOCKS):
            st_ref[base:base + SUBLANES, re[j]] = hr[j]
            st_ref[base:base + SUBLANES, im[j]] = hi[j]
        lst_ref[base:base + SUBLANES, :] = hl
    sout_ref[...] = st_ref[...]
    lhout_ref[...] = lst_ref[...]
    yl_ref[...] = _rms(lh_ref[...] * lg_ref[...], lgo_ref[...]).astype(BF16)

    for c in range(chunks):
        y = jnp.concatenate(
            [jnp.dot(jnp.concatenate([bu_ref[c, :, re[j]], bu_ref[c, :, im[j]]], axis=-1).astype(BF16), cc_ref[j],
                     preferred_element_type=F32) for j in range(S5_BLOCKS)], axis=-1)
        y = y + d_ref[...] * u_ref[c * ROWS:(c + 1) * ROWS, :]
        g = _gelu(y)
        z = jnp.dot(g.astype(BF16), wg_ref[...], preferred_element_type=F32) + bg_ref[...]
        out = g * jax.nn.sigmoid(z)
        y_ref[c * ROWS:(c + 1) * ROWS, :] = _rms(out, go_ref[...]).astype(BF16)


def _mixers(u, s0, ar8, ai8, bb, cc, d, w_glu_bf, b_glu, g_out_s5, lru_a, lru_b, lru_gg, lru_h0, g_out_lru, bt,
            chunks):
    tc = ROWS // bt
    t_rows = u.shape[0]
    step_rows = chunks * ROWS
    assert t_rows % step_rows == 0
    rows_f32 = pl.BlockSpec((step_rows, D_S5), lambda i: (i, 0))
    vec = _full((1, D_S5))
    return pl.pallas_call(
        functools.partial(_s5_body, tc, bt, chunks),
        grid=(t_rows // step_rows,),
        in_specs=[rows_f32, _full((bt, S5_COLS)), _full(ar8.shape), _full(ai8.shape), _full(bb.shape),
                  _full(cc.shape), vec, _full((D_S5, D_S5)), vec, vec,
                  rows_f32, rows_f32, rows_f32, _full((bt, D_LRU)), vec],
        out_specs=[rows_f32, _full((bt, S5_COLS)), rows_f32, _full((bt, D_LRU))],
        out_shape=[jax.ShapeDtypeStruct((t_rows, D_S5), BF16), jax.ShapeDtypeStruct((bt, S5_COLS), F32),
                   jax.ShapeDtypeStruct((t_rows, D_LRU), BF16), jax.ShapeDtypeStruct((bt, D_LRU), F32)],
        scratch_shapes=[pltpu.VMEM((chunks, ROWS, S5_COLS), F32), pltpu.VMEM((bt, S5_COLS), F32),
                        pltpu.VMEM((step_rows, D_LRU), F32), pltpu.VMEM((bt, D_LRU), F32)],
        compiler_params=_params("arbitrary"),
        name="s5_lru_mixers",
    )(u, s0, ar8, ai8, bb, cc, d, w_glu_bf, b_glu, g_out_s5, lru_a, lru_b, lru_gg, lru_h0, g_out_lru)


def _route_tile(xn, x_hi, rw_hi, rw_lo, rb):
    x_lo = (xn - x_hi.astype(F32)).astype(BF16)
    nt = (((1,), (1,)), ((), ()))
    logits = (lax.dot_general(rw_hi, x_hi, nt, preferred_element_type=F32)
              + lax.dot_general(rw_hi, x_lo, nt, preferred_element_type=F32)
              + lax.dot_general(rw_lo, x_hi, nt, preferred_element_type=F32)) + rb

    e_iota = lax.broadcasted_iota(jnp.int32, (N_EXPERTS, ROWS), 0).astype(F32)
    work = logits
    sels, vals = [], []
    for _ in range(TOP_K):
        m = jnp.max(work, axis=0, keepdims=True)
        idx = jnp.min(jnp.where(work == m, e_iota, float(N_EXPERTS)), axis=0, keepdims=True)
        sel = e_iota == idx
        work = jnp.where(sel, -jnp.inf, work)
        sels.append(sel)
        vals.append(m)
    exps = [jnp.exp(v - vals[0]) for v in vals]
    denom = exps[0] + exps[1] + exps[2] + exps[3]
    gates = [e / denom for e in exps]

    onehot = sels[0] | sels[1] | sels[2] | sels[3]
    rr = lax.broadcasted_iota(jnp.int32, (ROWS, ROWS), 0)
    cc = lax.broadcasted_iota(jnp.int32, (ROWS, ROWS), 1)
    before = (rr < cc).astype(BF16)
    prefix = jnp.dot(onehot.astype(BF16), before, preferred_element_type=F32)
    cnt = jnp.sum(onehot.astype(F32), axis=1, keepdims=True)
    cnt_pad = jnp.floor((cnt + (SEG_ALIGN - 1)) * (1.0 / SEG_ALIGN)) * SEG_ALIGN
    er = lax.broadcasted_iota(jnp.int32, (N_EXPERTS, N_EXPERTS), 0)
    ec = lax.broadcasted_iota(jnp.int32, (N_EXPERTS, N_EXPERTS), 1)
    seg_start = jnp.dot((ec < er).astype(BF16), jnp.broadcast_to(cnt_pad, (N_EXPERTS, LANES)).astype(BF16),
                        preferred_element_type=F32)[:, 0:1]
    where_to = prefix + seg_start
    poss = [jnp.sum(jnp.where(s, where_to, 0.0), axis=0, keepdims=True) for s in sels]

    s_iota = lax.broadcasted_iota(jnp.int32, (2 * TOP_K, ROWS), 0)
    pg = jnp.zeros((2 * TOP_K, ROWS), F32)
    for k in range(TOP_K):
        pg = jnp.where(s_iota == k, poss[k], pg)
        pg = jnp.where(s_iota == TOP_K + k, gates[k], pg)
    return pg, cnt


def _out_body(subtiles, tm_rows, *refs):
    ys_ref, yl_ref, x_ref, g1_ref, sc_ref, sh_ref, wo_ref, gf_ref, rwh_ref, rwl_ref, rb_ref, *refs = refs
    h_ref, xn_ref, pg_ref, pgt_ref, cnt_ref, *dma = refs
    if tm_rows:
        xbuf, sem = dma
        _time_major_ring(x_ref, xbuf, sem)
    for c in range(subtiles):
        rows = slice(c * ROWS, (c + 1) * ROWS)
        if tm_rows:
            tc = ROWS // xbuf.shape[2]
            x = xbuf[pl.program_id(0) % 2, c * tc:(c + 1) * tc].reshape(ROWS, D_MODEL)
        else:
            x = x_ref[rows, :]
        mixed = (jnp.dot(ys_ref[rows, :], wo_ref[0], preferred_element_type=F32)
                 + jnp.dot(yl_ref[rows, :], wo_ref[1], preferred_element_type=F32))
        h = x + g1_ref[0] * mixed
        h_ref[rows, :] = h
        xn = _rms(h, gf_ref[...]) * (1.0 + sc_ref[0]) + sh_ref[0]
        x_hi = xn.astype(BF16)
        xn_ref[rows, :] = x_hi
        pg, cnt = _route_tile(xn, x_hi, rwh_ref[...], rwl_ref[...], rb_ref[...])
        pg_ref[:, rows] = pg
        pgt_ref[rows, :] = pg.T
        cnt_ref[c] = jnp.broadcast_to(cnt, (N_EXPERTS, LANES))


def _out_proj(ys5, ylru, x, modpat, pattern, tm_rows, w_out_bf, g_ffn, rw_hi, rw_lo, router_b, subtiles):
    t_rows = ys5.shape[0]
    step_rows = subtiles * ROWS
    half_spec = pl.BlockSpec((step_rows, D_S5), lambda i: (i, 0))
    row_spec = pl.BlockSpec((step_rows, D_MODEL), lambda i: (i, 0))
    if tm_rows:
        bt = x.shape[0]
        x_spec = pl.BlockSpec(memory_space=pl.ANY)
        dma_scratch = [pltpu.VMEM((2, step_rows // bt, bt, D_MODEL), F32), pltpu.SemaphoreType.DMA((2,))]
    else:
        x_spec = row_spec
        dma_scratch = []
    mod_spec = lambda k: pl.BlockSpec((1, ROWS, D_MODEL), lambda i: (pattern, 0, k))
    return pl.pallas_call(
        functools.partial(_out_body, subtiles, tm_rows),
        grid=(t_rows // step_rows,),
        in_specs=[half_spec, half_spec, x_spec, mod_spec(2), mod_spec(4), mod_spec(3),
                  _full((2, D_S5, D_MODEL)), _full((1, D_MODEL)),
                  _full((N_EXPERTS, D_MODEL)), _full((N_EXPERTS, D_MODEL)), _full((N_EXPERTS, 1))],
        out_specs=[row_spec, row_spec,
                   pl.BlockSpec((2 * TOP_K, step_rows), lambda i: (0, i)),
                   pl.BlockSpec((step_rows, 2 * TOP_K), lambda i: (i, 0)),
                   pl.BlockSpec((subtiles, N_EXPERTS, LANES), lambda i: (i, 0, 0))],
        out_shape=[jax.ShapeDtypeStruct((t_rows, D_MODEL), F32), jax.ShapeDtypeStruct((t_rows, D_MODEL), BF16),
                   jax.ShapeDtypeStruct((2 * TOP_K, t_rows), F32),
                   jax.ShapeDtypeStruct((t_rows, 2 * TOP_K), F32),
                   jax.ShapeDtypeStruct((t_rows // ROWS, N_EXPERTS, LANES), F32)],
        scratch_shapes=dma_scratch,
        compiler_params=_params("arbitrary"),
        name="out_proj_router",
    )(ys5, ylru, x, modpat, modpat, modpat, w_out_bf, g_ffn, rw_hi, rw_lo, router_b)


def _start_segments(n_ref, hbm_ref, vmem_ref, hbm, vmem_buf, sem, step, to_hbm):
    for e in range(N_EXPERTS):
        n = n_ref[step * N_EXPERTS + e]
        h0 = hbm_ref[step * N_EXPERTS + e]
        v0 = vmem_ref[step * N_EXPERTS + e]

        def piece(off, size, h0=h0, v0=v0):
            h = hbm.at[pl.ds(pl.multiple_of(h0 + off, SEG_ALIGN), size)]
            v = vmem_buf.at[pl.ds(pl.multiple_of(v0 + off, SEG_ALIGN), size)]
            (pltpu.make_async_copy(v, h, sem) if to_hbm else pltpu.make_async_copy(h, v, sem)).start()

        _row_pieces(n, ROWS, piece)


def _wait_rows(total, largest, hbm, sem):
    _row_pieces(total, largest, lambda off, size: pltpu.make_async_copy(
        hbm.at[pl.ds(0, size)], hbm.at[pl.ds(0, size)], sem).wait())


def _dispatch_body(tiles_p, n_ref, glob_ref, local_ref, tot_ref, nb_ref, pg_ref, xp_ref, xsm_ref, xs_hbm, stage,
                   sem):
    j = pl.program_id(0)
    last = pl.num_programs(0) - 1
    slot = j % 2

    def unused_blocks(act):
        def blk(b, carry):
            act(pltpu.make_async_copy(stage.at[slot, pl.ds(0, MOE_TM)],
                                      xs_hbm.at[pl.ds(pl.multiple_of(b * MOE_TM, MOE_TM), MOE_TM)], sem.at[slot]))
            return carry
        lax.fori_loop(nb_ref[0], xs_hbm.shape[0] // MOE_TM, blk, 0)

    def wait_step(step, s):
        _wait_rows(tot_ref[step], WAIT_MAX_PIECE, xs_hbm, sem.at[s])

    @pl.when(j >= 2)
    def _():
        wait_step(jnp.maximum(j - 2, 0), slot)

    @pl.when(j < last)
    def _():
        x = jnp.where(j < tiles_p, xp_ref[...], xsm_ref[...])
        pos = pg_ref[0:TOP_K, :]
        r = lax.broadcasted_iota(jnp.int32, (SORT_CHUNK, ROWS), 0).astype(F32).astype(BF16)
        for c in range(SORT_ROWS // SORT_CHUNK):
            rel = (pos - float(c * SORT_CHUNK)).astype(BF16)
            pick = (r == rel[0:1]) | (r == rel[1:2]) | (r == rel[2:3]) | (r == rel[3:4])
            stage[slot, c * SORT_CHUNK:(c + 1) * SORT_CHUNK, :] = _pack_rows(jnp.dot(
                jnp.where(pick, jnp.ones((), BF16), jnp.zeros((), BF16)), x, preferred_element_type=F32))

    @pl.when(j == last)
    def _():
        stage[slot, 0:MOE_TM, :] = jnp.zeros((MOE_TM, PACKED), U32)

    _start_segments(n_ref, glob_ref, local_ref, xs_hbm, stage.at[slot], sem.at[slot], j, True)

    @pl.when(j == last)
    def _():
        unused_blocks(lambda c: c.start())

        @pl.when(j >= 1)
        def _():
            wait_step(jnp.maximum(j - 1, 0), 1 - slot)
        wait_step(j, slot)
        unused_blocks(lambda c: c.wait())


def _dispatch(seg_n, seg_glob, seg_local, seg_tot, nb_used, pg, xn_p, xn_s, n_rows):
    tiles_p = xn_p.shape[0] // ROWS
    tiles_s = xn_s.shape[0] // ROWS
    tiles = tiles_p + tiles_s
    grid_spec = pltpu.PrefetchScalarGridSpec(
        num_scalar_prefetch=5,
        grid=(tiles + 1,),
        in_specs=[pl.BlockSpec((2 * TOP_K, ROWS), lambda j, *_: (0, jnp.minimum(j, tiles - 1))),
                  pl.BlockSpec((ROWS, D_MODEL), lambda j, *_: (jnp.minimum(j, tiles_p - 1), 0)),
                  pl.BlockSpec((ROWS, D_MODEL), lambda j, *_: (jnp.clip(j - tiles_p, 0, tiles_s - 1), 0))],
        out_specs=pl.BlockSpec(memory_space=pl.ANY),
        scratch_shapes=[pltpu.VMEM((2, SORT_ROWS, PACKED), U32), pltpu.SemaphoreType.DMA((2,))],
    )
    return pl.pallas_call(
        functools.partial(_dispatch_body, tiles_p),
        grid_spec=grid_spec,
        out_shape=jax.ShapeDtypeStruct((n_rows, PACKED), U32),
        compiler_params=_params("arbitrary"),
        name="moe_dispatch",
    )(seg_n, seg_glob, seg_local, seg_tot, nb_used, pg, xn_p, xn_s)


def _moe_body(be_ref, nxt_ref, nv_ref, nb_ref, xs_ref, wgu_hbm, bgu_ref, wd_hbm, bd_ref, ys_ref,
              wgu_f32, wd_f32, wgu_bf, wd_bf, sem):
    i = pl.program_id(0)

    def weight_copies(e):
        return (pltpu.make_async_copy(wgu_hbm.at[e], wgu_f32, sem.at[0]),
                pltpu.make_async_copy(wd_hbm.at[e], wd_f32, sem.at[1]))

    @pl.when(i >= nb_ref[0])
    def _():
        ys_ref[...] = jnp.zeros_like(ys_ref)

    @pl.when(i < nb_ref[0])
    def _():
        e = be_ref[i]

        @pl.when(i == 0)
        def _():
            for c in weight_copies(e):
                c.start()

        @pl.when(jnp.logical_or(i == 0, e != be_ref[jnp.maximum(i - 1, 0)]))
        def _():
            for c in weight_copies(e):
                c.wait()
            wgu_bf[...] = wgu_f32[...].astype(BF16)
            wd_bf[...] = wd_f32[...].astype(BF16)

            @pl.when(nxt_ref[i] >= 0)
            def _():
                for c in weight_copies(nxt_ref[i]):
                    c.start()

        def ffn(packed):
            hg = jnp.dot(_unpack_rows(packed), wgu_bf[...], preferred_element_type=F32) + bgu_ref[0]
            gate = jnp.minimum(hg[:, :D_FF], SWIGLU_LIMIT)
            up = jnp.clip(hg[:, D_FF:], -SWIGLU_LIMIT, SWIGLU_LIMIT)
            act = (up + 1.0) * (gate * jax.nn.sigmoid(SWIGLU_ALPHA * gate))
            y = jnp.dot(act.astype(BF16), wd_bf[...], preferred_element_type=F32) + bd_ref[0]
            return _pack_rows(y.astype(BF16).astype(F32))

        for parts in range(1, MOE_TM // MOE_PART + 1):
            lo = (parts - 1) * MOE_PART
            last_path = parts == MOE_TM // MOE_PART

            @pl.when(jnp.logical_and(nv_ref[i] > lo, jnp.logical_or(last_path, nv_ref[i] <= lo + MOE_PART)))
            def _(parts=parts):
                rows = parts * MOE_PART
                ys_ref[0:rows, :] = ffn(xs_ref[0:rows, :])
                if rows < MOE_TM:
                    ys_ref[rows:, :] = jnp.zeros((MOE_TM - rows, PACKED), U32)


def _moe(xs, block_expert, block_next, block_rows, nb_used, w_gu, b_gu, w_down, b_down):
    n_blocks = xs.shape[0] // MOE_TM
    in_rows = pl.BlockSpec((MOE_TM, PACKED), lambda i, be, nx, nv, nb: (jnp.minimum(i, nb[0] - 1), 0))
    grid_spec = pltpu.PrefetchScalarGridSpec(
        num_scalar_prefetch=4,
        grid=(n_blocks,),
        in_specs=[in_rows,
                  pl.BlockSpec(memory_space=pl.ANY),
                  pl.BlockSpec((1, 1, 2 * D_FF), lambda i, be, nx, nv, nb: (be[i], 0, 0)),
                  pl.BlockSpec(memory_space=pl.ANY),
                  pl.BlockSpec((1, 1, D_MODEL), lambda i, be, nx, nv, nb: (be[i], 0, 0))],
        out_specs=pl.BlockSpec((MOE_TM, PACKED), lambda i, be, nx, nv, nb: (i, 0)),
        scratch_shapes=[pltpu.VMEM((D_MODEL, 2 * D_FF), F32), pltpu.VMEM((D_FF, D_MODEL), F32),
                        pltpu.VMEM((D_MODEL, 2 * D_FF), BF16), pltpu.VMEM((D_FF, D_MODEL), BF16),
                        pltpu.SemaphoreType.DMA((2,))],
    )
    return pl.pallas_call(
        _moe_body,
        grid_spec=grid_spec,
        out_shape=jax.ShapeDtypeStruct(xs.shape, U32),
        compiler_params=_params("arbitrary"),
        name="moe_experts",
    )(block_expert, block_next, block_rows, nb_used, xs, w_gu, b_gu, w_down, b_down)


def _fin_body(tiles_p, n_ref, glob_ref, local_ref, tot_ref, hp_ref, hs_ref, pgp_ref, pgs_ref, g2_ref, gf_ref, ys_hbm,
              op_hbm, os_ref, ybuf, sem, yst, osem):
    j = pl.program_id(0)
    tiles = pl.num_programs(0)
    slot = j % 2
    is_prompt = j < tiles_p

    def fetch(step, s):
        _start_segments(n_ref, glob_ref, local_ref, ys_hbm, ybuf.at[s], sem.at[s], step, False)

    @pl.when(j == 0)
    def _():
        ybuf[...] = jnp.zeros_like(ybuf)
        fetch(0, 0)

    _wait_rows(tot_ref[j], WAIT_MAX_PIECE, ys_hbm, sem.at[slot])

    @pl.when(j + 1 < tiles)
    def _():
        fetch(jnp.minimum(j + 1, tiles - 1), 1 - slot)

    pg = jnp.where(is_prompt, pgp_ref[...], pgs_ref[...])
    ff = jnp.zeros((ROWS, D_MODEL), F32)
    r = lax.broadcasted_iota(jnp.int32, (ROWS, SORT_CHUNK), 1).astype(F32).astype(BF16)
    gates = pg[:, TOP_K:].astype(BF16)
    for c in range(SORT_ROWS // SORT_CHUNK):
        rel = (pg[:, :TOP_K] - float(c * SORT_CHUNK)).astype(BF16)
        w = jnp.zeros((ROWS, SORT_CHUNK), BF16)
        for k in range(TOP_K):
            w = jnp.where(r == rel[:, k:k + 1], gates[:, k:k + 1], w)
        ff = ff + jnp.dot(w, _unpack_rows(ybuf[slot, c * SORT_CHUNK:(c + 1) * SORT_CHUNK, :]),
                          preferred_element_type=F32)
    h = jnp.where(is_prompt, hp_ref[...], hs_ref[...])
    y = _rms(h + g2_ref[0] * ff, gf_ref[...])

    tc, bp = yst.shape[1], yst.shape[2]

    def out_copies(tile, s, act):
        for b in range(bp):
            act(pltpu.make_async_copy(yst.at[s, :, b, :], op_hbm.at[b, pl.ds(tile * tc, tc), :], osem.at[s]))

    @pl.when(is_prompt)
    def _():
        @pl.when(j >= 2)
        def _():
            out_copies(0, slot, lambda c: c.wait())
        yst[slot] = y.reshape(tc, bp, D_MODEL)
        out_copies(j, slot, lambda c: c.start())

    @pl.when(jnp.logical_not(is_prompt))
    def _():
        os_ref[...] = y

    @pl.when(j == tiles - 1)
    def _():
        for tile in range(max(tiles_p - 2, 0), tiles_p):
            out_copies(0, tile % 2, lambda c: c.wait())


def _final(seg_n, seg_glob, seg_local, seg_tot, h_p, h_s, pgt_p, pgt_s, modpat, g_final, ys, bp, tiles_p, tiles_s):
    hp_spec, hs_spec = _tile_specs(tiles_p, tiles_s)
    grid_spec = pltpu.PrefetchScalarGridSpec(
        num_scalar_prefetch=4,
        grid=(tiles_p + tiles_s,),
        in_specs=[hp_spec, hs_spec,
                  pl.BlockSpec((ROWS, 2 * TOP_K), lambda j, *_: (jnp.minimum(j, tiles_p - 1), 0)),
                  pl.BlockSpec((ROWS, 2 * TOP_K), lambda j, *_: (jnp.clip(j - tiles_p, 0, tiles_s - 1), 0)),
                  _mod_spec(5, tiles_p), _full((1, D_MODEL)),
                  pl.BlockSpec(memory_space=pl.ANY)],
        out_specs=[pl.BlockSpec(memory_space=pl.ANY), hs_spec],
        scratch_shapes=[pltpu.VMEM((2, SORT_ROWS, PACKED), U32), pltpu.SemaphoreType.DMA((2,)),
                        pltpu.VMEM((2, ROWS // bp, bp, D_MODEL), F32), pltpu.SemaphoreType.DMA((2,))],
    )
    return pl.pallas_call(
        functools.partial(_fin_body, tiles_p),
        grid_spec=grid_spec,
        out_shape=[jax.ShapeDtypeStruct((bp, h_p.shape[0] // bp, D_MODEL), F32), jax.ShapeDtypeStruct(h_s.shape, F32)],
        compiler_params=_params("arbitrary"),
        name="combine_final",
    )(seg_n, seg_glob, seg_local, seg_tot, h_p, h_s, pgt_p, pgt_s, modpat, g_final, ys)


def _block_diag(w):
    h, i, j = w.shape
    return jnp.einsum('hij,hk->hikj', w, jnp.eye(h, dtype=w.dtype)).reshape(h * i, h * j)


def _s5_cols(re, im):
    b = re.shape[0]
    return jnp.concatenate([re.reshape(b, -1), im.reshape(b, -1)], axis=1)


def _s5_uncols(cols):
    b = cols.shape[0]
    return (cols[:, :S5_COLS // 2].reshape(b, S5_GROUPS, S5_STATE),
            cols[:, S5_COLS // 2:].reshape(b, S5_GROUPS, S5_STATE))


def _moe_rows_bound(tiles):
    worst = tiles * (TOP_K * ROWS + N_EXPERTS * (SEG_ALIGN - 1)) + N_EXPERTS * (MOE_TM - SEG_ALIGN)
    return (worst + MOE_TM - 1) // MOE_TM * MOE_TM


def _plan(cnt):
    cnt = cnt.astype(jnp.int32)
    tiles = cnt.shape[0]
    cp = (cnt + SEG_ALIGN - 1) // SEG_ALIGN * SEG_ALIGN
    local = jnp.cumsum(cp, axis=1) - cp
    group = jnp.sum(cp, axis=0)
    group_pad = (group + MOE_TM - 1) // MOE_TM * MOE_TM
    pend = jnp.cumsum(group_pad)
    pstart = pend - group_pad
    glob = pstart[None, :] + jnp.cumsum(cp, axis=0) - cp
    gap = group_pad - group
    seg_n = jnp.concatenate([cp, gap[None]], axis=0).reshape(-1)
    seg_local = jnp.concatenate([local, jnp.zeros((1, N_EXPERTS), jnp.int32)], axis=0).reshape(-1)
    seg_glob = jnp.concatenate([glob, (pstart + group)[None]], axis=0).reshape(-1)
    seg_tot = jnp.concatenate([jnp.sum(cp, axis=1), jnp.sum(gap)[None]]).astype(jnp.int32)
    n_blocks = _moe_rows_bound(tiles) // MOE_TM
    block_row0 = jnp.arange(n_blocks, dtype=jnp.int32) * MOE_TM
    block_expert = jnp.minimum(jnp.sum(block_row0[:, None] >= pend[None, :], axis=1), N_EXPERTS - 1).astype(jnp.int32)
    nb_used = (pend[-1] // MOE_TM).astype(jnp.int32).reshape(1)
    experts = jnp.arange(N_EXPERTS, dtype=jnp.int32)
    later_owner = jnp.where((experts[None, :] > experts[:, None]) & (group_pad[None, :] > 0), experts[None, :],
                            N_EXPERTS)
    next_owner = jnp.min(later_owner, axis=1)
    next_owner = jnp.where(next_owner == N_EXPERTS, -1, next_owner).astype(jnp.int32)
    owner = block_expert[:, None] == experts[None, :]
    block_next = jnp.sum(jnp.where(owner, next_owner[None, :], 0), axis=1).astype(jnp.int32)
    group_end = jnp.sum(jnp.where(owner, (pstart + group)[None, :], 0), axis=1)
    block_rows = jnp.clip(group_end - block_row0, 0, MOE_TM).astype(jnp.int32)
    return seg_n, seg_glob, seg_local, seg_tot, block_expert, block_next, block_rows, nb_used


def kernel(x_prompt, x_sample, state_s5_re, state_s5_im, state_lru_h, state_conv, c_prompt, c_sample, w_ada, b_ada, g_mix, w_in, s5_a_re, s5_a_im, s5_log_dt, s5_b_re, s5_b_im, s5_c_re, s5_c_im, s5_d, s5_w_glu, s5_b_glu, lru_conv_w, lru_conv_b, lru_w_a, lru_b_a, lru_w_x, lru_b_x, lru_lambda, g_out_s5, g_out_lru, w_out, g_ffn, router_w, router_b, moe_w_gu, moe_b_gu, moe_w_down, moe_b_down, g_final):
    assert w_ada.shape[0] == 1, "one layer"
    bp, lp, _ = x_prompt.shape
    bs, ls, _ = x_sample.shape
    assert ROWS % bp == 0 and ROWS % bs == 0 and (bp * lp) % ROWS == 0 and (bs * ls) % ROWS == 0
    tiles_p = bp * lp // ROWS
    tiles_s = bs * ls // ROWS
    row = lambda v: v.reshape(1, -1)

    ab_re, ab_im, bb_re, bb_im = _s5_prep(s5_a_re[0], s5_a_im[0], s5_log_dt[0].reshape(S5_GROUPS, 1),
                                          jnp.swapaxes(s5_b_re[0], 1, 2), jnp.swapaxes(s5_b_im[0], 1, 2))
    gpb = S5_GROUPS // S5_BLOCKS
    eye = jnp.eye(gpb, dtype=F32)

    def in_blocks(b):
        b = b.reshape(S5_BLOCKS, gpb, S5_GROUP_CH, S5_STATE)
        return jnp.einsum('bgjn,gh->bgjhn', b, eye).reshape(S5_BLOCKS, gpb * S5_GROUP_CH, gpb * S5_STATE)

    def out_blocks(c):
        c = c.reshape(S5_BLOCKS, gpb, S5_GROUP_CH, S5_STATE)
        return jnp.einsum('bgjn,gh->bgnhj', c, eye).reshape(S5_BLOCKS, gpb * S5_STATE, gpb * S5_GROUP_CH)

    ar8 = jnp.broadcast_to(ab_re.reshape(1, -1), (SUBLANES, S5_GROUPS * S5_STATE))
    ai8 = jnp.broadcast_to(ab_im.reshape(1, -1), (SUBLANES, S5_GROUPS * S5_STATE))
    bb = jnp.concatenate([in_blocks(bb_re), in_blocks(bb_im)], axis=-1).astype(BF16)
    cc = jnp.concatenate([out_blocks(s5_c_re[0]), -out_blocks(s5_c_im[0])], axis=1).astype(BF16)
    wa_bd = _block_diag(lru_w_a[0]).astype(BF16)
    wx_bd = _block_diag(lru_w_x[0]).astype(BF16)
    rw_t = router_w[0].T
    rw_hi = rw_t.astype(BF16)
    rw_lo = (rw_t - rw_hi.astype(F32)).astype(BF16)

    modpat = _adaln(jnp.concatenate([c_prompt, c_sample], axis=0), w_ada[0], row(b_ada[0]), bp)
    pair = 2 if tiles_p % 2 == 0 else 1

    def conv_tm(cv):
        return jnp.swapaxes(cv, 0, 1).reshape(-1, D_LRU)

    def conv_bm(cv, b):
        return jnp.swapaxes(cv.reshape(CONV_WIDTH - 1, b, D_LRU), 0, 1)

    x_s = jnp.swapaxes(x_sample, 0, 1).reshape(bs * ls, D_MODEL)
    in_args = (row(g_mix[0]), w_in[0].astype(BF16))
    lru_args = (lru_conv_w[0], row(lru_conv_b[0]), wa_bd, row(lru_b_a[0]), wx_bd, row(lru_b_x[0]),
                row(lru_lambda[0]))
    u_p, la_p, lb_p, lg_p, cvp = _in_proj(x_prompt, modpat, 0, True, *in_args,
                                          jnp.zeros(((CONV_WIDTH - 1) * bp, D_LRU), F32), *lru_args, bp, pair)
    u_s, la_s, lb_s, lg_s, cvs = _in_proj(x_s, modpat, 1, None, *in_args, conv_tm(state_conv[0]), *lru_args, bs, 1)

    s5_args = (ar8, ai8, bb, cc, row(s5_d[0]), s5_w_glu[0].astype(BF16), row(s5_b_glu[0]), row(g_out_s5[0]))
    ys5_p, s5p, ylru_p, hp = _mixers(u_p, jnp.zeros((bp, S5_COLS), F32), *s5_args, la_p, lb_p, lg_p,
                                     jnp.zeros((bp, D_LRU), F32), row(g_out_lru[0]), bp, pair)
    ys5_s, s5s, ylru_s, hs = _mixers(u_s, _s5_cols(state_s5_re[0], state_s5_im[0]), *s5_args, la_s, lb_s, lg_s,
                                     state_lru_h[0], row(g_out_lru[0]), bs, 1)

    out_args = (w_out[0].astype(BF16).reshape(2, D_S5, D_MODEL), row(g_ffn[0]), rw_hi, rw_lo,
                router_b[0].reshape(N_EXPERTS, 1))
    h_p, xn_p, pg_p, pgt_p, cnt_p = _out_proj(ys5_p, ylru_p, x_prompt, modpat, 0, True, *out_args, pair)
    h_s, xn_s, pg_s, pgt_s, cnt_s = _out_proj(ys5_s, ylru_s, x_s, modpat, 1, False, *out_args, 1)
    pg = jnp.concatenate([pg_p, pg_s], axis=1)
    cnt = jnp.concatenate([cnt_p, cnt_s], axis=0)

    seg_n, seg_glob, seg_local, seg_tot, block_expert, block_next, block_rows, nb_used = _plan(cnt[:, :, 0])
    xs = _dispatch(seg_n, seg_glob, seg_local, seg_tot, nb_used, pg, xn_p, xn_s,
                   _moe_rows_bound(tiles_p + tiles_s))
    ys = _moe(xs, block_expert, block_next, block_rows, nb_used, moe_w_gu[0], moe_b_gu[0].reshape(N_EXPERTS, 1, 2 * D_FF),
              moe_w_down[0], moe_b_down[0].reshape(N_EXPERTS, 1, D_MODEL))
    y_prompt, y_s = _final(seg_n, seg_glob, seg_local, seg_tot, h_p, h_s, pgt_p, pgt_s, modpat, row(g_final), ys,
                           bp, tiles_p, tiles_s)
    y_sample = jnp.swapaxes(y_s.reshape(ls, bs, D_MODEL), 0, 1)
    s5p_re, s5p_im = _s5_uncols(s5p)
    s5s_re, s5s_im = _s5_uncols(s5s)
    return (y_prompt, y_sample,
            s5p_re[None], s5p_im[None], hp[None], conv_bm(cvp, bp)[None],
            s5s_re[None], s5s_im[None], hs[None], conv_bm(cvs, bs)[None])
```
